```python
import math
import jax, jax.numpy as jnp
from jax import lax
import numpy as np

D_MODEL = 2048
BATCH = 4
SEQ = 2048
DEPTH = 2
DEC_BATCH = 8
DEC_SEQ = 4
PAST_LEN = 16384
PAGE_SIZE = 128

HEAD_DIM = 64
D_MIX = D_MODEL
D_RWKV = D_MIX // 2
D_NSA = D_MIX - D_RWKV
H_RWKV = D_RWKV // HEAD_DIM
H_NSA = D_NSA // HEAD_DIM
NSA_KV_HEADS = 4
NSA_GROUP = H_NSA // NSA_KV_HEADS
NSA_KV_COLS = NSA_KV_HEADS * HEAD_DIM
CMP_BLOCK = 32
SEL_BLOCK = 64
SEL_TOPN = 16
WINDOW = 512
Q_BLOCK = 128
W_LORA = 64
A_LORA = 64
G_LORA = 160
RWKV_COLS = 3 * D_RWKV + W_LORA + A_LORA + G_LORA
NSA_COLS = D_NSA + 6 * NSA_KV_COLS + 3 * H_NSA
IN_COLS = RWKV_COLS + NSA_COLS
P_HEADS = 8
N_KEYS = 128
N_EXPERTS = N_KEYS * N_KEYS
P_DKEY = 256
P_TOPK = 16
P_TOK_BLOCK = 128
LN_EPS = 1e-5
GN_EPS = 64e-5
DEEPNORM_ALPHA = (2 * DEPTH) ** 0.25
DEEPNORM_BETA = (8 * DEPTH) ** -0.25
FORCE_SCORE = 1e4
NEG_INF = -1e30

kernel_name = 'hymba_rwkv7_nsa_peer_decode_step'


def layer_norm(x, g, b):
    xf = x.astype(jnp.float32)
    mu = xf.mean(-1, keepdims=True)
    var = jnp.mean(jnp.square(xf - mu), -1, keepdims=True)
    return ((xf - mu) * lax.rsqrt(var + LN_EPS) * g + b).astype(x.dtype)


def masked_softmax(s, mask):
    s = jnp.where(mask, s.astype(jnp.float32), NEG_INF)
    p = jax.nn.softmax(s, axis=-1)
    return jnp.where(mask, p, 0.0)


def rwkv_time_mix(xr, shift_prev, s0, p):
    B, T, _ = xr.shape
    dt = xr.dtype
    prev = jnp.concatenate([shift_prev.astype(dt), xr[:, :-1]], axis=1)
    xs = xr + p['rwkv_mu'] * (prev - xr)
    cuts = [D_RWKV, 2 * D_RWKV, 3 * D_RWKV, 3 * D_RWKV + W_LORA, 3 * D_RWKV + W_LORA + A_LORA]
    r, k, v, wl, al, gl = jnp.split(xs, cuts, axis=-1)
    w = -jax.nn.softplus(-(p['rwkv_w0'] + jnp.tanh(wl) @ p['rwkv_w2'])) - 0.5
    decay = jnp.exp(-jnp.exp(w))
    a = jax.nn.sigmoid(p['rwkv_a0'] + al @ p['rwkv_a2'])
    g = jax.nn.sigmoid(gl) @ p['rwkv_g2']
    hd = lambda t: t.reshape(B, T, H_RWKV, HEAD_DIM)
    kkf = hd(k * p['rwkv_k_k']).astype(jnp.float32)
    kk = (kkf * lax.rsqrt(jnp.maximum(jnp.sum(kkf * kkf, -1, keepdims=True), 1e-24))).astype(dt)
    k = k * (1.0 + (a - 1.0) * p['rwkv_k_a'])
    r, decay, k, v, a = hd(r), hd(decay), hd(k), hd(v), hd(a)

    def step(S, inp):
        r_t, d_t, k_t, v_t, kk_t, a_t = inp
        sk = jnp.einsum('bhvk,bhk->bhv', S, kk_t)
        S = (S * d_t[:, :, None, :] - sk[..., None] * (kk_t * a_t)[:, :, None, :]
             + v_t[..., None] * k_t[:, :, None, :])
        return S, jnp.einsum('bhvk,bhk->bhv', S, r_t)

    seq_first = lambda t: jnp.moveaxis(t, 1, 0)
    s_T, y = lax.scan(step, s0.astype(dt), (seq_first(r), seq_first(decay), seq_first(k),
                                            seq_first(v), seq_first(kk), seq_first(a)))
    y = jnp.moveaxis(y, 0, 1).astype(jnp.float32)
    mu = y.mean(-1, keepdims=True)
    var = jnp.mean(jnp.square(y - mu), -1, keepdims=True)
    yn = ((y - mu) * lax.rsqrt(var + GN_EPS)).reshape(B, T, D_RWKV) * p['rwkv_gn_g'] + p['rwkv_gn_b']
    bonus = jnp.sum(r * k * p['rwkv_r_k'], -1, keepdims=True) * v
    out = (yn.astype(dt) + bonus.reshape(B, T, D_RWKV)) * g
    return out, s_T, xr[:, -1:]


def nsa_mix(xn, past_cmp, past_sel, win_prefix, pos0, n_keep, cmp_w):
    B, T, _ = xn.shape
    dt = xn.dtype
    cuts = [D_NSA + i * NSA_KV_COLS for i in range(7)]
    q, kc, vc, ksl, vsl, kw, vw, gl = jnp.split(xn, cuts, axis=-1)
    kvh = lambda t: t.reshape(B, T, NSA_KV_HEADS, HEAD_DIM)
    new_cmp = jnp.stack([kvh(kc), kvh(vc)], axis=2)
    new_sel = jnp.stack([kvh(ksl), kvh(vsl)], axis=2)
    new_win = jnp.stack([kvh(kw), kvh(vw)], axis=2)
    L = past_cmp.shape[1] + T
    L_pad = -(-L // SEL_BLOCK) * SEL_BLOCK
    pad = lambda t: jnp.pad(t, ((0, 0), (0, L_pad - L), (0, 0), (0, 0), (0, 0)))
    full_cmp = pad(jnp.concatenate([past_cmp.astype(dt), new_cmp], axis=1))
    full_sel = pad(jnp.concatenate([past_sel.astype(dt), new_sel], axis=1))
    n_cmp = L_pad // CMP_BLOCK
    n_sel = L_pad // SEL_BLOCK
    cblk = full_cmp.reshape(B, n_cmp, CMP_BLOCK, 2, NSA_KV_HEADS, HEAD_DIM)
    kcb = jnp.einsum('bnlkd,l->bnkd', cblk[:, :, :, 0], cmp_w[0])
    vcb = jnp.einsum('bnlkd,l->bnkd', cblk[:, :, :, 1], cmp_w[1])
    sblk = full_sel.reshape(B, n_sel, SEL_BLOCK, 2, NSA_KV_HEADS, HEAD_DIM)
    ksb = jnp.transpose(sblk[:, :, :, 0], (0, 3, 1, 2, 4))
    vsb = jnp.transpose(sblk[:, :, :, 1], (0, 3, 1, 2, 4))
    n_prefix = win_prefix.shape[1]
    win_pad = jnp.concatenate([jnp.zeros((B, WINDOW, 2, NSA_KV_HEADS, HEAD_DIM), dt),
                               win_prefix.astype(dt), new_win], axis=1)
    q = q.reshape(B, T, NSA_KV_HEADS, NSA_GROUP, HEAD_DIM)
    gates = jax.nn.sigmoid(gl.reshape(B, T, NSA_KV_HEADS, NSA_GROUP, 3))
    qc = math.gcd(T, Q_BLOCK)
    n_chunks = T // qc
    top_n = min(SEL_TOPN, n_sel)
    scale = HEAD_DIM ** -0.5
    bi = jnp.arange(B)[:, None, None, None]
    hi = jnp.arange(NSA_KV_HEADS)[None, None, :, None]
    cmp_end = (jnp.arange(n_cmp) + 1) * CMP_BLOCK - 1
    blk = jnp.arange(n_sel)
    lw = WINDOW - 1 + qc
    jw = jnp.arange(lw)
    iw = jnp.arange(qc)

    def chunk(args):
        c, q_c, g_c = args
        qpos = pos0 + c * qc + jnp.arange(qc)
        q_c = q_c * scale
        s = jnp.einsum('bqkgd,bnkd->bqkgn', q_c, kcb)
        p_c = masked_softmax(s, (cmp_end[None, :] <= qpos[:, None])[None, :, None, None, :])
        o_cmp = jnp.einsum('bqkgn,bnkd->bqkgd', p_c.astype(dt), vcb)
        imp = p_c.sum(3).reshape(B, qc, NSA_KV_HEADS, n_sel, SEL_BLOCK // CMP_BLOCK).sum(-1)
        cur = (qpos // SEL_BLOCK)[None, :, None, None]
        forced = (blk == 0) | (blk == cur) | (blk == cur - 1)
        imp = jnp.where(blk > cur, -1.0, jnp.where(forced, FORCE_SCORE, imp))
        top_val, top_idx = lax.top_k(imp, top_n)
        k_g = ksb[bi, hi, top_idx]
        v_g = vsb[bi, hi, top_idx]
        s = jnp.einsum('bqkgd,bqknld->bqkgnl', q_c, k_g)
        kpos = top_idx[..., None] * SEL_BLOCK + jnp.arange(SEL_BLOCK)
        m = (top_val >= 0)[..., None] & (kpos <= qpos[None, :, None, None, None])
        p_s = masked_softmax(s.reshape(B, qc, NSA_KV_HEADS, NSA_GROUP, top_n * SEL_BLOCK),
                             m.reshape(B, qc, NSA_KV_HEADS, 1, top_n * SEL_BLOCK))
        o_sel = jnp.einsum('bqkgnl,bqknld->bqkgd',
                           p_s.reshape(B, qc, NSA_KV_HEADS, NSA_GROUP, top_n, SEL_BLOCK).astype(dt), v_g)
        start = n_prefix + c * qc + 1
        wkv = lax.dynamic_slice_in_dim(win_pad, start, lw, axis=1)
        s = jnp.einsum('bqkgd,bjkd->bqkgj', q_c, wkv[:, :, 0])
        m = ((jw[None, :] >= iw[:, None]) & (jw[None, :] <= iw[:, None] + WINDOW - 1)
             & (start + jw[None, :] >= WINDOW))
        p_w = masked_softmax(s, m[None, :, None, None, :])
        o_win = jnp.einsum('bqkgj,bjkd->bqkgd', p_w.astype(dt), wkv[:, :, 1])
        return g_c[..., 0:1] * o_cmp + g_c[..., 1:2] * o_sel + g_c[..., 2:3] * o_win

    chunks = lambda t: jnp.moveaxis(t.reshape(B, n_chunks, qc, *t.shape[2:]), 1, 0)
    out = lax.map(chunk, (jnp.arange(n_chunks), chunks(q), chunks(gates)))
    out = jnp.moveaxis(out, 0, 1).reshape(B, T, D_NSA)
    return out, new_cmp, new_sel, win_pad[:, -n_keep:]


def peer_ffn(x, wq, subkeys, u_tab, v_tab):
    B, T, D = x.shape
    n = B * T
    xt = x.reshape(n, D)
    q = (xt @ wq).reshape(n, P_HEADS, 2, P_DKEY // 2)
    s = jnp.einsum('nhcd,hckd->nhck', q, subkeys).astype(jnp.float32)
    hv, hix = lax.top_k(s, P_TOPK)
    cand = (hv[:, :, 0, :, None] + hv[:, :, 1, None, :]).reshape(n, P_HEADS, P_TOPK * P_TOPK)
    cand_e = (hix[:, :, 0, :, None] * N_KEYS + hix[:, :, 1, None, :]).reshape(n, P_HEADS, P_TOPK * P_TOPK)
    tv, tpos = lax.top_k(cand, P_TOPK)
    expert = jnp.take_along_axis(cand_e, tpos, axis=-1)
    gate = jax.nn.softmax(tv, axis=-1).astype(x.dtype)
    cb = math.gcd(n, P_TOK_BLOCK)

    def block(args):
        xb, eb, gb = args
        h = jax.nn.gelu(jnp.einsum('cd,chkd->chk', xb, u_tab[eb]), approximate=False)
        return jnp.einsum('chk,chkd->cd', gb * h, v_tab[eb])

    out = lax.map(block, (xt.reshape(n // cb, cb, D), expert.reshape(n // cb, cb, P_HEADS, P_TOPK),
                          gate.reshape(n // cb, cb, P_HEADS, P_TOPK)))
    return out.reshape(B, T, D)


def hybrid_layer(x, pos0, past_cmp, past_sel, win_prefix, shift_prev, rwkv_s0, n_keep, p):
    proj = x @ p['w_in']
    y_r, s_T, new_shift = rwkv_time_mix(proj[..., :RWKV_COLS], shift_prev, rwkv_s0, p)
    y_n, new_cmp, new_sel, new_win = nsa_mix(proj[..., RWKV_COLS:], past_cmp, past_sel, win_prefix,
                                             pos0, n_keep, p['nsa_cmp_w'])
    h = jnp.concatenate([y_r, y_n], axis=-1) @ p['w_out']
    x = layer_norm(DEEPNORM_ALPHA * x + h, p['ln1_g'], p['ln1_b'])
    f = peer_ffn(x, p['peer_wq'], p['peer_subkeys'], p['peer_u'], p['peer_v'])
    x = layer_norm(DEEPNORM_ALPHA * x + f, p['ln2_g'], p['ln2_b'])
    return x, (new_cmp, new_sel, new_win, s_T, new_shift)


def setup_inputs(seed: int = 0) -> dict:
    key = jax.random.key(seed)
    ks = jax.random.split(key, 40)
    n_pages = PAST_LEN // PAGE_SIZE
    n_used = DEC_BATCH * n_pages
    n_pool = n_used + n_used // 4
    win_buf = min(WINDOW, PAST_LEN)
    f32 = jnp.float32
    nrm = lambda k, shape, s: s * jax.random.normal(k, shape, f32)
    page_table = jax.random.permutation(ks[0], n_pool)[:n_used].reshape(DEC_BATCH, n_pages).astype(jnp.int32)
    kv_tail = (2, NSA_KV_HEADS, HEAD_DIM)
    return {
        'x_prompt': nrm(ks[1], (BATCH, SEQ, D_MODEL), 1.0),
        'x_sample': nrm(ks[2], (DEC_BATCH, DEC_SEQ, D_MODEL), 1.0),
        'cache_cmp_kv': nrm(ks[3], (DEPTH, n_pool, PAGE_SIZE) + kv_tail, 1.0),
        'cache_sel_kv': nrm(ks[4], (DEPTH, n_pool, PAGE_SIZE) + kv_tail, 1.0),
        'page_table': page_table,
        'state_win_kv': nrm(ks[5], (DEPTH, DEC_BATCH, win_buf) + kv_tail, 1.0),
        'state_rwkv': nrm(ks[6], (DEPTH, DEC_BATCH, H_RWKV, HEAD_DIM, HEAD_DIM), 0.3),
        'state_shift': nrm(ks[7], (DEPTH, DEC_BATCH, 1, RWKV_COLS), 1.0),
        'w_in': nrm(ks[8], (DEPTH, D_MODEL, IN_COLS), D_MODEL ** -0.5),
        'rwkv_mu': jax.random.uniform(ks[9], (DEPTH, RWKV_COLS), f32),
        'rwkv_w0': nrm(ks[10], (DEPTH, D_RWKV), 0.5) - 1.0,
        'rwkv_w2': nrm(ks[11], (DEPTH, W_LORA, D_RWKV), 0.5 * W_LORA ** -0.5),
        'rwkv_a0': nrm(ks[12], (DEPTH, D_RWKV), 0.5),
        'rwkv_a2': nrm(ks[13], (DEPTH, A_LORA, D_RWKV), 0.5 * A_LORA ** -0.5),
        'rwkv_g2': nrm(ks[14], (DEPTH, G_LORA, D_RWKV), G_LORA ** -0.5),
        'rwkv_k_k': 0.85 + nrm(ks[15], (DEPTH, D_RWKV), 0.05),
        'rwkv_k_a': 1.0 + nrm(ks[16], (DEPTH, D_RWKV), 0.05),
        'rwkv_r_k': nrm(ks[17], (DEPTH, H_RWKV, HEAD_DIM), 0.1),
        'rwkv_gn_g': 1.0 + nrm(ks[18], (DEPTH, D_RWKV), 0.01),
        'rwkv_gn_b': nrm(ks[19], (DEPTH, D_RWKV), 0.01),
        'nsa_cmp_w': (1.0 + nrm(ks[20], (DEPTH, 2, CMP_BLOCK), 0.1)) / CMP_BLOCK,
        'w_out': nrm(ks[21], (DEPTH, D_MIX, D_MODEL), DEEPNORM_BETA * D_MIX ** -0.5),
        'ln1_g': 1.0 + nrm(ks[22], (DEPTH, D_MODEL), 0.01),
        'ln1_b': nrm(ks[23], (DEPTH, D_MODEL), 0.01),
        'peer_wq': nrm(ks[24], (DEPTH, D_MODEL, P_HEADS * P_DKEY), D_MODEL ** -0.5),
        'peer_subkeys': nrm(ks[25], (DEPTH, P_HEADS, 2, N_KEYS, P_DKEY // 2), (P_DKEY // 2) ** -0.5),
        'peer_u': nrm(ks[26], (DEPTH, N_EXPERTS, D_MODEL), D_MODEL ** -0.5),
        'peer_v': nrm(ks[27], (DEPTH, N_EXPERTS, D_MODEL), DEEPNORM_BETA * P_HEADS ** -0.5),
        'ln2_g': 1.0 + nrm(ks[28], (DEPTH, D_MODEL), 0.01),
        'ln2_b': nrm(ks[29], (DEPTH, D_MODEL), 0.01),
    }


def reference(x_prompt, x_sample, cache_cmp_kv, cache_sel_kv, page_table, state_win_kv, state_rwkv,
              state_shift, w_in, rwkv_mu, rwkv_w0, rwkv_w2, rwkv_a0, rwkv_a2, rwkv_g2, rwkv_k_k,
              rwkv_k_a, rwkv_r_k, rwkv_gn_g, rwkv_gn_b, nsa_cmp_w, w_out, ln1_g, ln1_b, peer_wq,
              peer_subkeys, peer_u, peer_v, ln2_g, ln2_b):
    bp = x_prompt.shape[0]
    bs = x_sample.shape[0]
    dt = x_prompt.dtype
    n_pages = PAST_LEN // PAGE_SIZE
    past_len = n_pages * PAGE_SIZE
    n_keep = state_win_kv.shape[2]
    empty_kv = jnp.zeros((bp, 0, 2, NSA_KV_HEADS, HEAD_DIM), dt)
    zero_shift = jnp.zeros((bp, 1, RWKV_COLS), dt)
    zero_state = jnp.zeros((bp, H_RWKV, HEAD_DIM, HEAD_DIM), dt)
    yp, ys = x_prompt, x_sample
    st_p, st_s = [], []
    for l in range(DEPTH):
        p = {'w_in': w_in[l], 'rwkv_mu': rwkv_mu[l], 'rwkv_w0': rwkv_w0[l], 'rwkv_w2': rwkv_w2[l],
             'rwkv_a0': rwkv_a0[l], 'rwkv_a2': rwkv_a2[l], 'rwkv_g2': rwkv_g2[l],
             'rwkv_k_k': rwkv_k_k[l], 'rwkv_k_a': rwkv_k_a[l], 'rwkv_r_k': rwkv_r_k[l],
             'rwkv_gn_g': rwkv_gn_g[l], 'rwkv_gn_b': rwkv_gn_b[l], 'nsa_cmp_w': nsa_cmp_w[l],
             'w_out': w_out[l], 'ln1_g': ln1_g[l], 'ln1_b': ln1_b[l], 'peer_wq': peer_wq[l],
             'peer_subkeys': peer_subkeys[l], 'peer_u': peer_u[l], 'peer_v': peer_v[l],
             'ln2_g': ln2_g[l], 'ln2_b': ln2_b[l]}
        yp, sp = hybrid_layer(yp, 0, empty_kv, empty_kv, empty_kv, zero_shift, zero_state, n_keep, p)
        past_cmp = cache_cmp_kv[l, page_table].reshape(bs, past_len, 2, NSA_KV_HEADS, HEAD_DIM)
        past_sel = cache_sel_kv[l, page_table].reshape(bs, past_len, 2, NSA_KV_HEADS, HEAD_DIM)
        ys, ss = hybrid_layer(ys, past_len, past_cmp, past_sel, state_win_kv[l], state_shift[l],
                              state_rwkv[l], n_keep, p)
        st_p.append(sp)
        st_s.append(ss)
    stk = lambda sts, i: jnp.stack([s[i] for s in sts], axis=0)
    return (yp, ys, stk(st_p, 0), stk(st_p, 1), stk(st_p, 2), stk(st_p, 3), stk(st_p, 4),
            stk(st_s, 0), stk(st_s, 1), stk(st_s, 2), stk(st_s, 3), stk(st_s, 4))
```

```python
import functools
import math

import jax
import jax.numpy as jnp
from jax import lax
from jax.experimental import pallas as pl
from jax.experimental.pallas import tpu as pltpu

D_MODEL = 2048
DEPTH = 2
PAST_LEN = 16384
PAGE_SIZE = 128
HEAD_DIM = 64
D_RWKV = D_MODEL // 2
D_NSA = D_MODEL - D_RWKV
H_RWKV = D_RWKV // HEAD_DIM
H_NSA = D_NSA // HEAD_DIM
NSA_KV_HEADS = 4
NSA_GROUP = H_NSA // NSA_KV_HEADS
NSA_KV_COLS = NSA_KV_HEADS * HEAD_DIM
CMP_BLOCK = 32
SEL_BLOCK = 64
SEL_TOPN = 16
WINDOW = 512
Q_BLOCK = 128
W_LORA = 64
A_LORA = 64
G_LORA = 160
RWKV_COLS = 3 * D_RWKV + W_LORA + A_LORA + G_LORA
NSA_COLS = D_NSA + 6 * NSA_KV_COLS + 3 * H_NSA
IN_COLS = RWKV_COLS + NSA_COLS
P_HEADS = 8
N_KEYS = 128
P_DKEY = 256
P_TOPK = 16
P_TOK_BLOCK = 128
LN_EPS = 1e-5
GN_EPS = 64e-5
DEEPNORM_ALPHA = (2 * DEPTH) ** 0.25
FORCE_SCORE = 1e4
NEG_INF = -1e30

LANE = 128


def _round_up(x, m):
    return -(-x // m) * m


def _mm_kernel(x_ref, w_ref, o_ref):
    o_ref[...] = jnp.dot(x_ref[...].astype(jnp.bfloat16), w_ref[...].astype(jnp.bfloat16),
                         preferred_element_type=jnp.float32)


def _matmul(x, w, *, tm=512, tn=512):
    m, k = x.shape
    n = w.shape[1]
    tm = min(tm, m)
    n_pad = _round_up(n, tn)
    if n_pad != n:
        w = jnp.pad(w, ((0, 0), (0, n_pad - n)))
    out = pl.pallas_call(
        _mm_kernel,
        grid=(m // tm, n_pad // tn),
        in_specs=[pl.BlockSpec((tm, k), lambda i, j: (i, 0)),
                  pl.BlockSpec((k, tn), lambda i, j: (0, j))],
        out_specs=pl.BlockSpec((tm, tn), lambda i, j: (i, j)),
        out_shape=jax.ShapeDtypeStruct((m, n_pad), jnp.float32),
        compiler_params=pltpu.CompilerParams(dimension_semantics=("arbitrary", "arbitrary"),
                                             vmem_limit_bytes=48 * 1024 * 1024),
    )(x, w)
    return out[:, :n] if n_pad != n else out


def _layer_norm(x, g, b):
    mu = x.mean(-1, keepdims=True)
    var = jnp.mean(jnp.square(x - mu), -1, keepdims=True)
    return (x - mu) * lax.rsqrt(var + LN_EPS) * g + b


def _masked_softmax(s, mask):
    s = jnp.where(mask, s.astype(jnp.float32), NEG_INF)
    p = jax.nn.softmax(s, axis=-1)
    return jnp.where(mask, p, 0.0)


def _rwkv_time_mix(xr, shift_prev, s0, p):
    B, T, _ = xr.shape
    dt = xr.dtype
    prev = jnp.concatenate([shift_prev.astype(dt), xr[:, :-1]], axis=1)
    xs = xr + p['rwkv_mu'] * (prev - xr)
    cuts = [D_RWKV, 2 * D_RWKV, 3 * D_RWKV, 3 * D_RWKV + W_LORA, 3 * D_RWKV + W_LORA + A_LORA]
    r, k, v, wl, al, gl = jnp.split(xs, cuts, axis=-1)
    w = -jax.nn.softplus(-(p['rwkv_w0'] + jnp.tanh(wl) @ p['rwkv_w2'])) - 0.5
    decay = jnp.exp(-jnp.exp(w))
    a = jax.nn.sigmoid(p['rwkv_a0'] + al @ p['rwkv_a2'])
    g = jax.nn.sigmoid(gl) @ p['rwkv_g2']
    hd = lambda t: t.reshape(B, T, H_RWKV, HEAD_DIM)
    kkf = hd(k * p['rwkv_k_k']).astype(jnp.float32)
    kk = (kkf * lax.rsqrt(jnp.maximum(jnp.sum(kkf * kkf, -1, keepdims=True), 1e-24))).astype(dt)
    k = k * (1.0 + (a - 1.0) * p['rwkv_k_a'])
    r, decay, k, v, a = hd(r), hd(decay), hd(k), hd(v), hd(a)

    def step(S, inp):
        r_t, d_t, k_t, v_t, kk_t, a_t = inp
        sk = jnp.einsum('bhvk,bhk->bhv', S, kk_t)
        S = (S * d_t[:, :, None, :] - sk[..., None] * (kk_t * a_t)[:, :, None, :]
             + v_t[..., None] * k_t[:, :, None, :])
        return S, jnp.einsum('bhvk,bhk->bhv', S, r_t)

    seq_first = lambda t: jnp.moveaxis(t, 1, 0)
    s_T, y = lax.scan(step, s0.astype(dt), (seq_first(r), seq_first(decay), seq_first(k),
                                            seq_first(v), seq_first(kk), seq_first(a)))
    y = jnp.moveaxis(y, 0, 1).astype(jnp.float32)
    mu = y.mean(-1, keepdims=True)
    var = jnp.mean(jnp.square(y - mu), -1, keepdims=True)
    yn = ((y - mu) * lax.rsqrt(var + GN_EPS)).reshape(B, T, D_RWKV) * p['rwkv_gn_g'] + p['rwkv_gn_b']
    bonus = jnp.sum(r * k * p['rwkv_r_k'], -1, keepdims=True) * v
    out = (yn.astype(dt) + bonus.reshape(B, T, D_RWKV)) * g
    return out, s_T, xr[:, -1:]


def _nsa_mix(xn, past_cmp, past_sel, win_prefix, pos0, n_keep, cmp_w):
    B, T, _ = xn.shape
    dt = xn.dtype
    cuts = [D_NSA + i * NSA_KV_COLS for i in range(7)]
    q, kc, vc, ksl, vsl, kw, vw, gl = jnp.split(xn, cuts, axis=-1)
    kvh = lambda t: t.reshape(B, T, NSA_KV_HEADS, HEAD_DIM)
    new_cmp = jnp.stack([kvh(kc), kvh(vc)], axis=2)
    new_sel = jnp.stack([kvh(ksl), kvh(vsl)], axis=2)
    new_win = jnp.stack([kvh(kw), kvh(vw)], axis=2)
    L = past_cmp.shape[1] + T
    L_pad = -(-L // SEL_BLOCK) * SEL_BLOCK
    pad = lambda t: jnp.pad(t, ((0, 0), (0, L_pad - L), (0, 0), (0, 0), (0, 0)))
    full_cmp = pad(jnp.concatenate([past_cmp.astype(dt), new_cmp], axis=1))
    full_sel = pad(jnp.concatenate([past_sel.astype(dt), new_sel], axis=1))
    n_cmp = L_pad // CMP_BLOCK
    n_sel = L_pad // SEL_BLOCK
    cblk = full_cmp.reshape(B, n_cmp, CMP_BLOCK, 2, NSA_KV_HEADS, HEAD_DIM)
    kcb = jnp.einsum('bnlkd,l->bnkd', cblk[:, :, :, 0], cmp_w[0])
    vcb = jnp.einsum('bnlkd,l->bnkd', cblk[:, :, :, 1], cmp_w[1])
    sblk = full_sel.reshape(B, n_sel, SEL_BLOCK, 2, NSA_KV_HEADS, HEAD_DIM)
    ksb = jnp.transpose(sblk[:, :, :, 0], (0, 3, 1, 2, 4))
    vsb = jnp.transpose(sblk[:, :, :, 1], (0, 3, 1, 2, 4))
    n_prefix = win_prefix.shape[1]
    win_pad = jnp.concatenate([jnp.zeros((B, WINDOW, 2, NSA_KV_HEADS, HEAD_DIM), dt),
                               win_prefix.astype(dt), new_win], axis=1)
    q = q.reshape(B, T, NSA_KV_HEADS, NSA_GROUP, HEAD_DIM)
    gates = jax.nn.sigmoid(gl.reshape(B, T, NSA_KV_HEADS, NSA_GROUP, 3))
    qc = math.gcd(T, Q_BLOCK)
    n_chunks = T // qc
    top_n = min(SEL_TOPN, n_sel)
    scale = HEAD_DIM ** -0.5
    bi = jnp.arange(B)[:, None, None, None]
    hi = jnp.arange(NSA_KV_HEADS)[None, None, :, None]
    cmp_end = (jnp.arange(n_cmp) + 1) * CMP_BLOCK - 1
    blk = jnp.arange(n_sel)
    lw = WINDOW - 1 + qc
    jw = jnp.arange(lw)
    iw = jnp.arange(qc)

    def chunk(args):
        c, q_c, g_c = args
        qpos = pos0 + c * qc + jnp.arange(qc)
        q_c = q_c * scale
        s = jnp.einsum('bqkgd,bnkd->bqkgn', q_c, kcb)
        p_c = _masked_softmax(s, (cmp_end[None, :] <= qpos[:, None])[None, :, None, None, :])
        o_cmp = jnp.einsum('bqkgn,bnkd->bqkgd', p_c.astype(dt), vcb)
        imp = p_c.sum(3).reshape(B, qc, NSA_KV_HEADS, n_sel, SEL_BLOCK // CMP_BLOCK).sum(-1)
        cur = (qpos // SEL_BLOCK)[None, :, None, None]
        forced = (blk == 0) | (blk == cur) | (blk == cur - 1)
        imp = jnp.where(blk > cur, -1.0, jnp.where(forced, FORCE_SCORE, imp))
        top_val, top_idx = lax.top_k(imp, top_n)
        k_g = ksb[bi, hi, top_idx]
        v_g = vsb[bi, hi, top_idx]
        s = jnp.einsum('bqkgd,bqknld->bqkgnl', q_c, k_g)
        kpos = top_idx[..., None] * SEL_BLOCK + jnp.arange(SEL_BLOCK)
        m = (top_val >= 0)[..., None] & (kpos <= qpos[None, :, None, None, None])
        p_s = _masked_softmax(s.reshape(B, qc, NSA_KV_HEADS, NSA_GROUP, top_n * SEL_BLOCK),
                              m.reshape(B, qc, NSA_KV_HEADS, 1, top_n * SEL_BLOCK))
        o_sel = jnp.einsum('bqkgnl,bqknld->bqkgd',
                           p_s.reshape(B, qc, NSA_KV_HEADS, NSA_GROUP, top_n, SEL_BLOCK).astype(dt), v_g)
        start = n_prefix + c * qc + 1
        wkv = lax.dynamic_slice_in_dim(win_pad, start, lw, axis=1)
        s = jnp.einsum('bqkgd,bjkd->bqkgj', q_c, wkv[:, :, 0])
        m = ((jw[None, :] >= iw[:, None]) & (jw[None, :] <= iw[:, None] + WINDOW - 1)
             & (start + jw[None, :] >= WINDOW))
        p_w = _masked_softmax(s, m[None, :, None, None, :])
        o_win = jnp.einsum('bqkgj,bjkd->bqkgd', p_w.astype(dt), wkv[:, :, 1])
        return g_c[..., 0:1] * o_cmp + g_c[..., 1:2] * o_sel + g_c[..., 2:3] * o_win

    chunks = lambda t: jnp.moveaxis(t.reshape(B, n_chunks, qc, *t.shape[2:]), 1, 0)
    out = lax.map(chunk, (jnp.arange(n_chunks), chunks(q), chunks(gates)))
    out = jnp.moveaxis(out, 0, 1).reshape(B, T, D_NSA)
    return out, new_cmp, new_sel, win_pad[:, -n_keep:]


def _peer_ffn(x, wq, subkeys, u_tab, v_tab):
    B, T, D = x.shape
    n = B * T
    xt = x.reshape(n, D)
    q = _matmul(xt, wq).reshape(n, P_HEADS, 2, P_DKEY // 2)
    s = jnp.einsum('nhcd,hckd->nhck', q, subkeys).astype(jnp.float32)
    hv, hix = lax.top_k(s, P_TOPK)
    cand = (hv[:, :, 0, :, None] + hv[:, :, 1, None, :]).reshape(n, P_HEADS, P_TOPK * P_TOPK)
    cand_e = (hix[:, :, 0, :, None] * N_KEYS + hix[:, :, 1, None, :]).reshape(n, P_HEADS, P_TOPK * P_TOPK)
    tv, tpos = lax.top_k(cand, P_TOPK)
    expert = jnp.take_along_axis(cand_e, tpos, axis=-1)
    gate = jax.nn.softmax(tv, axis=-1).astype(x.dtype)
    cb = math.gcd(n, P_TOK_BLOCK)

    def block(args):
        xb, eb, gb = args
        h = jax.nn.gelu(jnp.einsum('cd,chkd->chk', xb, u_tab[eb]), approximate=False)
        return jnp.einsum('chk,chkd->cd', gb * h, v_tab[eb])

    out = lax.map(block, (xt.reshape(n // cb, cb, D), expert.reshape(n // cb, cb, P_HEADS, P_TOPK),
                          gate.reshape(n // cb, cb, P_HEADS, P_TOPK)))
    return out.reshape(B, T, D)


def _hybrid_layer(x, pos0, past_cmp, past_sel, win_prefix, shift_prev, rwkv_s0, n_keep, p):
    B, T, D = x.shape
    proj = _matmul(x.reshape(B * T, D), p['w_in']).reshape(B, T, IN_COLS)
    y_r, s_T, new_shift = _rwkv_time_mix(proj[..., :RWKV_COLS], shift_prev, rwkv_s0, p)
    y_n, new_cmp, new_sel, new_win = _nsa_mix(proj[..., RWKV_COLS:], past_cmp, past_sel, win_prefix,
                                              pos0, n_keep, p['nsa_cmp_w'])
    h = _matmul(jnp.concatenate([y_r, y_n], axis=-1).reshape(B * T, D), p['w_out']).reshape(B, T, D)
    x = _layer_norm(DEEPNORM_ALPHA * x + h, p['ln1_g'], p['ln1_b'])
    f = _peer_ffn(x, p['peer_wq'], p['peer_subkeys'], p['peer_u'], p['peer_v'])
    x = _layer_norm(DEEPNORM_ALPHA * x + f, p['ln2_g'], p['ln2_b'])
    return x, (new_cmp, new_sel, new_win, s_T, new_shift)


def kernel(x_prompt, x_sample, cache_cmp_kv, cache_sel_kv, page_table, state_win_kv, state_rwkv,
           state_shift, w_in, rwkv_mu, rwkv_w0, rwkv_w2, rwkv_a0, rwkv_a2, rwkv_g2, rwkv_k_k,
           rwkv_k_a, rwkv_r_k, rwkv_gn_g, rwkv_gn_b, nsa_cmp_w, w_out, ln1_g, ln1_b, peer_wq,
           peer_subkeys, peer_u, peer_v, ln2_g, ln2_b):
    bp = x_prompt.shape[0]
    bs = x_sample.shape[0]
    dt = x_prompt.dtype
    n_pages = PAST_LEN // PAGE_SIZE
    past_len = n_pages * PAGE_SIZE
    n_keep = state_win_kv.shape[2]
    empty_kv = jnp.zeros((bp, 0, 2, NSA_KV_HEADS, HEAD_DIM), dt)
    zero_shift = jnp.zeros((bp, 1, RWKV_COLS), dt)
    zero_state = jnp.zeros((bp, H_RWKV, HEAD_DIM, HEAD_DIM), dt)
    yp, ys = x_prompt, x_sample
    st_p, st_s = [], []
    for l in range(DEPTH):
        p = {'w_in': w_in[l], 'rwkv_mu': rwkv_mu[l], 'rwkv_w0': rwkv_w0[l], 'rwkv_w2': rwkv_w2[l],
             'rwkv_a0': rwkv_a0[l], 'rwkv_a2': rwkv_a2[l], 'rwkv_g2': rwkv_g2[l],
             'rwkv_k_k': rwkv_k_k[l], 'rwkv_k_a': rwkv_k_a[l], 'rwkv_r_k': rwkv_r_k[l],
             'rwkv_gn_g': rwkv_gn_g[l], 'rwkv_gn_b': rwkv_gn_b[l], 'nsa_cmp_w': nsa_cmp_w[l],
             'w_out': w_out[l], 'ln1_g': ln1_g[l], 'ln1_b': ln1_b[l], 'peer_wq': peer_wq[l],
             'peer_subkeys': peer_subkeys[l], 'peer_u': peer_u[l], 'peer_v': peer_v[l],
             'ln2_g': ln2_g[l], 'ln2_b': ln2_b[l]}
        yp, sp = _hybrid_layer(yp, 0, empty_kv, empty_kv, empty_kv, zero_shift, zero_state, n_keep, p)
        past_cmp = cache_cmp_kv[l, page_table].reshape(bs, past_len, 2, NSA_KV_HEADS, HEAD_DIM)
        past_sel = cache_sel_kv[l, page_table].reshape(bs, past_len, 2, NSA_KV_HEADS, HEAD_DIM)
        ys, ss = _hybrid_layer(ys, past_len, past_cmp, past_sel, state_win_kv[l], state_shift[l],
                               state_rwkv[l], n_keep, p)
        st_p.append(sp)
        st_s.append(ss)
    stk = lambda sts, i: jnp.stack([s[i] for s in sts], axis=0)
    return (yp, ys, stk(st_p, 0), stk(st_p, 1), stk(st_p, 2), stk(st_p, 3), stk(st_p, 4),
            stk(st_s, 0), stk(st_s, 1), stk(st_s, 2), stk(st_s, 3), stk(st_s, 4))
```

```python
import functools
import math

import jax
import jax.numpy as jnp
from jax import lax
from jax.experimental import pallas as pl
from jax.experimental.pallas import tpu as pltpu

D_MODEL = 2048
DEPTH = 2
PAST_LEN = 16384
PAGE_SIZE = 128
HEAD_DIM = 64
D_RWKV = D_MODEL // 2
D_NSA = D_MODEL - D_RWKV
H_RWKV = D_RWKV // HEAD_DIM
H_NSA = D_NSA // HEAD_DIM
NSA_KV_HEADS = 4
NSA_GROUP = H_NSA // NSA_KV_HEADS
NSA_KV_COLS = NSA_KV_HEADS * HEAD_DIM
CMP_BLOCK = 32
SEL_BLOCK = 64
SEL_TOPN = 16
WINDOW = 512
Q_BLOCK = 128
W_LORA = 64
A_LORA = 64
G_LORA = 160
RWKV_COLS = 3 * D_RWKV + W_LORA + A_LORA + G_LORA
NSA_COLS = D_NSA + 6 * NSA_KV_COLS + 3 * H_NSA
IN_COLS = RWKV_COLS + NSA_COLS
P_HEADS = 8
N_KEYS = 128
P_DKEY = 256
P_TOPK = 16
P_TOK_BLOCK = 128
LN_EPS = 1e-5
GN_EPS = 64e-5
DEEPNORM_ALPHA = (2 * DEPTH) ** 0.25
FORCE_SCORE = 1e4
NEG_INF = -1e30

N_EXPERTS = N_KEYS * N_KEYS

LANE = 128
_VMEM_LIMIT = 56 * 1024 * 1024


def _round_up(x, m):
    return -(-x // m) * m


def _mm_kernel(x_ref, w_ref, o_ref):
    o_ref[...] = jnp.dot(x_ref[...].astype(jnp.bfloat16), w_ref[...].astype(jnp.bfloat16),
                         preferred_element_type=jnp.float32)


def _matmul(x, w, *, tm=512, tn=512):
    m, k = x.shape
    n = w.shape[1]
    tm = min(tm, m)
    n_pad = _round_up(n, tn)
    if n_pad != n:
        w = jnp.pad(w, ((0, 0), (0, n_pad - n)))
    out = pl.pallas_call(
        _mm_kernel,
        grid=(m // tm, n_pad // tn),
        in_specs=[pl.BlockSpec((tm, k), lambda i, j: (i, 0)),
                  pl.BlockSpec((k, tn), lambda i, j: (0, j))],
        out_specs=pl.BlockSpec((tm, tn), lambda i, j: (i, j)),
        out_shape=jax.ShapeDtypeStruct((m, n_pad), jnp.float32),
        compiler_params=pltpu.CompilerParams(dimension_semantics=("arbitrary", "arbitrary"),
                                             vmem_limit_bytes=48 * 1024 * 1024),
    )(x, w)
    return out[:, :n] if n_pad != n else out


def _layer_norm(x, g, b):
    mu = x.mean(-1, keepdims=True)
    var = jnp.mean(jnp.square(x - mu), -1, keepdims=True)
    return (x - mu) * lax.rsqrt(var + LN_EPS) * g + b


def _masked_softmax(s, mask):
    s = jnp.where(mask, s.astype(jnp.float32), NEG_INF)
    p = jax.nn.softmax(s, axis=-1)
    return jnp.where(mask, p, 0.0)


def _rwkv_time_mix(xr, shift_prev, s0, p):
    B, T, _ = xr.shape
    dt = xr.dtype
    prev = jnp.concatenate([shift_prev.astype(dt), xr[:, :-1]], axis=1)
    xs = xr + p['rwkv_mu'] * (prev - xr)
    cuts = [D_RWKV, 2 * D_RWKV, 3 * D_RWKV, 3 * D_RWKV + W_LORA, 3 * D_RWKV + W_LORA + A_LORA]
    r, k, v, wl, al, gl = jnp.split(xs, cuts, axis=-1)
    w = -jax.nn.softplus(-(p['rwkv_w0'] + jnp.tanh(wl) @ p['rwkv_w2'])) - 0.5
    decay = jnp.exp(-jnp.exp(w))
    a = jax.nn.sigmoid(p['rwkv_a0'] + al @ p['rwkv_a2'])
    g = jax.nn.sigmoid(gl) @ p['rwkv_g2']
    hd = lambda t: t.reshape(B, T, H_RWKV, HEAD_DIM)
    kkf = hd(k * p['rwkv_k_k']).astype(jnp.float32)
    kk = (kkf * lax.rsqrt(jnp.maximum(jnp.sum(kkf * kkf, -1, keepdims=True), 1e-24))).astype(dt)
    k = k * (1.0 + (a - 1.0) * p['rwkv_k_a'])
    r, decay, k, v, a = hd(r), hd(decay), hd(k), hd(v), hd(a)

    def step(S, inp):
        r_t, d_t, k_t, v_t, kk_t, a_t = inp
        sk = jnp.einsum('bhvk,bhk->bhv', S, kk_t)
        S = (S * d_t[:, :, None, :] - sk[..., None] * (kk_t * a_t)[:, :, None, :]
             + v_t[..., None] * k_t[:, :, None, :])
        return S, jnp.einsum('bhvk,bhk->bhv', S, r_t)

    seq_first = lambda t: jnp.moveaxis(t, 1, 0)
    s_T, y = lax.scan(step, s0.astype(dt), (seq_first(r), seq_first(decay), seq_first(k),
                                            seq_first(v), seq_first(kk), seq_first(a)))
    y = jnp.moveaxis(y, 0, 1).astype(jnp.float32)
    mu = y.mean(-1, keepdims=True)
    var = jnp.mean(jnp.square(y - mu), -1, keepdims=True)
    yn = ((y - mu) * lax.rsqrt(var + GN_EPS)).reshape(B, T, D_RWKV) * p['rwkv_gn_g'] + p['rwkv_gn_b']
    bonus = jnp.sum(r * k * p['rwkv_r_k'], -1, keepdims=True) * v
    out = (yn.astype(dt) + bonus.reshape(B, T, D_RWKV)) * g
    return out, s_T, xr[:, -1:]


def _nsa_mix(xn, past_cmp, past_sel, win_prefix, pos0, n_keep, cmp_w):
    B, T, _ = xn.shape
    dt = xn.dtype
    cuts = [D_NSA + i * NSA_KV_COLS for i in range(7)]
    q, kc, vc, ksl, vsl, kw, vw, gl = jnp.split(xn, cuts, axis=-1)
    kvh = lambda t: t.reshape(B, T, NSA_KV_HEADS, HEAD_DIM)
    new_cmp = jnp.stack([kvh(kc), kvh(vc)], axis=2)
    new_sel = jnp.stack([kvh(ksl), kvh(vsl)], axis=2)
    new_win = jnp.stack([kvh(kw), kvh(vw)], axis=2)
    L = past_cmp.shape[1] + T
    L_pad = -(-L // SEL_BLOCK) * SEL_BLOCK
    pad = lambda t: jnp.pad(t, ((0, 0), (0, L_pad - L), (0, 0), (0, 0), (0, 0)))
    full_cmp = pad(jnp.concatenate([past_cmp.astype(dt), new_cmp], axis=1))
    full_sel = pad(jnp.concatenate([past_sel.astype(dt), new_sel], axis=1))
    n_cmp = L_pad // CMP_BLOCK
    n_sel = L_pad // SEL_BLOCK
    cblk = full_cmp.reshape(B, n_cmp, CMP_BLOCK, 2, NSA_KV_HEADS, HEAD_DIM)
    kcb = jnp.einsum('bnlkd,l->bnkd', cblk[:, :, :, 0], cmp_w[0])
    vcb = jnp.einsum('bnlkd,l->bnkd', cblk[:, :, :, 1], cmp_w[1])
    sblk = full_sel.reshape(B, n_sel, SEL_BLOCK, 2, NSA_KV_HEADS, HEAD_DIM)
    ksb = jnp.transpose(sblk[:, :, :, 0], (0, 3, 1, 2, 4))
    vsb = jnp.transpose(sblk[:, :, :, 1], (0, 3, 1, 2, 4))
    n_prefix = win_prefix.shape[1]
    win_pad = jnp.concatenate([jnp.zeros((B, WINDOW, 2, NSA_KV_HEADS, HEAD_DIM), dt),
                               win_prefix.astype(dt), new_win], axis=1)
    q = q.reshape(B, T, NSA_KV_HEADS, NSA_GROUP, HEAD_DIM)
    gates = jax.nn.sigmoid(gl.reshape(B, T, NSA_KV_HEADS, NSA_GROUP, 3))
    qc = math.gcd(T, Q_BLOCK)
    n_chunks = T // qc
    top_n = min(SEL_TOPN, n_sel)
    scale = HEAD_DIM ** -0.5
    bi = jnp.arange(B)[:, None, None, None]
    hi = jnp.arange(NSA_KV_HEADS)[None, None, :, None]
    cmp_end = (jnp.arange(n_cmp) + 1) * CMP_BLOCK - 1
    blk = jnp.arange(n_sel)
    lw = WINDOW - 1 + qc
    jw = jnp.arange(lw)
    iw = jnp.arange(qc)

    def chunk(args):
        c, q_c, g_c = args
        qpos = pos0 + c * qc + jnp.arange(qc)
        q_c = q_c * scale
        s = jnp.einsum('bqkgd,bnkd->bqkgn', q_c, kcb)
        p_c = _masked_softmax(s, (cmp_end[None, :] <= qpos[:, None])[None, :, None, None, :])
        o_cmp = jnp.einsum('bqkgn,bnkd->bqkgd', p_c.astype(dt), vcb)
        imp = p_c.sum(3).reshape(B, qc, NSA_KV_HEADS, n_sel, SEL_BLOCK // CMP_BLOCK).sum(-1)
        cur = (qpos // SEL_BLOCK)[None, :, None, None]
        forced = (blk == 0) | (blk == cur) | (blk == cur - 1)
        imp = jnp.where(blk > cur, -1.0, jnp.where(forced, FORCE_SCORE, imp))
        top_val, top_idx = lax.top_k(imp, top_n)
        k_g = ksb[bi, hi, top_idx]
        v_g = vsb[bi, hi, top_idx]
        s = jnp.einsum('bqkgd,bqknld->bqkgnl', q_c, k_g)
        kpos = top_idx[..., None] * SEL_BLOCK + jnp.arange(SEL_BLOCK)
        m = (top_val >= 0)[..., None] & (kpos <= qpos[None, :, None, None, None])
        p_s = _masked_softmax(s.reshape(B, qc, NSA_KV_HEADS, NSA_GROUP, top_n * SEL_BLOCK),
                              m.reshape(B, qc, NSA_KV_HEADS, 1, top_n * SEL_BLOCK))
        o_sel = jnp.einsum('bqkgnl,bqknld->bqkgd',
                           p_s.reshape(B, qc, NSA_KV_HEADS, NSA_GROUP, top_n, SEL_BLOCK).astype(dt), v_g)
        start = n_prefix + c * qc + 1
        wkv = lax.dynamic_slice_in_dim(win_pad, start, lw, axis=1)
        s = jnp.einsum('bqkgd,bjkd->bqkgj', q_c, wkv[:, :, 0])
        m = ((jw[None, :] >= iw[:, None]) & (jw[None, :] <= iw[:, None] + WINDOW - 1)
             & (start + jw[None, :] >= WINDOW))
        p_w = _masked_softmax(s, m[None, :, None, None, :])
        o_win = jnp.einsum('bqkgj,bjkd->bqkgd', p_w.astype(dt), wkv[:, :, 1])
        return g_c[..., 0:1] * o_cmp + g_c[..., 1:2] * o_sel + g_c[..., 2:3] * o_win

    chunks = lambda t: jnp.moveaxis(t.reshape(B, n_chunks, qc, *t.shape[2:]), 1, 0)
    out = lax.map(chunk, (jnp.arange(n_chunks), chunks(q), chunks(gates)))
    out = jnp.moveaxis(out, 0, 1).reshape(B, T, D_NSA)
    return out, new_cmp, new_sel, win_pad[:, -n_keep:]


def _extract_top(buf_ref, out_ref, nrows, width):
    iota = lax.broadcasted_iota(jnp.int32, (nrows, width), 0)

    def body(r, carry):
        s = buf_ref[...]
        mx = jnp.max(s, axis=0, keepdims=True)
        first = jnp.min(jnp.where(s == mx, iota, nrows), axis=0, keepdims=True)
        buf_ref[...] = jnp.where(iota == first, -jnp.inf, s)
        out_ref[pl.ds(r, 1), :] = mx
        return carry

    lax.fori_loop(0, P_TOPK, body, 0)


def _peer_score_kernel(x_ref, wq_ref, sk_ref, s1_ref, s2_ref, st_ref, buf, cand, hv0, hv1, tv):
    tn = x_ref.shape[0]
    q = jnp.dot(x_ref[...].astype(jnp.bfloat16), wq_ref[...],
                preferred_element_type=jnp.float32).astype(jnp.bfloat16)
    for h in range(P_HEADS):
        for c, (s_ref, hv) in enumerate(((s1_ref, hv0), (s2_ref, hv1))):
            col = (2 * h + c) * N_KEYS
            s_t = lax.dot_general(sk_ref[2 * h + c], q[:, col:col + N_KEYS],
                                  (((1,), (1,)), ((), ())), preferred_element_type=jnp.float32)
            s_ref[h] = s_t
            buf[...] = s_t
            _extract_top(buf, hv, N_KEYS, tn)
        for a in range(P_TOPK):
            cand[a * P_TOPK:(a + 1) * P_TOPK, :] = hv0[a:a + 1, :] + hv1[...]
        _extract_top(cand, tv, P_TOPK * P_TOPK, tn)
        z = jnp.sum(jnp.exp(tv[...] - tv[0:1, :]), axis=0, keepdims=True)
        st_ref[0, h:h + 1, :] = tv[P_TOPK - 1:P_TOPK, :]
        st_ref[1, h:h + 1, :] = hv0[0:1, :]
        st_ref[2, h:h + 1, :] = hv1[0:1, :]
        st_ref[3, h:h + 1, :] = 1.0 / z


def _peer_scores(x, wq_bf, sk_bf, *, tn=256):
    n = x.shape[0]
    tn = min(tn, n)
    return pl.pallas_call(
        _peer_score_kernel,
        grid=(n // tn,),
        in_specs=[pl.BlockSpec((tn, D_MODEL), lambda i: (i, 0)),
                  pl.BlockSpec((D_MODEL, P_HEADS * P_DKEY), lambda i: (0, 0)),
                  pl.BlockSpec((2 * P_HEADS, N_KEYS, P_DKEY // 2), lambda i: (0, 0, 0))],
        out_specs=[pl.BlockSpec((P_HEADS, N_KEYS, tn), lambda i: (0, 0, i)),
                   pl.BlockSpec((P_HEADS, N_KEYS, tn), lambda i: (0, 0, i)),
                   pl.BlockSpec((4, P_HEADS, tn), lambda i: (0, 0, i))],
        out_shape=[jax.ShapeDtypeStruct((P_HEADS, N_KEYS, n), jnp.float32),
                   jax.ShapeDtypeStruct((P_HEADS, N_KEYS, n), jnp.float32),
                   jax.ShapeDtypeStruct((4, P_HEADS, n), jnp.float32)],
        scratch_shapes=[pltpu.VMEM((N_KEYS, tn), jnp.float32),
                        pltpu.VMEM((P_TOPK * P_TOPK, tn), jnp.float32),
                        pltpu.VMEM((P_TOPK, tn), jnp.float32),
                        pltpu.VMEM((P_TOPK, tn), jnp.float32),
                        pltpu.VMEM((P_TOPK, tn), jnp.float32)],
        compiler_params=pltpu.CompilerParams(dimension_semantics=("arbitrary",),
                                             vmem_limit_bytes=_VMEM_LIMIT),
        name="peer_scores",
    )(x, wq_bf, sk_bf)


def _peer_expert_kernel(x_ref, u_ref, v_ref, s1_ref, s2_ref, st_ref, o_ref, e2_scr, h_scr, p_scr,
                        *, chunk):
    j = pl.program_id(1)
    tn = x_ref.shape[0]
    te = u_ref.shape[0]

    @pl.when(j == 0)
    def _():
        o_ref[...] = jnp.zeros_like(o_ref)
        for h in range(P_HEADS):
            e2_scr[h] = jnp.exp(s2_ref[h] - st_ref[2, h:h + 1, :])

    h_scr[...] = lax.dot_general(x_ref[...], u_ref[...], (((1,), (1,)), ((), ())),
                                 preferred_element_type=jnp.float32)

    def chunk_body(c, carry):
        tsl = pl.ds(pl.multiple_of(c * chunk, chunk), chunk)
        for ii in range(te // N_KEYS):
            i = j * (te // N_KEYS) + ii
            w_t = jnp.zeros((N_KEYS, chunk), jnp.float32)
            for h in range(P_HEADS):
                a = s1_ref[h, pl.ds(i, 1), tsl]
                e1 = jnp.exp(a - st_ref[1, h:h + 1, tsl]) * st_ref[3, h:h + 1, tsl]
                val = a + s2_ref[h, :, tsl]
                w_t = w_t + jnp.where(val >= st_ref[0, h:h + 1, tsl], e1 * e2_scr[h, :, tsl], 0.0)
            hh = h_scr[tsl, ii * N_KEYS:(ii + 1) * N_KEYS]
            g = 0.5 * hh * (1.0 + lax.erf(hh * 0.7071067811865476))
            p_scr[tsl, ii * N_KEYS:(ii + 1) * N_KEYS] = (w_t.T * g).astype(jnp.bfloat16)
        return carry

    lax.fori_loop(0, tn // chunk, chunk_body, 0)
    o_ref[...] += jnp.dot(p_scr[...], v_ref[...], preferred_element_type=jnp.float32)


def _peer_experts(x_bf, u_bf, v_bf, s1t, s2t, stats, *, tn=1024, te=512, chunk=256):
    n = x_bf.shape[0]
    tn = min(tn, n)
    chunk = min(chunk, tn)
    once = pl.Buffered(1)
    return pl.pallas_call(
        functools.partial(_peer_expert_kernel, chunk=chunk),
        grid=(n // tn, N_EXPERTS // te),
        in_specs=[pl.BlockSpec((tn, D_MODEL), lambda i, j: (i, 0), pipeline_mode=once),
                  pl.BlockSpec((te, D_MODEL), lambda i, j: (j, 0)),
                  pl.BlockSpec((te, D_MODEL), lambda i, j: (j, 0)),
                  pl.BlockSpec((P_HEADS, N_KEYS, tn), lambda i, j: (0, 0, i), pipeline_mode=once),
                  pl.BlockSpec((P_HEADS, N_KEYS, tn), lambda i, j: (0, 0, i), pipeline_mode=once),
                  pl.BlockSpec((4, P_HEADS, tn), lambda i, j: (0, 0, i), pipeline_mode=once)],
        out_specs=pl.BlockSpec((tn, D_MODEL), lambda i, j: (i, 0)),
        out_shape=jax.ShapeDtypeStruct((n, D_MODEL), jnp.float32),
        scratch_shapes=[pltpu.VMEM((P_HEADS, N_KEYS, tn), jnp.float32),
                        pltpu.VMEM((tn, te), jnp.float32),
                        pltpu.VMEM((tn, te), jnp.bfloat16)],
        compiler_params=pltpu.CompilerParams(dimension_semantics=("arbitrary", "arbitrary"),
                                             vmem_limit_bytes=_VMEM_LIMIT),
        name="peer_experts",
    )(x_bf, u_bf, v_bf, s1t, s2t, stats)


def _peer_ffn(x, wq_bf, sk_bf, u_bf, v_bf):
    B, T, D = x.shape
    n = B * T
    xt = x.reshape(n, D)
    n_pad = _round_up(n, LANE)
    if n_pad != n:
        xt = jnp.pad(xt, ((0, n_pad - n), (0, 0)))
    s1t, s2t, stats = _peer_scores(xt, wq_bf, sk_bf)
    out = _peer_experts(xt.astype(jnp.bfloat16), u_bf, v_bf, s1t, s2t, stats)
    return out[:n].reshape(B, T, D)


NSA_TQ = 256


def _flash_tile(q4, k_t, v_t, mask4, m, l, acc):
    s = lax.dot_general(q4, k_t, (((1,), (1,)), ((), ())), preferred_element_type=jnp.float32)
    s = jnp.where(mask4, s, NEG_INF)
    m_new = jnp.maximum(m, jnp.max(s, axis=-1, keepdims=True))
    p = jnp.where(mask4, jnp.exp(s - m_new), 0.0)
    alpha = jnp.exp(m - m_new)
    l_new = alpha * l + jnp.sum(p, axis=-1, keepdims=True)
    acc_new = alpha * acc + jnp.dot(p.astype(jnp.bfloat16), v_t, preferred_element_type=jnp.float32)
    return m_new, l_new, acc_new


def _nsa_prompt_kernel(q_ref, gl_ref, kc_ref, vc_ref, ks_ref, vs_ref, kw_ref, vw_ref, cw_ref, ex_ref,
                       o_ref, kcb_scr, vcb_scr, imp_scr, selx_scr, *, seq):
    qt = pl.program_id(2)
    tq = NSA_TQ
    n_cmp = seq // CMP_BLOCK
    n_sel = seq // SEL_BLOCK
    G = NSA_GROUP
    q0 = qt * tq

    @pl.when(qt == 0)
    def _():
        kc = kc_ref[...].reshape(n_cmp, CMP_BLOCK, HEAD_DIM)
        vc = vc_ref[...].reshape(n_cmp, CMP_BLOCK, HEAD_DIM)
        kcb_scr[...] = jnp.zeros_like(kcb_scr)
        vcb_scr[...] = jnp.zeros_like(vcb_scr)
        kcb_scr[0:n_cmp, :] = jnp.sum(kc * cw_ref[0][None], axis=1).astype(jnp.bfloat16)
        vcb_scr[0:n_cmp, :] = jnp.sum(vc * cw_ref[1][None], axis=1).astype(jnp.bfloat16)

    qpos_col = q0 + lax.broadcasted_iota(jnp.int32, (tq, 1), 0)
    qs = [(q_ref[g] * (HEAD_DIM ** -0.5)).astype(jnp.bfloat16) for g in range(G)]
    gates = jax.nn.sigmoid(gl_ref[...])

    cmp_end = (lax.broadcasted_iota(jnp.int32, (1, LANE), 1) + 1) * CMP_BLOCK - 1
    cmask = (cmp_end <= qpos_col) & (lax.broadcasted_iota(jnp.int32, (1, LANE), 1) < n_cmp)
    o_cmp = []
    imp = jnp.zeros((tq, LANE), jnp.float32)
    for g in range(G):
        s = lax.dot_general(qs[g], kcb_scr[...], (((1,), (1,)), ((), ())),
                            preferred_element_type=jnp.float32)
        s = jnp.where(cmask, s, NEG_INF)
        e = jnp.exp(s - jnp.max(s, axis=-1, keepdims=True))
        p = jnp.where(cmask, e / jnp.sum(e, axis=-1, keepdims=True), 0.0)
        o_cmp.append(jnp.dot(p.astype(jnp.bfloat16), vcb_scr[...], preferred_element_type=jnp.float32))
        imp = imp + p
    imp_t = imp.T
    ratio = SEL_BLOCK // CMP_BLOCK
    parts = []
    for c in range(tq // LANE):
        imp_scr[c] = imp_t[:, c * LANE:(c + 1) * LANE]
        part = imp_scr[c, pl.ds(0, n_sel, stride=ratio), :]
        for r in range(1, ratio):
            part = part + imp_scr[c, pl.ds(r, n_sel, stride=ratio), :]
        parts.append(part)
    imp_s = jnp.concatenate(parts, axis=1)
    qpos_row = q0 + lax.broadcasted_iota(jnp.int32, (1, tq), 1)
    cur = qpos_row // SEL_BLOCK
    blk = lax.broadcasted_iota(jnp.int32, (n_sel, tq), 0)
    forced = (blk == 0) | (blk == cur) | (blk == cur - 1)
    imp_s = jnp.where(blk > cur, -1.0, jnp.where(forced, FORCE_SCORE, imp_s))
    rank = jnp.zeros((n_sel, tq), jnp.int32)
    for mm in range(n_sel):
        row = imp_s[mm:mm + 1, :]
        ahead = (row > imp_s) | ((row == imp_s) & (mm < blk))
        rank = rank + jnp.where(ahead, 1, 0)
    sel = jnp.where((rank < min(SEL_TOPN, n_sel)) & (imp_s >= 0.0), 1.0, 0.0)
    sel_pad = jnp.concatenate([sel, jnp.zeros((LANE - n_sel, tq), jnp.float32)], axis=0)
    sel_q = sel_pad.T.astype(jnp.bfloat16)
    selx_scr[...] = jnp.dot(sel_q, ex_ref[...], preferred_element_type=jnp.float32)

    q4 = jnp.concatenate(qs, axis=0)
    qpos4 = jnp.concatenate([qpos_col] * G, axis=0)
    kpos_l = lax.broadcasted_iota(jnp.int32, (1, tq), 1)
    init = (jnp.full((G * tq, 1), NEG_INF, jnp.float32), jnp.zeros((G * tq, 1), jnp.float32),
            jnp.zeros((G * tq, HEAD_DIM), jnp.float32))

    def sel_body(kt, carry):
        k0 = pl.multiple_of(kt * tq, tq)
        chosen = selx_scr[:, pl.ds(k0, tq)] > 0.5
        chosen4 = jnp.concatenate([chosen] * G, axis=0)
        mask4 = chosen4 & ((k0 + kpos_l) <= qpos4)
        return _flash_tile(q4, ks_ref[pl.ds(k0, tq), :], vs_ref[pl.ds(k0, tq), :], mask4, *carry)

    _, l_s, acc_s = lax.fori_loop(0, qt + 1, sel_body, init)
    o_sel = acc_s / l_s

    def win_body(kt, carry):
        k0 = pl.multiple_of(kt * tq, tq)
        kpos = k0 + kpos_l
        mask4 = (kpos <= qpos4) & (kpos >= qpos4 - (WINDOW - 1))
        return _flash_tile(q4, kw_ref[pl.ds(k0, tq), :], vw_ref[pl.ds(k0, tq), :], mask4, *carry)

    _, l_w, acc_w = lax.fori_loop(jnp.maximum(qt - WINDOW // tq, 0), qt + 1, win_body, init)
    o_win = acc_w / l_w

    for g in range(G):
        rows = slice(g * tq, (g + 1) * tq)
        o_ref[g] = (gates[:, 3 * g:3 * g + 1] * o_cmp[g] + gates[:, 3 * g + 1:3 * g + 2] * o_sel[rows]
                    + gates[:, 3 * g + 2:3 * g + 3] * o_win[rows])


def _nsa_prompt(xn, cmp_w):
    B, T, _ = xn.shape
    KVH, G, HD = NSA_KV_HEADS, NSA_GROUP, HEAD_DIM
    kvc = NSA_KV_COLS
    tq = NSA_TQ
    q = xn[..., :D_NSA].reshape(B, T, KVH, G, HD).transpose(0, 2, 3, 1, 4)
    kv = [xn[..., D_NSA + i * kvc:D_NSA + (i + 1) * kvc].reshape(B, T, KVH, HD).transpose(0, 2, 1, 3)
          for i in range(6)]
    kc, vc = kv[0], kv[1]
    ks, vs, kw, vw = [t.astype(jnp.bfloat16) for t in kv[2:]]
    gl = xn[..., D_NSA + 6 * kvc:].reshape(B, T, KVH, 3 * G).transpose(0, 2, 1, 3)
    cw = jnp.broadcast_to(cmp_w[:, :, None], (2, CMP_BLOCK, HD))
    expand = (jnp.arange(LANE)[:, None] == (jnp.arange(T) // SEL_BLOCK)[None, :]).astype(jnp.bfloat16)
    kv_spec = pl.BlockSpec((None, None, T, HD), lambda b, k, t: (b, k, 0, 0))
    out = pl.pallas_call(
        functools.partial(_nsa_prompt_kernel, seq=T),
        grid=(B, KVH, T // tq),
        in_specs=[pl.BlockSpec((None, None, G, tq, HD), lambda b, k, t: (b, k, 0, t, 0)),
                  pl.BlockSpec((None, None, tq, 3 * G), lambda b, k, t: (b, k, t, 0)),
                  kv_spec, kv_spec, kv_spec, kv_spec, kv_spec, kv_spec,
                  pl.BlockSpec((2, CMP_BLOCK, HD), lambda b, k, t: (0, 0, 0)),
                  pl.BlockSpec((LANE, T), lambda b, k, t: (0, 0))],
        out_specs=pl.BlockSpec((None, None, G, tq, HD), lambda b, k, t: (b, k, 0, t, 0)),
        out_shape=jax.ShapeDtypeStruct((B, KVH, G, T, HD), jnp.float32),
        scratch_shapes=[pltpu.VMEM((LANE, HD), jnp.bfloat16),
                        pltpu.VMEM((LANE, HD), jnp.bfloat16),
                        pltpu.VMEM((tq // LANE, LANE, LANE), jnp.float32),
                        pltpu.VMEM((tq, T), jnp.float32)],
        compiler_params=pltpu.CompilerParams(dimension_semantics=("arbitrary", "arbitrary", "arbitrary"),
                                             vmem_limit_bytes=_VMEM_LIMIT),
        name="nsa_prompt",
    )(q, gl, kc, vc, ks, vs, kw, vw, cw, expand)
    return out.transpose(0, 3, 1, 2, 4).reshape(B, T, D_NSA)


def _nsa_prompt_mix(xn, n_keep, cmp_w):
    B, T, _ = xn.shape
    kv5 = lambda i: xn[..., D_NSA + 2 * i * NSA_KV_COLS:D_NSA + 2 * (i + 1) * NSA_KV_COLS].reshape(
        B, T, 2, NSA_KV_HEADS, HEAD_DIM)
    return _nsa_prompt(xn, cmp_w), kv5(0), kv5(1), kv5(2)[:, -n_keep:]


def _hybrid_layer(x, pos0, past_cmp, past_sel, win_prefix, shift_prev, rwkv_s0, n_keep, p):
    B, T, D = x.shape
    proj = _matmul(x.reshape(B * T, D), p['w_in']).reshape(B, T, IN_COLS)
    y_r, s_T, new_shift = _rwkv_time_mix(proj[..., :RWKV_COLS], shift_prev, rwkv_s0, p)
    if past_cmp.shape[1] == 0 and T % NSA_TQ == 0 and T >= n_keep:
        y_n, new_cmp, new_sel, new_win = _nsa_prompt_mix(proj[..., RWKV_COLS:], n_keep, p['nsa_cmp_w'])
    else:
        y_n, new_cmp, new_sel, new_win = _nsa_mix(proj[..., RWKV_COLS:], past_cmp, past_sel, win_prefix,
                                                  pos0, n_keep, p['nsa_cmp_w'])
    h = _matmul(jnp.concatenate([y_r, y_n], axis=-1).reshape(B * T, D), p['w_out']).reshape(B, T, D)
    x = _layer_norm(DEEPNORM_ALPHA * x + h, p['ln1_g'], p['ln1_b'])
    f = _peer_ffn(x, p['peer_wq_bf'], p['peer_sk_bf'], p['peer_u_bf'], p['peer_v_bf'])
    x = _layer_norm(DEEPNORM_ALPHA * x + f, p['ln2_g'], p['ln2_b'])
    return x, (new_cmp, new_sel, new_win, s_T, new_shift)


def kernel(x_prompt, x_sample, cache_cmp_kv, cache_sel_kv, page_table, state_win_kv, state_rwkv,
           state_shift, w_in, rwkv_mu, rwkv_w0, rwkv_w2, rwkv_a0, rwkv_a2, rwkv_g2, rwkv_k_k,
           rwkv_k_a, rwkv_r_k, rwkv_gn_g, rwkv_gn_b, nsa_cmp_w, w_out, ln1_g, ln1_b, peer_wq,
           peer_subkeys, peer_u, peer_v, ln2_g, ln2_b):
    bp = x_prompt.shape[0]
    bs = x_sample.shape[0]
    dt = x_prompt.dtype
    n_pages = PAST_LEN // PAGE_SIZE
    past_len = n_pages * PAGE_SIZE
    n_keep = state_win_kv.shape[2]
    empty_kv = jnp.zeros((bp, 0, 2, NSA_KV_HEADS, HEAD_DIM), dt)
    zero_shift = jnp.zeros((bp, 1, RWKV_COLS), dt)
    zero_state = jnp.zeros((bp, H_RWKV, HEAD_DIM, HEAD_DIM), dt)
    yp, ys = x_prompt, x_sample
    st_p, st_s = [], []
    for l in range(DEPTH):
        p = {'w_in': w_in[l], 'rwkv_mu': rwkv_mu[l], 'rwkv_w0': rwkv_w0[l], 'rwkv_w2': rwkv_w2[l],
             'rwkv_a0': rwkv_a0[l], 'rwkv_a2': rwkv_a2[l], 'rwkv_g2': rwkv_g2[l],
             'rwkv_k_k': rwkv_k_k[l], 'rwkv_k_a': rwkv_k_a[l], 'rwkv_r_k': rwkv_r_k[l],
             'rwkv_gn_g': rwkv_gn_g[l], 'rwkv_gn_b': rwkv_gn_b[l], 'nsa_cmp_w': nsa_cmp_w[l],
             'w_out': w_out[l], 'ln1_g': ln1_g[l], 'ln1_b': ln1_b[l],
             'peer_wq_bf': peer_wq[l].astype(jnp.bfloat16),
             'peer_sk_bf': peer_subkeys[l].reshape(2 * P_HEADS, N_KEYS, P_DKEY // 2).astype(jnp.bfloat16),
             'peer_u_bf': peer_u[l].astype(jnp.bfloat16), 'peer_v_bf': peer_v[l].astype(jnp.bfloat16),
             'ln2_g': ln2_g[l], 'ln2_b': ln2_b[l]}
        yp, sp = _hybrid_layer(yp, 0, empty_kv, empty_kv, empty_kv, zero_shift, zero_state, n_keep, p)
        past_cmp = cache_cmp_kv[l, page_table].reshape(bs, past_len, 2, NSA_KV_HEADS, HEAD_DIM)
        past_sel = cache_sel_kv[l, page_table].reshape(bs, past_len, 2, NSA_KV_HEADS, HEAD_DIM)
        ys, ss = _hybrid_layer(ys, past_len, past_cmp, past_sel, state_win_kv[l], state_shift[l],
                               state_rwkv[l], n_keep, p)
        st_p.append(sp)
        st_s.append(ss)
    stk = lambda sts, i: jnp.stack([s[i] for s in sts], axis=0)
    return (yp, ys, stk(st_p, 0), stk(st_p, 1), stk(st_p, 2), stk(st_p, 3), stk(st_p, 4),
            stk(st_s, 0), stk(st_s, 1), stk(st_s, 2), stk(st_s, 3), stk(st_s, 4))
```

```python
import functools
import math

import jax
import jax.numpy as jnp
from jax import lax
from jax.experimental import pallas as pl
from jax.experimental.pallas import tpu as pltpu

D_MODEL = 2048
DEPTH = 2
PAST_LEN = 16384
PAGE_SIZE = 128
HEAD_DIM = 64
D_RWKV = D_MODEL // 2
D_NSA = D_MODEL - D_RWKV
H_RWKV = D_RWKV // HEAD_DIM
H_NSA = D_NSA // HEAD_DIM
NSA_KV_HEADS = 4
NSA_GROUP = H_NSA // NSA_KV_HEADS
NSA_KV_COLS = NSA_KV_HEADS * HEAD_DIM
CMP_BLOCK = 32
SEL_BLOCK = 64
SEL_TOPN = 16
WINDOW = 512
Q_BLOCK = 128
W_LORA = 64
A_LORA = 64
G_LORA = 160
RWKV_COLS = 3 * D_RWKV + W_LORA + A_LORA + G_LORA
NSA_COLS = D_NSA + 6 * NSA_KV_COLS + 3 * H_NSA
IN_COLS = RWKV_COLS + NSA_COLS
P_HEADS = 8
N_KEYS = 128
P_DKEY = 256
P_TOPK = 16
P_TOK_BLOCK = 128
LN_EPS = 1e-5
GN_EPS = 64e-5
DEEPNORM_ALPHA = (2 * DEPTH) ** 0.25
FORCE_SCORE = 1e4
NEG_INF = -1e30

N_EXPERTS = N_KEYS * N_KEYS

LANE = 128
_VMEM_LIMIT = 56 * 1024 * 1024


def _round_up(x, m):
    return -(-x // m) * m


def _mm_kernel(x_ref, w_ref, o_ref):
    o_ref[...] = jnp.dot(x_ref[...].astype(jnp.bfloat16), w_ref[...].astype(jnp.bfloat16),
                         preferred_element_type=jnp.float32)


def _matmul(x, w, *, tm=512, tn=512):
    m, k = x.shape
    n = w.shape[1]
    tm = min(tm, m)
    n_pad = _round_up(n, tn)
    if n_pad != n:
        w = jnp.pad(w, ((0, 0), (0, n_pad - n)))
    out = pl.pallas_call(
        _mm_kernel,
        grid=(m // tm, n_pad // tn),
        in_specs=[pl.BlockSpec((tm, k), lambda i, j: (i, 0)),
                  pl.BlockSpec((k, tn), lambda i, j: (0, j))],
        out_specs=pl.BlockSpec((tm, tn), lambda i, j: (i, j)),
        out_shape=jax.ShapeDtypeStruct((m, n_pad), jnp.float32),
        compiler_params=pltpu.CompilerParams(dimension_semantics=("arbitrary", "arbitrary"),
                                             vmem_limit_bytes=48 * 1024 * 1024),
    )(x, w)
    return out[:, :n] if n_pad != n else out


def _layer_norm(x, g, b):
    mu = x.mean(-1, keepdims=True)
    var = jnp.mean(jnp.square(x - mu), -1, keepdims=True)
    return (x - mu) * lax.rsqrt(var + LN_EPS) * g + b


def _masked_softmax(s, mask):
    s = jnp.where(mask, s.astype(jnp.float32), NEG_INF)
    p = jax.nn.softmax(s, axis=-1)
    return jnp.where(mask, p, 0.0)


RW_CHUNK = 64
RW_PAIRS = D_RWKV // LANE


def _bdot(a, b):
    return jnp.dot(a.astype(jnp.bfloat16), b.astype(jnp.bfloat16), preferred_element_type=jnp.float32)


def _bdot_nt(a, b):
    return lax.dot_general(a.astype(jnp.bfloat16), b.astype(jnp.bfloat16), (((1,), (1,)), ((), ())),
                           preferred_element_type=jnp.float32)


def _bdot_tn(a, b):
    return lax.dot_general(a.astype(jnp.bfloat16), b.astype(jnp.bfloat16), (((0,), (0,)), ((), ())),
                           preferred_element_type=jnp.float32)


def _rwkv_chunk_kernel(r_ref, lw_ref, k_ref, v_ref, kk_ref, b_ref, s0_ref, y_ref, st_ref):
    c = pl.program_id(1)
    C = RW_CHUNK
    f32 = jnp.float32

    @pl.when(c == 0)
    def _():
        st_ref[...] = s0_ref[...]

    row = lax.broadcasted_iota(jnp.int32, (C, C), 0)
    col = lax.broadcasted_iota(jnp.int32, (C, C), 1)
    tri_incl = col <= row
    tri_strict = col < row
    tri_f = jnp.where(tri_incl, 1.0, 0.0).astype(f32)
    eye = jnp.where(row == col, 1.0, 0.0).astype(f32)
    lane = lax.broadcasted_iota(jnp.int32, (1, LANE), 1)
    head0 = lane < HEAD_DIM
    r128 = lax.broadcasted_iota(jnp.int32, (LANE, LANE), 0)
    c128 = lax.broadcasted_iota(jnp.int32, (LANE, LANE), 1)
    same_head = (r128 < HEAD_DIM) == (c128 < HEAD_DIM)
    eye128 = r128 == c128

    pairs = range(RW_PAIRS)
    heads = (head0, ~head0)
    sls = [slice(pr * LANE, (pr + 1) * LANE) for pr in pairs]
    lw = [lw_ref[:, sl] for sl in sls]
    cum = [jnp.dot(tri_f, lw[pr], precision=lax.Precision.HIGHEST, preferred_element_type=f32)
           for pr in pairs]
    tot = [cum[pr][C - 1:C, :] for pr in pairs]
    v = [v_ref[:, sl] for sl in sls]
    k = [k_ref[:, sl] for sl in sls]
    b = [b_ref[:, sl] for sl in sls]
    x, ym = [], []
    for pr in pairs:
        e_neg = jnp.exp(-cum[pr])
        x.append(jnp.concatenate([kk_ref[:, sls[pr]] * jnp.exp(cum[pr] - lw[pr]),
                                  r_ref[:, sls[pr]] * jnp.exp(cum[pr])], axis=0))
        ym.append(jnp.concatenate([k[pr] * e_neg, b[pr] * e_neg], axis=0))
    a0 = [st_ref[pr] for pr in pairs]
    xa = [_bdot(x[pr], a0[pr]) for pr in pairs]
    g = [[_bdot_nt(jnp.where(hm, x[pr], 0.0), ym[pr]) for hm in heads] for pr in pairs]
    lkv = [[_bdot(jnp.where(tri_strict, g[pr][h][:C, :C], 0.0), v[pr]) for h in range(2)] for pr in pairs]
    npow = [[jnp.where(tri_strict, -g[pr][h][:C, C:], 0.0) for h in range(2)] for pr in pairs]
    tmat = [[eye + npow[pr][h] for h in range(2)] for pr in pairs]
    for _ in range(int(math.log2(C)) - 1):
        npow = [[_bdot(npow[pr][h], npow[pr][h]) for h in range(2)] for pr in pairs]
        tmat = [[tmat[pr][h] + _bdot(tmat[pr][h], npow[pr][h]) for h in range(2)] for pr in pairs]
    rhs = [xa[pr][:C] + jnp.where(head0, lkv[pr][0], lkv[pr][1]) for pr in pairs]
    w = [jnp.where(head0, _bdot(tmat[pr][0], rhs[pr]), _bdot(tmat[pr][1], rhs[pr])) for pr in pairs]
    vw = [jnp.concatenate([v[pr], w[pr]], axis=0) for pr in pairs]
    for pr in pairs:
        mr = [jnp.concatenate([jnp.where(tri_incl, g[pr][h][C:, :C], 0.0),
                               jnp.where(tri_incl, -g[pr][h][C:, C:], 0.0)], axis=1) for h in range(2)]
        y_ref[:, sls[pr]] = xa[pr][C:] + jnp.where(head0, _bdot(mr[0], vw[pr]), _bdot(mr[1], vw[pr]))
    for pr in pairs:
        e_rem = jnp.exp(tot[pr] - cum[pr])
        kb = jnp.concatenate([k[pr] * e_rem, -(b[pr] * e_rem)], axis=0)
        upd = _bdot_tn(kb, vw[pr])
        p_col = jnp.sum(jnp.where(eye128, jnp.exp(tot[pr]), 0.0), axis=1, keepdims=True)
        st_ref[pr] = a0[pr] * p_col + jnp.where(same_head, upd, 0.0)


def _rwkv_scan(r, lw, k, v, kk, b, s0):
    B, T, D = r.shape
    C = RW_CHUNK
    a = jnp.swapaxes(s0, -1, -2).reshape(B, RW_PAIRS, 2, HEAD_DIM, HEAD_DIM)
    z = jnp.zeros_like(a[:, :, 0])
    a0 = jnp.concatenate([jnp.concatenate([a[:, :, 0], z], axis=-1),
                          jnp.concatenate([z, a[:, :, 1]], axis=-1)], axis=-2)
    seq_spec = pl.BlockSpec((None, C, D), lambda bi, ci: (bi, ci, 0))
    st_spec = pl.BlockSpec((None, RW_PAIRS, LANE, LANE), lambda bi, ci: (bi, 0, 0, 0))
    y, st = pl.pallas_call(
        _rwkv_chunk_kernel,
        grid=(B, T // C),
        in_specs=[seq_spec] * 6 + [st_spec],
        out_specs=[seq_spec, st_spec],
        out_shape=[jax.ShapeDtypeStruct((B, T, D), jnp.float32),
                   jax.ShapeDtypeStruct((B, RW_PAIRS, LANE, LANE), jnp.float32)],
        compiler_params=pltpu.CompilerParams(dimension_semantics=("arbitrary", "arbitrary"),
                                             vmem_limit_bytes=_VMEM_LIMIT),
        name="rwkv_scan",
    )(r, lw, k, v, kk, b, a0)
    s_t = jnp.stack([st[:, :, :HEAD_DIM, :HEAD_DIM], st[:, :, HEAD_DIM:, HEAD_DIM:]], axis=2)
    return y, jnp.swapaxes(s_t.reshape(B, H_RWKV, HEAD_DIM, HEAD_DIM), -1, -2)


def _rwkv_time_mix(xr, shift_prev, s0, p):
    B, T, _ = xr.shape
    dt = xr.dtype
    prev = jnp.concatenate([shift_prev.astype(dt), xr[:, :-1]], axis=1)
    xs = xr + p['rwkv_mu'] * (prev - xr)
    cuts = [D_RWKV, 2 * D_RWKV, 3 * D_RWKV, 3 * D_RWKV + W_LORA, 3 * D_RWKV + W_LORA + A_LORA]
    r, k, v, wl, al, gl = jnp.split(xs, cuts, axis=-1)
    w = -jax.nn.softplus(-(p['rwkv_w0'] + jnp.tanh(wl) @ p['rwkv_w2'])) - 0.5
    lw = -jnp.exp(w)
    a = jax.nn.sigmoid(p['rwkv_a0'] + al @ p['rwkv_a2'])
    g = jax.nn.sigmoid(gl) @ p['rwkv_g2']
    hd = lambda t: t.reshape(B, T, H_RWKV, HEAD_DIM)
    kkf = hd(k * p['rwkv_k_k']).astype(jnp.float32)
    kk = (kkf * lax.rsqrt(jnp.maximum(jnp.sum(kkf * kkf, -1, keepdims=True), 1e-24))).astype(dt)
    kk = kk.reshape(B, T, D_RWKV)
    k = k * (1.0 + (a - 1.0) * p['rwkv_k_a'])
    t_pad = _round_up(T, RW_CHUNK)
    padt = lambda t: jnp.pad(t, ((0, 0), (0, t_pad - T), (0, 0)))
    y, s_T = _rwkv_scan(padt(r), padt(lw), padt(k), padt(v), padt(kk), padt(kk * a), s0.astype(dt))
    y = hd(y[:, :T]).astype(jnp.float32)
    r, k, v = hd(r), hd(k), hd(v)
    mu = y.mean(-1, keepdims=True)
    var = jnp.mean(jnp.square(y - mu), -1, keepdims=True)
    yn = ((y - mu) * lax.rsqrt(var + GN_EPS)).reshape(B, T, D_RWKV) * p['rwkv_gn_g'] + p['rwkv_gn_b']
    bonus = jnp.sum(r * k * p['rwkv_r_k'], -1, keepdims=True) * v
    out = (yn.astype(dt) + bonus.reshape(B, T, D_RWKV)) * g
    return out, s_T, xr[:, -1:]


def _nsa_mix(xn, past_cmp, past_sel, win_prefix, pos0, n_keep, cmp_w):
    B, T, _ = xn.shape
    dt = xn.dtype
    cuts = [D_NSA + i * NSA_KV_COLS for i in range(7)]
    q, kc, vc, ksl, vsl, kw, vw, gl = jnp.split(xn, cuts, axis=-1)
    kvh = lambda t: t.reshape(B, T, NSA_KV_HEADS, HEAD_DIM)
    new_cmp = jnp.stack([kvh(kc), kvh(vc)], axis=2)
    new_sel = jnp.stack([kvh(ksl), kvh(vsl)], axis=2)
    new_win = jnp.stack([kvh(kw), kvh(vw)], axis=2)
    L = past_cmp.shape[1] + T
    L_pad = -(-L // SEL_BLOCK) * SEL_BLOCK
    pad = lambda t: jnp.pad(t, ((0, 0), (0, L_pad - L), (0, 0), (0, 0), (0, 0)))
    full_cmp = pad(jnp.concatenate([past_cmp.astype(dt), new_cmp], axis=1))
    full_sel = pad(jnp.concatenate([past_sel.astype(dt), new_sel], axis=1))
    n_cmp = L_pad // CMP_BLOCK
    n_sel = L_pad // SEL_BLOCK
    cblk = full_cmp.reshape(B, n_cmp, CMP_BLOCK, 2, NSA_KV_HEADS, HEAD_DIM)
    kcb = jnp.einsum('bnlkd,l->bnkd', cblk[:, :, :, 0], cmp_w[0])
    vcb = jnp.einsum('bnlkd,l->bnkd', cblk[:, :, :, 1], cmp_w[1])
    sblk = full_sel.reshape(B, n_sel, SEL_BLOCK, 2, NSA_KV_HEADS, HEAD_DIM)
    ksb = jnp.transpose(sblk[:, :, :, 0], (0, 3, 1, 2, 4))
    vsb = jnp.transpose(sblk[:, :, :, 1], (0, 3, 1, 2, 4))
    n_prefix = win_prefix.shape[1]
    win_pad = jnp.concatenate([jnp.zeros((B, WINDOW, 2, NSA_KV_HEADS, HEAD_DIM), dt),
                               win_prefix.astype(dt), new_win], axis=1)
    q = q.reshape(B, T, NSA_KV_HEADS, NSA_GROUP, HEAD_DIM)
    gates = jax.nn.sigmoid(gl.reshape(B, T, NSA_KV_HEADS, NSA_GROUP, 3))
    qc = math.gcd(T, Q_BLOCK)
    n_chunks = T // qc
    top_n = min(SEL_TOPN, n_sel)
    scale = HEAD_DIM ** -0.5
    bi = jnp.arange(B)[:, None, None, None]
    hi = jnp.arange(NSA_KV_HEADS)[None, None, :, None]
    cmp_end = (jnp.arange(n_cmp) + 1) * CMP_BLOCK - 1
    blk = jnp.arange(n_sel)
    lw = WINDOW - 1 + qc
    jw = jnp.arange(lw)
    iw = jnp.arange(qc)

    def chunk(args):
        c, q_c, g_c = args
        qpos = pos0 + c * qc + jnp.arange(qc)
        q_c = q_c * scale
        s = jnp.einsum('bqkgd,bnkd->bqkgn', q_c, kcb)
        p_c = _masked_softmax(s, (cmp_end[None, :] <= qpos[:, None])[None, :, None, None, :])
        o_cmp = jnp.einsum('bqkgn,bnkd->bqkgd', p_c.astype(dt), vcb)
        imp = p_c.sum(3).reshape(B, qc, NSA_KV_HEADS, n_sel, SEL_BLOCK // CMP_BLOCK).sum(-1)
        cur = (qpos // SEL_BLOCK)[None, :, None, None]
        forced = (blk == 0) | (blk == cur) | (blk == cur - 1)
        imp = jnp.where(blk > cur, -1.0, jnp.where(forced, FORCE_SCORE, imp))
        top_val, top_idx = lax.top_k(imp, top_n)
        k_g = ksb[bi, hi, top_idx]
        v_g = vsb[bi, hi, top_idx]
        s = jnp.einsum('bqkgd,bqknld->bqkgnl', q_c, k_g)
        kpos = top_idx[..., None] * SEL_BLOCK + jnp.arange(SEL_BLOCK)
        m = (top_val >= 0)[..., None] & (kpos <= qpos[None, :, None, None, None])
        p_s = _masked_softmax(s.reshape(B, qc, NSA_KV_HEADS, NSA_GROUP, top_n * SEL_BLOCK),
                              m.reshape(B, qc, NSA_KV_HEADS, 1, top_n * SEL_BLOCK))
        o_sel = jnp.einsum('bqkgnl,bqknld->bqkgd',
                           p_s.reshape(B, qc, NSA_KV_HEADS, NSA_GROUP, top_n, SEL_BLOCK).astype(dt), v_g)
        start = n_prefix + c * qc + 1
        wkv = lax.dynamic_slice_in_dim(win_pad, start, lw, axis=1)
        s = jnp.einsum('bqkgd,bjkd->bqkgj', q_c, wkv[:, :, 0])
        m = ((jw[None, :] >= iw[:, None]) & (jw[None, :] <= iw[:, None] + WINDOW - 1)
             & (start + jw[None, :] >= WINDOW))
        p_w = _masked_softmax(s, m[None, :, None, None, :])
        o_win = jnp.einsum('bqkgj,bjkd->bqkgd', p_w.astype(dt), wkv[:, :, 1])
        return g_c[..., 0:1] * o_cmp + g_c[..., 1:2] * o_sel + g_c[..., 2:3] * o_win

    chunks = lambda t: jnp.moveaxis(t.reshape(B, n_chunks, qc, *t.shape[2:]), 1, 0)
    out = lax.map(chunk, (jnp.arange(n_chunks), chunks(q), chunks(gates)))
    out = jnp.moveaxis(out, 0, 1).reshape(B, T, D_NSA)
    return out, new_cmp, new_sel, win_pad[:, -n_keep:]


def _extract_top(buf_ref, out_ref, nrows, width):
    iota = lax.broadcasted_iota(jnp.int32, (nrows, width), 0)

    def body(r, carry):
        s = buf_ref[...]
        mx = jnp.max(s, axis=0, keepdims=True)
        first = jnp.min(jnp.where(s == mx, iota, nrows), axis=0, keepdims=True)
        buf_ref[...] = jnp.where(iota == first, -jnp.inf, s)
        out_ref[pl.ds(r, 1), :] = mx
        return carry

    lax.fori_loop(0, P_TOPK, body, 0)


def _peer_score_kernel(x_ref, wq_ref, sk_ref, s1_ref, s2_ref, st_ref, buf, cand, hv0, hv1, tv):
    tn = x_ref.shape[0]
    q = jnp.dot(x_ref[...].astype(jnp.bfloat16), wq_ref[...],
                preferred_element_type=jnp.float32).astype(jnp.bfloat16)
    for h in range(P_HEADS):
        for c, (s_ref, hv) in enumerate(((s1_ref, hv0), (s2_ref, hv1))):
            col = (2 * h + c) * N_KEYS
            s_t = lax.dot_general(sk_ref[2 * h + c], q[:, col:col + N_KEYS],
                                  (((1,), (1,)), ((), ())), preferred_element_type=jnp.float32)
            s_ref[h] = s_t
            buf[...] = s_t
            _extract_top(buf, hv, N_KEYS, tn)
        for a in range(P_TOPK):
            cand[a * P_TOPK:(a + 1) * P_TOPK, :] = hv0[a:a + 1, :] + hv1[...]
        _extract_top(cand, tv, P_TOPK * P_TOPK, tn)
        z = jnp.sum(jnp.exp(tv[...] - tv[0:1, :]), axis=0, keepdims=True)
        st_ref[0, h:h + 1, :] = tv[P_TOPK - 1:P_TOPK, :]
        st_ref[1, h:h + 1, :] = hv0[0:1, :]
        st_ref[2, h:h + 1, :] = hv1[0:1, :]
        st_ref[3, h:h + 1, :] = 1.0 / z


def _peer_scores(x, wq_bf, sk_bf, *, tn=256):
    n = x.shape[0]
    tn = min(tn, n)
    return pl.pallas_call(
        _peer_score_kernel,
        grid=(n // tn,),
        in_specs=[pl.BlockSpec((tn, D_MODEL), lambda i: (i, 0)),
                  pl.BlockSpec((D_MODEL, P_HEADS * P_DKEY), lambda i: (0, 0)),
                  pl.BlockSpec((2 * P_HEADS, N_KEYS, P_DKEY // 2), lambda i: (0, 0, 0))],
        out_specs=[pl.BlockSpec((P_HEADS, N_KEYS, tn), lambda i: (0, 0, i)),
                   pl.BlockSpec((P_HEADS, N_KEYS, tn), lambda i: (0, 0, i)),
                   pl.BlockSpec((4, P_HEADS, tn), lambda i: (0, 0, i))],
        out_shape=[jax.ShapeDtypeStruct((P_HEADS, N_KEYS, n), jnp.float32),
                   jax.ShapeDtypeStruct((P_HEADS, N_KEYS, n), jnp.float32),
                   jax.ShapeDtypeStruct((4, P_HEADS, n), jnp.float32)],
        scratch_shapes=[pltpu.VMEM((N_KEYS, tn), jnp.float32),
                        pltpu.VMEM((P_TOPK * P_TOPK, tn), jnp.float32),
                        pltpu.VMEM((P_TOPK, tn), jnp.float32),
                        pltpu.VMEM((P_TOPK, tn), jnp.float32),
                        pltpu.VMEM((P_TOPK, tn), jnp.float32)],
        compiler_params=pltpu.CompilerParams(dimension_semantics=("arbitrary",),
                                             vmem_limit_bytes=_VMEM_LIMIT),
        name="peer_scores",
    )(x, wq_bf, sk_bf)


def _peer_expert_kernel(x_ref, u_ref, v_ref, s1_ref, s2_ref, st_ref, o_ref, e2_scr, h_scr, p_scr,
                        *, chunk):
    j = pl.program_id(1)
    tn = x_ref.shape[0]
    te = u_ref.shape[0]

    @pl.when(j == 0)
    def _():
        o_ref[...] = jnp.zeros_like(o_ref)
        for h in range(P_HEADS):
            e2_scr[h] = jnp.exp(s2_ref[h] - st_ref[2, h:h + 1, :])

    h_scr[...] = lax.dot_general(x_ref[...], u_ref[...], (((1,), (1,)), ((), ())),
                                 preferred_element_type=jnp.float32)

    def chunk_body(c, carry):
        tsl = pl.ds(pl.multiple_of(c * chunk, chunk), chunk)
        for ii in range(te // N_KEYS):
            i = j * (te // N_KEYS) + ii
            w_t = jnp.zeros((N_KEYS, chunk), jnp.float32)
            for h in range(P_HEADS):
                a = s1_ref[h, pl.ds(i, 1), tsl]
                e1 = jnp.exp(a - st_ref[1, h:h + 1, tsl]) * st_ref[3, h:h + 1, tsl]
                val = a + s2_ref[h, :, tsl]
                w_t = w_t + jnp.where(val >= st_ref[0, h:h + 1, tsl], e1 * e2_scr[h, :, tsl], 0.0)
            hh = h_scr[tsl, ii * N_KEYS:(ii + 1) * N_KEYS]
            g = 0.5 * hh * (1.0 + lax.erf(hh * 0.7071067811865476))
            p_scr[tsl, ii * N_KEYS:(ii + 1) * N_KEYS] = (w_t.T * g).astype(jnp.bfloat16)
        return carry

    lax.fori_loop(0, tn // chunk, chunk_body, 0)
    o_ref[...] += jnp.dot(p_scr[...], v_ref[...], preferred_element_type=jnp.float32)


def _peer_experts(x_bf, u_bf, v_bf, s1t, s2t, stats, *, tn=1024, te=512, chunk=256):
    n = x_bf.shape[0]
    tn = min(tn, n)
    chunk = min(chunk, tn)
    once = pl.Buffered(1)
    return pl.pallas_call(
        functools.partial(_peer_expert_kernel, chunk=chunk),
        grid=(n // tn, N_EXPERTS // te),
        in_specs=[pl.BlockSpec((tn, D_MODEL), lambda i, j: (i, 0), pipeline_mode=once),
                  pl.BlockSpec((te, D_MODEL), lambda i, j: (j, 0)),
                  pl.BlockSpec((te, D_MODEL), lambda i, j: (j, 0)),
                  pl.BlockSpec((P_HEADS, N_KEYS, tn), lambda i, j: (0, 0, i), pipeline_mode=once),
                  pl.BlockSpec((P_HEADS, N_KEYS, tn), lambda i, j: (0, 0, i), pipeline_mode=once),
                  pl.BlockSpec((4, P_HEADS, tn), lambda i, j: (0, 0, i), pipeline_mode=once)],
        out_specs=pl.BlockSpec((tn, D_MODEL), lambda i, j: (i, 0)),
        out_shape=jax.ShapeDtypeStruct((n, D_MODEL), jnp.float32),
        scratch_shapes=[pltpu.VMEM((P_HEADS, N_KEYS, tn), jnp.float32),
                        pltpu.VMEM((tn, te), jnp.float32),
                        pltpu.VMEM((tn, te), jnp.bfloat16)],
        compiler_params=pltpu.CompilerParams(dimension_semantics=("arbitrary", "arbitrary"),
                                             vmem_limit_bytes=_VMEM_LIMIT),
        name="peer_experts",
    )(x_bf, u_bf, v_bf, s1t, s2t, stats)


def _peer_ffn(x, wq_bf, sk_bf, u_bf, v_bf):
    B, T, D = x.shape
    n = B * T
    xt = x.reshape(n, D)
    n_pad = _round_up(n, LANE)
    if n_pad != n:
        xt = jnp.pad(xt, ((0, n_pad - n), (0, 0)))
    s1t, s2t, stats = _peer_scores(xt, wq_bf, sk_bf)
    out = _peer_experts(xt.astype(jnp.bfloat16), u_bf, v_bf, s1t, s2t, stats)
    return out[:n].reshape(B, T, D)


NSA_TQ = 256


def _flash_tile(q4, k_t, v_t, mask4, m, l, acc):
    s = lax.dot_general(q4, k_t, (((1,), (1,)), ((), ())), preferred_element_type=jnp.float32)
    s = jnp.where(mask4, s, NEG_INF)
    m_new = jnp.maximum(m, jnp.max(s, axis=-1, keepdims=True))
    p = jnp.where(mask4, jnp.exp(s - m_new), 0.0)
    alpha = jnp.exp(m - m_new)
    l_new = alpha * l + jnp.sum(p, axis=-1, keepdims=True)
    acc_new = alpha * acc + jnp.dot(p.astype(jnp.bfloat16), v_t, preferred_element_type=jnp.float32)
    return m_new, l_new, acc_new


def _nsa_prompt_kernel(q_ref, gl_ref, kc_ref, vc_ref, ks_ref, vs_ref, kw_ref, vw_ref, cw_ref, ex_ref,
                       o_ref, kcb_scr, vcb_scr, imp_scr, selx_scr, *, seq):
    qt = pl.program_id(2)
    tq = NSA_TQ
    n_cmp = seq // CMP_BLOCK
    n_sel = seq // SEL_BLOCK
    G = NSA_GROUP
    q0 = qt * tq

    @pl.when(qt == 0)
    def _():
        kc = kc_ref[...].reshape(n_cmp, CMP_BLOCK, HEAD_DIM)
        vc = vc_ref[...].reshape(n_cmp, CMP_BLOCK, HEAD_DIM)
        kcb_scr[...] = jnp.zeros_like(kcb_scr)
        vcb_scr[...] = jnp.zeros_like(vcb_scr)
        kcb_scr[0:n_cmp, :] = jnp.sum(kc * cw_ref[0][None], axis=1).astype(jnp.bfloat16)
        vcb_scr[0:n_cmp, :] = jnp.sum(vc * cw_ref[1][None], axis=1).astype(jnp.bfloat16)

    qpos_col = q0 + lax.broadcasted_iota(jnp.int32, (tq, 1), 0)
    qs = [(q_ref[g] * (HEAD_DIM ** -0.5)).astype(jnp.bfloat16) for g in range(G)]
    gates = jax.nn.sigmoid(gl_ref[...])

    cmp_end = (lax.broadcasted_iota(jnp.int32, (1, LANE), 1) + 1) * CMP_BLOCK - 1
    cmask = (cmp_end <= qpos_col) & (lax.broadcasted_iota(jnp.int32, (1, LANE), 1) < n_cmp)
    o_cmp = []
    imp = jnp.zeros((tq, LANE), jnp.float32)
    for g in range(G):
        s = lax.dot_general(qs[g], kcb_scr[...], (((1,), (1,)), ((), ())),
                            preferred_element_type=jnp.float32)
        s = jnp.where(cmask, s, NEG_INF)
        e = jnp.exp(s - jnp.max(s, axis=-1, keepdims=True))
        p = jnp.where(cmask, e / jnp.sum(e, axis=-1, keepdims=True), 0.0)
        o_cmp.append(jnp.dot(p.astype(jnp.bfloat16), vcb_scr[...], preferred_element_type=jnp.float32))
        imp = imp + p
    imp_t = imp.T
    ratio = SEL_BLOCK // CMP_BLOCK
    parts = []
    for c in range(tq // LANE):
        imp_scr[c] = imp_t[:, c * LANE:(c + 1) * LANE]
        part = imp_scr[c, pl.ds(0, n_sel, stride=ratio), :]
        for r in range(1, ratio):
            part = part + imp_scr[c, pl.ds(r, n_sel, stride=ratio), :]
        parts.append(part)
    imp_s = jnp.concatenate(parts, axis=1)
    qpos_row = q0 + lax.broadcasted_iota(jnp.int32, (1, tq), 1)
    cur = qpos_row // SEL_BLOCK
    blk = lax.broadcasted_iota(jnp.int32, (n_sel, tq), 0)
    forced = (blk == 0) | (blk == cur) | (blk == cur - 1)
    imp_s = jnp.where(blk > cur, -1.0, jnp.where(forced, FORCE_SCORE, imp_s))
    rank = jnp.zeros((n_sel, tq), jnp.int32)
    for mm in range(n_sel):
        row = imp_s[mm:mm + 1, :]
        ahead = (row > imp_s) | ((row == imp_s) & (mm < blk))
        rank = rank + jnp.where(ahead, 1, 0)
    sel = jnp.where((rank < min(SEL_TOPN, n_sel)) & (imp_s >= 0.0), 1.0, 0.0)
    sel_pad = jnp.concatenate([sel, jnp.zeros((LANE - n_sel, tq), jnp.float32)], axis=0)
    sel_q = sel_pad.T.astype(jnp.bfloat16)
    selx_scr[...] = jnp.dot(sel_q, ex_ref[...], preferred_element_type=jnp.float32)

    q4 = jnp.concatenate(qs, axis=0)
    qpos4 = jnp.concatenate([qpos_col] * G, axis=0)
    kpos_l = lax.broadcasted_iota(jnp.int32, (1, tq), 1)
    init = (jnp.full((G * tq, 1), NEG_INF, jnp.float32), jnp.zeros((G * tq, 1), jnp.float32),
            jnp.zeros((G * tq, HEAD_DIM), jnp.float32))

    def sel_body(kt, carry):
        k0 = pl.multiple_of(kt * tq, tq)
        chosen = selx_scr[:, pl.ds(k0, tq)] > 0.5
        chosen4 = jnp.concatenate([chosen] * G, axis=0)
        mask4 = chosen4 & ((k0 + kpos_l) <= qpos4)
        return _flash_tile(q4, ks_ref[pl.ds(k0, tq), :], vs_ref[pl.ds(k0, tq), :], mask4, *carry)

    _, l_s, acc_s = lax.fori_loop(0, qt + 1, sel_body, init)
    o_sel = acc_s / l_s

    def win_body(kt, carry):
        k0 = pl.multiple_of(kt * tq, tq)
        kpos = k0 + kpos_l
        mask4 = (kpos <= qpos4) & (kpos >= qpos4 - (WINDOW - 1))
        return _flash_tile(q4, kw_ref[pl.ds(k0, tq), :], vw_ref[pl.ds(k0, tq), :], mask4, *carry)

    _, l_w, acc_w = lax.fori_loop(jnp.maximum(qt - WINDOW // tq, 0), qt + 1, win_body, init)
    o_win = acc_w / l_w

    for g in range(G):
        rows = slice(g * tq, (g + 1) * tq)
        o_ref[g] = (gates[:, 3 * g:3 * g + 1] * o_cmp[g] + gates[:, 3 * g + 1:3 * g + 2] * o_sel[rows]
                    + gates[:, 3 * g + 2:3 * g + 3] * o_win[rows])


def _nsa_prompt(xn, cmp_w):
    B, T, _ = xn.shape
    KVH, G, HD = NSA_KV_HEADS, NSA_GROUP, HEAD_DIM
    kvc = NSA_KV_COLS
    tq = NSA_TQ
    q = xn[..., :D_NSA].reshape(B, T, KVH, G, HD).transpose(0, 2, 3, 1, 4)
    kv = [xn[..., D_NSA + i * kvc:D_NSA + (i + 1) * kvc].reshape(B, T, KVH, HD).transpose(0, 2, 1, 3)
          for i in range(6)]
    kc, vc = kv[0], kv[1]
    ks, vs, kw, vw = [t.astype(jnp.bfloat16) for t in kv[2:]]
    gl = xn[..., D_NSA + 6 * kvc:].reshape(B, T, KVH, 3 * G).transpose(0, 2, 1, 3)
    cw = jnp.broadcast_to(cmp_w[:, :, None], (2, CMP_BLOCK, HD))
    expand = (jnp.arange(LANE)[:, None] == (jnp.arange(T) // SEL_BLOCK)[None, :]).astype(jnp.bfloat16)
    kv_spec = pl.BlockSpec((None, None, T, HD), lambda b, k, t: (b, k, 0, 0))
    out = pl.pallas_call(
        functools.partial(_nsa_prompt_kernel, seq=T),
        grid=(B, KVH, T // tq),
        in_specs=[pl.BlockSpec((None, None, G, tq, HD), lambda b, k, t: (b, k, 0, t, 0)),
                  pl.BlockSpec((None, None, tq, 3 * G), lambda b, k, t: (b, k, t, 0)),
                  kv_spec, kv_spec, kv_spec, kv_spec, kv_spec, kv_spec,
                  pl.BlockSpec((2, CMP_BLOCK, HD), lambda b, k, t: (0, 0, 0)),
                  pl.BlockSpec((LANE, T), lambda b, k, t: (0, 0))],
        out_specs=pl.BlockSpec((None, None, G, tq, HD), lambda b, k, t: (b, k, 0, t, 0)),
        out_shape=jax.ShapeDtypeStruct((B, KVH, G, T, HD), jnp.float32),
        scratch_shapes=[pltpu.VMEM((LANE, HD), jnp.bfloat16),
                        pltpu.VMEM((LANE, HD), jnp.bfloat16),
                        pltpu.VMEM((tq // LANE, LANE, LANE), jnp.float32),
                        pltpu.VMEM((tq, T), jnp.float32)],
        compiler_params=pltpu.CompilerParams(dimension_semantics=("arbitrary", "arbitrary", "arbitrary"),
                                             vmem_limit_bytes=_VMEM_LIMIT),
        name="nsa_prompt",
    )(q, gl, kc, vc, ks, vs, kw, vw, cw, expand)
    return out.transpose(0, 3, 1, 2, 4).reshape(B, T, D_NSA)


def _nsa_prompt_mix(xn, n_keep, cmp_w):
    B, T, _ = xn.shape
    kv5 = lambda i: xn[..., D_NSA + 2 * i * NSA_KV_COLS:D_NSA + 2 * (i + 1) * NSA_KV_COLS].reshape(
        B, T, 2, NSA_KV_HEADS, HEAD_DIM)
    return _nsa_prompt(xn, cmp_w), kv5(0), kv5(1), kv5(2)[:, -n_keep:]


def _hybrid_layer(x, pos0, past_cmp, past_sel, win_prefix, shift_prev, rwkv_s0, n_keep, p):
    B, T, D = x.shape
    proj = _matmul(x.reshape(B * T, D), p['w_in']).reshape(B, T, IN_COLS)
    y_r, s_T, new_shift = _rwkv_time_mix(proj[..., :RWKV_COLS], shift_prev, rwkv_s0, p)
    if past_cmp.shape[1] == 0 and T % NSA_TQ == 0 and T >= n_keep:
        y_n, new_cmp, new_sel, new_win = _nsa_prompt_mix(proj[..., RWKV_COLS:], n_keep, p['nsa_cmp_w'])
    else:
        y_n, new_cmp, new_sel, new_win = _nsa_mix(proj[..., RWKV_COLS:], past_cmp, past_sel, win_prefix,
                                                  pos0, n_keep, p['nsa_cmp_w'])
    h = _matmul(jnp.concatenate([y_r, y_n], axis=-1).reshape(B * T, D), p['w_out']).reshape(B, T, D)
    x = _layer_norm(DEEPNORM_ALPHA * x + h, p['ln1_g'], p['ln1_b'])
    f = _peer_ffn(x, p['peer_wq_bf'], p['peer_sk_bf'], p['peer_u_bf'], p['peer_v_bf'])
    x = _layer_norm(DEEPNORM_ALPHA * x + f, p['ln2_g'], p['ln2_b'])
    return x, (new_cmp, new_sel, new_win, s_T, new_shift)


def kernel(x_prompt, x_sample, cache_cmp_kv, cache_sel_kv, page_table, state_win_kv, state_rwkv,
           state_shift, w_in, rwkv_mu, rwkv_w0, rwkv_w2, rwkv_a0, rwkv_a2, rwkv_g2, rwkv_k_k,
           rwkv_k_a, rwkv_r_k, rwkv_gn_g, rwkv_gn_b, nsa_cmp_w, w_out, ln1_g, ln1_b, peer_wq,
           peer_subkeys, peer_u, peer_v, ln2_g, ln2_b):
    bp = x_prompt.shape[0]
    bs = x_sample.shape[0]
    dt = x_prompt.dtype
    n_pages = PAST_LEN // PAGE_SIZE
    past_len = n_pages * PAGE_SIZE
    n_keep = state_win_kv.shape[2]
    empty_kv = jnp.zeros((bp, 0, 2, NSA_KV_HEADS, HEAD_DIM), dt)
    zero_shift = jnp.zeros((bp, 1, RWKV_COLS), dt)
    zero_state = jnp.zeros((bp, H_RWKV, HEAD_DIM, HEAD_DIM), dt)
    yp, ys = x_prompt, x_sample
    st_p, st_s = [], []
    for l in range(DEPTH):
        p = {'w_in': w_in[l], 'rwkv_mu': rwkv_mu[l], 'rwkv_w0': rwkv_w0[l], 'rwkv_w2': rwkv_w2[l],
             'rwkv_a0': rwkv_a0[l], 'rwkv_a2': rwkv_a2[l], 'rwkv_g2': rwkv_g2[l],
             'rwkv_k_k': rwkv_k_k[l], 'rwkv_k_a': rwkv_k_a[l], 'rwkv_r_k': rwkv_r_k[l],
             'rwkv_gn_g': rwkv_gn_g[l], 'rwkv_gn_b': rwkv_gn_b[l], 'nsa_cmp_w': nsa_cmp_w[l],
             'w_out': w_out[l], 'ln1_g': ln1_g[l], 'ln1_b': ln1_b[l],
             'peer_wq_bf': peer_wq[l].astype(jnp.bfloat16),
             'peer_sk_bf': peer_subkeys[l].reshape(2 * P_HEADS, N_KEYS, P_DKEY // 2).astype(jnp.bfloat16),
             'peer_u_bf': peer_u[l].astype(jnp.bfloat16), 'peer_v_bf': peer_v[l].astype(jnp.bfloat16),
             'ln2_g': ln2_g[l], 'ln2_b': ln2_b[l]}
        yp, sp = _hybrid_layer(yp, 0, empty_kv, empty_kv, empty_kv, zero_shift, zero_state, n_keep, p)
        past_cmp = cache_cmp_kv[l, page_table].reshape(bs, past_len, 2, NSA_KV_HEADS, HEAD_DIM)
        past_sel = cache_sel_kv[l, page_table].reshape(bs, past_len, 2, NSA_KV_HEADS, HEAD_DIM)
        ys, ss = _hybrid_layer(ys, past_len, past_cmp, past_sel, state_win_kv[l], state_shift[l],
                               state_rwkv[l], n_keep, p)
        st_p.append(sp)
        st_s.append(ss)
    stk = lambda sts, i: jnp.stack([s[i] for s in sts], axis=0)
    return (yp, ys, stk(st_p, 0), stk(st_p, 1), stk(st_p, 2), stk(st_p, 3), stk(st_p, 4),
            stk(st_s, 0), stk(st_s, 1), stk(st_s, 2), stk(st_s, 3), stk(st_s, 4))
```

```python
import functools
import math

import jax
import jax.numpy as jnp
from jax import lax
from jax.experimental import pallas as pl
from jax.experimental.pallas import tpu as pltpu

D_MODEL = 2048
DEPTH = 2
PAST_LEN = 16384
PAGE_SIZE = 128
HEAD_DIM = 64
D_RWKV = D_MODEL // 2
D_NSA = D_MODEL - D_RWKV
H_RWKV = D_RWKV // HEAD_DIM
H_NSA = D_NSA // HEAD_DIM
NSA_KV_HEADS = 4
NSA_GROUP = H_NSA // NSA_KV_HEADS
NSA_KV_COLS = NSA_KV_HEADS * HEAD_DIM
CMP_BLOCK = 32
SEL_BLOCK = 64
SEL_TOPN = 16
WINDOW = 512
Q_BLOCK = 128
W_LORA = 64
A_LORA = 64
G_LORA = 160
RWKV_COLS = 3 * D_RWKV + W_LORA + A_LORA + G_LORA
NSA_COLS = D_NSA + 6 * NSA_KV_COLS + 3 * H_NSA
IN_COLS = RWKV_COLS + NSA_COLS
P_HEADS = 8
N_KEYS = 128
P_DKEY = 256
P_TOPK = 16
P_TOK_BLOCK = 128
LN_EPS = 1e-5
GN_EPS = 64e-5
DEEPNORM_ALPHA = (2 * DEPTH) ** 0.25
FORCE_SCORE = 1e4
NEG_INF = -1e30

N_EXPERTS = N_KEYS * N_KEYS

LANE = 128
_VMEM_LIMIT = 56 * 1024 * 1024


def _round_up(x, m):
    return -(-x // m) * m


def _mm_kernel(x_ref, w_ref, o_ref):
    o_ref[...] = jnp.dot(x_ref[...].astype(jnp.bfloat16), w_ref[...].astype(jnp.bfloat16),
                         preferred_element_type=jnp.float32)


def _matmul(x, w, *, tm=512, tn=512):
    m, k = x.shape
    n = w.shape[1]
    tm = min(tm, m)
    n_pad = _round_up(n, tn)
    if n_pad != n:
        w = jnp.pad(w, ((0, 0), (0, n_pad - n)))
    out = pl.pallas_call(
        _mm_kernel,
        grid=(m // tm, n_pad // tn),
        in_specs=[pl.BlockSpec((tm, k), lambda i, j: (i, 0)),
                  pl.BlockSpec((k, tn), lambda i, j: (0, j))],
        out_specs=pl.BlockSpec((tm, tn), lambda i, j: (i, j)),
        out_shape=jax.ShapeDtypeStruct((m, n_pad), jnp.float32),
        compiler_params=pltpu.CompilerParams(dimension_semantics=("arbitrary", "arbitrary"),
                                             vmem_limit_bytes=48 * 1024 * 1024),
    )(x, w)
    return out[:, :n] if n_pad != n else out


def _layer_norm(x, g, b):
    mu = x.mean(-1, keepdims=True)
    var = jnp.mean(jnp.square(x - mu), -1, keepdims=True)
    return (x - mu) * lax.rsqrt(var + LN_EPS) * g + b


RW_CHUNK = 64
RW_PAIRS = D_RWKV // LANE


def _bdot(a, b):
    return jnp.dot(a.astype(jnp.bfloat16), b.astype(jnp.bfloat16), preferred_element_type=jnp.float32)


def _bdot_nt(a, b):
    return lax.dot_general(a.astype(jnp.bfloat16), b.astype(jnp.bfloat16), (((1,), (1,)), ((), ())),
                           preferred_element_type=jnp.float32)


def _bdot_tn(a, b):
    return lax.dot_general(a.astype(jnp.bfloat16), b.astype(jnp.bfloat16), (((0,), (0,)), ((), ())),
                           preferred_element_type=jnp.float32)


def _rwkv_chunk_kernel(r_ref, lw_ref, k_ref, v_ref, kk_ref, b_ref, s0_ref, y_ref, st_ref):
    c = pl.program_id(1)
    C = RW_CHUNK
    f32 = jnp.float32

    @pl.when(c == 0)
    def _():
        st_ref[...] = s0_ref[...]

    row = lax.broadcasted_iota(jnp.int32, (C, C), 0)
    col = lax.broadcasted_iota(jnp.int32, (C, C), 1)
    tri_incl = col <= row
    tri_strict = col < row
    tri_f = jnp.where(tri_incl, 1.0, 0.0).astype(f32)
    eye = jnp.where(row == col, 1.0, 0.0).astype(f32)
    lane = lax.broadcasted_iota(jnp.int32, (1, LANE), 1)
    head0 = lane < HEAD_DIM
    r128 = lax.broadcasted_iota(jnp.int32, (LANE, LANE), 0)
    c128 = lax.broadcasted_iota(jnp.int32, (LANE, LANE), 1)
    same_head = (r128 < HEAD_DIM) == (c128 < HEAD_DIM)
    eye128 = r128 == c128

    pairs = range(RW_PAIRS)
    heads = (head0, ~head0)
    sls = [slice(pr * LANE, (pr + 1) * LANE) for pr in pairs]
    lw = [lw_ref[:, sl] for sl in sls]
    cum = [jnp.dot(tri_f, lw[pr], precision=lax.Precision.HIGHEST, preferred_element_type=f32)
           for pr in pairs]
    tot = [cum[pr][C - 1:C, :] for pr in pairs]
    v = [v_ref[:, sl] for sl in sls]
    k = [k_ref[:, sl] for sl in sls]
    b = [b_ref[:, sl] for sl in sls]
    x, ym = [], []
    for pr in pairs:
        e_neg = jnp.exp(-cum[pr])
        x.append(jnp.concatenate([kk_ref[:, sls[pr]] * jnp.exp(cum[pr] - lw[pr]),
                                  r_ref[:, sls[pr]] * jnp.exp(cum[pr])], axis=0))
        ym.append(jnp.concatenate([k[pr] * e_neg, b[pr] * e_neg], axis=0))
    a0 = [st_ref[pr] for pr in pairs]
    xa = [_bdot(x[pr], a0[pr]) for pr in pairs]
    g = [[_bdot_nt(jnp.where(hm, x[pr], 0.0), ym[pr]) for hm in heads] for pr in pairs]
    lkv = [[_bdot(jnp.where(tri_strict, g[pr][h][:C, :C], 0.0), v[pr]) for h in range(2)] for pr in pairs]
    npow = [[jnp.where(tri_strict, -g[pr][h][:C, C:], 0.0) for h in range(2)] for pr in pairs]
    tmat = [[eye + npow[pr][h] for h in range(2)] for pr in pairs]
    for _ in range(int(math.log2(C)) - 1):
        npow = [[_bdot(npow[pr][h], npow[pr][h]) for h in range(2)] for pr in pairs]
        tmat = [[tmat[pr][h] + _bdot(tmat[pr][h], npow[pr][h]) for h in range(2)] for pr in pairs]
    rhs = [xa[pr][:C] + jnp.where(head0, lkv[pr][0], lkv[pr][1]) for pr in pairs]
    w = [jnp.where(head0, _bdot(tmat[pr][0], rhs[pr]), _bdot(tmat[pr][1], rhs[pr])) for pr in pairs]
    vw = [jnp.concatenate([v[pr], w[pr]], axis=0) for pr in pairs]
    for pr in pairs:
        mr = [jnp.concatenate([jnp.where(tri_incl, g[pr][h][C:, :C], 0.0),
                               jnp.where(tri_incl, -g[pr][h][C:, C:], 0.0)], axis=1) for h in range(2)]
        y_ref[:, sls[pr]] = xa[pr][C:] + jnp.where(head0, _bdot(mr[0], vw[pr]), _bdot(mr[1], vw[pr]))
    for pr in pairs:
        e_rem = jnp.exp(tot[pr] - cum[pr])
        kb = jnp.concatenate([k[pr] * e_rem, -(b[pr] * e_rem)], axis=0)
        upd = _bdot_tn(kb, vw[pr])
        p_col = jnp.sum(jnp.where(eye128, jnp.exp(tot[pr]), 0.0), axis=1, keepdims=True)
        st_ref[pr] = a0[pr] * p_col + jnp.where(same_head, upd, 0.0)


def _rwkv_scan(r, lw, k, v, kk, b, s0):
    B, T, D = r.shape
    C = RW_CHUNK
    a = jnp.swapaxes(s0, -1, -2).reshape(B, RW_PAIRS, 2, HEAD_DIM, HEAD_DIM)
    z = jnp.zeros_like(a[:, :, 0])
    a0 = jnp.concatenate([jnp.concatenate([a[:, :, 0], z], axis=-1),
                          jnp.concatenate([z, a[:, :, 1]], axis=-1)], axis=-2)
    seq_spec = pl.BlockSpec((None, C, D), lambda bi, ci: (bi, ci, 0))
    st_spec = pl.BlockSpec((None, RW_PAIRS, LANE, LANE), lambda bi, ci: (bi, 0, 0, 0))
    y, st = pl.pallas_call(
        _rwkv_chunk_kernel,
        grid=(B, T // C),
        in_specs=[seq_spec] * 6 + [st_spec],
        out_specs=[seq_spec, st_spec],
        out_shape=[jax.ShapeDtypeStruct((B, T, D), jnp.float32),
                   jax.ShapeDtypeStruct((B, RW_PAIRS, LANE, LANE), jnp.float32)],
        compiler_params=pltpu.CompilerParams(dimension_semantics=("arbitrary", "arbitrary"),
                                             vmem_limit_bytes=_VMEM_LIMIT),
        name="rwkv_scan",
    )(r, lw, k, v, kk, b, a0)
    s_t = jnp.stack([st[:, :, :HEAD_DIM, :HEAD_DIM], st[:, :, HEAD_DIM:, HEAD_DIM:]], axis=2)
    return y, jnp.swapaxes(s_t.reshape(B, H_RWKV, HEAD_DIM, HEAD_DIM), -1, -2)


def _rwkv_time_mix(xr, shift_prev, s0, p):
    B, T, _ = xr.shape
    dt = xr.dtype
    prev = jnp.concatenate([shift_prev.astype(dt), xr[:, :-1]], axis=1)
    xs = xr + p['rwkv_mu'] * (prev - xr)
    cuts = [D_RWKV, 2 * D_RWKV, 3 * D_RWKV, 3 * D_RWKV + W_LORA, 3 * D_RWKV + W_LORA + A_LORA]
    r, k, v, wl, al, gl = jnp.split(xs, cuts, axis=-1)
    w = -jax.nn.softplus(-(p['rwkv_w0'] + jnp.tanh(wl) @ p['rwkv_w2'])) - 0.5
    lw = -jnp.exp(w)
    a = jax.nn.sigmoid(p['rwkv_a0'] + al @ p['rwkv_a2'])
    g = jax.nn.sigmoid(gl) @ p['rwkv_g2']
    hd = lambda t: t.reshape(B, T, H_RWKV, HEAD_DIM)
    kkf = hd(k * p['rwkv_k_k']).astype(jnp.float32)
    kk = (kkf * lax.rsqrt(jnp.maximum(jnp.sum(kkf * kkf, -1, keepdims=True), 1e-24))).astype(dt)
    kk = kk.reshape(B, T, D_RWKV)
    k = k * (1.0 + (a - 1.0) * p['rwkv_k_a'])
    t_pad = _round_up(T, RW_CHUNK)
    padt = lambda t: jnp.pad(t, ((0, 0), (0, t_pad - T), (0, 0)))
    y, s_T = _rwkv_scan(padt(r), padt(lw), padt(k), padt(v), padt(kk), padt(kk * a), s0.astype(dt))
    y = hd(y[:, :T]).astype(jnp.float32)
    r, k, v = hd(r), hd(k), hd(v)
    mu = y.mean(-1, keepdims=True)
    var = jnp.mean(jnp.square(y - mu), -1, keepdims=True)
    yn = ((y - mu) * lax.rsqrt(var + GN_EPS)).reshape(B, T, D_RWKV) * p['rwkv_gn_g'] + p['rwkv_gn_b']
    bonus = jnp.sum(r * k * p['rwkv_r_k'], -1, keepdims=True) * v
    out = (yn.astype(dt) + bonus.reshape(B, T, D_RWKV)) * g
    return out, s_T, xr[:, -1:]


def _extract_top(buf_ref, out_ref, nrows, width):
    iota = lax.broadcasted_iota(jnp.int32, (nrows, width), 0)

    def body(r, carry):
        s = buf_ref[...]
        mx = jnp.max(s, axis=0, keepdims=True)
        first = jnp.min(jnp.where(s == mx, iota, nrows), axis=0, keepdims=True)
        buf_ref[...] = jnp.where(iota == first, -jnp.inf, s)
        out_ref[pl.ds(r, 1), :] = mx
        return carry

    lax.fori_loop(0, P_TOPK, body, 0)


def _peer_score_kernel(x_ref, wq_ref, sk_ref, s1_ref, s2_ref, st_ref, buf, cand, hv0, hv1, tv):
    tn = x_ref.shape[0]
    q = jnp.dot(x_ref[...].astype(jnp.bfloat16), wq_ref[...],
                preferred_element_type=jnp.float32).astype(jnp.bfloat16)
    for h in range(P_HEADS):
        for c, (s_ref, hv) in enumerate(((s1_ref, hv0), (s2_ref, hv1))):
            col = (2 * h + c) * N_KEYS
            s_t = lax.dot_general(sk_ref[2 * h + c], q[:, col:col + N_KEYS],
                                  (((1,), (1,)), ((), ())), preferred_element_type=jnp.float32)
            s_ref[h] = s_t
            buf[...] = s_t
            _extract_top(buf, hv, N_KEYS, tn)
        for a in range(P_TOPK):
            cand[a * P_TOPK:(a + 1) * P_TOPK, :] = hv0[a:a + 1, :] + hv1[...]
        _extract_top(cand, tv, P_TOPK * P_TOPK, tn)
        z = jnp.sum(jnp.exp(tv[...] - tv[0:1, :]), axis=0, keepdims=True)
        st_ref[0, h:h + 1, :] = tv[P_TOPK - 1:P_TOPK, :]
        st_ref[1, h:h + 1, :] = hv0[0:1, :]
        st_ref[2, h:h + 1, :] = hv1[0:1, :]
        st_ref[3, h:h + 1, :] = 1.0 / z


def _peer_scores(x, wq_bf, sk_bf, *, tn=256):
    n = x.shape[0]
    tn = min(tn, n)
    return pl.pallas_call(
        _peer_score_kernel,
        grid=(n // tn,),
        in_specs=[pl.BlockSpec((tn, D_MODEL), lambda i: (i, 0)),
                  pl.BlockSpec((D_MODEL, P_HEADS * P_DKEY), lambda i: (0, 0)),
                  pl.BlockSpec((2 * P_HEADS, N_KEYS, P_DKEY // 2), lambda i: (0, 0, 0))],
        out_specs=[pl.BlockSpec((P_HEADS, N_KEYS, tn), lambda i: (0, 0, i)),
                   pl.BlockSpec((P_HEADS, N_KEYS, tn), lambda i: (0, 0, i)),
                   pl.BlockSpec((4, P_HEADS, tn), lambda i: (0, 0, i))],
        out_shape=[jax.ShapeDtypeStruct((P_HEADS, N_KEYS, n), jnp.float32),
                   jax.ShapeDtypeStruct((P_HEADS, N_KEYS, n), jnp.float32),
                   jax.ShapeDtypeStruct((4, P_HEADS, n), jnp.float32)],
        scratch_shapes=[pltpu.VMEM((N_KEYS, tn), jnp.float32),
                        pltpu.VMEM((P_TOPK * P_TOPK, tn), jnp.float32),
                        pltpu.VMEM((P_TOPK, tn), jnp.float32),
                        pltpu.VMEM((P_TOPK, tn), jnp.float32),
                        pltpu.VMEM((P_TOPK, tn), jnp.float32)],
        compiler_params=pltpu.CompilerParams(dimension_semantics=("arbitrary",),
                                             vmem_limit_bytes=_VMEM_LIMIT),
        name="peer_scores",
    )(x, wq_bf, sk_bf)


def _peer_expert_kernel(x_ref, u_ref, v_ref, s1_ref, s2_ref, st_ref, o_ref, e2_scr, h_scr, p_scr,
                        *, chunk):
    j = pl.program_id(1)
    tn = x_ref.shape[0]
    te = u_ref.shape[0]

    @pl.when(j == 0)
    def _():
        o_ref[...] = jnp.zeros_like(o_ref)
        for h in range(P_HEADS):
            e2_scr[h] = jnp.exp(s2_ref[h] - st_ref[2, h:h + 1, :])

    h_scr[...] = lax.dot_general(x_ref[...], u_ref[...], (((1,), (1,)), ((), ())),
                                 preferred_element_type=jnp.float32)

    def chunk_body(c, carry):
        tsl = pl.ds(pl.multiple_of(c * chunk, chunk), chunk)
        for ii in range(te // N_KEYS):
            i = j * (te // N_KEYS) + ii
            w_t = jnp.zeros((N_KEYS, chunk), jnp.float32)
            for h in range(P_HEADS):
                a = s1_ref[h, pl.ds(i, 1), tsl]
                e1 = jnp.exp(a - st_ref[1, h:h + 1, tsl]) * st_ref[3, h:h + 1, tsl]
                val = a + s2_ref[h, :, tsl]
                w_t = w_t + jnp.where(val >= st_ref[0, h:h + 1, tsl], e1 * e2_scr[h, :, tsl], 0.0)
            hh = h_scr[tsl, ii * N_KEYS:(ii + 1) * N_KEYS]
            g = 0.5 * hh * (1.0 + lax.erf(hh * 0.7071067811865476))
            p_scr[tsl, ii * N_KEYS:(ii + 1) * N_KEYS] = (w_t.T * g).astype(jnp.bfloat16)
        return carry

    lax.fori_loop(0, tn // chunk, chunk_body, 0)
    o_ref[...] += jnp.dot(p_scr[...], v_ref[...], preferred_element_type=jnp.float32)


def _peer_experts(x_bf, u_bf, v_bf, s1t, s2t, stats, *, tn=1024, te=512, chunk=256):
    n = x_bf.shape[0]
    tn = min(tn, n)
    chunk = min(chunk, tn)
    once = pl.Buffered(1)
    return pl.pallas_call(
        functools.partial(_peer_expert_kernel, chunk=chunk),
        grid=(n // tn, N_EXPERTS // te),
        in_specs=[pl.BlockSpec((tn, D_MODEL), lambda i, j: (i, 0), pipeline_mode=once),
                  pl.BlockSpec((te, D_MODEL), lambda i, j: (j, 0)),
                  pl.BlockSpec((te, D_MODEL), lambda i, j: (j, 0)),
                  pl.BlockSpec((P_HEADS, N_KEYS, tn), lambda i, j: (0, 0, i), pipeline_mode=once),
                  pl.BlockSpec((P_HEADS, N_KEYS, tn), lambda i, j: (0, 0, i), pipeline_mode=once),
                  pl.BlockSpec((4, P_HEADS, tn), lambda i, j: (0, 0, i), pipeline_mode=once)],
        out_specs=pl.BlockSpec((tn, D_MODEL), lambda i, j: (i, 0)),
        out_shape=jax.ShapeDtypeStruct((n, D_MODEL), jnp.float32),
        scratch_shapes=[pltpu.VMEM((P_HEADS, N_KEYS, tn), jnp.float32),
                        pltpu.VMEM((tn, te), jnp.float32),
                        pltpu.VMEM((tn, te), jnp.bfloat16)],
        compiler_params=pltpu.CompilerParams(dimension_semantics=("arbitrary", "arbitrary"),
                                             vmem_limit_bytes=_VMEM_LIMIT),
        name="peer_experts",
    )(x_bf, u_bf, v_bf, s1t, s2t, stats)


def _peer_ffn(x, wq_bf, sk_bf, u_bf, v_bf):
    B, T, D = x.shape
    n = B * T
    xt = x.reshape(n, D)
    n_pad = _round_up(n, LANE)
    if n_pad != n:
        xt = jnp.pad(xt, ((0, n_pad - n), (0, 0)))
    s1t, s2t, stats = _peer_scores(xt, wq_bf, sk_bf)
    out = _peer_experts(xt.astype(jnp.bfloat16), u_bf, v_bf, s1t, s2t, stats)
    return out[:n].reshape(B, T, D)


NSA_TQ = 256


def _flash_tile(q4, k_t, v_t, mask4, m, l, acc):
    s = lax.dot_general(q4, k_t, (((1,), (1,)), ((), ())), preferred_element_type=jnp.float32)
    s = jnp.where(mask4, s, NEG_INF)
    m_new = jnp.maximum(m, jnp.max(s, axis=-1, keepdims=True))
    p = jnp.where(mask4, jnp.exp(s - m_new), 0.0)
    alpha = jnp.exp(m - m_new)
    l_new = alpha * l + jnp.sum(p, axis=-1, keepdims=True)
    acc_new = alpha * acc + jnp.dot(p.astype(jnp.bfloat16), v_t, preferred_element_type=jnp.float32)
    return m_new, l_new, acc_new


def _nsa_prompt_kernel(q_ref, gl_ref, kc_ref, vc_ref, ks_ref, vs_ref, kw_ref, vw_ref, cw_ref, ex_ref,
                       o_ref, kcb_scr, vcb_scr, imp_scr, selx_scr, *, seq):
    qt = pl.program_id(2)
    tq = NSA_TQ
    n_cmp = seq // CMP_BLOCK
    n_sel = seq // SEL_BLOCK
    G = NSA_GROUP
    q0 = qt * tq

    @pl.when(qt == 0)
    def _():
        kc = kc_ref[...].reshape(n_cmp, CMP_BLOCK, HEAD_DIM)
        vc = vc_ref[...].reshape(n_cmp, CMP_BLOCK, HEAD_DIM)
        kcb_scr[...] = jnp.zeros_like(kcb_scr)
        vcb_scr[...] = jnp.zeros_like(vcb_scr)
        kcb_scr[0:n_cmp, :] = jnp.sum(kc * cw_ref[0][None], axis=1)
        vcb_scr[0:n_cmp, :] = jnp.sum(vc * cw_ref[1][None], axis=1).astype(jnp.bfloat16)

    qpos_col = q0 + lax.broadcasted_iota(jnp.int32, (tq, 1), 0)
    qf = [q_ref[g] * (HEAD_DIM ** -0.5) for g in range(G)]
    qs = [t.astype(jnp.bfloat16) for t in qf]
    gates = jax.nn.sigmoid(gl_ref[...])

    cmp_end = (lax.broadcasted_iota(jnp.int32, (1, LANE), 1) + 1) * CMP_BLOCK - 1
    cmask = (cmp_end <= qpos_col) & (lax.broadcasted_iota(jnp.int32, (1, LANE), 1) < n_cmp)
    o_cmp = []
    imp = jnp.zeros((tq, LANE), jnp.float32)
    for g in range(G):
        s = lax.dot_general(qf[g], kcb_scr[...], (((1,), (1,)), ((), ())),
                            precision=lax.Precision.HIGHEST, preferred_element_type=jnp.float32)
        s = jnp.where(cmask, s, NEG_INF)
        e = jnp.exp(s - jnp.max(s, axis=-1, keepdims=True))
        p = jnp.where(cmask, e / jnp.sum(e, axis=-1, keepdims=True), 0.0)
        o_cmp.append(jnp.dot(p.astype(jnp.bfloat16), vcb_scr[...], preferred_element_type=jnp.float32))
        imp = imp + p
    imp_t = imp.T
    ratio = SEL_BLOCK // CMP_BLOCK
    parts = []
    for c in range(tq // LANE):
        imp_scr[c] = imp_t[:, c * LANE:(c + 1) * LANE]
        part = imp_scr[c, pl.ds(0, n_sel, stride=ratio), :]
        for r in range(1, ratio):
            part = part + imp_scr[c, pl.ds(r, n_sel, stride=ratio), :]
        parts.append(part)
    imp_s = jnp.concatenate(parts, axis=1)
    qpos_row = q0 + lax.broadcasted_iota(jnp.int32, (1, tq), 1)
    cur = qpos_row // SEL_BLOCK
    blk = lax.broadcasted_iota(jnp.int32, (n_sel, tq), 0)
    forced = (blk == 0) | (blk == cur) | (blk == cur - 1)
    imp_s = jnp.where(blk > cur, -1.0, jnp.where(forced, FORCE_SCORE, imp_s))
    rank = jnp.zeros((n_sel, tq), jnp.int32)
    for mm in range(n_sel):
        row = imp_s[mm:mm + 1, :]
        ahead = (row > imp_s) | ((row == imp_s) & (mm < blk))
        rank = rank + jnp.where(ahead, 1, 0)
    sel = jnp.where((rank < min(SEL_TOPN, n_sel)) & (imp_s >= 0.0), 1.0, 0.0)
    sel_pad = jnp.concatenate([sel, jnp.zeros((LANE - n_sel, tq), jnp.float32)], axis=0)
    sel_q = sel_pad.T.astype(jnp.bfloat16)
    selx_scr[...] = jnp.dot(sel_q, ex_ref[...], preferred_element_type=jnp.float32)

    q4 = jnp.concatenate(qs, axis=0)
    qpos4 = jnp.concatenate([qpos_col] * G, axis=0)
    kpos_l = lax.broadcasted_iota(jnp.int32, (1, tq), 1)
    init = (jnp.full((G * tq, 1), NEG_INF, jnp.float32), jnp.zeros((G * tq, 1), jnp.float32),
            jnp.zeros((G * tq, HEAD_DIM), jnp.float32))

    def sel_body(kt, carry):
        k0 = pl.multiple_of(kt * tq, tq)
        chosen = selx_scr[:, pl.ds(k0, tq)] > 0.5
        chosen4 = jnp.concatenate([chosen] * G, axis=0)
        mask4 = chosen4 & ((k0 + kpos_l) <= qpos4)
        return _flash_tile(q4, ks_ref[pl.ds(k0, tq), :], vs_ref[pl.ds(k0, tq), :], mask4, *carry)

    _, l_s, acc_s = lax.fori_loop(0, qt + 1, sel_body, init)
    o_sel = acc_s / l_s

    def win_body(kt, carry):
        k0 = pl.multiple_of(kt * tq, tq)
        kpos = k0 + kpos_l
        mask4 = (kpos <= qpos4) & (kpos >= qpos4 - (WINDOW - 1))
        return _flash_tile(q4, kw_ref[pl.ds(k0, tq), :], vw_ref[pl.ds(k0, tq), :], mask4, *carry)

    _, l_w, acc_w = lax.fori_loop(jnp.maximum(qt - WINDOW // tq, 0), qt + 1, win_body, init)
    o_win = acc_w / l_w

    for g in range(G):
        rows = slice(g * tq, (g + 1) * tq)
        o_ref[g] = (gates[:, 3 * g:3 * g + 1] * o_cmp[g] + gates[:, 3 * g + 1:3 * g + 2] * o_sel[rows]
                    + gates[:, 3 * g + 2:3 * g + 3] * o_win[rows])


def _nsa_prompt(xn, cmp_w):
    B, T, _ = xn.shape
    KVH, G, HD = NSA_KV_HEADS, NSA_GROUP, HEAD_DIM
    kvc = NSA_KV_COLS
    tq = NSA_TQ
    q = xn[..., :D_NSA].reshape(B, T, KVH, G, HD).transpose(0, 2, 3, 1, 4)
    kv = [xn[..., D_NSA + i * kvc:D_NSA + (i + 1) * kvc].reshape(B, T, KVH, HD).transpose(0, 2, 1, 3)
          for i in range(6)]
    kc, vc = kv[0], kv[1]
    ks, vs, kw, vw = [t.astype(jnp.bfloat16) for t in kv[2:]]
    gl = xn[..., D_NSA + 6 * kvc:].reshape(B, T, KVH, 3 * G).transpose(0, 2, 1, 3)
    cw = jnp.broadcast_to(cmp_w[:, :, None], (2, CMP_BLOCK, HD))
    expand = (jnp.arange(LANE)[:, None] == (jnp.arange(T) // SEL_BLOCK)[None, :]).astype(jnp.bfloat16)
    kv_spec = pl.BlockSpec((None, None, T, HD), lambda b, k, t: (b, k, 0, 0))
    out = pl.pallas_call(
        functools.partial(_nsa_prompt_kernel, seq=T),
        grid=(B, KVH, T // tq),
        in_specs=[pl.BlockSpec((None, None, G, tq, HD), lambda b, k, t: (b, k, 0, t, 0)),
                  pl.BlockSpec((None, None, tq, 3 * G), lambda b, k, t: (b, k, t, 0)),
                  kv_spec, kv_spec, kv_spec, kv_spec, kv_spec, kv_spec,
                  pl.BlockSpec((2, CMP_BLOCK, HD), lambda b, k, t: (0, 0, 0)),
                  pl.BlockSpec((LANE, T), lambda b, k, t: (0, 0))],
        out_specs=pl.BlockSpec((None, None, G, tq, HD), lambda b, k, t: (b, k, 0, t, 0)),
        out_shape=jax.ShapeDtypeStruct((B, KVH, G, T, HD), jnp.float32),
        scratch_shapes=[pltpu.VMEM((LANE, HD), jnp.float32),
                        pltpu.VMEM((LANE, HD), jnp.bfloat16),
                        pltpu.VMEM((tq // LANE, LANE, LANE), jnp.float32),
                        pltpu.VMEM((tq, T), jnp.float32)],
        compiler_params=pltpu.CompilerParams(dimension_semantics=("arbitrary", "arbitrary", "arbitrary"),
                                             vmem_limit_bytes=_VMEM_LIMIT),
        name="nsa_prompt",
    )(q, gl, kc, vc, ks, vs, kw, vw, cw, expand)
    return out.transpose(0, 3, 1, 2, 4).reshape(B, T, D_NSA)


def _nsa_prompt_mix(xn, n_keep, cmp_w):
    B, T, _ = xn.shape
    kv5 = lambda i: xn[..., D_NSA + 2 * i * NSA_KV_COLS:D_NSA + 2 * (i + 1) * NSA_KV_COLS].reshape(
        B, T, 2, NSA_KV_HEADS, HEAD_DIM)
    return _nsa_prompt(xn, cmp_w), kv5(0), kv5(1), kv5(2)[:, -n_keep:]


NSA_PG = 4
PAGE_COLS = 2 * NSA_KV_COLS
NSA_ROWS = NSA_KV_HEADS * NSA_GROUP


def _page_specs(layer, n):
    def spec(j):
        return pl.BlockSpec((None, None, PAGE_SIZE, PAGE_COLS),
                            lambda b, s, pt: (layer, pt[b, s * n + j], 0, 0))
    return [spec(j) for j in range(n)]


def _cmp_pool_kernel(pt_ref, *refs):
    pages, cw_ref, o_ref = refs[:NSA_PG], refs[NSA_PG], refs[NSA_PG + 1]
    per_page = PAGE_SIZE // CMP_BLOCK
    out = []
    for pg in pages:
        blocks = pg[...].reshape(per_page, CMP_BLOCK, PAGE_COLS)
        out.append(jnp.sum(blocks * cw_ref[...][None], axis=1))
    o_ref[...] = jnp.concatenate(out, axis=0)


def _cmp_pool(cache, page_table, layer, cw):
    B, n_pages = page_table.shape
    per_step = NSA_PG * PAGE_SIZE // CMP_BLOCK
    return pl.pallas_call(
        _cmp_pool_kernel,
        grid_spec=pltpu.PrefetchScalarGridSpec(
            num_scalar_prefetch=1, grid=(B, n_pages // NSA_PG),
            in_specs=_page_specs(layer, NSA_PG) + [pl.BlockSpec((CMP_BLOCK, PAGE_COLS), lambda b, s, pt: (0, 0))],
            out_specs=pl.BlockSpec((None, per_step, PAGE_COLS), lambda b, s, pt: (b, s, 0))),
        out_shape=jax.ShapeDtypeStruct((B, n_pages * PAGE_SIZE // CMP_BLOCK, PAGE_COLS), jnp.float32),
        compiler_params=pltpu.CompilerParams(dimension_semantics=("arbitrary", "arbitrary"),
                                             vmem_limit_bytes=_VMEM_LIMIT),
        name="nsa_cmp_pool",
    )(page_table, *([cache] * NSA_PG), cw)


def _cmp_select_kernel(q_ref, kv_ref, ocmp_ref, sel_ref, pt_scr, imp_scr, *, past, steps):
    T = steps
    R = NSA_ROWS * T
    n_cmp = past // CMP_BLOCK
    n_sel = past // SEL_BLOCK + 1
    n_lane = sel_ref.shape[-1]
    ratio = SEL_BLOCK // CMP_BLOCK
    rows_t = lax.broadcasted_iota(jnp.int32, (R, 1), 0) % T
    qpos = past + rows_t
    cmp_end = (lax.broadcasted_iota(jnp.int32, (1, n_cmp), 1) + 1) * CMP_BLOCK - 1
    cmask = cmp_end <= qpos
    q = q_ref[...]
    p_all, o_all = [], []
    for kh in range(NSA_KV_HEADS):
        rows = slice(kh * NSA_GROUP * T, (kh + 1) * NSA_GROUP * T)
        kcb = kv_ref[:, kh * HEAD_DIM:(kh + 1) * HEAD_DIM]
        vcb = kv_ref[:, NSA_KV_COLS + kh * HEAD_DIM:NSA_KV_COLS + (kh + 1) * HEAD_DIM].astype(jnp.bfloat16)
        s = lax.dot_general(q[rows], kcb, (((1,), (1,)), ((), ())), precision=lax.Precision.HIGHEST,
                            preferred_element_type=jnp.float32)
        s = jnp.where(cmask[rows], s, NEG_INF)
        e = jnp.exp(s - jnp.max(s, axis=-1, keepdims=True))
        p = jnp.where(cmask[rows], e / jnp.sum(e, axis=-1, keepdims=True), 0.0)
        o_all.append(jnp.dot(p.astype(jnp.bfloat16), vcb, preferred_element_type=jnp.float32))
        imp = p[0:T]
        for g in range(1, NSA_GROUP):
            imp = imp + p[g * T:(g + 1) * T]
        p_all.extend([imp] * NSA_GROUP)
    ocmp_ref[...] = jnp.concatenate(o_all, axis=0)
    imp_rows = jnp.concatenate(p_all + [jnp.zeros((LANE - R, n_cmp), jnp.float32)], axis=0)
    for c in range(n_cmp // LANE):
        pt_scr[c] = imp_rows[:, c * LANE:(c + 1) * LANE].T
    per = LANE // ratio
    for c in range(n_cmp // LANE):
        part = pt_scr[c, pl.ds(0, per, stride=ratio), :]
        for r in range(1, ratio):
            part = part + pt_scr[c, pl.ds(r, per, stride=ratio), :]
        imp_scr[c * per:(c + 1) * per, :] = part
    n_rows = imp_scr.shape[0]
    imp_scr[n_sel - 1:n_rows, :] = jnp.zeros((n_rows - n_sel + 1, LANE), jnp.float32)
    lane_t = lax.broadcasted_iota(jnp.int32, (1, LANE), 1) % T
    cur = (past + lane_t) // SEL_BLOCK
    blk = lax.broadcasted_iota(jnp.int32, (n_rows, LANE), 0)
    forced = (blk == 0) | (blk == cur) | (blk == cur - 1)
    imp_s = jnp.where(blk > cur, -1.0, jnp.where(forced, FORCE_SCORE, imp_scr[...]))
    imp_s = jnp.where(blk < n_sel, imp_s, -2.0)
    imp_scr[...] = imp_s

    def rank_body(m, rank):
        row = imp_scr[pl.ds(m, 1), :]
        ahead = (row > imp_s) | ((row == imp_s) & (m < blk))
        return rank + jnp.where(ahead, 1, 0)

    rank = lax.fori_loop(0, n_sel, rank_body, jnp.zeros((n_rows, LANE), jnp.int32))
    sel = jnp.where((rank < min(SEL_TOPN, n_sel)) & (imp_s >= 0.0), 1.0, 0.0)
    sel = jnp.concatenate([sel, jnp.zeros((n_lane - n_rows, LANE), jnp.float32)], axis=0)
    for c in range(n_lane // LANE):
        sel_ref[:, c * LANE:(c + 1) * LANE] = sel[c * LANE:(c + 1) * LANE, :].T


def _cmp_select(q_rows, kvcb, past, steps):
    B, R, _ = q_rows.shape
    n_cmp = kvcb.shape[1]
    n_sel = past // SEL_BLOCK + 1
    n_lane = _round_up(n_sel, LANE)
    n_rows = _round_up(n_sel, 8)
    return pl.pallas_call(
        functools.partial(_cmp_select_kernel, past=past, steps=steps),
        grid=(B,),
        in_specs=[pl.BlockSpec((None, R, HEAD_DIM), lambda b: (b, 0, 0)),
                  pl.BlockSpec((None, n_cmp, PAGE_COLS), lambda b: (b, 0, 0))],
        out_specs=[pl.BlockSpec((None, R, HEAD_DIM), lambda b: (b, 0, 0)),
                   pl.BlockSpec((None, LANE, n_lane), lambda b: (b, 0, 0))],
        out_shape=[jax.ShapeDtypeStruct((B, R, HEAD_DIM), jnp.float32),
                   jax.ShapeDtypeStruct((B, LANE, n_lane), jnp.float32)],
        scratch_shapes=[pltpu.VMEM((n_cmp // LANE, LANE, LANE), jnp.float32),
                        pltpu.VMEM((n_rows, LANE), jnp.float32)],
        compiler_params=pltpu.CompilerParams(dimension_semantics=("arbitrary",),
                                             vmem_limit_bytes=_VMEM_LIMIT),
        name="nsa_cmp_select",
    )(q_rows, kvcb)


def _soft_update(s, mask, v, m, l, acc):
    s = jnp.where(mask, s, NEG_INF)
    m_new = jnp.maximum(m, jnp.max(s, axis=-1, keepdims=True))
    p = jnp.where(mask, jnp.exp(s - m_new), 0.0)
    alpha = jnp.exp(m - m_new)
    return (m_new, alpha * l + jnp.sum(p, axis=-1, keepdims=True),
            alpha * acc + jnp.dot(p.astype(jnp.bfloat16), v, preferred_element_type=jnp.float32))


def _sel_win_kernel(pt_ref, *refs, past, steps):
    pages = refs[:NSA_PG]
    (q_ref, sel_ref, ocmp_ref, nsel_ref, wpre_ref, nwin_ref, gate_ref,
     o_ref, m_scr, l_scr, acc_scr) = refs[NSA_PG:]
    s_id = pl.program_id(1)
    T = steps
    GT = NSA_GROUP * T
    R = NSA_ROWS * T
    n_lane = sel_ref.shape[-1]
    rows_t = lax.broadcasted_iota(jnp.int32, (GT, 1), 0) % T

    @pl.when(s_id == 0)
    def _():
        m_scr[...] = jnp.full_like(m_scr, NEG_INF)
        l_scr[...] = jnp.zeros_like(l_scr)
        acc_scr[...] = jnp.zeros_like(acc_scr)

    q = q_ref[...].astype(jnp.bfloat16)
    sel_bf = sel_ref[0:R, :].astype(jnp.bfloat16)
    n_keys = NSA_PG * PAGE_SIZE
    blk_id = lax.broadcasted_iota(jnp.int32, (n_lane, n_keys), 0)
    key_blk = s_id * (n_keys // SEL_BLOCK) + lax.broadcasted_iota(jnp.int32, (n_lane, n_keys), 1) // SEL_BLOCK

    def head_cols(x, kh, off):
        return x[:, off + kh * HEAD_DIM:off + (kh + 1) * HEAD_DIM]

    kv = jnp.concatenate([pg[...] for pg in pages], axis=0).astype(jnp.bfloat16)
    expand = jnp.where(blk_id == key_blk, 1.0, 0.0).astype(jnp.bfloat16)
    chosen = jnp.dot(sel_bf, expand, preferred_element_type=jnp.float32) > 0.5
    head_rows = [slice(kh * GT, (kh + 1) * GT) for kh in range(NSA_KV_HEADS)]
    scores = [lax.dot_general(q[rows], head_cols(kv, kh, 0), (((1,), (1,)), ((), ())),
                              preferred_element_type=jnp.float32) for kh, rows in enumerate(head_rows)]
    upd = [_soft_update(scores[kh], chosen[rows], head_cols(kv, kh, NSA_KV_COLS),
                        m_scr[rows], l_scr[rows], acc_scr[rows]) for kh, rows in enumerate(head_rows)]
    for kh, rows in enumerate(head_rows):
        m_scr[rows], l_scr[rows], acc_scr[rows] = upd[kh]

    @pl.when(s_id == pl.num_programs(1) - 1)
    def _():
        new_blk = past // SEL_BLOCK
        tpad = nsel_ref.shape[0]
        jn = lax.broadcasted_iota(jnp.int32, (1, tpad), 1)
        nsel = nsel_ref[...].astype(jnp.bfloat16)
        nwin = nwin_ref[...].astype(jnp.bfloat16)
        wpre = wpre_ref[...].astype(jnp.bfloat16)
        jp = lax.broadcasted_iota(jnp.int32, (1, WINDOW), 1)
        gates = jax.nn.sigmoid(gate_ref[...])
        for kh in range(NSA_KV_HEADS):
            rows = slice(kh * GT, (kh + 1) * GT)
            pick = sel_ref[kh * GT:(kh + 1) * GT, new_blk:new_blk + 1] > 0.5
            s = lax.dot_general(q[rows], head_cols(nsel, kh, 0), (((1,), (1,)), ((), ())),
                                preferred_element_type=jnp.float32)
            m, l, acc = _soft_update(s, pick & (jn <= rows_t) & (jn < T), head_cols(nsel, kh, NSA_KV_COLS),
                                     m_scr[rows], l_scr[rows], acc_scr[rows])
            o_sel = acc / l
            init = (jnp.full((GT, 1), NEG_INF, jnp.float32), jnp.zeros((GT, 1), jnp.float32),
                    jnp.zeros((GT, HEAD_DIM), jnp.float32))
            s = lax.dot_general(q[rows], head_cols(wpre, kh, 0), (((1,), (1,)), ((), ())),
                                preferred_element_type=jnp.float32)
            st = _soft_update(s, jp > rows_t, head_cols(wpre, kh, NSA_KV_COLS), *init)
            s = lax.dot_general(q[rows], head_cols(nwin, kh, 0), (((1,), (1,)), ((), ())),
                                preferred_element_type=jnp.float32)
            _, l_w, acc_w = _soft_update(s, (jn <= rows_t) & (jn < T), head_cols(nwin, kh, NSA_KV_COLS), *st)
            o_win = acc_w / l_w
            gt = gates[rows]
            o_ref[rows, :] = gt[:, 0:1] * ocmp_ref[rows, :] + gt[:, 1:2] * o_sel + gt[:, 2:3] * o_win


def _sel_win(cache, page_table, layer, q_rows, sel_rows, o_cmp, new_sel, win_prefix, new_win, gate_rows,
             past, steps):
    B, n_pages = page_table.shape
    R = q_rows.shape[1]
    n_lane = sel_rows.shape[-1]
    tpad = new_sel.shape[1]
    per_b = lambda shape: pl.BlockSpec((None,) + shape, lambda b, s, pt: (b,) + (0,) * len(shape))
    return pl.pallas_call(
        functools.partial(_sel_win_kernel, past=past, steps=steps),
        grid_spec=pltpu.PrefetchScalarGridSpec(
            num_scalar_prefetch=1, grid=(B, n_pages // NSA_PG),
            in_specs=_page_specs(layer, NSA_PG) + [
                per_b((R, HEAD_DIM)), per_b((LANE, n_lane)), per_b((R, HEAD_DIM)),
                per_b((tpad, PAGE_COLS)), per_b((WINDOW, PAGE_COLS)), per_b((tpad, PAGE_COLS)),
                per_b((R, 3))],
            out_specs=per_b((R, HEAD_DIM)),
            scratch_shapes=[pltpu.VMEM((R, 1), jnp.float32), pltpu.VMEM((R, 1), jnp.float32),
                            pltpu.VMEM((R, HEAD_DIM), jnp.float32)]),
        out_shape=jax.ShapeDtypeStruct((B, R, HEAD_DIM), jnp.float32),
        compiler_params=pltpu.CompilerParams(dimension_semantics=("arbitrary", "arbitrary"),
                                             vmem_limit_bytes=_VMEM_LIMIT),
        name="nsa_sel_win",
    )(page_table, *([cache] * NSA_PG), q_rows, sel_rows, o_cmp, new_sel, win_prefix, new_win, gate_rows)


def _nsa_decode_mix(xn, cache_cmp, cache_sel, page_table, layer, win_prefix, cmp_w):
    B, T, _ = xn.shape
    n_pages = page_table.shape[1]
    past = n_pages * PAGE_SIZE
    assert T <= SEL_BLOCK and n_pages % NSA_PG == 0 and win_prefix.shape[1] == WINDOW
    assert (past // CMP_BLOCK) % LANE == 0
    KVH, G, HD = NSA_KV_HEADS, NSA_GROUP, HEAD_DIM
    paged = lambda c: c.reshape(c.shape[0], c.shape[1], PAGE_SIZE, PAGE_COLS)
    kv5 = lambda i: xn[..., D_NSA + 2 * i * NSA_KV_COLS:D_NSA + 2 * (i + 1) * NSA_KV_COLS]
    new_cmp, new_sel, new_win = kv5(0), kv5(1), kv5(2)
    q_rows = (xn[..., :D_NSA] * (HD ** -0.5)).reshape(B, T, KVH, G, HD).transpose(0, 2, 3, 1, 4)
    q_rows = q_rows.reshape(B, KVH * G * T, HD)
    gate_rows = xn[..., D_NSA + 6 * NSA_KV_COLS:].reshape(B, T, KVH, G, 3).transpose(0, 2, 3, 1, 4)
    gate_rows = gate_rows.reshape(B, KVH * G * T, 3)
    cw = jnp.concatenate([jnp.broadcast_to(cmp_w[0][:, None], (CMP_BLOCK, NSA_KV_COLS)),
                          jnp.broadcast_to(cmp_w[1][:, None], (CMP_BLOCK, NSA_KV_COLS))], axis=1)
    kvcb = _cmp_pool(paged(cache_cmp), page_table, layer, cw)
    o_cmp, sel_rows = _cmp_select(q_rows, kvcb, past, T)
    tpad = _round_up(T, 8)
    padt = lambda t: jnp.pad(t, ((0, 0), (0, tpad - T), (0, 0)))
    out = _sel_win(paged(cache_sel), page_table, layer, q_rows, sel_rows, o_cmp, padt(new_sel),
                   win_prefix.reshape(B, WINDOW, PAGE_COLS), padt(new_win), gate_rows, past, T)
    y = out.reshape(B, KVH, G, T, HD).transpose(0, 3, 1, 2, 4).reshape(B, T, D_NSA)
    five = lambda t: t.reshape(B, -1, 2, KVH, HD)
    win_out = jnp.concatenate([win_prefix[:, T:], five(new_win)], axis=1)
    return y, five(new_cmp), five(new_sel), win_out


def _hybrid_layer(x, past, shift_prev, rwkv_s0, n_keep, p):
    B, T, D = x.shape
    proj = _matmul(x.reshape(B * T, D), p['w_in']).reshape(B, T, IN_COLS)
    y_r, s_T, new_shift = _rwkv_time_mix(proj[..., :RWKV_COLS], shift_prev, rwkv_s0, p)
    if past is None:
        assert T % NSA_TQ == 0 and T >= n_keep
        y_n, new_cmp, new_sel, new_win = _nsa_prompt_mix(proj[..., RWKV_COLS:], n_keep, p['nsa_cmp_w'])
    else:
        assert n_keep == WINDOW
        y_n, new_cmp, new_sel, new_win = _nsa_decode_mix(proj[..., RWKV_COLS:], *past, p['nsa_cmp_w'])
    h = _matmul(jnp.concatenate([y_r, y_n], axis=-1).reshape(B * T, D), p['w_out']).reshape(B, T, D)
    x = _layer_norm(DEEPNORM_ALPHA * x + h, p['ln1_g'], p['ln1_b'])
    f = _peer_ffn(x, p['peer_wq_bf'], p['peer_sk_bf'], p['peer_u_bf'], p['peer_v_bf'])
    x = _layer_norm(DEEPNORM_ALPHA * x + f, p['ln2_g'], p['ln2_b'])
    return x, (new_cmp, new_sel, new_win, s_T, new_shift)


def kernel(x_prompt, x_sample, cache_cmp_kv, cache_sel_kv, page_table, state_win_kv, state_rwkv,
           state_shift, w_in, rwkv_mu, rwkv_w0, rwkv_w2, rwkv_a0, rwkv_a2, rwkv_g2, rwkv_k_k,
           rwkv_k_a, rwkv_r_k, rwkv_gn_g, rwkv_gn_b, nsa_cmp_w, w_out, ln1_g, ln1_b, peer_wq,
           peer_subkeys, peer_u, peer_v, ln2_g, ln2_b):
    bp = x_prompt.shape[0]
    dt = x_prompt.dtype
    assert page_table.shape[1] * PAGE_SIZE == PAST_LEN
    n_keep = state_win_kv.shape[2]
    zero_shift = jnp.zeros((bp, 1, RWKV_COLS), dt)
    zero_state = jnp.zeros((bp, H_RWKV, HEAD_DIM, HEAD_DIM), dt)
    yp, ys = x_prompt, x_sample
    st_p, st_s = [], []
    for l in range(DEPTH):
        p = {'w_in': w_in[l], 'rwkv_mu': rwkv_mu[l], 'rwkv_w0': rwkv_w0[l], 'rwkv_w2': rwkv_w2[l],
             'rwkv_a0': rwkv_a0[l], 'rwkv_a2': rwkv_a2[l], 'rwkv_g2': rwkv_g2[l],
             'rwkv_k_k': rwkv_k_k[l], 'rwkv_k_a': rwkv_k_a[l], 'rwkv_r_k': rwkv_r_k[l],
             'rwkv_gn_g': rwkv_gn_g[l], 'rwkv_gn_b': rwkv_gn_b[l], 'nsa_cmp_w': nsa_cmp_w[l],
             'w_out': w_out[l], 'ln1_g': ln1_g[l], 'ln1_b': ln1_b[l],
             'peer_wq_bf': peer_wq[l].astype(jnp.bfloat16),
             'peer_sk_bf': peer_subkeys[l].reshape(2 * P_HEADS, N_KEYS, P_DKEY // 2).astype(jnp.bfloat16),
             'peer_u_bf': peer_u[l].astype(jnp.bfloat16), 'peer_v_bf': peer_v[l].astype(jnp.bfloat16),
             'ln2_g': ln2_g[l], 'ln2_b': ln2_b[l]}
        yp, sp = _hybrid_layer(yp, None, zero_shift, zero_state, n_keep, p)
        past = (cache_cmp_kv, cache_sel_kv, page_table, l, state_win_kv[l])
        ys, ss = _hybrid_layer(ys, past, state_shift[l], state_rwkv[l], n_keep, p)
        st_p.append(sp)
        st_s.append(ss)
    stk = lambda sts, i: jnp.stack([s[i] for s in sts], axis=0)
    return (yp, ys, stk(st_p, 0), stk(st_p, 1), stk(st_p, 2), stk(st_p, 3), stk(st_p, 4),
            stk(st_s, 0), stk(st_s, 1), stk(st_s, 2), stk(st_s, 3), stk(st_s, 4))
```

```python
import functools
import math

import jax
import jax.numpy as jnp
from jax import lax
from jax.experimental import pallas as pl
from jax.experimental.pallas import tpu as pltpu

D_MODEL = 2048
DEPTH = 2
PAST_LEN = 16384
PAGE_SIZE = 128
HEAD_DIM = 64
D_RWKV = D_MODEL // 2
D_NSA = D_MODEL - D_RWKV
H_RWKV = D_RWKV // HEAD_DIM
H_NSA = D_NSA // HEAD_DIM
NSA_KV_HEADS = 4
NSA_GROUP = H_NSA // NSA_KV_HEADS
NSA_KV_COLS = NSA_KV_HEADS * HEAD_DIM
CMP_BLOCK = 32
SEL_BLOCK = 64
SEL_TOPN = 16
WINDOW = 512
Q_BLOCK = 128
W_LORA = 64
A_LORA = 64
G_LORA = 160
RWKV_COLS = 3 * D_RWKV + W_LORA + A_LORA + G_LORA
NSA_COLS = D_NSA + 6 * NSA_KV_COLS + 3 * H_NSA
IN_COLS = RWKV_COLS + NSA_COLS
P_HEADS = 8
N_KEYS = 128
P_DKEY = 256
P_TOPK = 16
P_TOK_BLOCK = 128
LN_EPS = 1e-5
GN_EPS = 64e-5
DEEPNORM_ALPHA = (2 * DEPTH) ** 0.25
FORCE_SCORE = 1e4
NEG_INF = -1e30

N_EXPERTS = N_KEYS * N_KEYS

LANE = 128
_VMEM_LIMIT = 56 * 1024 * 1024


def _round_up(x, m):
    return -(-x // m) * m


def _mm_kernel(x_ref, w_ref, o_ref):
    o_ref[...] = jnp.dot(x_ref[...].astype(jnp.bfloat16), w_ref[...].astype(jnp.bfloat16),
                         preferred_element_type=jnp.float32)


def _matmul(x, w, *, tm=512, tn=512):
    m, k = x.shape
    n = w.shape[1]
    tm = min(tm, m)
    n_pad = _round_up(n, tn)
    if n_pad != n:
        w = jnp.pad(w, ((0, 0), (0, n_pad - n)))
    out = pl.pallas_call(
        _mm_kernel,
        grid=(m // tm, n_pad // tn),
        in_specs=[pl.BlockSpec((tm, k), lambda i, j: (i, 0)),
                  pl.BlockSpec((k, tn), lambda i, j: (0, j))],
        out_specs=pl.BlockSpec((tm, tn), lambda i, j: (i, j)),
        out_shape=jax.ShapeDtypeStruct((m, n_pad), jnp.float32),
        compiler_params=pltpu.CompilerParams(dimension_semantics=("arbitrary", "arbitrary"),
                                             vmem_limit_bytes=48 * 1024 * 1024),
    )(x, w)
    return out[:, :n] if n_pad != n else out


def _layer_norm(x, g, b):
    mu = x.mean(-1, keepdims=True)
    var = jnp.mean(jnp.square(x - mu), -1, keepdims=True)
    return (x - mu) * lax.rsqrt(var + LN_EPS) * g + b


RW_CHUNK = 64
RW_PAIRS = D_RWKV // LANE


def _bdot(a, b):
    return jnp.dot(a.astype(jnp.bfloat16), b.astype(jnp.bfloat16), preferred_element_type=jnp.float32)


def _bdot_nt(a, b):
    return lax.dot_general(a.astype(jnp.bfloat16), b.astype(jnp.bfloat16), (((1,), (1,)), ((), ())),
                           preferred_element_type=jnp.float32)


def _bdot_tn(a, b):
    return lax.dot_general(a.astype(jnp.bfloat16), b.astype(jnp.bfloat16), (((0,), (0,)), ((), ())),
                           preferred_element_type=jnp.float32)


def _rwkv_chunk_kernel(r_ref, lw_ref, k_ref, v_ref, kk_ref, b_ref, s0_ref, y_ref, st_ref):
    c = pl.program_id(1)
    C = RW_CHUNK
    f32 = jnp.float32

    @pl.when(c == 0)
    def _():
        st_ref[...] = s0_ref[...]

    row = lax.broadcasted_iota(jnp.int32, (C, C), 0)
    col = lax.broadcasted_iota(jnp.int32, (C, C), 1)
    tri_incl = col <= row
    tri_strict = col < row
    tri_f = jnp.where(tri_incl, 1.0, 0.0).astype(f32)
    eye = jnp.where(row == col, 1.0, 0.0).astype(f32)
    lane = lax.broadcasted_iota(jnp.int32, (1, LANE), 1)
    head0 = lane < HEAD_DIM
    r128 = lax.broadcasted_iota(jnp.int32, (LANE, LANE), 0)
    c128 = lax.broadcasted_iota(jnp.int32, (LANE, LANE), 1)
    same_head = (r128 < HEAD_DIM) == (c128 < HEAD_DIM)
    eye128 = r128 == c128

    pairs = range(RW_PAIRS)
    heads = (head0, ~head0)
    sls = [slice(pr * LANE, (pr + 1) * LANE) for pr in pairs]
    lw = [lw_ref[:, sl] for sl in sls]
    cum = [jnp.dot(tri_f, lw[pr], precision=lax.Precision.HIGHEST, preferred_element_type=f32)
           for pr in pairs]
    tot = [cum[pr][C - 1:C, :] for pr in pairs]
    v = [v_ref[:, sl] for sl in sls]
    k = [k_ref[:, sl] for sl in sls]
    b = [b_ref[:, sl] for sl in sls]
    x, ym = [], []
    for pr in pairs:
        e_neg = jnp.exp(-cum[pr])
        x.append(jnp.concatenate([kk_ref[:, sls[pr]] * jnp.exp(cum[pr] - lw[pr]),
                                  r_ref[:, sls[pr]] * jnp.exp(cum[pr])], axis=0))
        ym.append(jnp.concatenate([k[pr] * e_neg, b[pr] * e_neg], axis=0))
    a0 = [st_ref[pr] for pr in pairs]
    xa = [_bdot(x[pr], a0[pr]) for pr in pairs]
    g = [[_bdot_nt(jnp.where(hm, x[pr], 0.0), ym[pr]) for hm in heads] for pr in pairs]
    lkv = [[_bdot(jnp.where(tri_strict, g[pr][h][:C, :C], 0.0), v[pr]) for h in range(2)] for pr in pairs]
    npow = [[jnp.where(tri_strict, -g[pr][h][:C, C:], 0.0) for h in range(2)] for pr in pairs]
    tmat = [[eye + npow[pr][h] for h in range(2)] for pr in pairs]
    for _ in range(int(math.log2(C)) - 1):
        npow = [[_bdot(npow[pr][h], npow[pr][h]) for h in range(2)] for pr in pairs]
        tmat = [[tmat[pr][h] + _bdot(tmat[pr][h], npow[pr][h]) for h in range(2)] for pr in pairs]
    rhs = [xa[pr][:C] + jnp.where(head0, lkv[pr][0], lkv[pr][1]) for pr in pairs]
    w = [jnp.where(head0, _bdot(tmat[pr][0], rhs[pr]), _bdot(tmat[pr][1], rhs[pr])) for pr in pairs]
    vw = [jnp.concatenate([v[pr], w[pr]], axis=0) for pr in pairs]
    for pr in pairs:
        mr = [jnp.concatenate([jnp.where(tri_incl, g[pr][h][C:, :C], 0.0),
                               jnp.where(tri_incl, -g[pr][h][C:, C:], 0.0)], axis=1) for h in range(2)]
        y_ref[:, sls[pr]] = xa[pr][C:] + jnp.where(head0, _bdot(mr[0], vw[pr]), _bdot(mr[1], vw[pr]))
    for pr in pairs:
        e_rem = jnp.exp(tot[pr] - cum[pr])
        kb = jnp.concatenate([k[pr] * e_rem, -(b[pr] * e_rem)], axis=0)
        upd = _bdot_tn(kb, vw[pr])
        p_col = jnp.sum(jnp.where(eye128, jnp.exp(tot[pr]), 0.0), axis=1, keepdims=True)
        st_ref[pr] = a0[pr] * p_col + jnp.where(same_head, upd, 0.0)


def _rwkv_scan(r, lw, k, v, kk, b, s0):
    B, T, D = r.shape
    C = RW_CHUNK
    a = jnp.swapaxes(s0, -1, -2).reshape(B, RW_PAIRS, 2, HEAD_DIM, HEAD_DIM)
    z = jnp.zeros_like(a[:, :, 0])
    a0 = jnp.concatenate([jnp.concatenate([a[:, :, 0], z], axis=-1),
                          jnp.concatenate([z, a[:, :, 1]], axis=-1)], axis=-2)
    seq_spec = pl.BlockSpec((None, C, D), lambda bi, ci: (bi, ci, 0))
    st_spec = pl.BlockSpec((None, RW_PAIRS, LANE, LANE), lambda bi, ci: (bi, 0, 0, 0))
    y, st = pl.pallas_call(
        _rwkv_chunk_kernel,
        grid=(B, T // C),
        in_specs=[seq_spec] * 6 + [st_spec],
        out_specs=[seq_spec, st_spec],
        out_shape=[jax.ShapeDtypeStruct((B, T, D), jnp.float32),
                   jax.ShapeDtypeStruct((B, RW_PAIRS, LANE, LANE), jnp.float32)],
        compiler_params=pltpu.CompilerParams(dimension_semantics=("arbitrary", "arbitrary"),
                                             vmem_limit_bytes=_VMEM_LIMIT),
        name="rwkv_scan",
    )(r, lw, k, v, kk, b, a0)
    s_t = jnp.stack([st[:, :, :HEAD_DIM, :HEAD_DIM], st[:, :, HEAD_DIM:, HEAD_DIM:]], axis=2)
    return y, jnp.swapaxes(s_t.reshape(B, H_RWKV, HEAD_DIM, HEAD_DIM), -1, -2)


def _rwkv_time_mix(xr, shift_prev, s0, p):
    B, T, _ = xr.shape
    dt = xr.dtype
    prev = jnp.concatenate([shift_prev.astype(dt), xr[:, :-1]], axis=1)
    xs = xr + p['rwkv_mu'] * (prev - xr)
    cuts = [D_RWKV, 2 * D_RWKV, 3 * D_RWKV, 3 * D_RWKV + W_LORA, 3 * D_RWKV + W_LORA + A_LORA]
    r, k, v, wl, al, gl = jnp.split(xs, cuts, axis=-1)
    w = -jax.nn.softplus(-(p['rwkv_w0'] + jnp.tanh(wl) @ p['rwkv_w2'])) - 0.5
    lw = -jnp.exp(w)
    a = jax.nn.sigmoid(p['rwkv_a0'] + al @ p['rwkv_a2'])
    g = jax.nn.sigmoid(gl) @ p['rwkv_g2']
    hd = lambda t: t.reshape(B, T, H_RWKV, HEAD_DIM)
    kkf = hd(k * p['rwkv_k_k']).astype(jnp.float32)
    kk = (kkf * lax.rsqrt(jnp.maximum(jnp.sum(kkf * kkf, -1, keepdims=True), 1e-24))).astype(dt)
    kk = kk.reshape(B, T, D_RWKV)
    k = k * (1.0 + (a - 1.0) * p['rwkv_k_a'])
    t_pad = _round_up(T, RW_CHUNK)
    padt = lambda t: jnp.pad(t, ((0, 0), (0, t_pad - T), (0, 0)))
    y, s_T = _rwkv_scan(padt(r), padt(lw), padt(k), padt(v), padt(kk), padt(kk * a), s0.astype(dt))
    y = hd(y[:, :T]).astype(jnp.float32)
    r, k, v = hd(r), hd(k), hd(v)
    mu = y.mean(-1, keepdims=True)
    var = jnp.mean(jnp.square(y - mu), -1, keepdims=True)
    yn = ((y - mu) * lax.rsqrt(var + GN_EPS)).reshape(B, T, D_RWKV) * p['rwkv_gn_g'] + p['rwkv_gn_b']
    bonus = jnp.sum(r * k * p['rwkv_r_k'], -1, keepdims=True) * v
    out = (yn.astype(dt) + bonus.reshape(B, T, D_RWKV)) * g
    return out, s_T, xr[:, -1:]


def _extract_top(buf_ref, out_ref, nrows, width):
    iota = lax.broadcasted_iota(jnp.int32, (nrows, width), 0)

    def body(r, carry):
        s = buf_ref[...]
        mx = jnp.max(s, axis=0, keepdims=True)
        first = jnp.min(jnp.where(s == mx, iota, nrows), axis=0, keepdims=True)
        buf_ref[...] = jnp.where(iota == first, -jnp.inf, s)
        out_ref[pl.ds(r, 1), :] = mx
        return carry

    lax.fori_loop(0, P_TOPK, body, 0)


def _peer_score_kernel(x_ref, wq_ref, sk_ref, s1_ref, s2_ref, st_ref, buf, cand, hv0, hv1, tv):
    tn = x_ref.shape[0]
    q = jnp.dot(x_ref[...].astype(jnp.bfloat16), wq_ref[...],
                preferred_element_type=jnp.float32).astype(jnp.bfloat16)
    for h in range(P_HEADS):
        for c, (s_ref, hv) in enumerate(((s1_ref, hv0), (s2_ref, hv1))):
            col = (2 * h + c) * N_KEYS
            s_t = lax.dot_general(sk_ref[2 * h + c], q[:, col:col + N_KEYS],
                                  (((1,), (1,)), ((), ())), preferred_element_type=jnp.float32)
            s_ref[h] = s_t
            buf[...] = s_t
            _extract_top(buf, hv, N_KEYS, tn)
        pieces = [hv0[a:a + 1, :] + hv1[0:P_TOPK // (a + 1), :] for a in range(P_TOPK)]
        n_cand = sum(P_TOPK // (a + 1) for a in range(P_TOPK))
        pieces.append(jnp.full((cand.shape[0] - n_cand, tn), -jnp.inf, jnp.float32))
        cand[...] = jnp.concatenate(pieces, axis=0)
        _extract_top(cand, tv, cand.shape[0], tn)
        z = jnp.sum(jnp.exp(tv[...] - tv[0:1, :]), axis=0, keepdims=True)
        st_ref[0, h:h + 1, :] = tv[P_TOPK - 1:P_TOPK, :]
        st_ref[1, h:h + 1, :] = hv0[0:1, :]
        st_ref[2, h:h + 1, :] = hv1[0:1, :]
        st_ref[3, h:h + 1, :] = 1.0 / z


def _peer_scores(x, wq_bf, sk_bf, *, tn=256):
    n = x.shape[0]
    tn = min(tn, n)
    return pl.pallas_call(
        _peer_score_kernel,
        grid=(n // tn,),
        in_specs=[pl.BlockSpec((tn, D_MODEL), lambda i: (i, 0)),
                  pl.BlockSpec((D_MODEL, P_HEADS * P_DKEY), lambda i: (0, 0)),
                  pl.BlockSpec((2 * P_HEADS, N_KEYS, P_DKEY // 2), lambda i: (0, 0, 0))],
        out_specs=[pl.BlockSpec((P_HEADS, N_KEYS, tn), lambda i: (0, 0, i)),
                   pl.BlockSpec((P_HEADS, N_KEYS, tn), lambda i: (0, 0, i)),
                   pl.BlockSpec((4, P_HEADS, tn), lambda i: (0, 0, i))],
        out_shape=[jax.ShapeDtypeStruct((P_HEADS, N_KEYS, n), jnp.float32),
                   jax.ShapeDtypeStruct((P_HEADS, N_KEYS, n), jnp.float32),
                   jax.ShapeDtypeStruct((4, P_HEADS, n), jnp.float32)],
        scratch_shapes=[pltpu.VMEM((N_KEYS, tn), jnp.float32),
                        pltpu.VMEM((_round_up(sum(P_TOPK // (a + 1) for a in range(P_TOPK)), 8), tn),
                                   jnp.float32),
                        pltpu.VMEM((P_TOPK, tn), jnp.float32),
                        pltpu.VMEM((P_TOPK, tn), jnp.float32),
                        pltpu.VMEM((P_TOPK, tn), jnp.float32)],
        compiler_params=pltpu.CompilerParams(dimension_semantics=("arbitrary",),
                                             vmem_limit_bytes=_VMEM_LIMIT),
        name="peer_scores",
    )(x, wq_bf, sk_bf)


def _peer_expert_kernel(x_ref, u_ref, v_ref, s1_ref, s2_ref, st_ref, o_ref, e2_scr, h_scr, p_scr,
                        *, chunk):
    j = pl.program_id(1)
    tn = x_ref.shape[0]
    te = u_ref.shape[1]

    @pl.when(j == 0)
    def _():
        o_ref[...] = jnp.zeros_like(o_ref)
        for h in range(P_HEADS):
            e2_scr[h] = jnp.exp(s2_ref[h] - st_ref[2, h:h + 1, :])

    def hidden(c):
        tsl = slice(c * chunk, (c + 1) * chunk)
        h_scr[tsl, :] = jnp.dot(x_ref[tsl, :], u_ref[...], preferred_element_type=jnp.float32)

    a_rows = [[s1_ref[h, pl.ds(j * (te // N_KEYS) + ii, 1), :] for h in range(P_HEADS)]
              for ii in range(te // N_KEYS)]
    e1_rows = [[jnp.exp(a_rows[ii][h] - st_ref[1, h:h + 1, :]) * st_ref[3, h:h + 1, :]
                for h in range(P_HEADS)] for ii in range(te // N_KEYS)]

    def weigh(c):
        for sub in range(chunk // LANE):
            weigh_lanes(slice(c * chunk + sub * LANE, c * chunk + (sub + 1) * LANE))

    def weigh_lanes(tsl):
        for ii in range(te // N_KEYS):
            w_t = jnp.zeros((N_KEYS, LANE), jnp.float32)
            for h in range(P_HEADS):
                val = a_rows[ii][h][:, tsl] + s2_ref[h, :, tsl]
                w_t = w_t + jnp.where(val >= st_ref[0, h:h + 1, tsl],
                                      e1_rows[ii][h][:, tsl] * e2_scr[h, :, tsl], 0.0)
            hh = h_scr[tsl, ii * N_KEYS:(ii + 1) * N_KEYS]
            g = 0.5 * hh * (1.0 + lax.erf(hh * 0.7071067811865476))
            p_scr[tsl, ii * N_KEYS:(ii + 1) * N_KEYS] = (w_t.T * g).astype(jnp.bfloat16)

    def project(c):
        tsl = slice(c * chunk, (c + 1) * chunk)
        o_ref[tsl, :] += jnp.dot(p_scr[tsl, :], v_ref[...], preferred_element_type=jnp.float32)

    n_chunks = tn // chunk
    hidden(0)
    for c in range(n_chunks):
        if c + 1 < n_chunks:
            hidden(c + 1)
        weigh(c)
        project(c)


def _peer_experts(x_bf, u_bf, v_bf, s1t, s2t, stats, *, tn=1024, te=512, chunk=256):
    n = x_bf.shape[0]
    tn = min(tn, n)
    chunk = min(chunk, tn)
    once = pl.Buffered(1)
    return pl.pallas_call(
        functools.partial(_peer_expert_kernel, chunk=chunk),
        grid=(n // tn, N_EXPERTS // te),
        in_specs=[pl.BlockSpec((tn, D_MODEL), lambda i, j: (i, 0), pipeline_mode=once),
                  pl.BlockSpec((D_MODEL, te), lambda i, j: (0, j)),
                  pl.BlockSpec((te, D_MODEL), lambda i, j: (j, 0)),
                  pl.BlockSpec((P_HEADS, N_KEYS, tn), lambda i, j: (0, 0, i), pipeline_mode=once),
                  pl.BlockSpec((P_HEADS, N_KEYS, tn), lambda i, j: (0, 0, i), pipeline_mode=once),
                  pl.BlockSpec((4, P_HEADS, tn), lambda i, j: (0, 0, i), pipeline_mode=once)],
        out_specs=pl.BlockSpec((tn, D_MODEL), lambda i, j: (i, 0)),
        out_shape=jax.ShapeDtypeStruct((n, D_MODEL), jnp.float32),
        scratch_shapes=[pltpu.VMEM((P_HEADS, N_KEYS, tn), jnp.float32),
                        pltpu.VMEM((tn, te), jnp.float32),
                        pltpu.VMEM((tn, te), jnp.bfloat16)],
        compiler_params=pltpu.CompilerParams(dimension_semantics=("arbitrary", "arbitrary"),
                                             vmem_limit_bytes=_VMEM_LIMIT),
        name="peer_experts",
    )(x_bf, u_bf, v_bf, s1t, s2t, stats)


def _peer_ffn(x, wq_bf, sk_bf, u_bf, v_bf):
    B, T, D = x.shape
    n = B * T
    xt = x.reshape(n, D)
    n_pad = _round_up(n, LANE)
    if n_pad != n:
        xt = jnp.pad(xt, ((0, n_pad - n), (0, 0)))
    s1t, s2t, stats = _peer_scores(xt, wq_bf, sk_bf)
    out = _peer_experts(xt.astype(jnp.bfloat16), u_bf, v_bf, s1t, s2t, stats)
    return out[:n].reshape(B, T, D)


NSA_TQ = 256


def _flash_tile(q4, k_t, v_t, mask4, m, l, acc):
    s = lax.dot_general(q4, k_t, (((1,), (1,)), ((), ())), preferred_element_type=jnp.float32)
    s = jnp.where(mask4, s, NEG_INF)
    m_new = jnp.maximum(m, jnp.max(s, axis=-1, keepdims=True))
    p = jnp.where(mask4, jnp.exp(s - m_new), 0.0)
    alpha = jnp.exp(m - m_new)
    l_new = alpha * l + jnp.sum(p, axis=-1, keepdims=True)
    acc_new = alpha * acc + jnp.dot(p.astype(jnp.bfloat16), v_t, preferred_element_type=jnp.float32)
    return m_new, l_new, acc_new


def _nsa_prompt_kernel(q_ref, gl_ref, kc_ref, vc_ref, ks_ref, vs_ref, kw_ref, vw_ref, cw_ref, ex_ref,
                       o_ref, kcb_scr, vcb_scr, imp_scr, selx_scr, *, seq):
    qt = pl.program_id(2)
    tq = NSA_TQ
    n_cmp = seq // CMP_BLOCK
    n_sel = seq // SEL_BLOCK
    G = NSA_GROUP
    q0 = qt * tq

    @pl.when(qt == 0)
    def _():
        kc = kc_ref[...].reshape(n_cmp, CMP_BLOCK, HEAD_DIM)
        vc = vc_ref[...].reshape(n_cmp, CMP_BLOCK, HEAD_DIM)
        kcb_scr[...] = jnp.zeros_like(kcb_scr)
        vcb_scr[...] = jnp.zeros_like(vcb_scr)
        kcb_scr[0:n_cmp, :] = jnp.sum(kc * cw_ref[0][None], axis=1)
        vcb_scr[0:n_cmp, :] = jnp.sum(vc * cw_ref[1][None], axis=1).astype(jnp.bfloat16)

    qpos_col = q0 + lax.broadcasted_iota(jnp.int32, (tq, 1), 0)
    qf = [q_ref[g] * (HEAD_DIM ** -0.5) for g in range(G)]
    qs = [t.astype(jnp.bfloat16) for t in qf]
    gates = jax.nn.sigmoid(gl_ref[...])

    cmp_end = (lax.broadcasted_iota(jnp.int32, (1, LANE), 1) + 1) * CMP_BLOCK - 1
    cmask = (cmp_end <= qpos_col) & (lax.broadcasted_iota(jnp.int32, (1, LANE), 1) < n_cmp)
    o_cmp = []
    imp = jnp.zeros((tq, LANE), jnp.float32)
    for g in range(G):
        s = lax.dot_general(qf[g], kcb_scr[...], (((1,), (1,)), ((), ())),
                            precision=lax.Precision.HIGHEST, preferred_element_type=jnp.float32)
        s = jnp.where(cmask, s, NEG_INF)
        e = jnp.exp(s - jnp.max(s, axis=-1, keepdims=True))
        p = jnp.where(cmask, e / jnp.sum(e, axis=-1, keepdims=True), 0.0)
        o_cmp.append(jnp.dot(p.astype(jnp.bfloat16), vcb_scr[...], preferred_element_type=jnp.float32))
        imp = imp + p
    imp_t = imp.T
    ratio = SEL_BLOCK // CMP_BLOCK
    parts = []
    for c in range(tq // LANE):
        imp_scr[c] = imp_t[:, c * LANE:(c + 1) * LANE]
        part = imp_scr[c, pl.ds(0, n_sel, stride=ratio), :]
        for r in range(1, ratio):
            part = part + imp_scr[c, pl.ds(r, n_sel, stride=ratio), :]
        parts.append(part)
    imp_s = jnp.concatenate(parts, axis=1)
    qpos_row = q0 + lax.broadcasted_iota(jnp.int32, (1, tq), 1)
    cur = qpos_row // SEL_BLOCK
    blk = lax.broadcasted_iota(jnp.int32, (n_sel, tq), 0)
    forced = (blk == 0) | (blk == cur) | (blk == cur - 1)
    imp_s = jnp.where(blk > cur, -1.0, jnp.where(forced, FORCE_SCORE, imp_s))
    rank = jnp.zeros((n_sel, tq), jnp.int32)
    for mm in range(n_sel):
        row = imp_s[mm:mm + 1, :]
        ahead = (row > imp_s) | ((row == imp_s) & (mm < blk))
        rank = rank + jnp.where(ahead, 1, 0)
    sel = jnp.where((rank < min(SEL_TOPN, n_sel)) & (imp_s >= 0.0), 1.0, 0.0)
    sel_pad = jnp.concatenate([sel, jnp.zeros((LANE - n_sel, tq), jnp.float32)], axis=0)
    sel_q = sel_pad.T.astype(jnp.bfloat16)
    selx_scr[...] = jnp.dot(sel_q, ex_ref[...], preferred_element_type=jnp.float32)

    q4 = jnp.concatenate(qs, axis=0)
    qpos4 = jnp.concatenate([qpos_col] * G, axis=0)
    kpos_l = lax.broadcasted_iota(jnp.int32, (1, tq), 1)
    init = (jnp.full((G * tq, 1), NEG_INF, jnp.float32), jnp.zeros((G * tq, 1), jnp.float32),
            jnp.zeros((G * tq, HEAD_DIM), jnp.float32))

    def sel_body(kt, carry):
        k0 = pl.multiple_of(kt * tq, tq)
        chosen = selx_scr[:, pl.ds(k0, tq)] > 0.5
        chosen4 = jnp.concatenate([chosen] * G, axis=0)
        mask4 = chosen4 & ((k0 + kpos_l) <= qpos4)
        return _flash_tile(q4, ks_ref[pl.ds(k0, tq), :], vs_ref[pl.ds(k0, tq), :], mask4, *carry)

    _, l_s, acc_s = lax.fori_loop(0, qt + 1, sel_body, init)
    o_sel = acc_s / l_s

    def win_body(kt, carry):
        k0 = pl.multiple_of(kt * tq, tq)
        kpos = k0 + kpos_l
        mask4 = (kpos <= qpos4) & (kpos >= qpos4 - (WINDOW - 1))
        return _flash_tile(q4, kw_ref[pl.ds(k0, tq), :], vw_ref[pl.ds(k0, tq), :], mask4, *carry)

    _, l_w, acc_w = lax.fori_loop(jnp.maximum(qt - WINDOW // tq, 0), qt + 1, win_body, init)
    o_win = acc_w / l_w

    for g in range(G):
        rows = slice(g * tq, (g + 1) * tq)
        o_ref[g] = (gates[:, 3 * g:3 * g + 1] * o_cmp[g] + gates[:, 3 * g + 1:3 * g + 2] * o_sel[rows]
                    + gates[:, 3 * g + 2:3 * g + 3] * o_win[rows])


def _nsa_prompt(xn, cmp_w):
    B, T, _ = xn.shape
    KVH, G, HD = NSA_KV_HEADS, NSA_GROUP, HEAD_DIM
    kvc = NSA_KV_COLS
    tq = NSA_TQ
    q = xn[..., :D_NSA].reshape(B, T, KVH, G, HD).transpose(0, 2, 3, 1, 4)
    kv = [xn[..., D_NSA + i * kvc:D_NSA + (i + 1) * kvc].reshape(B, T, KVH, HD).transpose(0, 2, 1, 3)
          for i in range(6)]
    kc, vc = kv[0], kv[1]
    ks, vs, kw, vw = [t.astype(jnp.bfloat16) for t in kv[2:]]
    gl = xn[..., D_NSA + 6 * kvc:].reshape(B, T, KVH, 3 * G).transpose(0, 2, 1, 3)
    cw = jnp.broadcast_to(cmp_w[:, :, None], (2, CMP_BLOCK, HD))
    expand = (jnp.arange(LANE)[:, None] == (jnp.arange(T) // SEL_BLOCK)[None, :]).astype(jnp.bfloat16)
    kv_spec = pl.BlockSpec((None, None, T, HD), lambda b, k, t: (b, k, 0, 0))
    out = pl.pallas_call(
        functools.partial(_nsa_prompt_kernel, seq=T),
        grid=(B, KVH, T // tq),
        in_specs=[pl.BlockSpec((None, None, G, tq, HD), lambda b, k, t: (b, k, 0, t, 0)),
                  pl.BlockSpec((None, None, tq, 3 * G), lambda b, k, t: (b, k, t, 0)),
                  kv_spec, kv_spec, kv_spec, kv_spec, kv_spec, kv_spec,
                  pl.BlockSpec((2, CMP_BLOCK, HD), lambda b, k, t: (0, 0, 0)),
                  pl.BlockSpec((LANE, T), lambda b, k, t: (0, 0))],
        out_specs=pl.BlockSpec((None, None, G, tq, HD), lambda b, k, t: (b, k, 0, t, 0)),
        out_shape=jax.ShapeDtypeStruct((B, KVH, G, T, HD), jnp.float32),
        scratch_shapes=[pltpu.VMEM((LANE, HD), jnp.float32),
                        pltpu.VMEM((LANE, HD), jnp.bfloat16),
                        pltpu.VMEM((tq // LANE, LANE, LANE), jnp.float32),
                        pltpu.VMEM((tq, T), jnp.float32)],
        compiler_params=pltpu.CompilerParams(dimension_semantics=("arbitrary", "arbitrary", "arbitrary"),
                                             vmem_limit_bytes=_VMEM_LIMIT),
        name="nsa_prompt",
    )(q, gl, kc, vc, ks, vs, kw, vw, cw, expand)
    return out.transpose(0, 3, 1, 2, 4).reshape(B, T, D_NSA)


def _nsa_prompt_mix(xn, n_keep, cmp_w):
    B, T, _ = xn.shape
    kv5 = lambda i: xn[..., D_NSA + 2 * i * NSA_KV_COLS:D_NSA + 2 * (i + 1) * NSA_KV_COLS].reshape(
        B, T, 2, NSA_KV_HEADS, HEAD_DIM)
    return _nsa_prompt(xn, cmp_w), kv5(0), kv5(1), kv5(2)[:, -n_keep:]


NSA_PG = 4
PAGE_COLS = 2 * NSA_KV_COLS
NSA_ROWS = NSA_KV_HEADS * NSA_GROUP


def _page_specs(layer, n):
    def spec(j):
        return pl.BlockSpec((None, None, PAGE_SIZE, PAGE_COLS),
                            lambda b, s, pt: (layer, pt[b, s * n + j], 0, 0))
    return [spec(j) for j in range(n)]


def _cmp_pool_kernel(pt_ref, *refs):
    pages, cw_ref, o_ref = refs[:NSA_PG], refs[NSA_PG], refs[NSA_PG + 1]
    per_page = PAGE_SIZE // CMP_BLOCK
    out = []
    for pg in pages:
        blocks = pg[...].reshape(per_page, CMP_BLOCK, PAGE_COLS)
        out.append(jnp.sum(blocks * cw_ref[...][None], axis=1))
    o_ref[...] = jnp.concatenate(out, axis=0)


def _cmp_pool(cache, page_table, layer, cw):
    B, n_pages = page_table.shape
    per_step = NSA_PG * PAGE_SIZE // CMP_BLOCK
    return pl.pallas_call(
        _cmp_pool_kernel,
        grid_spec=pltpu.PrefetchScalarGridSpec(
            num_scalar_prefetch=1, grid=(B, n_pages // NSA_PG),
            in_specs=_page_specs(layer, NSA_PG) + [pl.BlockSpec((CMP_BLOCK, PAGE_COLS), lambda b, s, pt: (0, 0))],
            out_specs=pl.BlockSpec((None, per_step, PAGE_COLS), lambda b, s, pt: (b, s, 0))),
        out_shape=jax.ShapeDtypeStruct((B, n_pages * PAGE_SIZE // CMP_BLOCK, PAGE_COLS), jnp.float32),
        compiler_params=pltpu.CompilerParams(dimension_semantics=("arbitrary", "arbitrary"),
                                             vmem_limit_bytes=_VMEM_LIMIT),
        name="nsa_cmp_pool",
    )(page_table, *([cache] * NSA_PG), cw)


def _cmp_select_kernel(q_ref, kv_ref, ocmp_ref, sel_ref, pt_scr, imp_scr, *, past, steps):
    T = steps
    R = NSA_ROWS * T
    n_cmp = past // CMP_BLOCK
    n_sel = past // SEL_BLOCK + 1
    n_lane = sel_ref.shape[-1]
    ratio = SEL_BLOCK // CMP_BLOCK
    rows_t = lax.broadcasted_iota(jnp.int32, (R, 1), 0) % T
    qpos = past + rows_t
    cmp_end = (lax.broadcasted_iota(jnp.int32, (1, n_cmp), 1) + 1) * CMP_BLOCK - 1
    cmask = cmp_end <= qpos
    q = q_ref[...]
    p_all, o_all = [], []
    for kh in range(NSA_KV_HEADS):
        rows = slice(kh * NSA_GROUP * T, (kh + 1) * NSA_GROUP * T)
        kcb = kv_ref[:, kh * HEAD_DIM:(kh + 1) * HEAD_DIM]
        vcb = kv_ref[:, NSA_KV_COLS + kh * HEAD_DIM:NSA_KV_COLS + (kh + 1) * HEAD_DIM].astype(jnp.bfloat16)
        s = lax.dot_general(q[rows], kcb, (((1,), (1,)), ((), ())), precision=lax.Precision.HIGHEST,
                            preferred_element_type=jnp.float32)
        s = jnp.where(cmask[rows], s, NEG_INF)
        e = jnp.exp(s - jnp.max(s, axis=-1, keepdims=True))
        p = jnp.where(cmask[rows], e / jnp.sum(e, axis=-1, keepdims=True), 0.0)
        o_all.append(jnp.dot(p.astype(jnp.bfloat16), vcb, preferred_element_type=jnp.float32))
        imp = p[0:T]
        for g in range(1, NSA_GROUP):
            imp = imp + p[g * T:(g + 1) * T]
        p_all.extend([imp] * NSA_GROUP)
    ocmp_ref[...] = jnp.concatenate(o_all, axis=0)
    imp_rows = jnp.concatenate(p_all + [jnp.zeros((LANE - R, n_cmp), jnp.float32)], axis=0)
    for c in range(n_cmp // LANE):
        pt_scr[c] = imp_rows[:, c * LANE:(c + 1) * LANE].T
    per = LANE // ratio
    for c in range(n_cmp // LANE):
        part = pt_scr[c, pl.ds(0, per, stride=ratio), :]
        for r in range(1, ratio):
            part = part + pt_scr[c, pl.ds(r, per, stride=ratio), :]
        imp_scr[c * per:(c + 1) * per, :] = part
    n_rows = imp_scr.shape[0]
    imp_scr[n_sel - 1:n_rows, :] = jnp.zeros((n_rows - n_sel + 1, LANE), jnp.float32)
    lane_t = lax.broadcasted_iota(jnp.int32, (1, LANE), 1) % T
    cur = (past + lane_t) // SEL_BLOCK
    blk = lax.broadcasted_iota(jnp.int32, (n_rows, LANE), 0)
    forced = (blk == 0) | (blk == cur) | (blk == cur - 1)
    imp_s = jnp.where(blk > cur, -1.0, jnp.where(forced, FORCE_SCORE, imp_scr[...]))
    imp_s = jnp.where(blk < n_sel, imp_s, -2.0)
    imp_scr[...] = imp_s

    def rank_body(m, rank):
        row = imp_scr[pl.ds(m, 1), :]
        ahead = (row > imp_s) | ((row == imp_s) & (m < blk))
        return rank + jnp.where(ahead, 1, 0)

    rank = lax.fori_loop(0, n_sel, rank_body, jnp.zeros((n_rows, LANE), jnp.int32))
    sel = jnp.where((rank < min(SEL_TOPN, n_sel)) & (imp_s >= 0.0), 1.0, 0.0)
    sel = jnp.concatenate([sel, jnp.zeros((n_lane - n_rows, LANE), jnp.float32)], axis=0)
    for c in range(n_lane // LANE):
        sel_ref[:, c * LANE:(c + 1) * LANE] = sel[c * LANE:(c + 1) * LANE, :].T


def _cmp_select(q_rows, kvcb, past, steps):
    B, R, _ = q_rows.shape
    n_cmp = kvcb.shape[1]
    n_sel = past // SEL_BLOCK + 1
    n_lane = _round_up(n_sel, LANE)
    n_rows = _round_up(n_sel, 8)
    return pl.pallas_call(
        functools.partial(_cmp_select_kernel, past=past, steps=steps),
        grid=(B,),
        in_specs=[pl.BlockSpec((None, R, HEAD_DIM), lambda b: (b, 0, 0)),
                  pl.BlockSpec((None, n_cmp, PAGE_COLS), lambda b: (b, 0, 0))],
        out_specs=[pl.BlockSpec((None, R, HEAD_DIM), lambda b: (b, 0, 0)),
                   pl.BlockSpec((None, LANE, n_lane), lambda b: (b, 0, 0))],
        out_shape=[jax.ShapeDtypeStruct((B, R, HEAD_DIM), jnp.float32),
                   jax.ShapeDtypeStruct((B, LANE, n_lane), jnp.float32)],
        scratch_shapes=[pltpu.VMEM((n_cmp // LANE, LANE, LANE), jnp.float32),
                        pltpu.VMEM((n_rows, LANE), jnp.float32)],
        compiler_params=pltpu.CompilerParams(dimension_semantics=("arbitrary",),
                                             vmem_limit_bytes=_VMEM_LIMIT),
        name="nsa_cmp_select",
    )(q_rows, kvcb)


def _soft_update(s, mask, v, m, l, acc):
    s = jnp.where(mask, s, NEG_INF)
    m_new = jnp.maximum(m, jnp.max(s, axis=-1, keepdims=True))
    p = jnp.where(mask, jnp.exp(s - m_new), 0.0)
    alpha = jnp.exp(m - m_new)
    return (m_new, alpha * l + jnp.sum(p, axis=-1, keepdims=True),
            alpha * acc + jnp.dot(p.astype(jnp.bfloat16), v, preferred_element_type=jnp.float32))


def _sel_win_kernel(pt_ref, *refs, past, steps):
    pages = refs[:NSA_PG]
    (q_ref, sel_ref, ocmp_ref, nsel_ref, wpre_ref, nwin_ref, gate_ref,
     o_ref, m_scr, l_scr, acc_scr) = refs[NSA_PG:]
    s_id = pl.program_id(1)
    T = steps
    GT = NSA_GROUP * T
    R = NSA_ROWS * T
    n_lane = sel_ref.shape[-1]
    rows_t = lax.broadcasted_iota(jnp.int32, (GT, 1), 0) % T

    @pl.when(s_id == 0)
    def _():
        m_scr[...] = jnp.full_like(m_scr, NEG_INF)
        l_scr[...] = jnp.zeros_like(l_scr)
        acc_scr[...] = jnp.zeros_like(acc_scr)

    q = q_ref[...].astype(jnp.bfloat16)
    sel_bf = sel_ref[0:R, :].astype(jnp.bfloat16)
    n_keys = NSA_PG * PAGE_SIZE
    blk_id = lax.broadcasted_iota(jnp.int32, (n_lane, n_keys), 0)
    key_blk = s_id * (n_keys // SEL_BLOCK) + lax.broadcasted_iota(jnp.int32, (n_lane, n_keys), 1) // SEL_BLOCK

    def head_cols(x, kh, off):
        return x[:, off + kh * HEAD_DIM:off + (kh + 1) * HEAD_DIM]

    kv = jnp.concatenate([pg[...] for pg in pages], axis=0).astype(jnp.bfloat16)
    expand = jnp.where(blk_id == key_blk, 1.0, 0.0).astype(jnp.bfloat16)
    chosen = jnp.dot(sel_bf, expand, preferred_element_type=jnp.float32) > 0.5
    head_rows = [slice(kh * GT, (kh + 1) * GT) for kh in range(NSA_KV_HEADS)]
    scores = [lax.dot_general(q[rows], head_cols(kv, kh, 0), (((1,), (1,)), ((), ())),
                              preferred_element_type=jnp.float32) for kh, rows in enumerate(head_rows)]
    upd = [_soft_update(scores[kh], chosen[rows], head_cols(kv, kh, NSA_KV_COLS),
                        m_scr[rows], l_scr[rows], acc_scr[rows]) for kh, rows in enumerate(head_rows)]
    for kh, rows in enumerate(head_rows):
        m_scr[rows], l_scr[rows], acc_scr[rows] = upd[kh]

    @pl.when(s_id == pl.num_programs(1) - 1)
    def _():
        new_blk = past // SEL_BLOCK
        tpad = nsel_ref.shape[0]
        jn = lax.broadcasted_iota(jnp.int32, (1, tpad), 1)
        nsel = nsel_ref[...].astype(jnp.bfloat16)
        nwin = nwin_ref[...].astype(jnp.bfloat16)
        wpre = wpre_ref[...].astype(jnp.bfloat16)
        jp = lax.broadcasted_iota(jnp.int32, (1, WINDOW), 1)
        gates = jax.nn.sigmoid(gate_ref[...])
        for kh in range(NSA_KV_HEADS):
            rows = slice(kh * GT, (kh + 1) * GT)
            pick = sel_ref[kh * GT:(kh + 1) * GT, new_blk:new_blk + 1] > 0.5
            s = lax.dot_general(q[rows], head_cols(nsel, kh, 0), (((1,), (1,)), ((), ())),
                                preferred_element_type=jnp.float32)
            m, l, acc = _soft_update(s, pick & (jn <= rows_t) & (jn < T), head_cols(nsel, kh, NSA_KV_COLS),
                                     m_scr[rows], l_scr[rows], acc_scr[rows])
            o_sel = acc / l
            init = (jnp.full((GT, 1), NEG_INF, jnp.float32), jnp.zeros((GT, 1), jnp.float32),
                    jnp.zeros((GT, HEAD_DIM), jnp.float32))
            s = lax.dot_general(q[rows], head_cols(wpre, kh, 0), (((1,), (1,)), ((), ())),
                                preferred_element_type=jnp.float32)
            st = _soft_update(s, jp > rows_t, head_cols(wpre, kh, NSA_KV_COLS), *init)
            s = lax.dot_general(q[rows], head_cols(nwin, kh, 0), (((1,), (1,)), ((), ())),
                                preferred_element_type=jnp.float32)
            _, l_w, acc_w = _soft_update(s, (jn <= rows_t) & (jn < T), head_cols(nwin, kh, NSA_KV_COLS), *st)
            o_win = acc_w / l_w
            gt = gates[rows]
            o_ref[rows, :] = gt[:, 0:1] * ocmp_ref[rows, :] + gt[:, 1:2] * o_sel + gt[:, 2:3] * o_win


def _sel_win(cache, page_table, layer, q_rows, sel_rows, o_cmp, new_sel, win_prefix, new_win, gate_rows,
             past, steps):
    B, n_pages = page_table.shape
    R = q_rows.shape[1]
    n_lane = sel_rows.shape[-1]
    tpad = new_sel.shape[1]
    per_b = lambda shape: pl.BlockSpec((None,) + shape, lambda b, s, pt: (b,) + (0,) * len(shape))
    return pl.pallas_call(
        functools.partial(_sel_win_kernel, past=past, steps=steps),
        grid_spec=pltpu.PrefetchScalarGridSpec(
            num_scalar_prefetch=1, grid=(B, n_pages // NSA_PG),
            in_specs=_page_specs(layer, NSA_PG) + [
                per_b((R, HEAD_DIM)), per_b((LANE, n_lane)), per_b((R, HEAD_DIM)),
                per_b((tpad, PAGE_COLS)), per_b((WINDOW, PAGE_COLS)), per_b((tpad, PAGE_COLS)),
                per_b((R, 3))],
            out_specs=per_b((R, HEAD_DIM)),
            scratch_shapes=[pltpu.VMEM((R, 1), jnp.float32), pltpu.VMEM((R, 1), jnp.float32),
                            pltpu.VMEM((R, HEAD_DIM), jnp.float32)]),
        out_shape=jax.ShapeDtypeStruct((B, R, HEAD_DIM), jnp.float32),
        compiler_params=pltpu.CompilerParams(dimension_semantics=("arbitrary", "arbitrary"),
                                             vmem_limit_bytes=_VMEM_LIMIT),
        name="nsa_sel_win",
    )(page_table, *([cache] * NSA_PG), q_rows, sel_rows, o_cmp, new_sel, win_prefix, new_win, gate_rows)


def _nsa_decode_mix(xn, cache_cmp, cache_sel, page_table, layer, win_prefix, cmp_w):
    B, T, _ = xn.shape
    n_pages = page_table.shape[1]
    past = n_pages * PAGE_SIZE
    assert T <= SEL_BLOCK and n_pages % NSA_PG == 0 and win_prefix.shape[1] == WINDOW
    assert (past // CMP_BLOCK) % LANE == 0
    KVH, G, HD = NSA_KV_HEADS, NSA_GROUP, HEAD_DIM
    paged = lambda c: c.reshape(c.shape[0], c.shape[1], PAGE_SIZE, PAGE_COLS)
    kv5 = lambda i: xn[..., D_NSA + 2 * i * NSA_KV_COLS:D_NSA + 2 * (i + 1) * NSA_KV_COLS]
    new_cmp, new_sel, new_win = kv5(0), kv5(1), kv5(2)
    q_rows = (xn[..., :D_NSA] * (HD ** -0.5)).reshape(B, T, KVH, G, HD).transpose(0, 2, 3, 1, 4)
    q_rows = q_rows.reshape(B, KVH * G * T, HD)
    gate_rows = xn[..., D_NSA + 6 * NSA_KV_COLS:].reshape(B, T, KVH, G, 3).transpose(0, 2, 3, 1, 4)
    gate_rows = gate_rows.reshape(B, KVH * G * T, 3)
    cw = jnp.concatenate([jnp.broadcast_to(cmp_w[0][:, None], (CMP_BLOCK, NSA_KV_COLS)),
                          jnp.broadcast_to(cmp_w[1][:, None], (CMP_BLOCK, NSA_KV_COLS))], axis=1)
    kvcb = _cmp_pool(paged(cache_cmp), page_table, layer, cw)
    o_cmp, sel_rows = _cmp_select(q_rows, kvcb, past, T)
    tpad = _round_up(T, 8)
    padt = lambda t: jnp.pad(t, ((0, 0), (0, tpad - T), (0, 0)))
    out = _sel_win(paged(cache_sel), page_table, layer, q_rows, sel_rows, o_cmp, padt(new_sel),
                   win_prefix.reshape(B, WINDOW, PAGE_COLS), padt(new_win), gate_rows, past, T)
    y = out.reshape(B, KVH, G, T, HD).transpose(0, 3, 1, 2, 4).reshape(B, T, D_NSA)
    five = lambda t: t.reshape(B, -1, 2, KVH, HD)
    win_out = jnp.concatenate([win_prefix[:, T:], five(new_win)], axis=1)
    return y, five(new_cmp), five(new_sel), win_out


def _hybrid_layer(x, past, shift_prev, rwkv_s0, n_keep, p):
    B, T, D = x.shape
    proj = _matmul(x.reshape(B * T, D), p['w_in']).reshape(B, T, IN_COLS)
    y_r, s_T, new_shift = _rwkv_time_mix(proj[..., :RWKV_COLS], shift_prev, rwkv_s0, p)
    if past is None:
        assert T % NSA_TQ == 0 and T >= n_keep
        y_n, new_cmp, new_sel, new_win = _nsa_prompt_mix(proj[..., RWKV_COLS:], n_keep, p['nsa_cmp_w'])
    else:
        assert n_keep == WINDOW
        y_n, new_cmp, new_sel, new_win = _nsa_decode_mix(proj[..., RWKV_COLS:], *past, p['nsa_cmp_w'])
    h = _matmul(jnp.concatenate([y_r, y_n], axis=-1).reshape(B * T, D), p['w_out']).reshape(B, T, D)
    x = _layer_norm(DEEPNORM_ALPHA * x + h, p['ln1_g'], p['ln1_b'])
    f = _peer_ffn(x, p['peer_wq_bf'], p['peer_sk_bf'], p['peer_u_bf'], p['peer_v_bf'])
    x = _layer_norm(DEEPNORM_ALPHA * x + f, p['ln2_g'], p['ln2_b'])
    return x, (new_cmp, new_sel, new_win, s_T, new_shift)


def kernel(x_prompt, x_sample, cache_cmp_kv, cache_sel_kv, page_table, state_win_kv, state_rwkv,
           state_shift, w_in, rwkv_mu, rwkv_w0, rwkv_w2, rwkv_a0, rwkv_a2, rwkv_g2, rwkv_k_k,
           rwkv_k_a, rwkv_r_k, rwkv_gn_g, rwkv_gn_b, nsa_cmp_w, w_out, ln1_g, ln1_b, peer_wq,
           peer_subkeys, peer_u, peer_v, ln2_g, ln2_b):
    bp = x_prompt.shape[0]
    dt = x_prompt.dtype
    assert page_table.shape[1] * PAGE_SIZE == PAST_LEN
    n_keep = state_win_kv.shape[2]
    zero_shift = jnp.zeros((bp, 1, RWKV_COLS), dt)
    zero_state = jnp.zeros((bp, H_RWKV, HEAD_DIM, HEAD_DIM), dt)
    yp, ys = x_prompt, x_sample
    st_p, st_s = [], []
    for l in range(DEPTH):
        p = {'w_in': w_in[l], 'rwkv_mu': rwkv_mu[l], 'rwkv_w0': rwkv_w0[l], 'rwkv_w2': rwkv_w2[l],
             'rwkv_a0': rwkv_a0[l], 'rwkv_a2': rwkv_a2[l], 'rwkv_g2': rwkv_g2[l],
             'rwkv_k_k': rwkv_k_k[l], 'rwkv_k_a': rwkv_k_a[l], 'rwkv_r_k': rwkv_r_k[l],
             'rwkv_gn_g': rwkv_gn_g[l], 'rwkv_gn_b': rwkv_gn_b[l], 'nsa_cmp_w': nsa_cmp_w[l],
             'w_out': w_out[l], 'ln1_g': ln1_g[l], 'ln1_b': ln1_b[l],
             'peer_wq_bf': peer_wq[l].astype(jnp.bfloat16),
             'peer_sk_bf': peer_subkeys[l].reshape(2 * P_HEADS, N_KEYS, P_DKEY // 2).astype(jnp.bfloat16),
             'peer_u_bf': peer_u[l].astype(jnp.bfloat16).T, 'peer_v_bf': peer_v[l].astype(jnp.bfloat16),
             'ln2_g': ln2_g[l], 'ln2_b': ln2_b[l]}
        yp, sp = _hybrid_layer(yp, None, zero_shift, zero_state, n_keep, p)
        past = (cache_cmp_kv, cache_sel_kv, page_table, l, state_win_kv[l])
        ys, ss = _hybrid_layer(ys, past, state_shift[l], state_rwkv[l], n_keep, p)
        st_p.append(sp)
        st_s.append(ss)
    stk = lambda sts, i: jnp.stack([s[i] for s in sts], axis=0)
    return (yp, ys, stk(st_p, 0), stk(st_p, 1), stk(st_p, 2), stk(st_p, 3), stk(st_p, 4),
            stk(st_s, 0), stk(st_s, 1), stk(st_s, 2), stk(st_s, 3), stk(st_s, 4))
```

```python
import functools
import math

import jax
import jax.numpy as jnp
from jax import lax
from jax.experimental import pallas as pl
from jax.experimental.pallas import tpu as pltpu

D_MODEL = 2048
DEPTH = 2
PAST_LEN = 16384
PAGE_SIZE = 128
HEAD_DIM = 64
D_RWKV = D_MODEL // 2
D_NSA = D_MODEL - D_RWKV
H_RWKV = D_RWKV // HEAD_DIM
H_NSA = D_NSA // HEAD_DIM
NSA_KV_HEADS = 4
NSA_GROUP = H_NSA // NSA_KV_HEADS
NSA_KV_COLS = NSA_KV_HEADS * HEAD_DIM
CMP_BLOCK = 32
SEL_BLOCK = 64
SEL_TOPN = 16
WINDOW = 512
Q_BLOCK = 128
W_LORA = 64
A_LORA = 64
G_LORA = 160
RWKV_COLS = 3 * D_RWKV + W_LORA + A_LORA + G_LORA
NSA_COLS = D_NSA + 6 * NSA_KV_COLS + 3 * H_NSA
IN_COLS = RWKV_COLS + NSA_COLS
P_HEADS = 8
N_KEYS = 128
P_DKEY = 256
P_TOPK = 16
P_TOK_BLOCK = 128
LN_EPS = 1e-5
GN_EPS = 64e-5
DEEPNORM_ALPHA = (2 * DEPTH) ** 0.25
FORCE_SCORE = 1e4
NEG_INF = -1e30

N_EXPERTS = N_KEYS * N_KEYS

LANE = 128
_VMEM_LIMIT = 56 * 1024 * 1024


def _round_up(x, m):
    return -(-x // m) * m


def _mm_kernel(x_ref, w_ref, o_ref):
    o_ref[...] = jnp.dot(x_ref[...].astype(jnp.bfloat16), w_ref[...].astype(jnp.bfloat16),
                         preferred_element_type=jnp.float32)


def _matmul(x, w, *, tm=512, tn=512):
    m, k = x.shape
    n = w.shape[1]
    tm = min(tm, m)
    n_pad = _round_up(n, tn)
    if n_pad != n:
        w = jnp.pad(w, ((0, 0), (0, n_pad - n)))
    out = pl.pallas_call(
        _mm_kernel,
        grid=(m // tm, n_pad // tn),
        in_specs=[pl.BlockSpec((tm, k), lambda i, j: (i, 0)),
                  pl.BlockSpec((k, tn), lambda i, j: (0, j))],
        out_specs=pl.BlockSpec((tm, tn), lambda i, j: (i, j)),
        out_shape=jax.ShapeDtypeStruct((m, n_pad), jnp.float32),
        compiler_params=pltpu.CompilerParams(dimension_semantics=("arbitrary", "arbitrary"),
                                             vmem_limit_bytes=48 * 1024 * 1024),
    )(x, w)
    return out[:, :n] if n_pad != n else out


def _layer_norm(x, g, b):
    mu = x.mean(-1, keepdims=True)
    var = jnp.mean(jnp.square(x - mu), -1, keepdims=True)
    return (x - mu) * lax.rsqrt(var + LN_EPS) * g + b


RW_CHUNK = 64
RW_PAIRS = D_RWKV // LANE


def _bdot(a, b):
    return jnp.dot(a.astype(jnp.bfloat16), b.astype(jnp.bfloat16), preferred_element_type=jnp.float32)


def _bdot_nt(a, b):
    return lax.dot_general(a.astype(jnp.bfloat16), b.astype(jnp.bfloat16), (((1,), (1,)), ((), ())),
                           preferred_element_type=jnp.float32)


def _bdot_tn(a, b):
    return lax.dot_general(a.astype(jnp.bfloat16), b.astype(jnp.bfloat16), (((0,), (0,)), ((), ())),
                           preferred_element_type=jnp.float32)


def _head_sum(x, same_head_bf):
    hi = x.astype(jnp.bfloat16)
    lo = (x - hi.astype(jnp.float32)).astype(jnp.bfloat16)
    return (jnp.dot(hi, same_head_bf, preferred_element_type=jnp.float32)
            + jnp.dot(lo, same_head_bf, preferred_element_type=jnp.float32))


def _rwkv_mix_kernel(x_ref, shift_ref, mu_ref, pv_ref, w2_ref, a2_ref, g2_ref, s0_ref, y_ref, st_ref,
                     prev_scr, *, steps):
    c = pl.program_id(1)
    C = RW_CHUNK
    f32 = jnp.float32

    @pl.when(c == 0)
    def _():
        st_ref[...] = s0_ref[...]
        prev_scr[...] = shift_ref[...]

    xr = x_ref[:, 0:RWKV_COLS]
    row_c = lax.broadcasted_iota(jnp.int32, (C, 1), 0)
    prev = jnp.where(row_c == 0, prev_scr[...], pltpu.roll(xr, 1, 0))
    prev_scr[...] = xr[C - 1:C, :]
    xs = xr + mu_ref[...] * (prev - xr)
    o_wl = 3 * D_RWKV
    r_all, k_all, v_all = xs[:, 0:D_RWKV], xs[:, D_RWKV:2 * D_RWKV], xs[:, 2 * D_RWKV:o_wl]
    wl = xs[:, o_wl:o_wl + W_LORA]
    al = xs[:, o_wl + W_LORA:o_wl + W_LORA + A_LORA]
    gl = xs[:, o_wl + W_LORA + A_LORA:RWKV_COLS]
    w0, a0_, k_k, k_a = pv_ref[0:1, :], pv_ref[1:2, :], pv_ref[2:3, :], pv_ref[3:4, :]
    r_k, gn_g, gn_b = pv_ref[4:5, :], pv_ref[5:6, :], pv_ref[6:7, :]
    w_all = -jax.nn.softplus(-(w0 + _bdot(jnp.tanh(wl), w2_ref[...]))) - 0.5
    lw_all = -jnp.exp(w_all)
    a_all = jax.nn.sigmoid(a0_ + _bdot(al, a2_ref[...]))
    gate_all = _bdot(jax.nn.sigmoid(gl), g2_ref[...])
    kkf_all = k_all * k_k
    k2_all = k_all * (1.0 + (a_all - 1.0) * k_a)
    if steps % C:
        live = (c * C + row_c) < steps
        lw_all = jnp.where(live, lw_all, 0.0)
        kkf_all = jnp.where(live, kkf_all, 0.0)
        k2_all = jnp.where(live, k2_all, 0.0)
        v_all = jnp.where(live, v_all, 0.0)

    row = lax.broadcasted_iota(jnp.int32, (C, C), 0)
    col = lax.broadcasted_iota(jnp.int32, (C, C), 1)
    tri_incl = col <= row
    tri_strict = col < row
    tri_f = jnp.where(tri_incl, 1.0, 0.0).astype(f32)
    eye = jnp.where(row == col, 1.0, 0.0).astype(f32)
    lane = lax.broadcasted_iota(jnp.int32, (1, LANE), 1)
    head0 = lane < HEAD_DIM
    r128 = lax.broadcasted_iota(jnp.int32, (LANE, LANE), 0)
    c128 = lax.broadcasted_iota(jnp.int32, (LANE, LANE), 1)
    same_head = (r128 < HEAD_DIM) == (c128 < HEAD_DIM)
    same_head_bf = jnp.where(same_head, 1.0, 0.0).astype(jnp.bfloat16)
    eye128 = r128 == c128

    pairs = range(RW_PAIRS)
    heads = (head0, ~head0)
    sls = [slice(pr * LANE, (pr + 1) * LANE) for pr in pairs]
    lw = [lw_all[:, sl] for sl in sls]
    cum = [jnp.dot(tri_f, lw[pr], precision=lax.Precision.HIGHEST, preferred_element_type=f32)
           for pr in pairs]
    tot = [cum[pr][C - 1:C, :] for pr in pairs]
    v = [v_all[:, sl] for sl in sls]
    k = [k2_all[:, sl] for sl in sls]
    r = [r_all[:, sl] for sl in sls]
    kkf = [kkf_all[:, sl] for sl in sls]
    ss = [_head_sum(kkf[pr] * kkf[pr], same_head_bf) for pr in pairs]
    kk = [kkf[pr] * lax.rsqrt(jnp.maximum(ss[pr], 1e-24)) for pr in pairs]
    b = [kk[pr] * a_all[:, sls[pr]] for pr in pairs]
    x, ym = [], []
    for pr in pairs:
        e_neg = jnp.exp(-cum[pr])
        x.append(jnp.concatenate([kk[pr] * jnp.exp(cum[pr] - lw[pr]),
                                  r[pr] * jnp.exp(cum[pr])], axis=0))
        ym.append(jnp.concatenate([k[pr] * e_neg, b[pr] * e_neg], axis=0))
    a0 = [st_ref[pr] for pr in pairs]
    xa = [_bdot(x[pr], a0[pr]) for pr in pairs]
    g = [[_bdot_nt(jnp.where(hm, x[pr], 0.0), ym[pr]) for hm in heads] for pr in pairs]
    lkv = [[_bdot(jnp.where(tri_strict, g[pr][h][:C, :C], 0.0), v[pr]) for h in range(2)] for pr in pairs]
    npow = [[jnp.where(tri_strict, -g[pr][h][:C, C:], 0.0) for h in range(2)] for pr in pairs]
    tmat = [[eye + npow[pr][h] for h in range(2)] for pr in pairs]
    for _ in range(int(math.log2(C)) - 1):
        npow = [[_bdot(npow[pr][h], npow[pr][h]) for h in range(2)] for pr in pairs]
        tmat = [[tmat[pr][h] + _bdot(tmat[pr][h], npow[pr][h]) for h in range(2)] for pr in pairs]
    rhs = [xa[pr][:C] + jnp.where(head0, lkv[pr][0], lkv[pr][1]) for pr in pairs]
    w = [jnp.where(head0, _bdot(tmat[pr][0], rhs[pr]), _bdot(tmat[pr][1], rhs[pr])) for pr in pairs]
    vw = [jnp.concatenate([v[pr], w[pr]], axis=0) for pr in pairs]
    y = []
    for pr in pairs:
        mr = [jnp.concatenate([jnp.where(tri_incl, g[pr][h][C:, :C], 0.0),
                               jnp.where(tri_incl, -g[pr][h][C:, C:], 0.0)], axis=1) for h in range(2)]
        y.append(xa[pr][C:] + jnp.where(head0, _bdot(mr[0], vw[pr]), _bdot(mr[1], vw[pr])))
    inv_hd = 1.0 / HEAD_DIM
    mean = [_head_sum(y[pr], same_head_bf) * inv_hd for pr in pairs]
    dev = [y[pr] - mean[pr] for pr in pairs]
    var = [_head_sum(dev[pr] * dev[pr], same_head_bf) * inv_hd for pr in pairs]
    rk = [_head_sum(r[pr] * k[pr] * r_k[:, sls[pr]], same_head_bf) for pr in pairs]
    for pr in pairs:
        yn = dev[pr] * lax.rsqrt(var[pr] + GN_EPS) * gn_g[:, sls[pr]] + gn_b[:, sls[pr]]
        y_ref[:, sls[pr]] = (yn + rk[pr] * v[pr]) * gate_all[:, sls[pr]]
    for pr in pairs:
        e_rem = jnp.exp(tot[pr] - cum[pr])
        kb = jnp.concatenate([k[pr] * e_rem, -(b[pr] * e_rem)], axis=0)
        upd = _bdot_tn(kb, vw[pr])
        p_col = jnp.sum(jnp.where(eye128, jnp.exp(tot[pr]), 0.0), axis=1, keepdims=True)
        st_ref[pr] = a0[pr] * p_col + jnp.where(same_head, upd, 0.0)


def _rwkv_time_mix(proj, shift_prev, s0, p):
    B, T, W = proj.shape
    C = RW_CHUNK
    t_pad = _round_up(T, C)
    x = proj if t_pad == T else jnp.pad(proj, ((0, 0), (0, t_pad - T), (0, 0)))
    a = jnp.swapaxes(s0, -1, -2).reshape(B, RW_PAIRS, 2, HEAD_DIM, HEAD_DIM)
    z = jnp.zeros_like(a[:, :, 0])
    a0 = jnp.concatenate([jnp.concatenate([a[:, :, 0], z], axis=-1),
                          jnp.concatenate([z, a[:, :, 1]], axis=-1)], axis=-2)
    zero = jnp.zeros((D_RWKV,), jnp.float32)
    pv = jnp.stack([p['rwkv_w0'], p['rwkv_a0'], p['rwkv_k_k'], p['rwkv_k_a'], p['rwkv_r_k'].reshape(-1),
                    p['rwkv_gn_g'], p['rwkv_gn_b'], zero])
    bf = lambda t: t.astype(jnp.bfloat16)
    full = lambda shape: pl.BlockSpec(shape, lambda bi, ci: (0,) * len(shape))
    st_spec = pl.BlockSpec((None, RW_PAIRS, LANE, LANE), lambda bi, ci: (bi, 0, 0, 0))
    y, st = pl.pallas_call(
        functools.partial(_rwkv_mix_kernel, steps=T),
        grid=(B, t_pad // C),
        in_specs=[pl.BlockSpec((None, C, W), lambda bi, ci: (bi, ci, 0)),
                  pl.BlockSpec((None, 1, RWKV_COLS), lambda bi, ci: (bi, 0, 0)),
                  full((1, RWKV_COLS)), full((8, D_RWKV)), full((W_LORA, D_RWKV)),
                  full((A_LORA, D_RWKV)), full((G_LORA, D_RWKV)), st_spec],
        out_specs=[pl.BlockSpec((None, C, D_RWKV), lambda bi, ci: (bi, ci, 0)), st_spec],
        out_shape=[jax.ShapeDtypeStruct((B, t_pad, D_RWKV), jnp.float32),
                   jax.ShapeDtypeStruct((B, RW_PAIRS, LANE, LANE), jnp.float32)],
        scratch_shapes=[pltpu.VMEM((1, RWKV_COLS), jnp.float32)],
        compiler_params=pltpu.CompilerParams(dimension_semantics=("arbitrary", "arbitrary"),
                                             vmem_limit_bytes=_VMEM_LIMIT),
        name="rwkv_mix",
    )(x, shift_prev, p['rwkv_mu'].reshape(1, RWKV_COLS), pv, bf(p['rwkv_w2']), bf(p['rwkv_a2']),
      bf(p['rwkv_g2']), a0)
    s_t = jnp.stack([st[:, :, :HEAD_DIM, :HEAD_DIM], st[:, :, HEAD_DIM:, HEAD_DIM:]], axis=2)
    s_t = jnp.swapaxes(s_t.reshape(B, H_RWKV, HEAD_DIM, HEAD_DIM), -1, -2)
    return y[:, :T], s_t, proj[:, T - 1:T, :RWKV_COLS]


def _extract_top(buf_ref, out_ref, nrows, width):
    iota = lax.broadcasted_iota(jnp.int32, (nrows, width), 0)

    def body(r, carry):
        s = buf_ref[...]
        mx = jnp.max(s, axis=0, keepdims=True)
        first = jnp.min(jnp.where(s == mx, iota, nrows), axis=0, keepdims=True)
        buf_ref[...] = jnp.where(iota == first, -jnp.inf, s)
        out_ref[pl.ds(r, 1), :] = mx
        return carry

    lax.fori_loop(0, P_TOPK, body, 0)


def _peer_score_kernel(x_ref, wq_ref, sk_ref, s1_ref, s2_ref, st_ref, buf, cand, hv0, hv1, tv):
    tn = x_ref.shape[0]
    q = jnp.dot(x_ref[...].astype(jnp.bfloat16), wq_ref[...],
                preferred_element_type=jnp.float32).astype(jnp.bfloat16)
    for h in range(P_HEADS):
        for c, (s_ref, hv) in enumerate(((s1_ref, hv0), (s2_ref, hv1))):
            col = (2 * h + c) * N_KEYS
            s_t = lax.dot_general(sk_ref[2 * h + c], q[:, col:col + N_KEYS],
                                  (((1,), (1,)), ((), ())), preferred_element_type=jnp.float32)
            s_ref[h] = s_t
            buf[...] = s_t
            _extract_top(buf, hv, N_KEYS, tn)
        pieces = [hv0[a:a + 1, :] + hv1[0:P_TOPK // (a + 1), :] for a in range(P_TOPK)]
        n_cand = sum(P_TOPK // (a + 1) for a in range(P_TOPK))
        pieces.append(jnp.full((cand.shape[0] - n_cand, tn), -jnp.inf, jnp.float32))
        cand[...] = jnp.concatenate(pieces, axis=0)
        _extract_top(cand, tv, cand.shape[0], tn)
        z = jnp.sum(jnp.exp(tv[...] - tv[0:1, :]), axis=0, keepdims=True)
        st_ref[0, h:h + 1, :] = tv[P_TOPK - 1:P_TOPK, :]
        st_ref[1, h:h + 1, :] = hv0[0:1, :]
        st_ref[2, h:h + 1, :] = hv1[0:1, :]
        st_ref[3, h:h + 1, :] = 1.0 / z


def _peer_scores(x, wq_bf, sk_bf, *, tn=256):
    n = x.shape[0]
    tn = min(tn, n)
    return pl.pallas_call(
        _peer_score_kernel,
        grid=(n // tn,),
        in_specs=[pl.BlockSpec((tn, D_MODEL), lambda i: (i, 0)),
                  pl.BlockSpec((D_MODEL, P_HEADS * P_DKEY), lambda i: (0, 0)),
                  pl.BlockSpec((2 * P_HEADS, N_KEYS, P_DKEY // 2), lambda i: (0, 0, 0))],
        out_specs=[pl.BlockSpec((P_HEADS, N_KEYS, tn), lambda i: (0, 0, i)),
                   pl.BlockSpec((P_HEADS, N_KEYS, tn), lambda i: (0, 0, i)),
                   pl.BlockSpec((4, P_HEADS, tn), lambda i: (0, 0, i))],
        out_shape=[jax.ShapeDtypeStruct((P_HEADS, N_KEYS, n), jnp.float32),
                   jax.ShapeDtypeStruct((P_HEADS, N_KEYS, n), jnp.float32),
                   jax.ShapeDtypeStruct((4, P_HEADS, n), jnp.float32)],
        scratch_shapes=[pltpu.VMEM((N_KEYS, tn), jnp.float32),
                        pltpu.VMEM((_round_up(sum(P_TOPK // (a + 1) for a in range(P_TOPK)), 8), tn),
                                   jnp.float32),
                        pltpu.VMEM((P_TOPK, tn), jnp.float32),
                        pltpu.VMEM((P_TOPK, tn), jnp.float32),
                        pltpu.VMEM((P_TOPK, tn), jnp.float32)],
        compiler_params=pltpu.CompilerParams(dimension_semantics=("arbitrary",),
                                             vmem_limit_bytes=_VMEM_LIMIT),
        name="peer_scores",
    )(x, wq_bf, sk_bf)


def _peer_expert_kernel(x_ref, u_ref, v_ref, s1_ref, s2_ref, st_ref, o_ref, e2_scr, h_scr, p_scr,
                        *, chunk):
    j = pl.program_id(1)
    tn = x_ref.shape[0]
    te = u_ref.shape[1]

    @pl.when(j == 0)
    def _():
        o_ref[...] = jnp.zeros_like(o_ref)
        for h in range(P_HEADS):
            e2_scr[h] = jnp.exp(s2_ref[h] - st_ref[2, h:h + 1, :])

    def hidden(c):
        tsl = slice(c * chunk, (c + 1) * chunk)
        h_scr[tsl, :] = jnp.dot(x_ref[tsl, :], u_ref[...], preferred_element_type=jnp.float32)

    a_rows = [[s1_ref[h, pl.ds(j * (te // N_KEYS) + ii, 1), :] for h in range(P_HEADS)]
              for ii in range(te // N_KEYS)]
    e1_rows = [[jnp.exp(a_rows[ii][h] - st_ref[1, h:h + 1, :]) * st_ref[3, h:h + 1, :]
                for h in range(P_HEADS)] for ii in range(te // N_KEYS)]

    def weigh(c):
        for sub in range(chunk // LANE):
            weigh_lanes(slice(c * chunk + sub * LANE, c * chunk + (sub + 1) * LANE))

    def weigh_lanes(tsl):
        for ii in range(te // N_KEYS):
            w_t = jnp.zeros((N_KEYS, LANE), jnp.float32)
            for h in range(P_HEADS):
                val = a_rows[ii][h][:, tsl] + s2_ref[h, :, tsl]
                w_t = w_t + jnp.where(val >= st_ref[0, h:h + 1, tsl],
                                      e1_rows[ii][h][:, tsl] * e2_scr[h, :, tsl], 0.0)
            hh = h_scr[tsl, ii * N_KEYS:(ii + 1) * N_KEYS]
            g = 0.5 * hh * (1.0 + lax.erf(hh * 0.7071067811865476))
            p_scr[tsl, ii * N_KEYS:(ii + 1) * N_KEYS] = (w_t.T * g).astype(jnp.bfloat16)

    def project(c):
        tsl = slice(c * chunk, (c + 1) * chunk)
        o_ref[tsl, :] += jnp.dot(p_scr[tsl, :], v_ref[...], preferred_element_type=jnp.float32)

    n_chunks = tn // chunk
    hidden(0)
    for c in range(n_chunks):
        if c + 1 < n_chunks:
            hidden(c + 1)
        weigh(c)
        project(c)


def _peer_experts(x_bf, u_bf, v_bf, s1t, s2t, stats, *, tn=1024, te=512, chunk=256):
    n = x_bf.shape[0]
    tn = min(tn, n)
    chunk = min(chunk, tn)
    once = pl.Buffered(1)
    return pl.pallas_call(
        functools.partial(_peer_expert_kernel, chunk=chunk),
        grid=(n // tn, N_EXPERTS // te),
        in_specs=[pl.BlockSpec((tn, D_MODEL), lambda i, j: (i, 0), pipeline_mode=once),
                  pl.BlockSpec((D_MODEL, te), lambda i, j: (0, j)),
                  pl.BlockSpec((te, D_MODEL), lambda i, j: (j, 0)),
                  pl.BlockSpec((P_HEADS, N_KEYS, tn), lambda i, j: (0, 0, i), pipeline_mode=once),
                  pl.BlockSpec((P_HEADS, N_KEYS, tn), lambda i, j: (0, 0, i), pipeline_mode=once),
                  pl.BlockSpec((4, P_HEADS, tn), lambda i, j: (0, 0, i), pipeline_mode=once)],
        out_specs=pl.BlockSpec((tn, D_MODEL), lambda i, j: (i, 0)),
        out_shape=jax.ShapeDtypeStruct((n, D_MODEL), jnp.float32),
        scratch_shapes=[pltpu.VMEM((P_HEADS, N_KEYS, tn), jnp.float32),
                        pltpu.VMEM((tn, te), jnp.float32),
                        pltpu.VMEM((tn, te), jnp.bfloat16)],
        compiler_params=pltpu.CompilerParams(dimension_semantics=("arbitrary", "arbitrary"),
                                             vmem_limit_bytes=_VMEM_LIMIT),
        name="peer_experts",
    )(x_bf, u_bf, v_bf, s1t, s2t, stats)


def _peer_ffn(x, wq_bf, sk_bf, u_bf, v_bf):
    B, T, D = x.shape
    n = B * T
    xt = x.reshape(n, D)
    n_pad = _round_up(n, LANE)
    if n_pad != n:
        xt = jnp.pad(xt, ((0, n_pad - n), (0, 0)))
    s1t, s2t, stats = _peer_scores(xt, wq_bf, sk_bf)
    out = _peer_experts(xt.astype(jnp.bfloat16), u_bf, v_bf, s1t, s2t, stats)
    return out[:n].reshape(B, T, D)


NSA_TQ = 256
NSA_GSPLIT = 4


def _flash_tile(q4, k_t, v_t, mask4, m, l, acc):
    s = lax.dot_general(q4, k_t, (((1,), (1,)), ((), ())), preferred_element_type=jnp.float32)
    s = jnp.where(mask4, s, NEG_INF)
    m_new = jnp.maximum(m, jnp.max(s, axis=-1, keepdims=True))
    p = jnp.where(mask4, jnp.exp(s - m_new), 0.0)
    alpha = jnp.exp(m - m_new)
    l_new = alpha * l + jnp.sum(p, axis=-1, keepdims=True)
    acc_new = alpha * acc + jnp.dot(p.astype(jnp.bfloat16), v_t, preferred_element_type=jnp.float32)
    return m_new, l_new, acc_new


def _nsa_prompt_kernel(q_ref, gl_ref, kc_ref, vc_ref, ks_ref, vs_ref, kw_ref, vw_ref, cw_ref, ex_ref,
                       o_ref, kcb_scr, vcb_scr, imp_scr, selx_scr, *, seq):
    qt = pl.program_id(2)
    tq = NSA_TQ
    n_cmp = seq // CMP_BLOCK
    n_sel = seq // SEL_BLOCK
    G = NSA_GROUP
    q0 = qt * tq

    @pl.when(qt == 0)
    def _():
        kc = kc_ref[...].reshape(n_cmp, CMP_BLOCK, HEAD_DIM)
        vc = vc_ref[...].reshape(n_cmp, CMP_BLOCK, HEAD_DIM)
        kcb_scr[...] = jnp.zeros_like(kcb_scr)
        vcb_scr[...] = jnp.zeros_like(vcb_scr)
        kcb_scr[0:n_cmp, :] = jnp.sum(kc * cw_ref[0][None], axis=1)
        vcb_scr[0:n_cmp, :] = jnp.sum(vc * cw_ref[1][None], axis=1).astype(jnp.bfloat16)

    qpos_col = q0 + lax.broadcasted_iota(jnp.int32, (tq, 1), 0)
    qf = [q_ref[g] * (HEAD_DIM ** -0.5) for g in range(G)]
    qs = [t.astype(jnp.bfloat16) for t in qf]
    gates = jax.nn.sigmoid(gl_ref[...])

    cmp_end = (lax.broadcasted_iota(jnp.int32, (1, LANE), 1) + 1) * CMP_BLOCK - 1
    cmask = (cmp_end <= qpos_col) & (lax.broadcasted_iota(jnp.int32, (1, LANE), 1) < n_cmp)
    o_cmp = []
    imp = jnp.zeros((tq, LANE), jnp.float32)
    for g in range(G):
        s = lax.dot_general(qf[g], kcb_scr[...], (((1,), (1,)), ((), ())),
                            precision=lax.Precision.HIGHEST, preferred_element_type=jnp.float32)
        s = jnp.where(cmask, s, NEG_INF)
        e = jnp.exp(s - jnp.max(s, axis=-1, keepdims=True))
        p = jnp.where(cmask, e / jnp.sum(e, axis=-1, keepdims=True), 0.0)
        o_cmp.append(jnp.dot(p.astype(jnp.bfloat16), vcb_scr[...], preferred_element_type=jnp.float32))
        imp = imp + p
    imp_t = imp.T
    ratio = SEL_BLOCK // CMP_BLOCK
    parts = []
    for c in range(tq // LANE):
        imp_scr[c] = imp_t[:, c * LANE:(c + 1) * LANE]
        part = imp_scr[c, pl.ds(0, n_sel, stride=ratio), :]
        for r in range(1, ratio):
            part = part + imp_scr[c, pl.ds(r, n_sel, stride=ratio), :]
        parts.append(part)
    imp_s = jnp.concatenate(parts, axis=1)
    qpos_row = q0 + lax.broadcasted_iota(jnp.int32, (1, tq), 1)
    cur = qpos_row // SEL_BLOCK
    blk = lax.broadcasted_iota(jnp.int32, (n_sel, tq), 0)
    forced = (blk == 0) | (blk == cur) | (blk == cur - 1)
    imp_s = jnp.where(blk > cur, -1.0, jnp.where(forced, FORCE_SCORE, imp_s))
    rank = jnp.zeros((n_sel, tq), jnp.int32)
    for mm in range(n_sel):
        row = imp_s[mm:mm + 1, :]
        ahead = (row > imp_s) | ((row == imp_s) & (mm < blk))
        rank = rank + jnp.where(ahead, 1, 0)
    sel = jnp.where((rank < min(SEL_TOPN, n_sel)) & (imp_s >= 0.0), 1.0, 0.0)
    sel_pad = jnp.concatenate([sel, jnp.zeros((LANE - n_sel, tq), jnp.float32)], axis=0)
    sel_q = sel_pad.T.astype(jnp.bfloat16)
    selx_scr[...] = jnp.dot(sel_q, ex_ref[...], preferred_element_type=jnp.float32)

    kpos_l = lax.broadcasted_iota(jnp.int32, (1, tq), 1)
    GS = NSA_GSPLIT
    qpos_r = jnp.concatenate([qpos_col] * GS, axis=0)
    init = (jnp.full((GS * tq, 1), NEG_INF, jnp.float32), jnp.zeros((GS * tq, 1), jnp.float32),
            jnp.zeros((GS * tq, HEAD_DIM), jnp.float32))

    for g0 in range(0, G, GS):
        q_r = jnp.concatenate(qs[g0:g0 + GS], axis=0)

        def sel_body(kt, carry):
            k0 = pl.multiple_of(kt * tq, tq)
            chosen = selx_scr[:, pl.ds(k0, tq)] > 0.5
            mask = jnp.concatenate([chosen] * GS, axis=0) & ((k0 + kpos_l) <= qpos_r)
            return _flash_tile(q_r, ks_ref[pl.ds(k0, tq), :], vs_ref[pl.ds(k0, tq), :], mask, *carry)

        _, l_s, acc_s = lax.fori_loop(0, qt + 1, sel_body, init)
        o_sel = acc_s / l_s

        def win_body(kt, carry):
            k0 = pl.multiple_of(kt * tq, tq)
            kpos = k0 + kpos_l
            mask = (kpos <= qpos_r) & (kpos >= qpos_r - (WINDOW - 1))
            return _flash_tile(q_r, kw_ref[pl.ds(k0, tq), :], vw_ref[pl.ds(k0, tq), :], mask, *carry)

        _, l_w, acc_w = lax.fori_loop(jnp.maximum(qt - WINDOW // tq, 0), qt + 1, win_body, init)
        o_win = acc_w / l_w

        for gi in range(GS):
            g = g0 + gi
            rows = slice(gi * tq, (gi + 1) * tq)
            o_ref[g] = (gates[:, 3 * g:3 * g + 1] * o_cmp[g] + gates[:, 3 * g + 1:3 * g + 2] * o_sel[rows]
                        + gates[:, 3 * g + 2:3 * g + 3] * o_win[rows])


def _nsa_prompt(xn, cmp_w):
    B, T, _ = xn.shape
    KVH, G, HD = NSA_KV_HEADS, NSA_GROUP, HEAD_DIM
    kvc = NSA_KV_COLS
    tq = NSA_TQ
    q = xn[..., :D_NSA].reshape(B, T, KVH, G, HD).transpose(0, 2, 3, 1, 4)
    kv = [xn[..., D_NSA + i * kvc:D_NSA + (i + 1) * kvc].reshape(B, T, KVH, HD).transpose(0, 2, 1, 3)
          for i in range(6)]
    kc, vc = kv[0], kv[1]
    ks, vs, kw, vw = [t.astype(jnp.bfloat16) for t in kv[2:]]
    gl = xn[..., D_NSA + 6 * kvc:].reshape(B, T, KVH, 3 * G).transpose(0, 2, 1, 3)
    cw = jnp.broadcast_to(cmp_w[:, :, None], (2, CMP_BLOCK, HD))
    expand = (jnp.arange(LANE)[:, None] == (jnp.arange(T) // SEL_BLOCK)[None, :]).astype(jnp.bfloat16)
    kv_spec = pl.BlockSpec((None, None, T, HD), lambda b, k, t: (b, k, 0, 0))
    out = pl.pallas_call(
        functools.partial(_nsa_prompt_kernel, seq=T),
        grid=(B, KVH, T // tq),
        in_specs=[pl.BlockSpec((None, None, G, tq, HD), lambda b, k, t: (b, k, 0, t, 0)),
                  pl.BlockSpec((None, None, tq, 3 * G), lambda b, k, t: (b, k, t, 0)),
                  kv_spec, kv_spec, kv_spec, kv_spec, kv_spec, kv_spec,
                  pl.BlockSpec((2, CMP_BLOCK, HD), lambda b, k, t: (0, 0, 0)),
                  pl.BlockSpec((LANE, T), lambda b, k, t: (0, 0))],
        out_specs=pl.BlockSpec((None, None, G, tq, HD), lambda b, k, t: (b, k, 0, t, 0)),
        out_shape=jax.ShapeDtypeStruct((B, KVH, G, T, HD), jnp.float32),
        scratch_shapes=[pltpu.VMEM((LANE, HD), jnp.float32),
                        pltpu.VMEM((LANE, HD), jnp.bfloat16),
                        pltpu.VMEM((tq // LANE, LANE, LANE), jnp.float32),
                        pltpu.VMEM((tq, T), jnp.float32)],
        compiler_params=pltpu.CompilerParams(dimension_semantics=("arbitrary", "arbitrary", "arbitrary"),
                                             vmem_limit_bytes=_VMEM_LIMIT),
        name="nsa_prompt",
    )(q, gl, kc, vc, ks, vs, kw, vw, cw, expand)
    return out.transpose(0, 3, 1, 2, 4).reshape(B, T, D_NSA)


def _nsa_prompt_mix(xn, n_keep, cmp_w):
    B, T, _ = xn.shape
    kv5 = lambda i: xn[..., D_NSA + 2 * i * NSA_KV_COLS:D_NSA + 2 * (i + 1) * NSA_KV_COLS].reshape(
        B, T, 2, NSA_KV_HEADS, HEAD_DIM)
    return _nsa_prompt(xn, cmp_w), kv5(0), kv5(1), kv5(2)[:, -n_keep:]


NSA_PG = 4
PAGE_COLS = 2 * NSA_KV_COLS
NSA_ROWS = NSA_KV_HEADS * NSA_GROUP


def _page_specs(layer, n):
    def spec(j):
        return pl.BlockSpec((None, None, PAGE_SIZE, PAGE_COLS),
                            lambda b, s, pt: (layer, pt[b, s * n + j], 0, 0))
    return [spec(j) for j in range(n)]


def _cmp_pool_kernel(pt_ref, *refs):
    pages, cw_ref, o_ref = refs[:NSA_PG], refs[NSA_PG], refs[NSA_PG + 1]
    per_page = PAGE_SIZE // CMP_BLOCK
    out = []
    for pg in pages:
        blocks = pg[...].reshape(per_page, CMP_BLOCK, PAGE_COLS)
        out.append(jnp.sum(blocks * cw_ref[...][None], axis=1))
    o_ref[...] = jnp.concatenate(out, axis=0)


def _cmp_pool(cache, page_table, layer, cw):
    B, n_pages = page_table.shape
    per_step = NSA_PG * PAGE_SIZE // CMP_BLOCK
    return pl.pallas_call(
        _cmp_pool_kernel,
        grid_spec=pltpu.PrefetchScalarGridSpec(
            num_scalar_prefetch=1, grid=(B, n_pages // NSA_PG),
            in_specs=_page_specs(layer, NSA_PG) + [pl.BlockSpec((CMP_BLOCK, PAGE_COLS), lambda b, s, pt: (0, 0))],
            out_specs=pl.BlockSpec((None, per_step, PAGE_COLS), lambda b, s, pt: (b, s, 0))),
        out_shape=jax.ShapeDtypeStruct((B, n_pages * PAGE_SIZE // CMP_BLOCK, PAGE_COLS), jnp.float32),
        compiler_params=pltpu.CompilerParams(dimension_semantics=("arbitrary", "arbitrary"),
                                             vmem_limit_bytes=_VMEM_LIMIT),
        name="nsa_cmp_pool",
    )(page_table, *([cache] * NSA_PG), cw)


def _cmp_select_kernel(q_ref, kv_ref, ocmp_ref, sel_ref, pt_scr, imp_scr, *, past, steps):
    T = steps
    R = NSA_ROWS * T
    n_cmp = past // CMP_BLOCK
    n_sel = past // SEL_BLOCK + 1
    n_lane = sel_ref.shape[-1]
    ratio = SEL_BLOCK // CMP_BLOCK
    rows_t = lax.broadcasted_iota(jnp.int32, (R, 1), 0) % T
    qpos = past + rows_t
    cmp_end = (lax.broadcasted_iota(jnp.int32, (1, n_cmp), 1) + 1) * CMP_BLOCK - 1
    cmask = cmp_end <= qpos
    q = q_ref[...]
    p_all, o_all = [], []
    for kh in range(NSA_KV_HEADS):
        rows = slice(kh * NSA_GROUP * T, (kh + 1) * NSA_GROUP * T)
        kcb = kv_ref[:, kh * HEAD_DIM:(kh + 1) * HEAD_DIM]
        vcb = kv_ref[:, NSA_KV_COLS + kh * HEAD_DIM:NSA_KV_COLS + (kh + 1) * HEAD_DIM].astype(jnp.bfloat16)
        s = lax.dot_general(q[rows], kcb, (((1,), (1,)), ((), ())), precision=lax.Precision.HIGHEST,
                            preferred_element_type=jnp.float32)
        s = jnp.where(cmask[rows], s, NEG_INF)
        e = jnp.exp(s - jnp.max(s, axis=-1, keepdims=True))
        p = jnp.where(cmask[rows], e / jnp.sum(e, axis=-1, keepdims=True), 0.0)
        o_all.append(jnp.dot(p.astype(jnp.bfloat16), vcb, preferred_element_type=jnp.float32))
        imp = p[0:T]
        for g in range(1, NSA_GROUP):
            imp = imp + p[g * T:(g + 1) * T]
        p_all.extend([imp] * NSA_GROUP)
    ocmp_ref[...] = jnp.concatenate(o_all, axis=0)
    imp_rows = jnp.concatenate(p_all + [jnp.zeros((LANE - R, n_cmp), jnp.float32)], axis=0)
    for c in range(n_cmp // LANE):
        pt_scr[c] = imp_rows[:, c * LANE:(c + 1) * LANE].T
    per = LANE // ratio
    for c in range(n_cmp // LANE):
        part = pt_scr[c, pl.ds(0, per, stride=ratio), :]
        for r in range(1, ratio):
            part = part + pt_scr[c, pl.ds(r, per, stride=ratio), :]
        imp_scr[c * per:(c + 1) * per, :] = part
    n_rows = imp_scr.shape[0]
    imp_scr[n_sel - 1:n_rows, :] = jnp.zeros((n_rows - n_sel + 1, LANE), jnp.float32)
    lane_t = lax.broadcasted_iota(jnp.int32, (1, LANE), 1) % T
    cur = (past + lane_t) // SEL_BLOCK
    blk = lax.broadcasted_iota(jnp.int32, (n_rows, LANE), 0)
    forced = (blk == 0) | (blk == cur) | (blk == cur - 1)
    imp_s = jnp.where(blk > cur, -1.0, jnp.where(forced, FORCE_SCORE, imp_scr[...]))
    imp_s = jnp.where(blk < n_sel, imp_s, -2.0)
    imp_scr[...] = imp_s

    def rank_body(m, rank):
        row = imp_scr[pl.ds(m, 1), :]
        ahead = (row > imp_s) | ((row == imp_s) & (m < blk))
        return rank + jnp.where(ahead, 1, 0)

    rank = lax.fori_loop(0, n_sel, rank_body, jnp.zeros((n_rows, LANE), jnp.int32))
    sel = jnp.where((rank < min(SEL_TOPN, n_sel)) & (imp_s >= 0.0), 1.0, 0.0)
    sel = jnp.concatenate([sel, jnp.zeros((n_lane - n_rows, LANE), jnp.float32)], axis=0)
    for c in range(n_lane // LANE):
        sel_ref[:, c * LANE:(c + 1) * LANE] = sel[c * LANE:(c + 1) * LANE, :].T


def _cmp_select(q_rows, kvcb, past, steps):
    B, R, _ = q_rows.shape
    n_cmp = kvcb.shape[1]
    n_sel = past // SEL_BLOCK + 1
    n_lane = _round_up(n_sel, LANE)
    n_rows = _round_up(n_sel, 8)
    return pl.pallas_call(
        functools.partial(_cmp_select_kernel, past=past, steps=steps),
        grid=(B,),
        in_specs=[pl.BlockSpec((None, R, HEAD_DIM), lambda b: (b, 0, 0)),
                  pl.BlockSpec((None, n_cmp, PAGE_COLS), lambda b: (b, 0, 0))],
        out_specs=[pl.BlockSpec((None, R, HEAD_DIM), lambda b: (b, 0, 0)),
                   pl.BlockSpec((None, LANE, n_lane), lambda b: (b, 0, 0))],
        out_shape=[jax.ShapeDtypeStruct((B, R, HEAD_DIM), jnp.float32),
                   jax.ShapeDtypeStruct((B, LANE, n_lane), jnp.float32)],
        scratch_shapes=[pltpu.VMEM((n_cmp // LANE, LANE, LANE), jnp.float32),
                        pltpu.VMEM((n_rows, LANE), jnp.float32)],
        compiler_params=pltpu.CompilerParams(dimension_semantics=("arbitrary",),
                                             vmem_limit_bytes=_VMEM_LIMIT),
        name="nsa_cmp_select",
    )(q_rows, kvcb)


def _soft_update(s, mask, v, m, l, acc):
    s = jnp.where(mask, s, NEG_INF)
    m_new = jnp.maximum(m, jnp.max(s, axis=-1, keepdims=True))
    p = jnp.where(mask, jnp.exp(s - m_new), 0.0)
    alpha = jnp.exp(m - m_new)
    return (m_new, alpha * l + jnp.sum(p, axis=-1, keepdims=True),
            alpha * acc + jnp.dot(p.astype(jnp.bfloat16), v, preferred_element_type=jnp.float32))


def _sel_win_kernel(pt_ref, *refs, past, steps):
    pages = refs[:NSA_PG]
    (q_ref, sel_ref, ocmp_ref, nsel_ref, wpre_ref, nwin_ref, gate_ref,
     o_ref, m_scr, l_scr, acc_scr) = refs[NSA_PG:]
    s_id = pl.program_id(1)
    T = steps
    GT = NSA_GROUP * T
    R = NSA_ROWS * T
    n_lane = sel_ref.shape[-1]
    rows_t = lax.broadcasted_iota(jnp.int32, (GT, 1), 0) % T

    @pl.when(s_id == 0)
    def _():
        m_scr[...] = jnp.full_like(m_scr, NEG_INF)
        l_scr[...] = jnp.zeros_like(l_scr)
        acc_scr[...] = jnp.zeros_like(acc_scr)

    q = q_ref[...].astype(jnp.bfloat16)
    sel_bf = sel_ref[0:R, :].astype(jnp.bfloat16)
    n_keys = NSA_PG * PAGE_SIZE
    blk_id = lax.broadcasted_iota(jnp.int32, (n_lane, n_keys), 0)
    key_blk = s_id * (n_keys // SEL_BLOCK) + lax.broadcasted_iota(jnp.int32, (n_lane, n_keys), 1) // SEL_BLOCK

    def head_cols(x, kh, off):
        return x[:, off + kh * HEAD_DIM:off + (kh + 1) * HEAD_DIM]

    kv = jnp.concatenate([pg[...] for pg in pages], axis=0).astype(jnp.bfloat16)
    expand = jnp.where(blk_id == key_blk, 1.0, 0.0).astype(jnp.bfloat16)
    chosen = jnp.dot(sel_bf, expand, preferred_element_type=jnp.float32) > 0.5
    head_rows = [slice(kh * GT, (kh + 1) * GT) for kh in range(NSA_KV_HEADS)]
    scores = [lax.dot_general(q[rows], head_cols(kv, kh, 0), (((1,), (1,)), ((), ())),
                              preferred_element_type=jnp.float32) for kh, rows in enumerate(head_rows)]
    upd = [_soft_update(scores[kh], chosen[rows], head_cols(kv, kh, NSA_KV_COLS),
                        m_scr[rows], l_scr[rows], acc_scr[rows]) for kh, rows in enumerate(head_rows)]
    for kh, rows in enumerate(head_rows):
        m_scr[rows], l_scr[rows], acc_scr[rows] = upd[kh]

    @pl.when(s_id == pl.num_programs(1) - 1)
    def _():
        new_blk = past // SEL_BLOCK
        tpad = nsel_ref.shape[0]
        jn = lax.broadcasted_iota(jnp.int32, (1, tpad), 1)
        nsel = nsel_ref[...].astype(jnp.bfloat16)
        nwin = nwin_ref[...].astype(jnp.bfloat16)
        wpre = wpre_ref[...].astype(jnp.bfloat16)
        jp = lax.broadcasted_iota(jnp.int32, (1, WINDOW), 1)
        gates = jax.nn.sigmoid(gate_ref[...])
        for kh in range(NSA_KV_HEADS):
            rows = slice(kh * GT, (kh + 1) * GT)
            pick = sel_ref[kh * GT:(kh + 1) * GT, new_blk:new_blk + 1] > 0.5
            s = lax.dot_general(q[rows], head_cols(nsel, kh, 0), (((1,), (1,)), ((), ())),
                                preferred_element_type=jnp.float32)
            m, l, acc = _soft_update(s, pick & (jn <= rows_t) & (jn < T), head_cols(nsel, kh, NSA_KV_COLS),
                                     m_scr[rows], l_scr[rows], acc_scr[rows])
            o_sel = acc / l
            init = (jnp.full((GT, 1), NEG_INF, jnp.float32), jnp.zeros((GT, 1), jnp.float32),
                    jnp.zeros((GT, HEAD_DIM), jnp.float32))
            s = lax.dot_general(q[rows], head_cols(wpre, kh, 0), (((1,), (1,)), ((), ())),
                                preferred_element_type=jnp.float32)
            st = _soft_update(s, jp > rows_t, head_cols(wpre, kh, NSA_KV_COLS), *init)
            s = lax.dot_general(q[rows], head_cols(nwin, kh, 0), (((1,), (1,)), ((), ())),
                                preferred_element_type=jnp.float32)
            _, l_w, acc_w = _soft_update(s, (jn <= rows_t) & (jn < T), head_cols(nwin, kh, NSA_KV_COLS), *st)
            o_win = acc_w / l_w
            gt = gates[rows]
            o_ref[rows, :] = gt[:, 0:1] * ocmp_ref[rows, :] + gt[:, 1:2] * o_sel + gt[:, 2:3] * o_win


def _sel_win(cache, page_table, layer, q_rows, sel_rows, o_cmp, new_sel, win_prefix, new_win, gate_rows,
             past, steps):
    B, n_pages = page_table.shape
    R = q_rows.shape[1]
    n_lane = sel_rows.shape[-1]
    tpad = new_sel.shape[1]
    per_b = lambda shape: pl.BlockSpec((None,) + shape, lambda b, s, pt: (b,) + (0,) * len(shape))
    return pl.pallas_call(
        functools.partial(_sel_win_kernel, past=past, steps=steps),
        grid_spec=pltpu.PrefetchScalarGridSpec(
            num_scalar_prefetch=1, grid=(B, n_pages // NSA_PG),
            in_specs=_page_specs(layer, NSA_PG) + [
                per_b((R, HEAD_DIM)), per_b((LANE, n_lane)), per_b((R, HEAD_DIM)),
                per_b((tpad, PAGE_COLS)), per_b((WINDOW, PAGE_COLS)), per_b((tpad, PAGE_COLS)),
                per_b((R, 3))],
            out_specs=per_b((R, HEAD_DIM)),
            scratch_shapes=[pltpu.VMEM((R, 1), jnp.float32), pltpu.VMEM((R, 1), jnp.float32),
                            pltpu.VMEM((R, HEAD_DIM), jnp.float32)]),
        out_shape=jax.ShapeDtypeStruct((B, R, HEAD_DIM), jnp.float32),
        compiler_params=pltpu.CompilerParams(dimension_semantics=("arbitrary", "arbitrary"),
                                             vmem_limit_bytes=_VMEM_LIMIT),
        name="nsa_sel_win",
    )(page_table, *([cache] * NSA_PG), q_rows, sel_rows, o_cmp, new_sel, win_prefix, new_win, gate_rows)


def _nsa_decode_mix(xn, cache_cmp, cache_sel, page_table, layer, win_prefix, cmp_w):
    B, T, _ = xn.shape
    n_pages = page_table.shape[1]
    past = n_pages * PAGE_SIZE
    assert T <= SEL_BLOCK and n_pages % NSA_PG == 0 and win_prefix.shape[1] == WINDOW
    assert (past // CMP_BLOCK) % LANE == 0
    KVH, G, HD = NSA_KV_HEADS, NSA_GROUP, HEAD_DIM
    paged = lambda c: c.reshape(c.shape[0], c.shape[1], PAGE_SIZE, PAGE_COLS)
    kv5 = lambda i: xn[..., D_NSA + 2 * i * NSA_KV_COLS:D_NSA + 2 * (i + 1) * NSA_KV_COLS]
    new_cmp, new_sel, new_win = kv5(0), kv5(1), kv5(2)
    q_rows = (xn[..., :D_NSA] * (HD ** -0.5)).reshape(B, T, KVH, G, HD).transpose(0, 2, 3, 1, 4)
    q_rows = q_rows.reshape(B, KVH * G * T, HD)
    gate_rows = xn[..., D_NSA + 6 * NSA_KV_COLS:].reshape(B, T, KVH, G, 3).transpose(0, 2, 3, 1, 4)
    gate_rows = gate_rows.reshape(B, KVH * G * T, 3)
    cw = jnp.concatenate([jnp.broadcast_to(cmp_w[0][:, None], (CMP_BLOCK, NSA_KV_COLS)),
                          jnp.broadcast_to(cmp_w[1][:, None], (CMP_BLOCK, NSA_KV_COLS))], axis=1)
    kvcb = _cmp_pool(paged(cache_cmp), page_table, layer, cw)
    o_cmp, sel_rows = _cmp_select(q_rows, kvcb, past, T)
    tpad = _round_up(T, 8)
    padt = lambda t: jnp.pad(t, ((0, 0), (0, tpad - T), (0, 0)))
    out = _sel_win(paged(cache_sel), page_table, layer, q_rows, sel_rows, o_cmp, padt(new_sel),
                   win_prefix.reshape(B, WINDOW, PAGE_COLS), padt(new_win), gate_rows, past, T)
    y = out.reshape(B, KVH, G, T, HD).transpose(0, 3, 1, 2, 4).reshape(B, T, D_NSA)
    five = lambda t: t.reshape(B, -1, 2, KVH, HD)
    win_out = jnp.concatenate([win_prefix[:, T:], five(new_win)], axis=1)
    return y, five(new_cmp), five(new_sel), win_out


def _out_proj_kernel(yr_ref, yn_ref, x_ref, w_ref, g_ref, b_ref, o_ref):
    half = yr_ref.shape[1]
    h = (jnp.dot(yr_ref[...].astype(jnp.bfloat16), w_ref[0:half, :], preferred_element_type=jnp.float32)
         + jnp.dot(yn_ref[...].astype(jnp.bfloat16), w_ref[half:, :], preferred_element_type=jnp.float32))
    z = DEEPNORM_ALPHA * x_ref[...] + h
    mu = jnp.mean(z, axis=-1, keepdims=True)
    dev = z - mu
    var = jnp.mean(dev * dev, axis=-1, keepdims=True)
    o_ref[...] = dev * lax.rsqrt(var + LN_EPS) * g_ref[...] + b_ref[...]


def _out_proj_norm(y_r, y_n, x, w_bf, g, b, *, tm=256):
    n, d = x.shape
    tm = min(tm, n)
    half = y_r.shape[1]
    rows = lambda w: pl.BlockSpec((tm, w), lambda i: (i, 0))
    full = lambda shape: pl.BlockSpec(shape, lambda i: (0, 0))
    return pl.pallas_call(
        _out_proj_kernel,
        grid=(n // tm,),
        in_specs=[rows(half), rows(y_n.shape[1]), rows(d), full(w_bf.shape), full((1, d)), full((1, d))],
        out_specs=rows(d),
        out_shape=jax.ShapeDtypeStruct((n, d), jnp.float32),
        compiler_params=pltpu.CompilerParams(dimension_semantics=("arbitrary",),
                                             vmem_limit_bytes=_VMEM_LIMIT),
        name="out_proj_norm",
    )(y_r, y_n, x, w_bf, g.reshape(1, d), b.reshape(1, d))


def _hybrid_layer(x, past, shift_prev, rwkv_s0, n_keep, p):
    B, T, D = x.shape
    proj = _matmul(x.reshape(B * T, D), p['w_in']).reshape(B, T, IN_COLS)
    y_r, s_T, new_shift = _rwkv_time_mix(proj, shift_prev, rwkv_s0, p)
    if past is None:
        assert T % NSA_TQ == 0 and T >= n_keep
        y_n, new_cmp, new_sel, new_win = _nsa_prompt_mix(proj[..., RWKV_COLS:], n_keep, p['nsa_cmp_w'])
    else:
        assert n_keep == WINDOW
        y_n, new_cmp, new_sel, new_win = _nsa_decode_mix(proj[..., RWKV_COLS:], *past, p['nsa_cmp_w'])
    x = _out_proj_norm(y_r.reshape(B * T, D_RWKV), y_n.reshape(B * T, D_NSA), x.reshape(B * T, D),
                       p['w_out_bf'], p['ln1_g'], p['ln1_b']).reshape(B, T, D)
    f = _peer_ffn(x, p['peer_wq_bf'], p['peer_sk_bf'], p['peer_u_bf'], p['peer_v_bf'])
    x = _layer_norm(DEEPNORM_ALPHA * x + f, p['ln2_g'], p['ln2_b'])
    return x, (new_cmp, new_sel, new_win, s_T, new_shift)


def kernel(x_prompt, x_sample, cache_cmp_kv, cache_sel_kv, page_table, state_win_kv, state_rwkv,
           state_shift, w_in, rwkv_mu, rwkv_w0, rwkv_w2, rwkv_a0, rwkv_a2, rwkv_g2, rwkv_k_k,
           rwkv_k_a, rwkv_r_k, rwkv_gn_g, rwkv_gn_b, nsa_cmp_w, w_out, ln1_g, ln1_b, peer_wq,
           peer_subkeys, peer_u, peer_v, ln2_g, ln2_b):
    bp = x_prompt.shape[0]
    dt = x_prompt.dtype
    assert page_table.shape[1] * PAGE_SIZE == PAST_LEN
    n_keep = state_win_kv.shape[2]
    zero_shift = jnp.zeros((bp, 1, RWKV_COLS), dt)
    zero_state = jnp.zeros((bp, H_RWKV, HEAD_DIM, HEAD_DIM), dt)
    yp, ys = x_prompt, x_sample
    st_p, st_s = [], []
    for l in range(DEPTH):
        p = {'w_in': w_in[l], 'rwkv_mu': rwkv_mu[l], 'rwkv_w0': rwkv_w0[l], 'rwkv_w2': rwkv_w2[l],
             'rwkv_a0': rwkv_a0[l], 'rwkv_a2': rwkv_a2[l], 'rwkv_g2': rwkv_g2[l],
             'rwkv_k_k': rwkv_k_k[l], 'rwkv_k_a': rwkv_k_a[l], 'rwkv_r_k': rwkv_r_k[l],
             'rwkv_gn_g': rwkv_gn_g[l], 'rwkv_gn_b': rwkv_gn_b[l], 'nsa_cmp_w': nsa_cmp_w[l],
             'w_out_bf': w_out[l].astype(jnp.bfloat16), 'ln1_g': ln1_g[l], 'ln1_b': ln1_b[l],
             'peer_wq_bf': peer_wq[l].astype(jnp.bfloat16),
             'peer_sk_bf': peer_subkeys[l].reshape(2 * P_HEADS, N_KEYS, P_DKEY // 2).astype(jnp.bfloat16),
             'peer_u_bf': peer_u[l].astype(jnp.bfloat16).T, 'peer_v_bf': peer_v[l].astype(jnp.bfloat16),
             'ln2_g': ln2_g[l], 'ln2_b': ln2_b[l]}
        yp, sp = _hybrid_layer(yp, None, zero_shift, zero_state, n_keep, p)
        past = (cache_cmp_kv, cache_sel_kv, page_table, l, state_win_kv[l])
        ys, ss = _hybrid_layer(ys, past, state_shift[l], state_rwkv[l], n_keep, p)
        st_p.append(sp)
        st_s.append(ss)
    stk = lambda sts, i: jnp.stack([s[i] for s in sts], axis=0)
    return (yp, ys, stk(st_p, 0), stk(st_p, 1), stk(st_p, 2), stk(st_p, 3), stk(st_p, 4),
            stk(st_s, 0), stk(st_s, 1), stk(st_s, 2), stk(st_s, 3), stk(st_s, 4))
```

```python
import functools
import math

import jax
import jax.numpy as jnp
from jax import lax
from jax.experimental import pallas as pl
from jax.experimental.pallas import tpu as pltpu

D_MODEL = 2048
DEPTH = 2
PAST_LEN = 16384
PAGE_SIZE = 128
HEAD_DIM = 64
D_RWKV = D_MODEL // 2
D_NSA = D_MODEL - D_RWKV
H_RWKV = D_RWKV // HEAD_DIM
H_NSA = D_NSA // HEAD_DIM
NSA_KV_HEADS = 4
NSA_GROUP = H_NSA // NSA_KV_HEADS
NSA_KV_COLS = NSA_KV_HEADS * HEAD_DIM
CMP_BLOCK = 32
SEL_BLOCK = 64
SEL_TOPN = 16
WINDOW = 512
Q_BLOCK = 128
W_LORA = 64
A_LORA = 64
G_LORA = 160
RWKV_COLS = 3 * D_RWKV + W_LORA + A_LORA + G_LORA
NSA_COLS = D_NSA + 6 * NSA_KV_COLS + 3 * H_NSA
IN_COLS = RWKV_COLS + NSA_COLS
P_HEADS = 8
N_KEYS = 128
P_DKEY = 256
P_TOPK = 16
P_TOK_BLOCK = 128
LN_EPS = 1e-5
GN_EPS = 64e-5
DEEPNORM_ALPHA = (2 * DEPTH) ** 0.25
FORCE_SCORE = 1e4
NEG_INF = -1e30

N_EXPERTS = N_KEYS * N_KEYS

LANE = 128
_VMEM_LIMIT = 56 * 1024 * 1024


def _round_up(x, m):
    return -(-x // m) * m


def _mm_kernel(x_ref, w_ref, o_ref):
    o_ref[...] = jnp.dot(x_ref[...].astype(jnp.bfloat16), w_ref[...].astype(jnp.bfloat16),
                         preferred_element_type=jnp.float32)


def _matmul(x, w, *, tm=512, tn=512):
    m, k = x.shape
    n = w.shape[1]
    tm = min(tm, m)
    n_pad = _round_up(n, tn)
    if n_pad != n:
        w = jnp.pad(w, ((0, 0), (0, n_pad - n)))
    out = pl.pallas_call(
        _mm_kernel,
        grid=(m // tm, n_pad // tn),
        in_specs=[pl.BlockSpec((tm, k), lambda i, j: (i, 0)),
                  pl.BlockSpec((k, tn), lambda i, j: (0, j))],
        out_specs=pl.BlockSpec((tm, tn), lambda i, j: (i, j)),
        out_shape=jax.ShapeDtypeStruct((m, n_pad), jnp.float32),
        compiler_params=pltpu.CompilerParams(dimension_semantics=("arbitrary", "arbitrary"),
                                             vmem_limit_bytes=48 * 1024 * 1024),
    )(x, w)
    return out[:, :n] if n_pad != n else out


def _layer_norm(x, g, b):
    mu = x.mean(-1, keepdims=True)
    var = jnp.mean(jnp.square(x - mu), -1, keepdims=True)
    return (x - mu) * lax.rsqrt(var + LN_EPS) * g + b


RW_CHUNK = 64
RW_PAIRS = D_RWKV // LANE


def _bdot(a, b):
    return jnp.dot(a.astype(jnp.bfloat16), b.astype(jnp.bfloat16), preferred_element_type=jnp.float32)


def _bdot_nt(a, b):
    return lax.dot_general(a.astype(jnp.bfloat16), b.astype(jnp.bfloat16), (((1,), (1,)), ((), ())),
                           preferred_element_type=jnp.float32)


def _bdot_tn(a, b):
    return lax.dot_general(a.astype(jnp.bfloat16), b.astype(jnp.bfloat16), (((0,), (0,)), ((), ())),
                           preferred_element_type=jnp.float32)


def _head_sum(x, same_head_bf):
    hi = x.astype(jnp.bfloat16)
    lo = (x - hi.astype(jnp.float32)).astype(jnp.bfloat16)
    return (jnp.dot(hi, same_head_bf, preferred_element_type=jnp.float32)
            + jnp.dot(lo, same_head_bf, preferred_element_type=jnp.float32))


def _rwkv_mix_kernel(x_ref, shift_ref, mu_ref, pv_ref, w2_ref, a2_ref, g2_ref, s0_ref, y_ref, st_ref,
                     prev_scr, *, steps):
    c = pl.program_id(1)
    C = RW_CHUNK
    f32 = jnp.float32

    @pl.when(c == 0)
    def _():
        st_ref[...] = s0_ref[...]
        prev_scr[...] = shift_ref[...]

    xr = x_ref[:, 0:RWKV_COLS]
    row_c = lax.broadcasted_iota(jnp.int32, (C, 1), 0)
    prev = jnp.where(row_c == 0, prev_scr[...], pltpu.roll(xr, 1, 0))
    prev_scr[...] = xr[C - 1:C, :]
    xs = xr + mu_ref[...] * (prev - xr)
    o_wl = 3 * D_RWKV
    r_all, k_all, v_all = xs[:, 0:D_RWKV], xs[:, D_RWKV:2 * D_RWKV], xs[:, 2 * D_RWKV:o_wl]
    wl = xs[:, o_wl:o_wl + W_LORA]
    al = xs[:, o_wl + W_LORA:o_wl + W_LORA + A_LORA]
    gl = xs[:, o_wl + W_LORA + A_LORA:RWKV_COLS]
    w0, a0_, k_k, k_a = pv_ref[0:1, :], pv_ref[1:2, :], pv_ref[2:3, :], pv_ref[3:4, :]
    r_k, gn_g, gn_b = pv_ref[4:5, :], pv_ref[5:6, :], pv_ref[6:7, :]
    w_all = -jax.nn.softplus(-(w0 + _bdot(jnp.tanh(wl), w2_ref[...]))) - 0.5
    lw_all = -jnp.exp(w_all)
    a_all = jax.nn.sigmoid(a0_ + _bdot(al, a2_ref[...]))
    gate_all = _bdot(jax.nn.sigmoid(gl), g2_ref[...])
    kkf_all = k_all * k_k
    k2_all = k_all * (1.0 + (a_all - 1.0) * k_a)
    if steps % C:
        live = (c * C + row_c) < steps
        lw_all = jnp.where(live, lw_all, 0.0)
        kkf_all = jnp.where(live, kkf_all, 0.0)
        k2_all = jnp.where(live, k2_all, 0.0)
        v_all = jnp.where(live, v_all, 0.0)

    row = lax.broadcasted_iota(jnp.int32, (C, C), 0)
    col = lax.broadcasted_iota(jnp.int32, (C, C), 1)
    tri_incl = col <= row
    tri_strict = col < row
    tri_f = jnp.where(tri_incl, 1.0, 0.0).astype(f32)
    eye = jnp.where(row == col, 1.0, 0.0).astype(f32)
    lane = lax.broadcasted_iota(jnp.int32, (1, LANE), 1)
    head0 = lane < HEAD_DIM
    r128 = lax.broadcasted_iota(jnp.int32, (LANE, LANE), 0)
    c128 = lax.broadcasted_iota(jnp.int32, (LANE, LANE), 1)
    same_head = (r128 < HEAD_DIM) == (c128 < HEAD_DIM)
    same_head_bf = jnp.where(same_head, 1.0, 0.0).astype(jnp.bfloat16)
    eye128 = r128 == c128

    pairs = range(RW_PAIRS)
    heads = (head0, ~head0)
    sls = [slice(pr * LANE, (pr + 1) * LANE) for pr in pairs]
    lw = [lw_all[:, sl] for sl in sls]
    cum = [jnp.dot(tri_f, lw[pr], precision=lax.Precision.HIGHEST, preferred_element_type=f32)
           for pr in pairs]
    tot = [cum[pr][C - 1:C, :] for pr in pairs]
    v = [v_all[:, sl] for sl in sls]
    k = [k2_all[:, sl] for sl in sls]
    r = [r_all[:, sl] for sl in sls]
    kkf = [kkf_all[:, sl] for sl in sls]
    ss = [_head_sum(kkf[pr] * kkf[pr], same_head_bf) for pr in pairs]
    kk = [kkf[pr] * lax.rsqrt(jnp.maximum(ss[pr], 1e-24)) for pr in pairs]
    b = [kk[pr] * a_all[:, sls[pr]] for pr in pairs]
    x, ym = [], []
    for pr in pairs:
        e_neg = jnp.exp(-cum[pr])
        x.append(jnp.concatenate([kk[pr] * jnp.exp(cum[pr] - lw[pr]),
                                  r[pr] * jnp.exp(cum[pr])], axis=0))
        ym.append(jnp.concatenate([k[pr] * e_neg, b[pr] * e_neg], axis=0))
    a0 = [st_ref[pr] for pr in pairs]
    xa = [_bdot(x[pr], a0[pr]) for pr in pairs]
    g = [[_bdot_nt(jnp.where(hm, x[pr], 0.0), ym[pr]) for hm in heads] for pr in pairs]
    lkv = [[_bdot(jnp.where(tri_strict, g[pr][h][:C, :C], 0.0), v[pr]) for h in range(2)] for pr in pairs]
    npow = [[jnp.where(tri_strict, -g[pr][h][:C, C:], 0.0) for h in range(2)] for pr in pairs]
    tmat = [[eye + npow[pr][h] for h in range(2)] for pr in pairs]
    for _ in range(int(math.log2(C)) - 1):
        npow = [[_bdot(npow[pr][h], npow[pr][h]) for h in range(2)] for pr in pairs]
        tmat = [[tmat[pr][h] + _bdot(tmat[pr][h], npow[pr][h]) for h in range(2)] for pr in pairs]
    rhs = [xa[pr][:C] + jnp.where(head0, lkv[pr][0], lkv[pr][1]) for pr in pairs]
    w = [jnp.where(head0, _bdot(tmat[pr][0], rhs[pr]), _bdot(tmat[pr][1], rhs[pr])) for pr in pairs]
    vw = [jnp.concatenate([v[pr], w[pr]], axis=0) for pr in pairs]
    y = []
    for pr in pairs:
        mr = [jnp.concatenate([jnp.where(tri_incl, g[pr][h][C:, :C], 0.0),
                               jnp.where(tri_incl, -g[pr][h][C:, C:], 0.0)], axis=1) for h in range(2)]
        y.append(xa[pr][C:] + jnp.where(head0, _bdot(mr[0], vw[pr]), _bdot(mr[1], vw[pr])))
    inv_hd = 1.0 / HEAD_DIM
    mean = [_head_sum(y[pr], same_head_bf) * inv_hd for pr in pairs]
    dev = [y[pr] - mean[pr] for pr in pairs]
    var = [_head_sum(dev[pr] * dev[pr], same_head_bf) * inv_hd for pr in pairs]
    rk = [_head_sum(r[pr] * k[pr] * r_k[:, sls[pr]], same_head_bf) for pr in pairs]
    for pr in pairs:
        yn = dev[pr] * lax.rsqrt(var[pr] + GN_EPS) * gn_g[:, sls[pr]] + gn_b[:, sls[pr]]
        y_ref[:, sls[pr]] = (yn + rk[pr] * v[pr]) * gate_all[:, sls[pr]]
    for pr in pairs:
        e_rem = jnp.exp(tot[pr] - cum[pr])
        kb = jnp.concatenate([k[pr] * e_rem, -(b[pr] * e_rem)], axis=0)
        upd = _bdot_tn(kb, vw[pr])
        p_col = jnp.sum(jnp.where(eye128, jnp.exp(tot[pr]), 0.0), axis=1, keepdims=True)
        st_ref[pr] = a0[pr] * p_col + jnp.where(same_head, upd, 0.0)


def _rwkv_time_mix(proj, shift_prev, s0, p):
    B, T, W = proj.shape
    C = RW_CHUNK
    t_pad = _round_up(T, C)
    x = proj if t_pad == T else jnp.pad(proj, ((0, 0), (0, t_pad - T), (0, 0)))
    a = jnp.swapaxes(s0, -1, -2).reshape(B, RW_PAIRS, 2, HEAD_DIM, HEAD_DIM)
    z = jnp.zeros_like(a[:, :, 0])
    a0 = jnp.concatenate([jnp.concatenate([a[:, :, 0], z], axis=-1),
                          jnp.concatenate([z, a[:, :, 1]], axis=-1)], axis=-2)
    zero = jnp.zeros((D_RWKV,), jnp.float32)
    pv = jnp.stack([p['rwkv_w0'], p['rwkv_a0'], p['rwkv_k_k'], p['rwkv_k_a'], p['rwkv_r_k'].reshape(-1),
                    p['rwkv_gn_g'], p['rwkv_gn_b'], zero])
    bf = lambda t: t.astype(jnp.bfloat16)
    full = lambda shape: pl.BlockSpec(shape, lambda bi, ci: (0,) * len(shape))
    st_spec = pl.BlockSpec((None, RW_PAIRS, LANE, LANE), lambda bi, ci: (bi, 0, 0, 0))
    y, st = pl.pallas_call(
        functools.partial(_rwkv_mix_kernel, steps=T),
        grid=(B, t_pad // C),
        in_specs=[pl.BlockSpec((None, C, W), lambda bi, ci: (bi, ci, 0)),
                  pl.BlockSpec((None, 1, RWKV_COLS), lambda bi, ci: (bi, 0, 0)),
                  full((1, RWKV_COLS)), full((8, D_RWKV)), full((W_LORA, D_RWKV)),
                  full((A_LORA, D_RWKV)), full((G_LORA, D_RWKV)), st_spec],
        out_specs=[pl.BlockSpec((None, C, D_RWKV), lambda bi, ci: (bi, ci, 0)), st_spec],
        out_shape=[jax.ShapeDtypeStruct((B, t_pad, D_RWKV), jnp.float32),
                   jax.ShapeDtypeStruct((B, RW_PAIRS, LANE, LANE), jnp.float32)],
        scratch_shapes=[pltpu.VMEM((1, RWKV_COLS), jnp.float32)],
        compiler_params=pltpu.CompilerParams(dimension_semantics=("arbitrary", "arbitrary"),
                                             vmem_limit_bytes=_VMEM_LIMIT),
        name="rwkv_mix",
    )(x, shift_prev, p['rwkv_mu'].reshape(1, RWKV_COLS), pv, bf(p['rwkv_w2']), bf(p['rwkv_a2']),
      bf(p['rwkv_g2']), a0)
    s_t = jnp.stack([st[:, :, :HEAD_DIM, :HEAD_DIM], st[:, :, HEAD_DIM:, HEAD_DIM:]], axis=2)
    s_t = jnp.swapaxes(s_t.reshape(B, H_RWKV, HEAD_DIM, HEAD_DIM), -1, -2)
    return y[:, :T], s_t, proj[:, T - 1:T, :RWKV_COLS]


def _extract_top(buf_ref, out_ref, nrows, width):
    iota = lax.broadcasted_iota(jnp.int32, (nrows, width), 0)

    def body(r, carry):
        s = buf_ref[...]
        mx = jnp.max(s, axis=0, keepdims=True)
        first = jnp.min(jnp.where(s == mx, iota, nrows), axis=0, keepdims=True)
        buf_ref[...] = jnp.where(iota == first, -jnp.inf, s)
        out_ref[pl.ds(r, 1), :] = mx
        return carry

    lax.fori_loop(0, P_TOPK, body, 0)


def _peer_score_kernel(x_ref, wq_ref, sk_ref, s1_ref, s2_ref, st_ref, buf, cand, hv0, hv1, tv):
    tn = x_ref.shape[0]
    q = jnp.dot(x_ref[...].astype(jnp.bfloat16), wq_ref[...],
                preferred_element_type=jnp.float32).astype(jnp.bfloat16)
    for h in range(P_HEADS):
        for c, (s_ref, hv) in enumerate(((s1_ref, hv0), (s2_ref, hv1))):
            col = (2 * h + c) * N_KEYS
            s_t = lax.dot_general(sk_ref[2 * h + c], q[:, col:col + N_KEYS],
                                  (((1,), (1,)), ((), ())), preferred_element_type=jnp.float32)
            s_ref[h] = s_t
            buf[...] = s_t
            _extract_top(buf, hv, N_KEYS, tn)
        pieces = [hv0[a:a + 1, :] + hv1[0:P_TOPK // (a + 1), :] for a in range(P_TOPK)]
        n_cand = sum(P_TOPK // (a + 1) for a in range(P_TOPK))
        pieces.append(jnp.full((cand.shape[0] - n_cand, tn), -jnp.inf, jnp.float32))
        cand[...] = jnp.concatenate(pieces, axis=0)
        _extract_top(cand, tv, cand.shape[0], tn)
        z = jnp.sum(jnp.exp(tv[...] - tv[0:1, :]), axis=0, keepdims=True)
        st_ref[0, h:h + 1, :] = tv[P_TOPK - 1:P_TOPK, :]
        st_ref[1, h:h + 1, :] = hv0[0:1, :]
        st_ref[2, h:h + 1, :] = hv1[0:1, :]
        st_ref[3, h:h + 1, :] = 1.0 / z


def _peer_scores(x, wq_bf, sk_bf, *, tn=256):
    n = x.shape[0]
    tn = min(tn, n)
    return pl.pallas_call(
        _peer_score_kernel,
        grid=(n // tn,),
        in_specs=[pl.BlockSpec((tn, D_MODEL), lambda i: (i, 0)),
                  pl.BlockSpec((D_MODEL, P_HEADS * P_DKEY), lambda i: (0, 0)),
                  pl.BlockSpec((2 * P_HEADS, N_KEYS, P_DKEY // 2), lambda i: (0, 0, 0))],
        out_specs=[pl.BlockSpec((P_HEADS, N_KEYS, tn), lambda i: (0, 0, i)),
                   pl.BlockSpec((P_HEADS, N_KEYS, tn), lambda i: (0, 0, i)),
                   pl.BlockSpec((4, P_HEADS, tn), lambda i: (0, 0, i))],
        out_shape=[jax.ShapeDtypeStruct((P_HEADS, N_KEYS, n), jnp.float32),
                   jax.ShapeDtypeStruct((P_HEADS, N_KEYS, n), jnp.float32),
                   jax.ShapeDtypeStruct((4, P_HEADS, n), jnp.float32)],
        scratch_shapes=[pltpu.VMEM((N_KEYS, tn), jnp.float32),
                        pltpu.VMEM((_round_up(sum(P_TOPK // (a + 1) for a in range(P_TOPK)), 8), tn),
                                   jnp.float32),
                        pltpu.VMEM((P_TOPK, tn), jnp.float32),
                        pltpu.VMEM((P_TOPK, tn), jnp.float32),
                        pltpu.VMEM((P_TOPK, tn), jnp.float32)],
        compiler_params=pltpu.CompilerParams(dimension_semantics=("arbitrary",),
                                             vmem_limit_bytes=_VMEM_LIMIT),
        name="peer_scores",
    )(x, wq_bf, sk_bf)


def _peer_expert_kernel(x_ref, u_ref, v_ref, s1_ref, s2_ref, st_ref, o_ref, e2_scr, h_scr, p_scr,
                        *, chunk):
    j = pl.program_id(1)
    tn = x_ref.shape[0]
    te = u_ref.shape[1]

    @pl.when(j == 0)
    def _():
        o_ref[...] = jnp.zeros_like(o_ref)
        for h in range(P_HEADS):
            e2_scr[h] = jnp.exp(s2_ref[h] - st_ref[2, h:h + 1, :])

    def hidden(c):
        tsl = slice(c * chunk, (c + 1) * chunk)
        h_scr[tsl, :] = jnp.dot(x_ref[tsl, :], u_ref[...], preferred_element_type=jnp.float32)

    a_rows = [[s1_ref[h, pl.ds(j * (te // N_KEYS) + ii, 1), :] for h in range(P_HEADS)]
              for ii in range(te // N_KEYS)]
    e1_rows = [[jnp.exp(a_rows[ii][h] - st_ref[1, h:h + 1, :]) * st_ref[3, h:h + 1, :]
                for h in range(P_HEADS)] for ii in range(te // N_KEYS)]

    def weigh(c):
        for sub in range(chunk // LANE):
            weigh_lanes(slice(c * chunk + sub * LANE, c * chunk + (sub + 1) * LANE))

    def weigh_lanes(tsl):
        for ii in range(te // N_KEYS):
            w_t = jnp.zeros((N_KEYS, LANE), jnp.float32)
            for h in range(P_HEADS):
                val = a_rows[ii][h][:, tsl] + s2_ref[h, :, tsl]
                w_t = w_t + jnp.where(val >= st_ref[0, h:h + 1, tsl],
                                      e1_rows[ii][h][:, tsl] * e2_scr[h, :, tsl], 0.0)
            hh = h_scr[tsl, ii * N_KEYS:(ii + 1) * N_KEYS]
            g = 0.5 * hh * (1.0 + lax.erf(hh * 0.7071067811865476))
            p_scr[tsl, ii * N_KEYS:(ii + 1) * N_KEYS] = (w_t.T * g).astype(jnp.bfloat16)

    def project(c):
        tsl = slice(c * chunk, (c + 1) * chunk)
        o_ref[tsl, :] += jnp.dot(p_scr[tsl, :], v_ref[...], preferred_element_type=jnp.float32)

    n_chunks = tn // chunk
    hidden(0)
    for c in range(n_chunks):
        if c + 1 < n_chunks:
            hidden(c + 1)
        weigh(c)
        project(c)


def _peer_experts(x_bf, u_bf, v_bf, s1t, s2t, stats, *, tn=1024, te=512, chunk=256):
    n = x_bf.shape[0]
    tn = min(tn, n)
    chunk = min(chunk, tn)
    once = pl.Buffered(1)
    return pl.pallas_call(
        functools.partial(_peer_expert_kernel, chunk=chunk),
        grid=(n // tn, N_EXPERTS // te),
        in_specs=[pl.BlockSpec((tn, D_MODEL), lambda i, j: (i, 0), pipeline_mode=once),
                  pl.BlockSpec((D_MODEL, te), lambda i, j: (0, j)),
                  pl.BlockSpec((te, D_MODEL), lambda i, j: (j, 0)),
                  pl.BlockSpec((P_HEADS, N_KEYS, tn), lambda i, j: (0, 0, i), pipeline_mode=once),
                  pl.BlockSpec((P_HEADS, N_KEYS, tn), lambda i, j: (0, 0, i), pipeline_mode=once),
                  pl.BlockSpec((4, P_HEADS, tn), lambda i, j: (0, 0, i), pipeline_mode=once)],
        out_specs=pl.BlockSpec((tn, D_MODEL), lambda i, j: (i, 0)),
        out_shape=jax.ShapeDtypeStruct((n, D_MODEL), jnp.float32),
        scratch_shapes=[pltpu.VMEM((P_HEADS, N_KEYS, tn), jnp.float32),
                        pltpu.VMEM((tn, te), jnp.float32),
                        pltpu.VMEM((tn, te), jnp.bfloat16)],
        compiler_params=pltpu.CompilerParams(dimension_semantics=("arbitrary", "arbitrary"),
                                             vmem_limit_bytes=_VMEM_LIMIT),
        name="peer_experts",
    )(x_bf, u_bf, v_bf, s1t, s2t, stats)


def _peer_ffn(x, wq_bf, sk_bf, u_bf, v_bf):
    B, T, D = x.shape
    n = B * T
    xt = x.reshape(n, D)
    n_pad = _round_up(n, LANE)
    if n_pad != n:
        xt = jnp.pad(xt, ((0, n_pad - n), (0, 0)))
    s1t, s2t, stats = _peer_scores(xt, wq_bf, sk_bf)
    out = _peer_experts(xt.astype(jnp.bfloat16), u_bf, v_bf, s1t, s2t, stats)
    return out[:n].reshape(B, T, D)


NSA_TQ = 256
NSA_GSPLIT = 4


def _flash_tile(q4, k_t, v_t, mask4, m, l, acc):
    s = lax.dot_general(q4, k_t, (((1,), (1,)), ((), ())), preferred_element_type=jnp.float32)
    s = jnp.where(mask4, s, NEG_INF)
    m_new = jnp.maximum(m, jnp.max(s, axis=-1, keepdims=True))
    p = jnp.where(mask4, jnp.exp(s - m_new), 0.0)
    alpha = jnp.exp(m - m_new)
    l_new = alpha * l + jnp.sum(p, axis=-1, keepdims=True)
    acc_new = alpha * acc + jnp.dot(p.astype(jnp.bfloat16), v_t, preferred_element_type=jnp.float32)
    return m_new, l_new, acc_new


def _nsa_prompt_kernel(q_ref, gl_ref, kc_ref, vc_ref, ks_ref, vs_ref, kw_ref, vw_ref, cw_ref, ex_ref,
                       o_ref, kcb_scr, vcb_scr, imp_scr, selx_scr, *, seq):
    qt = pl.program_id(2)
    tq = NSA_TQ
    n_cmp = seq // CMP_BLOCK
    n_sel = seq // SEL_BLOCK
    G = NSA_GROUP
    q0 = qt * tq

    @pl.when(qt == 0)
    def _():
        kc = kc_ref[...].reshape(n_cmp, CMP_BLOCK, HEAD_DIM)
        vc = vc_ref[...].reshape(n_cmp, CMP_BLOCK, HEAD_DIM)
        kcb_scr[...] = jnp.zeros_like(kcb_scr)
        vcb_scr[...] = jnp.zeros_like(vcb_scr)
        kcb_scr[0:n_cmp, :] = jnp.sum(kc * cw_ref[0][None], axis=1)
        vcb_scr[0:n_cmp, :] = jnp.sum(vc * cw_ref[1][None], axis=1).astype(jnp.bfloat16)

    qpos_col = q0 + lax.broadcasted_iota(jnp.int32, (tq, 1), 0)
    qf = [q_ref[g] * (HEAD_DIM ** -0.5) for g in range(G)]
    qs = [t.astype(jnp.bfloat16) for t in qf]
    gates = jax.nn.sigmoid(gl_ref[...])

    cmp_end = (lax.broadcasted_iota(jnp.int32, (1, LANE), 1) + 1) * CMP_BLOCK - 1
    cmask = (cmp_end <= qpos_col) & (lax.broadcasted_iota(jnp.int32, (1, LANE), 1) < n_cmp)
    o_cmp = []
    imp = jnp.zeros((tq, LANE), jnp.float32)
    for g in range(G):
        s = lax.dot_general(qf[g], kcb_scr[...], (((1,), (1,)), ((), ())),
                            precision=lax.Precision.HIGHEST, preferred_element_type=jnp.float32)
        s = jnp.where(cmask, s, NEG_INF)
        e = jnp.exp(s - jnp.max(s, axis=-1, keepdims=True))
        p = jnp.where(cmask, e / jnp.sum(e, axis=-1, keepdims=True), 0.0)
        o_cmp.append(jnp.dot(p.astype(jnp.bfloat16), vcb_scr[...], preferred_element_type=jnp.float32))
        imp = imp + p
    imp_t = imp.T
    ratio = SEL_BLOCK // CMP_BLOCK
    parts = []
    for c in range(tq // LANE):
        imp_scr[c] = imp_t[:, c * LANE:(c + 1) * LANE]
        part = imp_scr[c, pl.ds(0, n_sel, stride=ratio), :]
        for r in range(1, ratio):
            part = part + imp_scr[c, pl.ds(r, n_sel, stride=ratio), :]
        parts.append(part)
    imp_s = jnp.concatenate(parts, axis=1)
    qpos_row = q0 + lax.broadcasted_iota(jnp.int32, (1, tq), 1)
    cur = qpos_row // SEL_BLOCK
    blk = lax.broadcasted_iota(jnp.int32, (n_sel, tq), 0)
    forced = (blk == 0) | (blk == cur) | (blk == cur - 1)
    imp_s = jnp.where(blk > cur, -1.0, jnp.where(forced, FORCE_SCORE, imp_s))
    rank = jnp.zeros((n_sel, tq), jnp.int32)
    for mm in range(n_sel):
        row = imp_s[mm:mm + 1, :]
        ahead = (row > imp_s) | ((row == imp_s) & (mm < blk))
        rank = rank + jnp.where(ahead, 1, 0)
    sel = jnp.where((rank < min(SEL_TOPN, n_sel)) & (imp_s >= 0.0), 1.0, 0.0)
    sel_pad = jnp.concatenate([sel, jnp.zeros((LANE - n_sel, tq), jnp.float32)], axis=0)
    sel_q = sel_pad.T.astype(jnp.bfloat16)
    selx_scr[...] = jnp.dot(sel_q, ex_ref[...], preferred_element_type=jnp.float32)

    kpos_l = lax.broadcasted_iota(jnp.int32, (1, tq), 1)
    GS = NSA_GSPLIT
    qpos_r = jnp.concatenate([qpos_col] * GS, axis=0)
    init = (jnp.full((GS * tq, 1), NEG_INF, jnp.float32), jnp.zeros((GS * tq, 1), jnp.float32),
            jnp.zeros((GS * tq, HEAD_DIM), jnp.float32))

    for g0 in range(0, G, GS):
        q_r = jnp.concatenate(qs[g0:g0 + GS], axis=0)

        def sel_body(kt, carry):
            k0 = pl.multiple_of(kt * tq, tq)
            chosen = selx_scr[:, pl.ds(k0, tq)] > 0.5
            mask = jnp.concatenate([chosen] * GS, axis=0) & ((k0 + kpos_l) <= qpos_r)
            return _flash_tile(q_r, ks_ref[pl.ds(k0, tq), :], vs_ref[pl.ds(k0, tq), :], mask, *carry)

        _, l_s, acc_s = lax.fori_loop(0, qt + 1, sel_body, init)
        o_sel = acc_s / l_s

        def win_body(kt, carry):
            k0 = pl.multiple_of(kt * tq, tq)
            kpos = k0 + kpos_l
            mask = (kpos <= qpos_r) & (kpos >= qpos_r - (WINDOW - 1))
            return _flash_tile(q_r, kw_ref[pl.ds(k0, tq), :], vw_ref[pl.ds(k0, tq), :], mask, *carry)

        _, l_w, acc_w = lax.fori_loop(jnp.maximum(qt - WINDOW // tq, 0), qt + 1, win_body, init)
        o_win = acc_w / l_w

        for gi in range(GS):
            g = g0 + gi
            rows = slice(gi * tq, (gi + 1) * tq)
            o_ref[g] = (gates[:, 3 * g:3 * g + 1] * o_cmp[g] + gates[:, 3 * g + 1:3 * g + 2] * o_sel[rows]
                        + gates[:, 3 * g + 2:3 * g + 3] * o_win[rows])


def _nsa_prompt(xn, cmp_w):
    B, T, _ = xn.shape
    KVH, G, HD = NSA_KV_HEADS, NSA_GROUP, HEAD_DIM
    kvc = NSA_KV_COLS
    tq = NSA_TQ
    q = xn[..., :D_NSA].reshape(B, T, KVH, G, HD).transpose(0, 2, 3, 1, 4)
    kv = [xn[..., D_NSA + i * kvc:D_NSA + (i + 1) * kvc].reshape(B, T, KVH, HD).transpose(0, 2, 1, 3)
          for i in range(6)]
    kc, vc = kv[0], kv[1]
    ks, vs, kw, vw = [t.astype(jnp.bfloat16) for t in kv[2:]]
    gl = xn[..., D_NSA + 6 * kvc:].reshape(B, T, KVH, 3 * G).transpose(0, 2, 1, 3)
    cw = jnp.broadcast_to(cmp_w[:, :, None], (2, CMP_BLOCK, HD))
    expand = (jnp.arange(LANE)[:, None] == (jnp.arange(T) // SEL_BLOCK)[None, :]).astype(jnp.bfloat16)
    kv_spec = pl.BlockSpec((None, None, T, HD), lambda b, k, t: (b, k, 0, 0))
    out = pl.pallas_call(
        functools.partial(_nsa_prompt_kernel, seq=T),
        grid=(B, KVH, T // tq),
        in_specs=[pl.BlockSpec((None, None, G, tq, HD), lambda b, k, t: (b, k, 0, t, 0)),
                  pl.BlockSpec((None, None, tq, 3 * G), lambda b, k, t: (b, k, t, 0)),
                  kv_spec, kv_spec, kv_spec, kv_spec, kv_spec, kv_spec,
                  pl.BlockSpec((2, CMP_BLOCK, HD), lambda b, k, t: (0, 0, 0)),
                  pl.BlockSpec((LANE, T), lambda b, k, t: (0, 0))],
        out_specs=pl.BlockSpec((None, None, G, tq, HD), lambda b, k, t: (b, k, 0, t, 0)),
        out_shape=jax.ShapeDtypeStruct((B, KVH, G, T, HD), jnp.float32),
        scratch_shapes=[pltpu.VMEM((LANE, HD), jnp.float32),
                        pltpu.VMEM((LANE, HD), jnp.bfloat16),
                        pltpu.VMEM((tq // LANE, LANE, LANE), jnp.float32),
                        pltpu.VMEM((tq, T), jnp.float32)],
        compiler_params=pltpu.CompilerParams(dimension_semantics=("arbitrary", "arbitrary", "arbitrary"),
                                             vmem_limit_bytes=_VMEM_LIMIT),
        name="nsa_prompt",
    )(q, gl, kc, vc, ks, vs, kw, vw, cw, expand)
    return out.transpose(0, 3, 1, 2, 4).reshape(B, T, D_NSA)


def _nsa_prompt_mix(xn, n_keep, cmp_w):
    B, T, _ = xn.shape
    kv5 = lambda i: xn[..., D_NSA + 2 * i * NSA_KV_COLS:D_NSA + 2 * (i + 1) * NSA_KV_COLS].reshape(
        B, T, 2, NSA_KV_HEADS, HEAD_DIM)
    return _nsa_prompt(xn, cmp_w), kv5(0), kv5(1), kv5(2)[:, -n_keep:]


NSA_PG = 4
NSA_POOL_PG = LANE * CMP_BLOCK // PAGE_SIZE
PAGE_COLS = 2 * NSA_KV_COLS
NSA_ROWS = NSA_KV_HEADS * NSA_GROUP


def _page_specs(layer, n):
    def spec(j):
        return pl.BlockSpec((None, None, 2, NSA_KV_HEADS, HEAD_DIM, PAGE_SIZE),
                            lambda b, s, pt: (layer, pt[b, s * n + j], 0, 0, 0, 0))
    return [spec(j) for j in range(n)]


def _rows_last(t):
    nd = t.ndim
    return jnp.transpose(t, tuple(range(nd - 4)) + (nd - 3, nd - 2, nd - 1, nd - 4))


def _cmp_pool_kernel(pt_ref, *refs):
    pages, cw_ref, o_ref = refs[:NSA_POOL_PG], refs[NSA_POOL_PG], refs[NSA_POOL_PG + 1]
    per_page = PAGE_SIZE // CMP_BLOCK
    row_blk = lax.broadcasted_iota(jnp.int32, (PAGE_SIZE, LANE), 0) // CMP_BLOCK
    col = lax.broadcasted_iota(jnp.int32, (PAGE_SIZE, LANE), 1)
    acc = [jnp.zeros((NSA_KV_COLS, LANE), jnp.float32) for _ in range(2)]
    for j, pg in enumerate(pages):
        seg = jnp.where(col == j * per_page + row_blk, 1.0, 0.0).astype(jnp.bfloat16)
        for kv in range(2):
            xw = pg[kv].reshape(NSA_KV_COLS, PAGE_SIZE) * cw_ref[kv:kv + 1, :]
            hi = xw.astype(jnp.bfloat16)
            lo = (xw - hi.astype(jnp.float32)).astype(jnp.bfloat16)
            acc[kv] = (acc[kv] + jnp.dot(hi, seg, preferred_element_type=jnp.float32)
                       + jnp.dot(lo, seg, preferred_element_type=jnp.float32))
    o_ref[0:NSA_KV_COLS, :] = acc[0]
    o_ref[NSA_KV_COLS:, :] = acc[1]


def _cmp_pool(cache, page_table, layer, cw):
    B, n_pages = page_table.shape
    return pl.pallas_call(
        _cmp_pool_kernel,
        grid_spec=pltpu.PrefetchScalarGridSpec(
            num_scalar_prefetch=1, grid=(B, n_pages // NSA_POOL_PG),
            in_specs=_page_specs(layer, NSA_POOL_PG) + [pl.BlockSpec((2, PAGE_SIZE), lambda b, s, pt: (0, 0))],
            out_specs=pl.BlockSpec((None, PAGE_COLS, LANE), lambda b, s, pt: (b, 0, s))),
        out_shape=jax.ShapeDtypeStruct((B, PAGE_COLS, n_pages * PAGE_SIZE // CMP_BLOCK), jnp.float32),
        compiler_params=pltpu.CompilerParams(dimension_semantics=("arbitrary", "arbitrary"),
                                             vmem_limit_bytes=_VMEM_LIMIT),
        name="nsa_cmp_pool",
    )(page_table, *([cache] * NSA_POOL_PG), cw)


def _cmp_select_kernel(q_ref, kv_ref, ocmp_ref, sel_ref, pt_scr, imp_scr, *, past, steps):
    T = steps
    R = NSA_ROWS * T
    n_cmp = past // CMP_BLOCK
    n_sel = past // SEL_BLOCK + 1
    n_lane = sel_ref.shape[-1]
    ratio = SEL_BLOCK // CMP_BLOCK
    rows_t = lax.broadcasted_iota(jnp.int32, (R, 1), 0) % T
    qpos = past + rows_t
    cmp_end = (lax.broadcasted_iota(jnp.int32, (1, n_cmp), 1) + 1) * CMP_BLOCK - 1
    cmask = cmp_end <= qpos
    q = q_ref[...]
    p_all, o_all = [], []
    for kh in range(NSA_KV_HEADS):
        rows = slice(kh * NSA_GROUP * T, (kh + 1) * NSA_GROUP * T)
        kcb_t = kv_ref[kh * HEAD_DIM:(kh + 1) * HEAD_DIM, :]
        vcb_t = kv_ref[NSA_KV_COLS + kh * HEAD_DIM:NSA_KV_COLS + (kh + 1) * HEAD_DIM, :].astype(jnp.bfloat16)
        s = jnp.dot(q[rows], kcb_t, precision=lax.Precision.HIGHEST, preferred_element_type=jnp.float32)
        s = jnp.where(cmask[rows], s, NEG_INF)
        e = jnp.exp(s - jnp.max(s, axis=-1, keepdims=True))
        p = jnp.where(cmask[rows], e / jnp.sum(e, axis=-1, keepdims=True), 0.0)
        o_all.append(lax.dot_general(p.astype(jnp.bfloat16), vcb_t, (((1,), (1,)), ((), ())),
                                     preferred_element_type=jnp.float32))
        imp = p[0:T]
        for g in range(1, NSA_GROUP):
            imp = imp + p[g * T:(g + 1) * T]
        p_all.extend([imp] * NSA_GROUP)
    ocmp_ref[...] = jnp.concatenate(o_all, axis=0)
    imp_rows = jnp.concatenate(p_all + [jnp.zeros((LANE - R, n_cmp), jnp.float32)], axis=0)
    for c in range(n_cmp // LANE):
        pt_scr[c] = imp_rows[:, c * LANE:(c + 1) * LANE].T
    per = LANE // ratio
    for c in range(n_cmp // LANE):
        part = pt_scr[c, pl.ds(0, per, stride=ratio), :]
        for r in range(1, ratio):
            part = part + pt_scr[c, pl.ds(r, per, stride=ratio), :]
        imp_scr[c * per:(c + 1) * per, :] = part
    n_rows = imp_scr.shape[0]
    imp_scr[n_sel - 1:n_rows, :] = jnp.zeros((n_rows - n_sel + 1, LANE), jnp.float32)
    lane_t = lax.broadcasted_iota(jnp.int32, (1, LANE), 1) % T
    cur = (past + lane_t) // SEL_BLOCK
    blk = lax.broadcasted_iota(jnp.int32, (n_rows, LANE), 0)
    forced = (blk == 0) | (blk == cur) | (blk == cur - 1)
    imp_s = jnp.where(blk > cur, -1.0, jnp.where(forced, FORCE_SCORE, imp_scr[...]))
    imp_s = jnp.where(blk < n_sel, imp_s, -2.0)
    imp_scr[...] = imp_s

    def rank_body(m, rank):
        row = imp_scr[pl.ds(m, 1), :]
        ahead = (row > imp_s) | ((row == imp_s) & (m < blk))
        return rank + jnp.where(ahead, 1, 0)

    rank = lax.fori_loop(0, n_sel, rank_body, jnp.zeros((n_rows, LANE), jnp.int32))
    sel = jnp.where((rank < min(SEL_TOPN, n_sel)) & (imp_s >= 0.0), 1.0, 0.0)
    sel = jnp.concatenate([sel, jnp.zeros((n_lane - n_rows, LANE), jnp.float32)], axis=0)
    for c in range(n_lane // LANE):
        sel_ref[:, c * LANE:(c + 1) * LANE] = sel[c * LANE:(c + 1) * LANE, :].T


def _cmp_select(q_rows, kvcb, past, steps):
    B, R, _ = q_rows.shape
    n_cmp = kvcb.shape[2]
    n_sel = past // SEL_BLOCK + 1
    n_lane = _round_up(n_sel, LANE)
    n_rows = _round_up(n_sel, 8)
    return pl.pallas_call(
        functools.partial(_cmp_select_kernel, past=past, steps=steps),
        grid=(B,),
        in_specs=[pl.BlockSpec((None, R, HEAD_DIM), lambda b: (b, 0, 0)),
                  pl.BlockSpec((None, PAGE_COLS, n_cmp), lambda b: (b, 0, 0))],
        out_specs=[pl.BlockSpec((None, R, HEAD_DIM), lambda b: (b, 0, 0)),
                   pl.BlockSpec((None, LANE, n_lane), lambda b: (b, 0, 0))],
        out_shape=[jax.ShapeDtypeStruct((B, R, HEAD_DIM), jnp.float32),
                   jax.ShapeDtypeStruct((B, LANE, n_lane), jnp.float32)],
        scratch_shapes=[pltpu.VMEM((n_cmp // LANE, LANE, LANE), jnp.float32),
                        pltpu.VMEM((n_rows, LANE), jnp.float32)],
        compiler_params=pltpu.CompilerParams(dimension_semantics=("arbitrary",),
                                             vmem_limit_bytes=_VMEM_LIMIT),
        name="nsa_cmp_select",
    )(q_rows, kvcb)


def _soft_update(s, mask, v, m, l, acc, keys_last=False):
    s = jnp.where(mask, s, NEG_INF)
    m_new = jnp.maximum(m, jnp.max(s, axis=-1, keepdims=True))
    p = jnp.where(mask, jnp.exp(s - m_new), 0.0)
    alpha = jnp.exp(m - m_new)
    pv = lax.dot_general(p.astype(jnp.bfloat16), v, (((1,), (1 if keys_last else 0,)), ((), ())),
                         preferred_element_type=jnp.float32)
    return m_new, alpha * l + jnp.sum(p, axis=-1, keepdims=True), alpha * acc + pv


def _sel_win_kernel(pt_ref, *refs, past, steps):
    pages = refs[:NSA_PG]
    (q_ref, sel_ref, ocmp_ref, nsel_ref, wpre_ref, nwin_ref, gate_ref,
     o_ref, m_scr, l_scr, acc_scr) = refs[NSA_PG:]
    s_id = pl.program_id(1)
    T = steps
    GT = NSA_GROUP * T
    R = NSA_ROWS * T
    n_lane = sel_ref.shape[-1]
    rows_t = lax.broadcasted_iota(jnp.int32, (GT, 1), 0) % T

    @pl.when(s_id == 0)
    def _():
        m_scr[...] = jnp.full_like(m_scr, NEG_INF)
        l_scr[...] = jnp.zeros_like(l_scr)
        acc_scr[...] = jnp.zeros_like(acc_scr)

    q = q_ref[...].astype(jnp.bfloat16)
    sel_bf = sel_ref[0:R, :].astype(jnp.bfloat16)
    n_keys = NSA_PG * PAGE_SIZE
    blk_id = lax.broadcasted_iota(jnp.int32, (n_lane, n_keys), 0)
    key_blk = s_id * (n_keys // SEL_BLOCK) + lax.broadcasted_iota(jnp.int32, (n_lane, n_keys), 1) // SEL_BLOCK

    def head_cols(x, kh, off):
        return x[:, off + kh * HEAD_DIM:off + (kh + 1) * HEAD_DIM]

    def keys_t(kv, kh):
        return jnp.concatenate([pg[kv, kh] for pg in pages], axis=1).astype(jnp.bfloat16)

    expand = jnp.where(blk_id == key_blk, 1.0, 0.0).astype(jnp.bfloat16)
    chosen = jnp.dot(sel_bf, expand, preferred_element_type=jnp.float32) > 0.5
    head_rows = [slice(kh * GT, (kh + 1) * GT) for kh in range(NSA_KV_HEADS)]
    scores = [jnp.dot(q[rows], keys_t(0, kh), preferred_element_type=jnp.float32)
              for kh, rows in enumerate(head_rows)]
    upd = [_soft_update(scores[kh], chosen[rows], keys_t(1, kh), m_scr[rows], l_scr[rows], acc_scr[rows],
                        keys_last=True) for kh, rows in enumerate(head_rows)]
    for kh, rows in enumerate(head_rows):
        m_scr[rows], l_scr[rows], acc_scr[rows] = upd[kh]

    @pl.when(s_id == pl.num_programs(1) - 1)
    def _():
        new_blk = past // SEL_BLOCK
        tpad = nsel_ref.shape[0]
        jn = lax.broadcasted_iota(jnp.int32, (1, tpad), 1)
        nsel = nsel_ref[...].astype(jnp.bfloat16)
        nwin = nwin_ref[...].astype(jnp.bfloat16)
        jp = lax.broadcasted_iota(jnp.int32, (1, WINDOW), 1)
        gates = jax.nn.sigmoid(gate_ref[...])
        for kh in range(NSA_KV_HEADS):
            rows = slice(kh * GT, (kh + 1) * GT)
            pick = sel_ref[kh * GT:(kh + 1) * GT, new_blk:new_blk + 1] > 0.5
            s = lax.dot_general(q[rows], head_cols(nsel, kh, 0), (((1,), (1,)), ((), ())),
                                preferred_element_type=jnp.float32)
            m, l, acc = _soft_update(s, pick & (jn <= rows_t) & (jn < T), head_cols(nsel, kh, NSA_KV_COLS),
                                     m_scr[rows], l_scr[rows], acc_scr[rows])
            o_sel = acc / l
            init = (jnp.full((GT, 1), NEG_INF, jnp.float32), jnp.zeros((GT, 1), jnp.float32),
                    jnp.zeros((GT, HEAD_DIM), jnp.float32))
            s = jnp.dot(q[rows], wpre_ref[0, kh].astype(jnp.bfloat16), preferred_element_type=jnp.float32)
            st = _soft_update(s, jp > rows_t, wpre_ref[1, kh].astype(jnp.bfloat16), *init, keys_last=True)
            s = lax.dot_general(q[rows], head_cols(nwin, kh, 0), (((1,), (1,)), ((), ())),
                                preferred_element_type=jnp.float32)
            _, l_w, acc_w = _soft_update(s, (jn <= rows_t) & (jn < T), head_cols(nwin, kh, NSA_KV_COLS), *st)
            o_win = acc_w / l_w
            gt = gates[rows]
            o_ref[rows, :] = gt[:, 0:1] * ocmp_ref[rows, :] + gt[:, 1:2] * o_sel + gt[:, 2:3] * o_win


def _sel_win(cache, page_table, layer, q_rows, sel_rows, o_cmp, new_sel, win_prefix, new_win, gate_rows,
             past, steps):
    B, n_pages = page_table.shape
    R = q_rows.shape[1]
    n_lane = sel_rows.shape[-1]
    tpad = new_sel.shape[1]
    per_b = lambda shape: pl.BlockSpec((None,) + shape, lambda b, s, pt: (b,) + (0,) * len(shape))
    return pl.pallas_call(
        functools.partial(_sel_win_kernel, past=past, steps=steps),
        grid_spec=pltpu.PrefetchScalarGridSpec(
            num_scalar_prefetch=1, grid=(B, n_pages // NSA_PG),
            in_specs=_page_specs(layer, NSA_PG) + [
                per_b((R, HEAD_DIM)), per_b((LANE, n_lane)), per_b((R, HEAD_DIM)),
                per_b((tpad, PAGE_COLS)), per_b((2, NSA_KV_HEADS, HEAD_DIM, WINDOW)), per_b((tpad, PAGE_COLS)),
                per_b((R, 3))],
            out_specs=per_b((R, HEAD_DIM)),
            scratch_shapes=[pltpu.VMEM((R, 1), jnp.float32), pltpu.VMEM((R, 1), jnp.float32),
                            pltpu.VMEM((R, HEAD_DIM), jnp.float32)]),
        out_shape=jax.ShapeDtypeStruct((B, R, HEAD_DIM), jnp.float32),
        compiler_params=pltpu.CompilerParams(dimension_semantics=("arbitrary", "arbitrary"),
                                             vmem_limit_bytes=_VMEM_LIMIT),
        name="nsa_sel_win",
    )(page_table, *([cache] * NSA_PG), q_rows, sel_rows, o_cmp, new_sel, win_prefix, new_win, gate_rows)


def _nsa_decode_mix(xn, cache_cmp, cache_sel, page_table, layer, win_prefix, cmp_w):
    B, T, _ = xn.shape
    n_pages = page_table.shape[1]
    past = n_pages * PAGE_SIZE
    assert T <= SEL_BLOCK and n_pages % NSA_POOL_PG == 0 and win_prefix.shape[1] == WINDOW
    KVH, G, HD = NSA_KV_HEADS, NSA_GROUP, HEAD_DIM
    kv5 = lambda i: xn[..., D_NSA + 2 * i * NSA_KV_COLS:D_NSA + 2 * (i + 1) * NSA_KV_COLS]
    new_cmp, new_sel, new_win = kv5(0), kv5(1), kv5(2)
    q_rows = (xn[..., :D_NSA] * (HD ** -0.5)).reshape(B, T, KVH, G, HD).transpose(0, 2, 3, 1, 4)
    q_rows = q_rows.reshape(B, KVH * G * T, HD)
    gate_rows = xn[..., D_NSA + 6 * NSA_KV_COLS:].reshape(B, T, KVH, G, 3).transpose(0, 2, 3, 1, 4)
    gate_rows = gate_rows.reshape(B, KVH * G * T, 3)
    cw = jnp.tile(cmp_w, (1, PAGE_SIZE // CMP_BLOCK))
    kvcb = _cmp_pool(_rows_last(cache_cmp), page_table, layer, cw)
    o_cmp, sel_rows = _cmp_select(q_rows, kvcb, past, T)
    tpad = _round_up(T, 8)
    padt = lambda t: jnp.pad(t, ((0, 0), (0, tpad - T), (0, 0)))
    out = _sel_win(_rows_last(cache_sel), page_table, layer, q_rows, sel_rows, o_cmp, padt(new_sel),
                   _rows_last(win_prefix), padt(new_win), gate_rows, past, T)
    y = out.reshape(B, KVH, G, T, HD).transpose(0, 3, 1, 2, 4).reshape(B, T, D_NSA)
    five = lambda t: t.reshape(B, -1, 2, KVH, HD)
    win_out = jnp.concatenate([win_prefix[:, T:], five(new_win)], axis=1)
    return y, five(new_cmp), five(new_sel), win_out


def _out_proj_kernel(yr_ref, yn_ref, x_ref, w_ref, g_ref, b_ref, o_ref):
    half = yr_ref.shape[1]
    h = (jnp.dot(yr_ref[...].astype(jnp.bfloat16), w_ref[0:half, :], preferred_element_type=jnp.float32)
         + jnp.dot(yn_ref[...].astype(jnp.bfloat16), w_ref[half:, :], preferred_element_type=jnp.float32))
    z = DEEPNORM_ALPHA * x_ref[...] + h
    mu = jnp.mean(z, axis=-1, keepdims=True)
    dev = z - mu
    var = jnp.mean(dev * dev, axis=-1, keepdims=True)
    o_ref[...] = dev * lax.rsqrt(var + LN_EPS) * g_ref[...] + b_ref[...]


def _out_proj_norm(y_r, y_n, x, w_bf, g, b, *, tm=256):
    n, d = x.shape
    tm = min(tm, n)
    half = y_r.shape[1]
    rows = lambda w: pl.BlockSpec((tm, w), lambda i: (i, 0))
    full = lambda shape: pl.BlockSpec(shape, lambda i: (0, 0))
    return pl.pallas_call(
        _out_proj_kernel,
        grid=(n // tm,),
        in_specs=[rows(half), rows(y_n.shape[1]), rows(d), full(w_bf.shape), full((1, d)), full((1, d))],
        out_specs=rows(d),
        out_shape=jax.ShapeDtypeStruct((n, d), jnp.float32),
        compiler_params=pltpu.CompilerParams(dimension_semantics=("arbitrary",),
                                             vmem_limit_bytes=_VMEM_LIMIT),
        name="out_proj_norm",
    )(y_r, y_n, x, w_bf, g.reshape(1, d), b.reshape(1, d))


def _hybrid_layer(x, past, shift_prev, rwkv_s0, n_keep, p):
    B, T, D = x.shape
    proj = _matmul(x.reshape(B * T, D), p['w_in']).reshape(B, T, IN_COLS)
    y_r, s_T, new_shift = _rwkv_time_mix(proj, shift_prev, rwkv_s0, p)
    if past is None:
        assert T % NSA_TQ == 0 and T >= n_keep
        y_n, new_cmp, new_sel, new_win = _nsa_prompt_mix(proj[..., RWKV_COLS:], n_keep, p['nsa_cmp_w'])
    else:
        assert n_keep == WINDOW
        y_n, new_cmp, new_sel, new_win = _nsa_decode_mix(proj[..., RWKV_COLS:], *past, p['nsa_cmp_w'])
    x = _out_proj_norm(y_r.reshape(B * T, D_RWKV), y_n.reshape(B * T, D_NSA), x.reshape(B * T, D),
                       p['w_out_bf'], p['ln1_g'], p['ln1_b']).reshape(B, T, D)
    f = _peer_ffn(x, p['peer_wq_bf'], p['peer_sk_bf'], p['peer_u_bf'], p['peer_v_bf'])
    x = _layer_norm(DEEPNORM_ALPHA * x + f, p['ln2_g'], p['ln2_b'])
    return x, (new_cmp, new_sel, new_win, s_T, new_shift)


def kernel(x_prompt, x_sample, cache_cmp_kv, cache_sel_kv, page_table, state_win_kv, state_rwkv,
           state_shift, w_in, rwkv_mu, rwkv_w0, rwkv_w2, rwkv_a0, rwkv_a2, rwkv_g2, rwkv_k_k,
           rwkv_k_a, rwkv_r_k, rwkv_gn_g, rwkv_gn_b, nsa_cmp_w, w_out, ln1_g, ln1_b, peer_wq,
           peer_subkeys, peer_u, peer_v, ln2_g, ln2_b):
    bp = x_prompt.shape[0]
    dt = x_prompt.dtype
    assert page_table.shape[1] * PAGE_SIZE == PAST_LEN
    n_keep = state_win_kv.shape[2]
    zero_shift = jnp.zeros((bp, 1, RWKV_COLS), dt)
    zero_state = jnp.zeros((bp, H_RWKV, HEAD_DIM, HEAD_DIM), dt)
    yp, ys = x_prompt, x_sample
    st_p, st_s = [], []
    for l in range(DEPTH):
        p = {'w_in': w_in[l], 'rwkv_mu': rwkv_mu[l], 'rwkv_w0': rwkv_w0[l], 'rwkv_w2': rwkv_w2[l],
             'rwkv_a0': rwkv_a0[l], 'rwkv_a2': rwkv_a2[l], 'rwkv_g2': rwkv_g2[l],
             'rwkv_k_k': rwkv_k_k[l], 'rwkv_k_a': rwkv_k_a[l], 'rwkv_r_k': rwkv_r_k[l],
             'rwkv_gn_g': rwkv_gn_g[l], 'rwkv_gn_b': rwkv_gn_b[l], 'nsa_cmp_w': nsa_cmp_w[l],
             'w_out_bf': w_out[l].astype(jnp.bfloat16), 'ln1_g': ln1_g[l], 'ln1_b': ln1_b[l],
             'peer_wq_bf': peer_wq[l].astype(jnp.bfloat16),
             'peer_sk_bf': peer_subkeys[l].reshape(2 * P_HEADS, N_KEYS, P_DKEY // 2).astype(jnp.bfloat16),
             'peer_u_bf': peer_u[l].astype(jnp.bfloat16).T, 'peer_v_bf': peer_v[l].astype(jnp.bfloat16),
             'ln2_g': ln2_g[l], 'ln2_b': ln2_b[l]}
        yp, sp = _hybrid_layer(yp, None, zero_shift, zero_state, n_keep, p)
        past = (cache_cmp_kv, cache_sel_kv, page_table, l, state_win_kv[l])
        ys, ss = _hybrid_layer(ys, past, state_shift[l], state_rwkv[l], n_keep, p)
        st_p.append(sp)
        st_s.append(ss)
    stk = lambda sts, i: jnp.stack([s[i] for s in sts], axis=0)
    return (yp, ys, stk(st_p, 0), stk(st_p, 1), stk(st_p, 2), stk(st_p, 3), stk(st_p, 4),
            stk(st_s, 0), stk(st_s, 1), stk(st_s, 2), stk(st_s, 3), stk(st_s, 4))
```

```python
import functools
import math

import jax
import jax.numpy as jnp
from jax import lax
from jax.experimental import pallas as pl
from jax.experimental.pallas import tpu as pltpu

D_MODEL = 2048
DEPTH = 2
PAST_LEN = 16384
PAGE_SIZE = 128
HEAD_DIM = 64
D_RWKV = D_MODEL // 2
D_NSA = D_MODEL - D_RWKV
H_RWKV = D_RWKV // HEAD_DIM
H_NSA = D_NSA // HEAD_DIM
NSA_KV_HEADS = 4
NSA_GROUP = H_NSA // NSA_KV_HEADS
NSA_KV_COLS = NSA_KV_HEADS * HEAD_DIM
CMP_BLOCK = 32
SEL_BLOCK = 64
SEL_TOPN = 16
WINDOW = 512
W_LORA = 64
A_LORA = 64
G_LORA = 160
RWKV_COLS = 3 * D_RWKV + W_LORA + A_LORA + G_LORA
NSA_COLS = D_NSA + 6 * NSA_KV_COLS + 3 * H_NSA
IN_COLS = RWKV_COLS + NSA_COLS
P_HEADS = 8
N_KEYS = 128
P_DKEY = 256
P_TOPK = 16
LN_EPS = 1e-5
GN_EPS = 64e-5
DEEPNORM_ALPHA = (2 * DEPTH) ** 0.25
FORCE_SCORE = 1e4
NEG_INF = -1e30

N_EXPERTS = N_KEYS * N_KEYS

LANE = 128
_VMEM_LIMIT = 56 * 1024 * 1024


def _round_up(x, m):
    return -(-x // m) * m


PROJ_TILE = 1024
IN_COLS_PAD = _round_up(IN_COLS, PROJ_TILE)


def _mm_kernel(x_ref, w_ref, o_ref):
    o_ref[...] = jnp.dot(x_ref[...].astype(jnp.bfloat16), w_ref[...], preferred_element_type=jnp.float32)


def _matmul(x, w_bf):
    m, k = x.shape
    n = w_bf.shape[1]
    tm = min(PROJ_TILE, m)
    tn = PROJ_TILE
    return pl.pallas_call(
        _mm_kernel,
        grid=(m // tm, n // tn),
        in_specs=[pl.BlockSpec((tm, k), lambda i, j: (i, 0)),
                  pl.BlockSpec((k, tn), lambda i, j: (0, j))],
        out_specs=pl.BlockSpec((tm, tn), lambda i, j: (i, j)),
        out_shape=jax.ShapeDtypeStruct((m, n), jnp.float32),
        compiler_params=pltpu.CompilerParams(dimension_semantics=("arbitrary", "arbitrary"),
                                             vmem_limit_bytes=_VMEM_LIMIT),
        name="in_proj",
    )(x, w_bf)


def _layer_norm(x, g, b):
    mu = x.mean(-1, keepdims=True)
    var = jnp.mean(jnp.square(x - mu), -1, keepdims=True)
    return (x - mu) * lax.rsqrt(var + LN_EPS) * g + b


RW_CHUNK = 64
RW_PAIRS = D_RWKV // LANE


def _bdot(a, b):
    return jnp.dot(a.astype(jnp.bfloat16), b.astype(jnp.bfloat16), preferred_element_type=jnp.float32)


def _bdot_nt(a, b):
    return lax.dot_general(a.astype(jnp.bfloat16), b.astype(jnp.bfloat16), (((1,), (1,)), ((), ())),
                           preferred_element_type=jnp.float32)


def _bdot_tn(a, b):
    return lax.dot_general(a.astype(jnp.bfloat16), b.astype(jnp.bfloat16), (((0,), (0,)), ((), ())),
                           preferred_element_type=jnp.float32)


def _head_sum(x, same_head_bf):
    hi = x.astype(jnp.bfloat16)
    lo = (x - hi.astype(jnp.float32)).astype(jnp.bfloat16)
    return (jnp.dot(hi, same_head_bf, preferred_element_type=jnp.float32)
            + jnp.dot(lo, same_head_bf, preferred_element_type=jnp.float32))


def _rwkv_mix_kernel(x_ref, shift_ref, mu_ref, pv_ref, w2_ref, a2_ref, g2_ref, s0_ref, y_ref, st_ref,
                     prev_scr, *, steps):
    c = pl.program_id(1)
    C = RW_CHUNK
    f32 = jnp.float32

    @pl.when(c == 0)
    def _():
        st_ref[...] = s0_ref[...]
        prev_scr[...] = shift_ref[...]

    xr = x_ref[:, 0:RWKV_COLS]
    row_c = lax.broadcasted_iota(jnp.int32, (C, 1), 0)
    prev = jnp.where(row_c == 0, prev_scr[...], pltpu.roll(xr, 1, 0))
    prev_scr[...] = xr[C - 1:C, :]
    xs = xr + mu_ref[...] * (prev - xr)
    o_wl = 3 * D_RWKV
    r_all, k_all, v_all = xs[:, 0:D_RWKV], xs[:, D_RWKV:2 * D_RWKV], xs[:, 2 * D_RWKV:o_wl]
    wl = xs[:, o_wl:o_wl + W_LORA]
    al = xs[:, o_wl + W_LORA:o_wl + W_LORA + A_LORA]
    gl = xs[:, o_wl + W_LORA + A_LORA:RWKV_COLS]
    w0, a0_, k_k, k_a = pv_ref[0:1, :], pv_ref[1:2, :], pv_ref[2:3, :], pv_ref[3:4, :]
    r_k, gn_g, gn_b = pv_ref[4:5, :], pv_ref[5:6, :], pv_ref[6:7, :]
    w_all = -jax.nn.softplus(-(w0 + _bdot(jnp.tanh(wl), w2_ref[...]))) - 0.5
    lw_all = -jnp.exp(w_all)
    a_all = jax.nn.sigmoid(a0_ + _bdot(al, a2_ref[...]))
    gate_all = _bdot(jax.nn.sigmoid(gl), g2_ref[...])
    kkf_all = k_all * k_k
    k2_all = k_all * (1.0 + (a_all - 1.0) * k_a)
    if steps % C:
        live = (c * C + row_c) < steps
        lw_all = jnp.where(live, lw_all, 0.0)
        kkf_all = jnp.where(live, kkf_all, 0.0)
        k2_all = jnp.where(live, k2_all, 0.0)
        v_all = jnp.where(live, v_all, 0.0)

    row = lax.broadcasted_iota(jnp.int32, (C, C), 0)
    col = lax.broadcasted_iota(jnp.int32, (C, C), 1)
    tri_incl = col <= row
    tri_strict = col < row
    tri_f = jnp.where(tri_incl, 1.0, 0.0).astype(f32)
    eye = jnp.where(row == col, 1.0, 0.0).astype(f32)
    lane = lax.broadcasted_iota(jnp.int32, (1, LANE), 1)
    head0 = lane < HEAD_DIM
    r128 = lax.broadcasted_iota(jnp.int32, (LANE, LANE), 0)
    c128 = lax.broadcasted_iota(jnp.int32, (LANE, LANE), 1)
    same_head = (r128 < HEAD_DIM) == (c128 < HEAD_DIM)
    same_head_bf = jnp.where(same_head, 1.0, 0.0).astype(jnp.bfloat16)
    eye128 = r128 == c128

    pairs = range(RW_PAIRS)
    heads = (head0, ~head0)
    sls = [slice(pr * LANE, (pr + 1) * LANE) for pr in pairs]
    lw = [lw_all[:, sl] for sl in sls]
    cum = [jnp.dot(tri_f, lw[pr], precision=lax.Precision.HIGHEST, preferred_element_type=f32)
           for pr in pairs]
    tot = [cum[pr][C - 1:C, :] for pr in pairs]
    v = [v_all[:, sl] for sl in sls]
    k = [k2_all[:, sl] for sl in sls]
    r = [r_all[:, sl] for sl in sls]
    kkf = [kkf_all[:, sl] for sl in sls]
    ss = [_head_sum(kkf[pr] * kkf[pr], same_head_bf) for pr in pairs]
    kk = [kkf[pr] * lax.rsqrt(jnp.maximum(ss[pr], 1e-24)) for pr in pairs]
    b = [kk[pr] * a_all[:, sls[pr]] for pr in pairs]
    x, ym = [], []
    for pr in pairs:
        e_neg = jnp.exp(-cum[pr])
        x.append(jnp.concatenate([kk[pr] * jnp.exp(cum[pr] - lw[pr]),
                                  r[pr] * jnp.exp(cum[pr])], axis=0))
        ym.append(jnp.concatenate([k[pr] * e_neg, b[pr] * e_neg], axis=0))
    a0 = [st_ref[pr] for pr in pairs]
    xa = [_bdot(x[pr], a0[pr]) for pr in pairs]
    g = [[_bdot_nt(jnp.where(hm, x[pr], 0.0), ym[pr]) for hm in heads] for pr in pairs]
    lkv = [[_bdot(jnp.where(tri_strict, g[pr][h][:C, :C], 0.0), v[pr]) for h in range(2)] for pr in pairs]
    npow = [[jnp.where(tri_strict, -g[pr][h][:C, C:], 0.0) for h in range(2)] for pr in pairs]
    tmat = [[eye + npow[pr][h] for h in range(2)] for pr in pairs]
    for _ in range(int(math.log2(C)) - 1):
        npow = [[_bdot(npow[pr][h], npow[pr][h]) for h in range(2)] for pr in pairs]
        tmat = [[tmat[pr][h] + _bdot(tmat[pr][h], npow[pr][h]) for h in range(2)] for pr in pairs]
    rhs = [xa[pr][:C] + jnp.where(head0, lkv[pr][0], lkv[pr][1]) for pr in pairs]
    w = [jnp.where(head0, _bdot(tmat[pr][0], rhs[pr]), _bdot(tmat[pr][1], rhs[pr])) for pr in pairs]
    vw = [jnp.concatenate([v[pr], w[pr]], axis=0) for pr in pairs]
    y = []
    for pr in pairs:
        mr = [jnp.concatenate([jnp.where(tri_incl, g[pr][h][C:, :C], 0.0),
                               jnp.where(tri_incl, -g[pr][h][C:, C:], 0.0)], axis=1) for h in range(2)]
        y.append(xa[pr][C:] + jnp.where(head0, _bdot(mr[0], vw[pr]), _bdot(mr[1], vw[pr])))
    inv_hd = 1.0 / HEAD_DIM
    mean = [_head_sum(y[pr], same_head_bf) * inv_hd for pr in pairs]
    dev = [y[pr] - mean[pr] for pr in pairs]
    var = [_head_sum(dev[pr] * dev[pr], same_head_bf) * inv_hd for pr in pairs]
    rk = [_head_sum(r[pr] * k[pr] * r_k[:, sls[pr]], same_head_bf) for pr in pairs]
    for pr in pairs:
        yn = dev[pr] * lax.rsqrt(var[pr] + GN_EPS) * gn_g[:, sls[pr]] + gn_b[:, sls[pr]]
        y_ref[:, sls[pr]] = (yn + rk[pr] * v[pr]) * gate_all[:, sls[pr]]
    for pr in pairs:
        e_rem = jnp.exp(tot[pr] - cum[pr])
        kb = jnp.concatenate([k[pr] * e_rem, -(b[pr] * e_rem)], axis=0)
        upd = _bdot_tn(kb, vw[pr])
        p_col = jnp.sum(jnp.where(eye128, jnp.exp(tot[pr]), 0.0), axis=1, keepdims=True)
        st_ref[pr] = a0[pr] * p_col + jnp.where(same_head, upd, 0.0)


def _rwkv_time_mix(proj, shift_prev, s0, p):
    B, T, W = proj.shape
    C = RW_CHUNK
    t_pad = _round_up(T, C)
    x = proj if t_pad == T else jnp.pad(proj, ((0, 0), (0, t_pad - T), (0, 0)))
    a = jnp.swapaxes(s0, -1, -2).reshape(B, RW_PAIRS, 2, HEAD_DIM, HEAD_DIM)
    z = jnp.zeros_like(a[:, :, 0])
    a0 = jnp.concatenate([jnp.concatenate([a[:, :, 0], z], axis=-1),
                          jnp.concatenate([z, a[:, :, 1]], axis=-1)], axis=-2)
    zero = jnp.zeros((D_RWKV,), jnp.float32)
    pv = jnp.stack([p['rwkv_w0'], p['rwkv_a0'], p['rwkv_k_k'], p['rwkv_k_a'], p['rwkv_r_k'].reshape(-1),
                    p['rwkv_gn_g'], p['rwkv_gn_b'], zero])
    bf = lambda t: t.astype(jnp.bfloat16)
    full = lambda shape: pl.BlockSpec(shape, lambda bi, ci: (0,) * len(shape))
    st_spec = pl.BlockSpec((None, RW_PAIRS, LANE, LANE), lambda bi, ci: (bi, 0, 0, 0))
    y, st = pl.pallas_call(
        functools.partial(_rwkv_mix_kernel, steps=T),
        grid=(B, t_pad // C),
        in_specs=[pl.BlockSpec((None, C, W), lambda bi, ci: (bi, ci, 0)),
                  pl.BlockSpec((None, 1, RWKV_COLS), lambda bi, ci: (bi, 0, 0)),
                  full((1, RWKV_COLS)), full((8, D_RWKV)), full((W_LORA, D_RWKV)),
                  full((A_LORA, D_RWKV)), full((G_LORA, D_RWKV)), st_spec],
        out_specs=[pl.BlockSpec((None, C, D_RWKV), lambda bi, ci: (bi, ci, 0)), st_spec],
        out_shape=[jax.ShapeDtypeStruct((B, t_pad, D_RWKV), jnp.float32),
                   jax.ShapeDtypeStruct((B, RW_PAIRS, LANE, LANE), jnp.float32)],
        scratch_shapes=[pltpu.VMEM((1, RWKV_COLS), jnp.float32)],
        compiler_params=pltpu.CompilerParams(dimension_semantics=("arbitrary", "arbitrary"),
                                             vmem_limit_bytes=_VMEM_LIMIT),
        name="rwkv_mix",
    )(x, shift_prev, p['rwkv_mu'].reshape(1, RWKV_COLS), pv, bf(p['rwkv_w2']), bf(p['rwkv_a2']),
      bf(p['rwkv_g2']), a0)
    s_t = jnp.stack([st[:, :, :HEAD_DIM, :HEAD_DIM], st[:, :, HEAD_DIM:, HEAD_DIM:]], axis=2)
    s_t = jnp.swapaxes(s_t.reshape(B, H_RWKV, HEAD_DIM, HEAD_DIM), -1, -2)
    return y[:, :T], s_t, proj[:, T - 1:T, :RWKV_COLS]


def _extract_top(buf_ref, out_ref, nrows, width):
    iota = lax.broadcasted_iota(jnp.int32, (nrows, width), 0)

    def body(r, carry):
        s = buf_ref[...]
        mx = jnp.max(s, axis=0, keepdims=True)
        first = jnp.min(jnp.where(s == mx, iota, nrows), axis=0, keepdims=True)
        buf_ref[...] = jnp.where(iota == first, -jnp.inf, s)
        out_ref[pl.ds(r, 1), :] = mx
        return carry

    lax.fori_loop(0, P_TOPK, body, 0)


def _peer_score_kernel(x_ref, wq_ref, sk_ref, s1_ref, s2_ref, st_ref, buf, cand, hv0, hv1, tv):
    tn = x_ref.shape[0]
    q = jnp.dot(x_ref[...].astype(jnp.bfloat16), wq_ref[...],
                preferred_element_type=jnp.float32).astype(jnp.bfloat16)
    for h in range(P_HEADS):
        for c, (s_ref, hv) in enumerate(((s1_ref, hv0), (s2_ref, hv1))):
            col = (2 * h + c) * N_KEYS
            s_t = lax.dot_general(sk_ref[2 * h + c], q[:, col:col + N_KEYS],
                                  (((1,), (1,)), ((), ())), preferred_element_type=jnp.float32)
            s_ref[h] = s_t
            buf[...] = s_t
            _extract_top(buf, hv, N_KEYS, tn)
        pieces = [hv0[a:a + 1, :] + hv1[0:P_TOPK // (a + 1), :] for a in range(P_TOPK)]
        n_cand = sum(P_TOPK // (a + 1) for a in range(P_TOPK))
        pieces.append(jnp.full((cand.shape[0] - n_cand, tn), -jnp.inf, jnp.float32))
        cand[...] = jnp.concatenate(pieces, axis=0)
        _extract_top(cand, tv, cand.shape[0], tn)
        z = jnp.sum(jnp.exp(tv[...] - tv[0:1, :]), axis=0, keepdims=True)
        st_ref[0, h:h + 1, :] = tv[P_TOPK - 1:P_TOPK, :]
        st_ref[1, h:h + 1, :] = hv0[0:1, :]
        st_ref[2, h:h + 1, :] = hv1[0:1, :]
        st_ref[3, h:h + 1, :] = 1.0 / z


def _peer_scores(x, wq_bf, sk_bf, *, tn=256):
    n = x.shape[0]
    tn = min(tn, n)
    return pl.pallas_call(
        _peer_score_kernel,
        grid=(n // tn,),
        in_specs=[pl.BlockSpec((tn, D_MODEL), lambda i: (i, 0)),
                  pl.BlockSpec((D_MODEL, P_HEADS * P_DKEY), lambda i: (0, 0)),
                  pl.BlockSpec((2 * P_HEADS, N_KEYS, P_DKEY // 2), lambda i: (0, 0, 0))],
        out_specs=[pl.BlockSpec((P_HEADS, N_KEYS, tn), lambda i: (0, 0, i)),
                   pl.BlockSpec((P_HEADS, N_KEYS, tn), lambda i: (0, 0, i)),
                   pl.BlockSpec((4, P_HEADS, tn), lambda i: (0, 0, i))],
        out_shape=[jax.ShapeDtypeStruct((P_HEADS, N_KEYS, n), jnp.float32),
                   jax.ShapeDtypeStruct((P_HEADS, N_KEYS, n), jnp.float32),
                   jax.ShapeDtypeStruct((4, P_HEADS, n), jnp.float32)],
        scratch_shapes=[pltpu.VMEM((N_KEYS, tn), jnp.float32),
                        pltpu.VMEM((_round_up(sum(P_TOPK // (a + 1) for a in range(P_TOPK)), 8), tn),
                                   jnp.float32),
                        pltpu.VMEM((P_TOPK, tn), jnp.float32),
                        pltpu.VMEM((P_TOPK, tn), jnp.float32),
                        pltpu.VMEM((P_TOPK, tn), jnp.float32)],
        compiler_params=pltpu.CompilerParams(dimension_semantics=("arbitrary",),
                                             vmem_limit_bytes=_VMEM_LIMIT),
        name="peer_scores",
    )(x, wq_bf, sk_bf)


def _peer_expert_kernel(x_ref, u_ref, v_ref, s1_ref, s2_ref, st_ref, o_ref, e2_scr, h_scr, p_scr,
                        *, chunk):
    j = pl.program_id(1)
    tn = x_ref.shape[0]
    te = u_ref.shape[1]

    @pl.when(j == 0)
    def _():
        o_ref[...] = jnp.zeros_like(o_ref)
        for h in range(P_HEADS):
            e2_scr[h] = jnp.exp(s2_ref[h] - st_ref[2, h:h + 1, :])

    def hidden(c):
        tsl = slice(c * chunk, (c + 1) * chunk)
        h_scr[tsl, :] = jnp.dot(x_ref[tsl, :], u_ref[...], preferred_element_type=jnp.float32)

    a_rows = [[s1_ref[h, pl.ds(j * (te // N_KEYS) + ii, 1), :] for h in range(P_HEADS)]
              for ii in range(te // N_KEYS)]
    e1_rows = [[jnp.exp(a_rows[ii][h] - st_ref[1, h:h + 1, :]) * st_ref[3, h:h + 1, :]
                for h in range(P_HEADS)] for ii in range(te // N_KEYS)]

    def weigh(c):
        for sub in range(chunk // LANE):
            weigh_lanes(slice(c * chunk + sub * LANE, c * chunk + (sub + 1) * LANE))

    def weigh_lanes(tsl):
        for ii in range(te // N_KEYS):
            w_t = jnp.zeros((N_KEYS, LANE), jnp.float32)
            for h in range(P_HEADS):
                val = a_rows[ii][h][:, tsl] + s2_ref[h, :, tsl]
                w_t = w_t + jnp.where(val >= st_ref[0, h:h + 1, tsl],
                                      e1_rows[ii][h][:, tsl] * e2_scr[h, :, tsl], 0.0)
            hh = h_scr[tsl, ii * N_KEYS:(ii + 1) * N_KEYS]
            g = 0.5 * hh * (1.0 + lax.erf(hh * 0.7071067811865476))
            p_scr[tsl, ii * N_KEYS:(ii + 1) * N_KEYS] = (w_t.T * g).astype(jnp.bfloat16)

    def project(c):
        tsl = slice(c * chunk, (c + 1) * chunk)
        o_ref[tsl, :] += jnp.dot(p_scr[tsl, :], v_ref[...], preferred_element_type=jnp.float32)

    n_chunks = tn // chunk
    hidden(0)
    for c in range(n_chunks):
        if c + 1 < n_chunks:
            hidden(c + 1)
        weigh(c)
        project(c)


def _peer_experts(x_bf, u_bf, v_bf, s1t, s2t, stats, *, tn=1024, te=512, chunk=256):
    n = x_bf.shape[0]
    tn = min(tn, n)
    chunk = min(chunk, tn)
    once = pl.Buffered(1)
    return pl.pallas_call(
        functools.partial(_peer_expert_kernel, chunk=chunk),
        grid=(n // tn, N_EXPERTS // te),
        in_specs=[pl.BlockSpec((tn, D_MODEL), lambda i, j: (i, 0), pipeline_mode=once),
                  pl.BlockSpec((D_MODEL, te), lambda i, j: (0, j)),
                  pl.BlockSpec((te, D_MODEL), lambda i, j: (j, 0)),
                  pl.BlockSpec((P_HEADS, N_KEYS, tn), lambda i, j: (0, 0, i), pipeline_mode=once),
                  pl.BlockSpec((P_HEADS, N_KEYS, tn), lambda i, j: (0, 0, i), pipeline_mode=once),
                  pl.BlockSpec((4, P_HEADS, tn), lambda i, j: (0, 0, i), pipeline_mode=once)],
        out_specs=pl.BlockSpec((tn, D_MODEL), lambda i, j: (i, 0)),
        out_shape=jax.ShapeDtypeStruct((n, D_MODEL), jnp.float32),
        scratch_shapes=[pltpu.VMEM((P_HEADS, N_KEYS, tn), jnp.float32),
                        pltpu.VMEM((tn, te), jnp.float32),
                        pltpu.VMEM((tn, te), jnp.bfloat16)],
        compiler_params=pltpu.CompilerParams(dimension_semantics=("arbitrary", "arbitrary"),
                                             vmem_limit_bytes=_VMEM_LIMIT),
        name="peer_experts",
    )(x_bf, u_bf, v_bf, s1t, s2t, stats)


def _peer_ffn(x, wq_bf, sk_bf, u_bf, v_bf):
    B, T, D = x.shape
    n = B * T
    xt = x.reshape(n, D)
    n_pad = _round_up(n, LANE)
    if n_pad != n:
        xt = jnp.pad(xt, ((0, n_pad - n), (0, 0)))
    s1t, s2t, stats = _peer_scores(xt, wq_bf, sk_bf)
    out = _peer_experts(xt.astype(jnp.bfloat16), u_bf, v_bf, s1t, s2t, stats)
    return out[:n].reshape(B, T, D)


NSA_TQ = 256
NSA_GSPLIT = 4


def _flash_tile(q4, k_t, v_t, mask4, m, l, acc):
    s = lax.dot_general(q4, k_t, (((1,), (1,)), ((), ())), preferred_element_type=jnp.float32)
    s = jnp.where(mask4, s, NEG_INF)
    m_new = jnp.maximum(m, jnp.max(s, axis=-1, keepdims=True))
    p = jnp.where(mask4, jnp.exp(s - m_new), 0.0)
    alpha = jnp.exp(m - m_new)
    l_new = alpha * l + jnp.sum(p, axis=-1, keepdims=True)
    acc_new = alpha * acc + jnp.dot(p.astype(jnp.bfloat16), v_t, preferred_element_type=jnp.float32)
    return m_new, l_new, acc_new


def _nsa_prompt_kernel(q_ref, gl_ref, kc_ref, vc_ref, ks_ref, vs_ref, kw_ref, vw_ref, cw_ref, ex_ref,
                       o_ref, kcb_scr, vcb_scr, imp_scr, selx_scr, *, seq):
    qt = pl.program_id(2)
    tq = NSA_TQ
    n_cmp = seq // CMP_BLOCK
    n_sel = seq // SEL_BLOCK
    G = NSA_GROUP
    q0 = qt * tq

    @pl.when(qt == 0)
    def _():
        kc = kc_ref[...].reshape(n_cmp, CMP_BLOCK, HEAD_DIM)
        vc = vc_ref[...].reshape(n_cmp, CMP_BLOCK, HEAD_DIM)
        kcb_scr[...] = jnp.zeros_like(kcb_scr)
        vcb_scr[...] = jnp.zeros_like(vcb_scr)
        kcb_scr[0:n_cmp, :] = jnp.sum(kc * cw_ref[0][None], axis=1)
        vcb_scr[0:n_cmp, :] = jnp.sum(vc * cw_ref[1][None], axis=1).astype(jnp.bfloat16)

    qpos_col = q0 + lax.broadcasted_iota(jnp.int32, (tq, 1), 0)
    qf = [q_ref[g] * (HEAD_DIM ** -0.5) for g in range(G)]
    qs = [t.astype(jnp.bfloat16) for t in qf]
    gates = jax.nn.sigmoid(gl_ref[...])

    cmp_end = (lax.broadcasted_iota(jnp.int32, (1, LANE), 1) + 1) * CMP_BLOCK - 1
    cmask = (cmp_end <= qpos_col) & (lax.broadcasted_iota(jnp.int32, (1, LANE), 1) < n_cmp)
    o_cmp = []
    imp = jnp.zeros((tq, LANE), jnp.float32)
    for g in range(G):
        s = lax.dot_general(qf[g], kcb_scr[...], (((1,), (1,)), ((), ())),
                            precision=lax.Precision.HIGHEST, preferred_element_type=jnp.float32)
        s = jnp.where(cmask, s, NEG_INF)
        e = jnp.exp(s - jnp.max(s, axis=-1, keepdims=True))
        p = jnp.where(cmask, e / jnp.sum(e, axis=-1, keepdims=True), 0.0)
        o_cmp.append(jnp.dot(p.astype(jnp.bfloat16), vcb_scr[...], preferred_element_type=jnp.float32))
        imp = imp + p
    imp_t = imp.T
    ratio = SEL_BLOCK // CMP_BLOCK
    parts = []
    for c in range(tq // LANE):
        imp_scr[c] = imp_t[:, c * LANE:(c + 1) * LANE]
        part = imp_scr[c, pl.ds(0, n_sel, stride=ratio), :]
        for r in range(1, ratio):
            part = part + imp_scr[c, pl.ds(r, n_sel, stride=ratio), :]
        parts.append(part)
    imp_s = jnp.concatenate(parts, axis=1)
    qpos_row = q0 + lax.broadcasted_iota(jnp.int32, (1, tq), 1)
    cur = qpos_row // SEL_BLOCK
    blk = lax.broadcasted_iota(jnp.int32, (n_sel, tq), 0)
    forced = (blk == 0) | (blk == cur) | (blk == cur - 1)
    imp_s = jnp.where(blk > cur, -1.0, jnp.where(forced, FORCE_SCORE, imp_s))
    rank = jnp.zeros((n_sel, tq), jnp.int32)
    for mm in range(n_sel):
        row = imp_s[mm:mm + 1, :]
        ahead = (row > imp_s) | ((row == imp_s) & (mm < blk))
        rank = rank + jnp.where(ahead, 1, 0)
    sel = jnp.where((rank < min(SEL_TOPN, n_sel)) & (imp_s >= 0.0), 1.0, 0.0)
    sel_pad = jnp.concatenate([sel, jnp.zeros((LANE - n_sel, tq), jnp.float32)], axis=0)
    sel_q = sel_pad.T.astype(jnp.bfloat16)
    selx_scr[...] = jnp.dot(sel_q, ex_ref[...], preferred_element_type=jnp.float32)

    kpos_l = lax.broadcasted_iota(jnp.int32, (1, tq), 1)
    GS = NSA_GSPLIT
    qpos_r = jnp.concatenate([qpos_col] * GS, axis=0)
    init = (jnp.full((GS * tq, 1), NEG_INF, jnp.float32), jnp.zeros((GS * tq, 1), jnp.float32),
            jnp.zeros((GS * tq, HEAD_DIM), jnp.float32))

    for g0 in range(0, G, GS):
        q_r = jnp.concatenate(qs[g0:g0 + GS], axis=0)

        def sel_body(kt, carry):
            k0 = pl.multiple_of(kt * tq, tq)
            chosen = selx_scr[:, pl.ds(k0, tq)] > 0.5
            mask = jnp.concatenate([chosen] * GS, axis=0) & ((k0 + kpos_l) <= qpos_r)
            return _flash_tile(q_r, ks_ref[pl.ds(k0, tq), :], vs_ref[pl.ds(k0, tq), :], mask, *carry)

        _, l_s, acc_s = lax.fori_loop(0, qt + 1, sel_body, init)
        o_sel = acc_s / l_s

        def win_body(kt, carry):
            k0 = pl.multiple_of(kt * tq, tq)
            kpos = k0 + kpos_l
            mask = (kpos <= qpos_r) & (kpos >= qpos_r - (WINDOW - 1))
            return _flash_tile(q_r, kw_ref[pl.ds(k0, tq), :], vw_ref[pl.ds(k0, tq), :], mask, *carry)

        _, l_w, acc_w = lax.fori_loop(jnp.maximum(qt - WINDOW // tq, 0), qt + 1, win_body, init)
        o_win = acc_w / l_w

        for gi in range(GS):
            g = g0 + gi
            rows = slice(gi * tq, (gi + 1) * tq)
            o_ref[g] = (gates[:, 3 * g:3 * g + 1] * o_cmp[g] + gates[:, 3 * g + 1:3 * g + 2] * o_sel[rows]
                        + gates[:, 3 * g + 2:3 * g + 3] * o_win[rows])


def _nsa_cols(proj, a, b):
    return proj[..., RWKV_COLS + a:RWKV_COLS + b]


def _nsa_prompt(proj, cmp_w):
    B, T, _ = proj.shape
    KVH, G, HD = NSA_KV_HEADS, NSA_GROUP, HEAD_DIM
    kvc = NSA_KV_COLS
    tq = NSA_TQ
    q = _nsa_cols(proj, 0, D_NSA).reshape(B, T, KVH, G, HD).transpose(0, 2, 3, 1, 4)
    kv = [_nsa_cols(proj, D_NSA + i * kvc, D_NSA + (i + 1) * kvc).reshape(B, T, KVH, HD).transpose(0, 2, 1, 3)
          for i in range(6)]
    kc, vc = kv[0], kv[1]
    ks, vs, kw, vw = [t.astype(jnp.bfloat16) for t in kv[2:]]
    gl = _nsa_cols(proj, D_NSA + 6 * kvc, NSA_COLS).reshape(B, T, KVH, 3 * G).transpose(0, 2, 1, 3)
    cw = jnp.broadcast_to(cmp_w[:, :, None], (2, CMP_BLOCK, HD))
    expand = (jnp.arange(LANE)[:, None] == (jnp.arange(T) // SEL_BLOCK)[None, :]).astype(jnp.bfloat16)
    kv_spec = pl.BlockSpec((None, None, T, HD), lambda b, k, t: (b, k, 0, 0))
    out = pl.pallas_call(
        functools.partial(_nsa_prompt_kernel, seq=T),
        grid=(B, KVH, T // tq),
        in_specs=[pl.BlockSpec((None, None, G, tq, HD), lambda b, k, t: (b, k, 0, t, 0)),
                  pl.BlockSpec((None, None, tq, 3 * G), lambda b, k, t: (b, k, t, 0)),
                  kv_spec, kv_spec, kv_spec, kv_spec, kv_spec, kv_spec,
                  pl.BlockSpec((2, CMP_BLOCK, HD), lambda b, k, t: (0, 0, 0)),
                  pl.BlockSpec((LANE, T), lambda b, k, t: (0, 0))],
        out_specs=pl.BlockSpec((None, None, G, tq, HD), lambda b, k, t: (b, k, 0, t, 0)),
        out_shape=jax.ShapeDtypeStruct((B, KVH, G, T, HD), jnp.float32),
        scratch_shapes=[pltpu.VMEM((LANE, HD), jnp.float32),
                        pltpu.VMEM((LANE, HD), jnp.bfloat16),
                        pltpu.VMEM((tq // LANE, LANE, LANE), jnp.float32),
                        pltpu.VMEM((tq, T), jnp.float32)],
        compiler_params=pltpu.CompilerParams(dimension_semantics=("arbitrary", "arbitrary", "arbitrary"),
                                             vmem_limit_bytes=_VMEM_LIMIT),
        name="nsa_prompt",
    )(q, gl, kc, vc, ks, vs, kw, vw, cw, expand)
    return out.transpose(0, 3, 1, 2, 4).reshape(B, T, D_NSA)


def _nsa_prompt_mix(proj, n_keep, cmp_w):
    B, T, _ = proj.shape
    kv5 = lambda i: _nsa_cols(proj, D_NSA + 2 * i * NSA_KV_COLS, D_NSA + 2 * (i + 1) * NSA_KV_COLS).reshape(
        B, T, 2, NSA_KV_HEADS, HEAD_DIM)
    return _nsa_prompt(proj, cmp_w), kv5(0), kv5(1), kv5(2)[:, -n_keep:]


NSA_PG = 4
NSA_POOL_PG = LANE * CMP_BLOCK // PAGE_SIZE
PAGE_COLS = 2 * NSA_KV_COLS
NSA_ROWS = NSA_KV_HEADS * NSA_GROUP


def _page_specs(layer, n):
    def spec(j):
        return pl.BlockSpec((None, None, 2, NSA_KV_HEADS, HEAD_DIM, PAGE_SIZE),
                            lambda b, s, pt: (layer, pt[b, s * n + j], 0, 0, 0, 0))
    return [spec(j) for j in range(n)]


def _rows_last(t):
    nd = t.ndim
    return jnp.transpose(t, tuple(range(nd - 4)) + (nd - 3, nd - 2, nd - 1, nd - 4))


def _cmp_pool_kernel(pt_ref, *refs):
    pages, cw_ref, o_ref = refs[:NSA_POOL_PG], refs[NSA_POOL_PG], refs[NSA_POOL_PG + 1]
    per_page = PAGE_SIZE // CMP_BLOCK
    row_blk = lax.broadcasted_iota(jnp.int32, (PAGE_SIZE, LANE), 0) // CMP_BLOCK
    col = lax.broadcasted_iota(jnp.int32, (PAGE_SIZE, LANE), 1)
    acc = [jnp.zeros((NSA_KV_COLS, LANE), jnp.float32) for _ in range(2)]
    for j, pg in enumerate(pages):
        seg = jnp.where(col == j * per_page + row_blk, 1.0, 0.0).astype(jnp.bfloat16)
        for kv in range(2):
            xw = pg[kv].reshape(NSA_KV_COLS, PAGE_SIZE) * cw_ref[kv:kv + 1, :]
            hi = xw.astype(jnp.bfloat16)
            lo = (xw - hi.astype(jnp.float32)).astype(jnp.bfloat16)
            acc[kv] = (acc[kv] + jnp.dot(hi, seg, preferred_element_type=jnp.float32)
                       + jnp.dot(lo, seg, preferred_element_type=jnp.float32))
    o_ref[0:NSA_KV_COLS, :] = acc[0]
    o_ref[NSA_KV_COLS:, :] = acc[1]


def _cmp_pool(cache, page_table, layer, cw):
    B, n_pages = page_table.shape
    return pl.pallas_call(
        _cmp_pool_kernel,
        grid_spec=pltpu.PrefetchScalarGridSpec(
            num_scalar_prefetch=1, grid=(B, n_pages // NSA_POOL_PG),
            in_specs=_page_specs(layer, NSA_POOL_PG) + [pl.BlockSpec((2, PAGE_SIZE), lambda b, s, pt: (0, 0))],
            out_specs=pl.BlockSpec((None, PAGE_COLS, LANE), lambda b, s, pt: (b, 0, s))),
        out_shape=jax.ShapeDtypeStruct((B, PAGE_COLS, n_pages * PAGE_SIZE // CMP_BLOCK), jnp.float32),
        compiler_params=pltpu.CompilerParams(dimension_semantics=("arbitrary", "arbitrary"),
                                             vmem_limit_bytes=_VMEM_LIMIT),
        name="nsa_cmp_pool",
    )(page_table, *([cache] * NSA_POOL_PG), cw)


def _cmp_select_kernel(q_ref, kv_ref, ocmp_ref, sel_ref, pt_scr, imp_scr, *, past, steps):
    T = steps
    R = NSA_ROWS * T
    n_cmp = past // CMP_BLOCK
    n_sel = past // SEL_BLOCK + 1
    n_lane = sel_ref.shape[-1]
    ratio = SEL_BLOCK // CMP_BLOCK
    rows_t = lax.broadcasted_iota(jnp.int32, (R, 1), 0) % T
    qpos = past + rows_t
    cmp_end = (lax.broadcasted_iota(jnp.int32, (1, n_cmp), 1) + 1) * CMP_BLOCK - 1
    cmask = cmp_end <= qpos
    q = q_ref[...]
    p_all, o_all = [], []
    for kh in range(NSA_KV_HEADS):
        rows = slice(kh * NSA_GROUP * T, (kh + 1) * NSA_GROUP * T)
        kcb_t = kv_ref[kh * HEAD_DIM:(kh + 1) * HEAD_DIM, :]
        vcb_t = kv_ref[NSA_KV_COLS + kh * HEAD_DIM:NSA_KV_COLS + (kh + 1) * HEAD_DIM, :].astype(jnp.bfloat16)
        s = jnp.dot(q[rows], kcb_t, precision=lax.Precision.HIGHEST, preferred_element_type=jnp.float32)
        s = jnp.where(cmask[rows], s, NEG_INF)
        e = jnp.exp(s - jnp.max(s, axis=-1, keepdims=True))
        p = jnp.where(cmask[rows], e / jnp.sum(e, axis=-1, keepdims=True), 0.0)
        o_all.append(lax.dot_general(p.astype(jnp.bfloat16), vcb_t, (((1,), (1,)), ((), ())),
                                     preferred_element_type=jnp.float32))
        imp = p[0:T]
        for g in range(1, NSA_GROUP):
            imp = imp + p[g * T:(g + 1) * T]
        p_all.extend([imp] * NSA_GROUP)
    ocmp_ref[...] = jnp.concatenate(o_all, axis=0)
    imp_rows = jnp.concatenate(p_all + [jnp.zeros((LANE - R, n_cmp), jnp.float32)], axis=0)
    for c in range(n_cmp // LANE):
        pt_scr[c] = imp_rows[:, c * LANE:(c + 1) * LANE].T
    per = LANE // ratio
    for c in range(n_cmp // LANE):
        part = pt_scr[c, pl.ds(0, per, stride=ratio), :]
        for r in range(1, ratio):
            part = part + pt_scr[c, pl.ds(r, per, stride=ratio), :]
        imp_scr[c * per:(c + 1) * per, :] = part
    n_rows = imp_scr.shape[0]
    imp_scr[n_sel - 1:n_rows, :] = jnp.zeros((n_rows - n_sel + 1, LANE), jnp.float32)
    lane_t = lax.broadcasted_iota(jnp.int32, (1, LANE), 1) % T
    cur = (past + lane_t) // SEL_BLOCK
    blk = lax.broadcasted_iota(jnp.int32, (n_rows, LANE), 0)
    forced = (blk == 0) | (blk == cur) | (blk == cur - 1)
    imp_s = jnp.where(blk > cur, -1.0, jnp.where(forced, FORCE_SCORE, imp_scr[...]))
    imp_s = jnp.where(blk < n_sel, imp_s, -2.0)
    imp_scr[...] = imp_s

    def rank_body(m, rank):
        row = imp_scr[pl.ds(m, 1), :]
        ahead = (row > imp_s) | ((row == imp_s) & (m < blk))
        return rank + jnp.where(ahead, 1, 0)

    rank = lax.fori_loop(0, n_sel, rank_body, jnp.zeros((n_rows, LANE), jnp.int32))
    sel = jnp.where((rank < min(SEL_TOPN, n_sel)) & (imp_s >= 0.0), 1.0, 0.0)
    sel = jnp.concatenate([sel, jnp.zeros((n_lane - n_rows, LANE), jnp.float32)], axis=0)
    for c in range(n_lane // LANE):
        sel_ref[:, c * LANE:(c + 1) * LANE] = sel[c * LANE:(c + 1) * LANE, :].T


def _cmp_select(q_rows, kvcb, past, steps):
    B, R, _ = q_rows.shape
    n_cmp = kvcb.shape[2]
    n_sel = past // SEL_BLOCK + 1
    n_lane = _round_up(n_sel, LANE)
    n_rows = _round_up(n_sel, 8)
    return pl.pallas_call(
        functools.partial(_cmp_select_kernel, past=past, steps=steps),
        grid=(B,),
        in_specs=[pl.BlockSpec((None, R, HEAD_DIM), lambda b: (b, 0, 0)),
                  pl.BlockSpec((None, PAGE_COLS, n_cmp), lambda b: (b, 0, 0))],
        out_specs=[pl.BlockSpec((None, R, HEAD_DIM), lambda b: (b, 0, 0)),
                   pl.BlockSpec((None, LANE, n_lane), lambda b: (b, 0, 0))],
        out_shape=[jax.ShapeDtypeStruct((B, R, HEAD_DIM), jnp.float32),
                   jax.ShapeDtypeStruct((B, LANE, n_lane), jnp.float32)],
        scratch_shapes=[pltpu.VMEM((n_cmp // LANE, LANE, LANE), jnp.float32),
                        pltpu.VMEM((n_rows, LANE), jnp.float32)],
        compiler_params=pltpu.CompilerParams(dimension_semantics=("arbitrary",),
                                             vmem_limit_bytes=_VMEM_LIMIT),
        name="nsa_cmp_select",
    )(q_rows, kvcb)


def _soft_update(s, mask, v, m, l, acc, keys_last=False):
    s = jnp.where(mask, s, NEG_INF)
    m_new = jnp.maximum(m, jnp.max(s, axis=-1, keepdims=True))
    p = jnp.where(mask, jnp.exp(s - m_new), 0.0)
    alpha = jnp.exp(m - m_new)
    pv = lax.dot_general(p.astype(jnp.bfloat16), v, (((1,), (1 if keys_last else 0,)), ((), ())),
                         preferred_element_type=jnp.float32)
    return m_new, alpha * l + jnp.sum(p, axis=-1, keepdims=True), alpha * acc + pv


def _sel_win_kernel(pt_ref, *refs, past, steps):
    pages = refs[:NSA_PG]
    (q_ref, sel_ref, ocmp_ref, nsel_ref, wpre_ref, nwin_ref, gate_ref,
     o_ref, m_scr, l_scr, acc_scr) = refs[NSA_PG:]
    s_id = pl.program_id(1)
    T = steps
    GT = NSA_GROUP * T
    R = NSA_ROWS * T
    n_lane = sel_ref.shape[-1]
    rows_t = lax.broadcasted_iota(jnp.int32, (GT, 1), 0) % T

    @pl.when(s_id == 0)
    def _():
        m_scr[...] = jnp.full_like(m_scr, NEG_INF)
        l_scr[...] = jnp.zeros_like(l_scr)
        acc_scr[...] = jnp.zeros_like(acc_scr)

    q = q_ref[...].astype(jnp.bfloat16)
    sel_bf = sel_ref[0:R, :].astype(jnp.bfloat16)
    n_keys = NSA_PG * PAGE_SIZE
    blk_id = lax.broadcasted_iota(jnp.int32, (n_lane, n_keys), 0)
    key_blk = s_id * (n_keys // SEL_BLOCK) + lax.broadcasted_iota(jnp.int32, (n_lane, n_keys), 1) // SEL_BLOCK

    def head_cols(x, kh, off):
        return x[:, off + kh * HEAD_DIM:off + (kh + 1) * HEAD_DIM]

    def keys_t(kv, kh):
        return jnp.concatenate([pg[kv, kh] for pg in pages], axis=1).astype(jnp.bfloat16)

    expand = jnp.where(blk_id == key_blk, 1.0, 0.0).astype(jnp.bfloat16)
    chosen = jnp.dot(sel_bf, expand, preferred_element_type=jnp.float32) > 0.5
    head_rows = [slice(kh * GT, (kh + 1) * GT) for kh in range(NSA_KV_HEADS)]
    scores = [jnp.dot(q[rows], keys_t(0, kh), preferred_element_type=jnp.float32)
              for kh, rows in enumerate(head_rows)]
    upd = [_soft_update(scores[kh], chosen[rows], keys_t(1, kh), m_scr[rows], l_scr[rows], acc_scr[rows],
                        keys_last=True) for kh, rows in enumerate(head_rows)]
    for kh, rows in enumerate(head_rows):
        m_scr[rows], l_scr[rows], acc_scr[rows] = upd[kh]

    @pl.when(s_id == pl.num_programs(1) - 1)
    def _():
        new_blk = past // SEL_BLOCK
        tpad = nsel_ref.shape[0]
        jn = lax.broadcasted_iota(jnp.int32, (1, tpad), 1)
        nsel = nsel_ref[...].astype(jnp.bfloat16)
        nwin = nwin_ref[...].astype(jnp.bfloat16)
        jp = lax.broadcasted_iota(jnp.int32, (1, WINDOW), 1)
        gates = jax.nn.sigmoid(gate_ref[...])
        for kh in range(NSA_KV_HEADS):
            rows = slice(kh * GT, (kh + 1) * GT)
            pick = sel_ref[kh * GT:(kh + 1) * GT, new_blk:new_blk + 1] > 0.5
            s = lax.dot_general(q[rows], head_cols(nsel, kh, 0), (((1,), (1,)), ((), ())),
                                preferred_element_type=jnp.float32)
            m, l, acc = _soft_update(s, pick & (jn <= rows_t) & (jn < T), head_cols(nsel, kh, NSA_KV_COLS),
                                     m_scr[rows], l_scr[rows], acc_scr[rows])
            o_sel = acc / l
            init = (jnp.full((GT, 1), NEG_INF, jnp.float32), jnp.zeros((GT, 1), jnp.float32),
                    jnp.zeros((GT, HEAD_DIM), jnp.float32))
            s = jnp.dot(q[rows], wpre_ref[0, kh].astype(jnp.bfloat16), preferred_element_type=jnp.float32)
            st = _soft_update(s, jp > rows_t, wpre_ref[1, kh].astype(jnp.bfloat16), *init, keys_last=True)
            s = lax.dot_general(q[rows], head_cols(nwin, kh, 0), (((1,), (1,)), ((), ())),
                                preferred_element_type=jnp.float32)
            _, l_w, acc_w = _soft_update(s, (jn <= rows_t) & (jn < T), head_cols(nwin, kh, NSA_KV_COLS), *st)
            o_win = acc_w / l_w
            gt = gates[rows]
            o_ref[rows, :] = gt[:, 0:1] * ocmp_ref[rows, :] + gt[:, 1:2] * o_sel + gt[:, 2:3] * o_win


def _sel_win(cache, page_table, layer, q_rows, sel_rows, o_cmp, new_sel, win_prefix, new_win, gate_rows,
             past, steps):
    B, n_pages = page_table.shape
    R = q_rows.shape[1]
    n_lane = sel_rows.shape[-1]
    tpad = new_sel.shape[1]
    per_b = lambda shape: pl.BlockSpec((None,) + shape, lambda b, s, pt: (b,) + (0,) * len(shape))
    return pl.pallas_call(
        functools.partial(_sel_win_kernel, past=past, steps=steps),
        grid_spec=pltpu.PrefetchScalarGridSpec(
            num_scalar_prefetch=1, grid=(B, n_pages // NSA_PG),
            in_specs=_page_specs(layer, NSA_PG) + [
                per_b((R, HEAD_DIM)), per_b((LANE, n_lane)), per_b((R, HEAD_DIM)),
                per_b((tpad, PAGE_COLS)), per_b((2, NSA_KV_HEADS, HEAD_DIM, WINDOW)), per_b((tpad, PAGE_COLS)),
                per_b((R, 3))],
            out_specs=per_b((R, HEAD_DIM)),
            scratch_shapes=[pltpu.VMEM((R, 1), jnp.float32), pltpu.VMEM((R, 1), jnp.float32),
                            pltpu.VMEM((R, HEAD_DIM), jnp.float32)]),
        out_shape=jax.ShapeDtypeStruct((B, R, HEAD_DIM), jnp.float32),
        compiler_params=pltpu.CompilerParams(dimension_semantics=("arbitrary", "arbitrary"),
                                             vmem_limit_bytes=_VMEM_LIMIT),
        name="nsa_sel_win",
    )(page_table, *([cache] * NSA_PG), q_rows, sel_rows, o_cmp, new_sel, win_prefix, new_win, gate_rows)


def _nsa_decode_mix(proj, cache_cmp, cache_sel, page_table, layer, win_prefix, cmp_w):
    B, T, _ = proj.shape
    xn = proj[..., RWKV_COLS:IN_COLS]
    n_pages = page_table.shape[1]
    past = n_pages * PAGE_SIZE
    assert T <= SEL_BLOCK and n_pages % NSA_POOL_PG == 0 and win_prefix.shape[1] == WINDOW
    KVH, G, HD = NSA_KV_HEADS, NSA_GROUP, HEAD_DIM
    kv5 = lambda i: xn[..., D_NSA + 2 * i * NSA_KV_COLS:D_NSA + 2 * (i + 1) * NSA_KV_COLS]
    new_cmp, new_sel, new_win = kv5(0), kv5(1), kv5(2)
    q_rows = (xn[..., :D_NSA] * (HD ** -0.5)).reshape(B, T, KVH, G, HD).transpose(0, 2, 3, 1, 4)
    q_rows = q_rows.reshape(B, KVH * G * T, HD)
    gate_rows = xn[..., D_NSA + 6 * NSA_KV_COLS:].reshape(B, T, KVH, G, 3).transpose(0, 2, 3, 1, 4)
    gate_rows = gate_rows.reshape(B, KVH * G * T, 3)
    cw = jnp.tile(cmp_w, (1, PAGE_SIZE // CMP_BLOCK))
    kvcb = _cmp_pool(_rows_last(cache_cmp), page_table, layer, cw)
    o_cmp, sel_rows = _cmp_select(q_rows, kvcb, past, T)
    tpad = _round_up(T, 8)
    padt = lambda t: jnp.pad(t, ((0, 0), (0, tpad - T), (0, 0)))
    out = _sel_win(_rows_last(cache_sel), page_table, layer, q_rows, sel_rows, o_cmp, padt(new_sel),
                   _rows_last(win_prefix), padt(new_win), gate_rows, past, T)
    y = out.reshape(B, KVH, G, T, HD).transpose(0, 3, 1, 2, 4).reshape(B, T, D_NSA)
    five = lambda t: t.reshape(B, -1, 2, KVH, HD)
    win_out = jnp.concatenate([win_prefix[:, T:], five(new_win)], axis=1)
    return y, five(new_cmp), five(new_sel), win_out


def _out_proj_kernel(yr_ref, yn_ref, x_ref, w_ref, g_ref, b_ref, o_ref):
    half = yr_ref.shape[1]
    h = (jnp.dot(yr_ref[...].astype(jnp.bfloat16), w_ref[0:half, :], preferred_element_type=jnp.float32)
         + jnp.dot(yn_ref[...].astype(jnp.bfloat16), w_ref[half:, :], preferred_element_type=jnp.float32))
    z = DEEPNORM_ALPHA * x_ref[...] + h
    mu = jnp.mean(z, axis=-1, keepdims=True)
    dev = z - mu
    var = jnp.mean(dev * dev, axis=-1, keepdims=True)
    o_ref[...] = dev * lax.rsqrt(var + LN_EPS) * g_ref[...] + b_ref[...]


def _out_proj_norm(y_r, y_n, x, w_bf, g, b, *, tm=256):
    n, d = x.shape
    tm = min(tm, n)
    half = y_r.shape[1]
    rows = lambda w: pl.BlockSpec((tm, w), lambda i: (i, 0))
    full = lambda shape: pl.BlockSpec(shape, lambda i: (0, 0))
    return pl.pallas_call(
        _out_proj_kernel,
        grid=(n // tm,),
        in_specs=[rows(half), rows(y_n.shape[1]), rows(d), full(w_bf.shape), full((1, d)), full((1, d))],
        out_specs=rows(d),
        out_shape=jax.ShapeDtypeStruct((n, d), jnp.float32),
        compiler_params=pltpu.CompilerParams(dimension_semantics=("arbitrary",),
                                             vmem_limit_bytes=_VMEM_LIMIT),
        name="out_proj_norm",
    )(y_r, y_n, x, w_bf, g.reshape(1, d), b.reshape(1, d))


def _hybrid_layer(x, past, shift_prev, rwkv_s0, n_keep, p):
    B, T, D = x.shape
    proj = _matmul(x.reshape(B * T, D), p['w_in_bf']).reshape(B, T, IN_COLS_PAD)
    y_r, s_T, new_shift = _rwkv_time_mix(proj, shift_prev, rwkv_s0, p)
    if past is None:
        assert T % NSA_TQ == 0 and T >= n_keep
        y_n, new_cmp, new_sel, new_win = _nsa_prompt_mix(proj, n_keep, p['nsa_cmp_w'])
    else:
        assert n_keep == WINDOW
        y_n, new_cmp, new_sel, new_win = _nsa_decode_mix(proj, *past, p['nsa_cmp_w'])
    x = _out_proj_norm(y_r.reshape(B * T, D_RWKV), y_n.reshape(B * T, D_NSA), x.reshape(B * T, D),
                       p['w_out_bf'], p['ln1_g'], p['ln1_b']).reshape(B, T, D)
    f = _peer_ffn(x, p['peer_wq_bf'], p['peer_sk_bf'], p['peer_u_bf'], p['peer_v_bf'])
    x = _layer_norm(DEEPNORM_ALPHA * x + f, p['ln2_g'], p['ln2_b'])
    return x, (new_cmp, new_sel, new_win, s_T, new_shift)


def kernel(x_prompt, x_sample, cache_cmp_kv, cache_sel_kv, page_table, state_win_kv, state_rwkv,
           state_shift, w_in, rwkv_mu, rwkv_w0, rwkv_w2, rwkv_a0, rwkv_a2, rwkv_g2, rwkv_k_k,
           rwkv_k_a, rwkv_r_k, rwkv_gn_g, rwkv_gn_b, nsa_cmp_w, w_out, ln1_g, ln1_b, peer_wq,
           peer_subkeys, peer_u, peer_v, ln2_g, ln2_b):
    bp = x_prompt.shape[0]
    dt = x_prompt.dtype
    assert page_table.shape[1] * PAGE_SIZE == PAST_LEN
    n_keep = state_win_kv.shape[2]
    zero_shift = jnp.zeros((bp, 1, RWKV_COLS), dt)
    zero_state = jnp.zeros((bp, H_RWKV, HEAD_DIM, HEAD_DIM), dt)
    yp, ys = x_prompt, x_sample
    st_p, st_s = [], []
    for l in range(DEPTH):
        p = {'w_in_bf': jnp.pad(w_in[l].astype(jnp.bfloat16), ((0, 0), (0, IN_COLS_PAD - IN_COLS))),
             'rwkv_mu': rwkv_mu[l], 'rwkv_w0': rwkv_w0[l], 'rwkv_w2': rwkv_w2[l],
             'rwkv_a0': rwkv_a0[l], 'rwkv_a2': rwkv_a2[l], 'rwkv_g2': rwkv_g2[l],
             'rwkv_k_k': rwkv_k_k[l], 'rwkv_k_a': rwkv_k_a[l], 'rwkv_r_k': rwkv_r_k[l],
             'rwkv_gn_g': rwkv_gn_g[l], 'rwkv_gn_b': rwkv_gn_b[l], 'nsa_cmp_w': nsa_cmp_w[l],
             'w_out_bf': w_out[l].astype(jnp.bfloat16), 'ln1_g': ln1_g[l], 'ln1_b': ln1_b[l],
             'peer_wq_bf': peer_wq[l].astype(jnp.bfloat16),
             'peer_sk_bf': peer_subkeys[l].reshape(2 * P_HEADS, N_KEYS, P_DKEY // 2).astype(jnp.bfloat16),
             'peer_u_bf': peer_u[l].astype(jnp.bfloat16).T, 'peer_v_bf': peer_v[l].astype(jnp.bfloat16),
             'ln2_g': ln2_g[l], 'ln2_b': ln2_b[l]}
        yp, sp = _hybrid_layer(yp, None, zero_shift, zero_state, n_keep, p)
        past = (cache_cmp_kv, cache_sel_kv, page_table, l, state_win_kv[l])
        ys, ss = _hybrid_layer(ys, past, state_shift[l], state_rwkv[l], n_keep, p)
        st_p.append(sp)
        st_s.append(ss)
    stk = lambda sts, i: jnp.stack([s[i] for s in sts], axis=0)
    return (yp, ys, stk(st_p, 0), stk(st_p, 1), stk(st_p, 2), stk(st_p, 3), stk(st_p, 4),
            stk(st_s, 0), stk(st_s, 1), stk(st_s, 2), stk(st_s, 3), stk(st_s, 4))
```

```python
import functools
import math

import jax
import jax.numpy as jnp
from jax import lax
from jax.experimental import pallas as pl
from jax.experimental.pallas import tpu as pltpu

D_MODEL = 2048
DEPTH = 2
PAST_LEN = 16384
PAGE_SIZE = 128
HEAD_DIM = 64
D_RWKV = D_MODEL // 2
D_NSA = D_MODEL - D_RWKV
H_RWKV = D_RWKV // HEAD_DIM
H_NSA = D_NSA // HEAD_DIM
NSA_KV_HEADS = 4
NSA_GROUP = H_NSA // NSA_KV_HEADS
NSA_KV_COLS = NSA_KV_HEADS * HEAD_DIM
CMP_BLOCK = 32
SEL_BLOCK = 64
SEL_TOPN = 16
WINDOW = 512
W_LORA = 64
A_LORA = 64
G_LORA = 160
RWKV_COLS = 3 * D_RWKV + W_LORA + A_LORA + G_LORA
NSA_COLS = D_NSA + 6 * NSA_KV_COLS + 3 * H_NSA
IN_COLS = RWKV_COLS + NSA_COLS
P_HEADS = 8
N_KEYS = 128
P_DKEY = 256
P_TOPK = 16
LN_EPS = 1e-5
GN_EPS = 64e-5
DEEPNORM_ALPHA = (2 * DEPTH) ** 0.25
FORCE_SCORE = 1e4
NEG_INF = -1e30

N_EXPERTS = N_KEYS * N_KEYS

LANE = 128
SUBLANE = 8
_VMEM_LIMIT = 56 * 1024 * 1024


def _round_up(x, m):
    return -(-x // m) * m


PROJ_TILE = 1024
IN_COLS_PAD = _round_up(IN_COLS, PROJ_TILE)


def _mm_kernel(x_ref, w_ref, o_ref):
    o_ref[...] = jnp.dot(x_ref[...].astype(jnp.bfloat16), w_ref[...], preferred_element_type=jnp.float32)


def _matmul(x, w_bf):
    m, k = x.shape
    n = w_bf.shape[1]
    tm = min(PROJ_TILE, m)
    tn = PROJ_TILE
    return pl.pallas_call(
        _mm_kernel,
        grid=(m // tm, n // tn),
        in_specs=[pl.BlockSpec((tm, k), lambda i, j: (i, 0)),
                  pl.BlockSpec((k, tn), lambda i, j: (0, j))],
        out_specs=pl.BlockSpec((tm, tn), lambda i, j: (i, j)),
        out_shape=jax.ShapeDtypeStruct((m, n), jnp.float32),
        compiler_params=pltpu.CompilerParams(dimension_semantics=("arbitrary", "arbitrary"),
                                             vmem_limit_bytes=_VMEM_LIMIT),
        name="in_proj",
    )(x, w_bf)


RW_CHUNK = 64
RW_PAIRS = D_RWKV // LANE


def _bdot(a, b):
    return jnp.dot(a.astype(jnp.bfloat16), b.astype(jnp.bfloat16), preferred_element_type=jnp.float32)


def _bdot_nt(a, b):
    return lax.dot_general(a.astype(jnp.bfloat16), b.astype(jnp.bfloat16), (((1,), (1,)), ((), ())),
                           preferred_element_type=jnp.float32)


def _bdot_tn(a, b):
    return lax.dot_general(a.astype(jnp.bfloat16), b.astype(jnp.bfloat16), (((0,), (0,)), ((), ())),
                           preferred_element_type=jnp.float32)


def _head_sum(x, same_head_bf):
    hi = x.astype(jnp.bfloat16)
    lo = (x - hi.astype(jnp.float32)).astype(jnp.bfloat16)
    return (jnp.dot(hi, same_head_bf, preferred_element_type=jnp.float32)
            + jnp.dot(lo, same_head_bf, preferred_element_type=jnp.float32))


def _rwkv_mix_kernel(x_ref, shift_ref, mu_ref, pv_ref, w2_ref, a2_ref, g2_ref, s0_ref, y_ref, st_ref,
                     prev_scr, *, steps):
    c = pl.program_id(1)
    C = RW_CHUNK
    f32 = jnp.float32

    @pl.when(c == 0)
    def _():
        st_ref[...] = s0_ref[...]
        prev_scr[...] = shift_ref[...]

    xr = x_ref[:, 0:RWKV_COLS]
    row_c = lax.broadcasted_iota(jnp.int32, (C, 1), 0)
    prev = jnp.where(row_c == 0, prev_scr[...], pltpu.roll(xr, 1, 0))
    prev_scr[...] = xr[C - 1:C, :]
    xs = xr + mu_ref[...] * (prev - xr)
    o_wl = 3 * D_RWKV
    r_all, k_all, v_all = xs[:, 0:D_RWKV], xs[:, D_RWKV:2 * D_RWKV], xs[:, 2 * D_RWKV:o_wl]
    wl = xs[:, o_wl:o_wl + W_LORA]
    al = xs[:, o_wl + W_LORA:o_wl + W_LORA + A_LORA]
    gl = xs[:, o_wl + W_LORA + A_LORA:RWKV_COLS]
    w0, a0_, k_k, k_a = pv_ref[0:1, :], pv_ref[1:2, :], pv_ref[2:3, :], pv_ref[3:4, :]
    r_k, gn_g, gn_b = pv_ref[4:5, :], pv_ref[5:6, :], pv_ref[6:7, :]
    w_all = -jax.nn.softplus(-(w0 + _bdot(jnp.tanh(wl), w2_ref[...]))) - 0.5
    lw_all = -jnp.exp(w_all)
    a_all = jax.nn.sigmoid(a0_ + _bdot(al, a2_ref[...]))
    gate_all = _bdot(jax.nn.sigmoid(gl), g2_ref[...])
    kkf_all = k_all * k_k
    k2_all = k_all * (1.0 + (a_all - 1.0) * k_a)
    if steps % C:
        live = (c * C + row_c) < steps
        lw_all = jnp.where(live, lw_all, 0.0)
        kkf_all = jnp.where(live, kkf_all, 0.0)
        k2_all = jnp.where(live, k2_all, 0.0)
        v_all = jnp.where(live, v_all, 0.0)

    row = lax.broadcasted_iota(jnp.int32, (C, C), 0)
    col = lax.broadcasted_iota(jnp.int32, (C, C), 1)
    tri_incl = col <= row
    tri_strict = col < row
    tri_f = jnp.where(tri_incl, 1.0, 0.0).astype(f32)
    eye = jnp.where(row == col, 1.0, 0.0).astype(f32)
    lane = lax.broadcasted_iota(jnp.int32, (1, LANE), 1)
    head0 = lane < HEAD_DIM
    r128 = lax.broadcasted_iota(jnp.int32, (LANE, LANE), 0)
    c128 = lax.broadcasted_iota(jnp.int32, (LANE, LANE), 1)
    same_head = (r128 < HEAD_DIM) == (c128 < HEAD_DIM)
    same_head_bf = jnp.where(same_head, 1.0, 0.0).astype(jnp.bfloat16)
    eye128 = r128 == c128

    pairs = range(RW_PAIRS)
    heads = (head0, ~head0)
    sls = [slice(pr * LANE, (pr + 1) * LANE) for pr in pairs]
    lw = [lw_all[:, sl] for sl in sls]
    cum = [jnp.dot(tri_f, lw[pr], precision=lax.Precision.HIGHEST, preferred_element_type=f32)
           for pr in pairs]
    tot = [cum[pr][C - 1:C, :] for pr in pairs]
    v = [v_all[:, sl] for sl in sls]
    k = [k2_all[:, sl] for sl in sls]
    r = [r_all[:, sl] for sl in sls]
    kkf = [kkf_all[:, sl] for sl in sls]
    ss = [_head_sum(kkf[pr] * kkf[pr], same_head_bf) for pr in pairs]
    kk = [kkf[pr] * lax.rsqrt(jnp.maximum(ss[pr], 1e-24)) for pr in pairs]
    b = [kk[pr] * a_all[:, sls[pr]] for pr in pairs]
    x, ym = [], []
    for pr in pairs:
        e_neg = jnp.exp(-cum[pr])
        x.append(jnp.concatenate([kk[pr] * jnp.exp(cum[pr] - lw[pr]),
                                  r[pr] * jnp.exp(cum[pr])], axis=0))
        ym.append(jnp.concatenate([k[pr] * e_neg, b[pr] * e_neg], axis=0))
    a0 = [st_ref[pr] for pr in pairs]
    xa = [_bdot(x[pr], a0[pr]) for pr in pairs]
    g = [[_bdot_nt(jnp.where(hm, x[pr], 0.0), ym[pr]) for hm in heads] for pr in pairs]
    lkv = [[_bdot(jnp.where(tri_strict, g[pr][h][:C, :C], 0.0), v[pr]) for h in range(2)] for pr in pairs]
    npow = [[jnp.where(tri_strict, -g[pr][h][:C, C:], 0.0) for h in range(2)] for pr in pairs]
    tmat = [[eye + npow[pr][h] for h in range(2)] for pr in pairs]
    for _ in range(int(math.log2(C)) - 1):
        npow = [[_bdot(npow[pr][h], npow[pr][h]) for h in range(2)] for pr in pairs]
        tmat = [[tmat[pr][h] + _bdot(tmat[pr][h], npow[pr][h]) for h in range(2)] for pr in pairs]
    rhs = [xa[pr][:C] + jnp.where(head0, lkv[pr][0], lkv[pr][1]) for pr in pairs]
    w = [jnp.where(head0, _bdot(tmat[pr][0], rhs[pr]), _bdot(tmat[pr][1], rhs[pr])) for pr in pairs]
    vw = [jnp.concatenate([v[pr], w[pr]], axis=0) for pr in pairs]
    y = []
    for pr in pairs:
        mr = [jnp.concatenate([jnp.where(tri_incl, g[pr][h][C:, :C], 0.0),
                               jnp.where(tri_incl, -g[pr][h][C:, C:], 0.0)], axis=1) for h in range(2)]
        y.append(xa[pr][C:] + jnp.where(head0, _bdot(mr[0], vw[pr]), _bdot(mr[1], vw[pr])))
    inv_hd = 1.0 / HEAD_DIM
    mean = [_head_sum(y[pr], same_head_bf) * inv_hd for pr in pairs]
    dev = [y[pr] - mean[pr] for pr in pairs]
    var = [_head_sum(dev[pr] * dev[pr], same_head_bf) * inv_hd for pr in pairs]
    rk = [_head_sum(r[pr] * k[pr] * r_k[:, sls[pr]], same_head_bf) for pr in pairs]
    for pr in pairs:
        yn = dev[pr] * lax.rsqrt(var[pr] + GN_EPS) * gn_g[:, sls[pr]] + gn_b[:, sls[pr]]
        y_ref[:, sls[pr]] = (yn + rk[pr] * v[pr]) * gate_all[:, sls[pr]]
    for pr in pairs:
        e_rem = jnp.exp(tot[pr] - cum[pr])
        kb = jnp.concatenate([k[pr] * e_rem, -(b[pr] * e_rem)], axis=0)
        upd = _bdot_tn(kb, vw[pr])
        p_col = jnp.sum(jnp.where(eye128, jnp.exp(tot[pr]), 0.0), axis=1, keepdims=True)
        st_ref[pr] = a0[pr] * p_col + jnp.where(same_head, upd, 0.0)


def _rwkv_time_mix(proj, shift_prev, s0, p):
    B, T, W = proj.shape
    C = RW_CHUNK
    t_pad = _round_up(T, C)
    x = proj if t_pad == T else jnp.pad(proj, ((0, 0), (0, t_pad - T), (0, 0)))
    a = jnp.swapaxes(s0, -1, -2).reshape(B, RW_PAIRS, 2, HEAD_DIM, HEAD_DIM)
    z = jnp.zeros_like(a[:, :, 0])
    a0 = jnp.concatenate([jnp.concatenate([a[:, :, 0], z], axis=-1),
                          jnp.concatenate([z, a[:, :, 1]], axis=-1)], axis=-2)
    zero = jnp.zeros((D_RWKV,), jnp.float32)
    pv = jnp.stack([p['rwkv_w0'], p['rwkv_a0'], p['rwkv_k_k'], p['rwkv_k_a'], p['rwkv_r_k'].reshape(-1),
                    p['rwkv_gn_g'], p['rwkv_gn_b'], zero])
    bf = lambda t: t.astype(jnp.bfloat16)
    full = lambda shape: pl.BlockSpec(shape, lambda bi, ci: (0,) * len(shape))
    st_spec = pl.BlockSpec((None, RW_PAIRS, LANE, LANE), lambda bi, ci: (bi, 0, 0, 0))
    y, st = pl.pallas_call(
        functools.partial(_rwkv_mix_kernel, steps=T),
        grid=(B, t_pad // C),
        in_specs=[pl.BlockSpec((None, C, W), lambda bi, ci: (bi, ci, 0)),
                  pl.BlockSpec((None, 1, RWKV_COLS), lambda bi, ci: (bi, 0, 0)),
                  full((1, RWKV_COLS)), full((8, D_RWKV)), full((W_LORA, D_RWKV)),
                  full((A_LORA, D_RWKV)), full((G_LORA, D_RWKV)), st_spec],
        out_specs=[pl.BlockSpec((None, C, D_RWKV), lambda bi, ci: (bi, ci, 0)), st_spec],
        out_shape=[jax.ShapeDtypeStruct((B, t_pad, D_RWKV), jnp.float32),
                   jax.ShapeDtypeStruct((B, RW_PAIRS, LANE, LANE), jnp.float32)],
        scratch_shapes=[pltpu.VMEM((1, RWKV_COLS), jnp.float32)],
        compiler_params=pltpu.CompilerParams(dimension_semantics=("arbitrary", "arbitrary"),
                                             vmem_limit_bytes=_VMEM_LIMIT),
        name="rwkv_mix",
    )(x, shift_prev, p['rwkv_mu'].reshape(1, RWKV_COLS), pv, bf(p['rwkv_w2']), bf(p['rwkv_a2']),
      bf(p['rwkv_g2']), a0)
    s_t = jnp.stack([st[:, :, :HEAD_DIM, :HEAD_DIM], st[:, :, HEAD_DIM:, HEAD_DIM:]], axis=2)
    s_t = jnp.swapaxes(s_t.reshape(B, H_RWKV, HEAD_DIM, HEAD_DIM), -1, -2)
    return y[:, :T], s_t, proj[:, T - 1:T, :RWKV_COLS]


def _extract_top(buf_ref, out_ref):
    n_slabs, nrows, width = buf_ref.shape
    iota = lax.broadcasted_iota(jnp.int32, (nrows, width), 0)

    def body(r, carry):
        for g in range(n_slabs):
            s = buf_ref[g]
            mx = jnp.max(s, axis=0, keepdims=True)
            first = jnp.min(jnp.where(s == mx, iota, nrows), axis=0, keepdims=True)
            buf_ref[g] = jnp.where(iota == first, -jnp.inf, s)
            out_ref[g, pl.ds(r, 1), :] = mx
        return carry

    lax.fori_loop(0, P_TOPK, body, 0)


def _peer_score_kernel(x_ref, wq_ref, sk_ref, s1_ref, s2_ref, st_ref, buf, cand, hv, tv):
    tn = x_ref.shape[0]
    q = jnp.dot(x_ref[...].astype(jnp.bfloat16), wq_ref[...],
                preferred_element_type=jnp.float32).astype(jnp.bfloat16)
    for h in range(P_HEADS):
        for c, s_ref in enumerate((s1_ref, s2_ref)):
            col = (2 * h + c) * N_KEYS
            s_t = lax.dot_general(sk_ref[2 * h + c], q[:, col:col + N_KEYS],
                                  (((1,), (1,)), ((), ())), preferred_element_type=jnp.float32)
            s_ref[h] = s_t
            buf[2 * h + c] = s_t
    _extract_top(buf, hv)
    n_cand = sum(P_TOPK // (a + 1) for a in range(P_TOPK))
    for h in range(P_HEADS):
        pieces = [hv[2 * h, a:a + 1, :] + hv[2 * h + 1, 0:P_TOPK // (a + 1), :] for a in range(P_TOPK)]
        pieces.append(jnp.full((cand.shape[1] - n_cand, tn), -jnp.inf, jnp.float32))
        cand[h] = jnp.concatenate(pieces, axis=0)
    _extract_top(cand, tv)
    for h in range(P_HEADS):
        z = jnp.sum(jnp.exp(tv[h] - tv[h, 0:1, :]), axis=0, keepdims=True)
        st_ref[0, h:h + 1, :] = tv[h, P_TOPK - 1:P_TOPK, :]
        st_ref[1, h:h + 1, :] = hv[2 * h, 0:1, :]
        st_ref[2, h:h + 1, :] = hv[2 * h + 1, 0:1, :]
        st_ref[3, h:h + 1, :] = 1.0 / z


def _peer_scores(x, wq_bf, sk_bf, *, tn=256):
    n = x.shape[0]
    tn = min(tn, n)
    return pl.pallas_call(
        _peer_score_kernel,
        grid=(n // tn,),
        in_specs=[pl.BlockSpec((tn, D_MODEL), lambda i: (i, 0)),
                  pl.BlockSpec((D_MODEL, P_HEADS * P_DKEY), lambda i: (0, 0)),
                  pl.BlockSpec((2 * P_HEADS, N_KEYS, P_DKEY // 2), lambda i: (0, 0, 0))],
        out_specs=[pl.BlockSpec((P_HEADS, N_KEYS, tn), lambda i: (0, 0, i)),
                   pl.BlockSpec((P_HEADS, N_KEYS, tn), lambda i: (0, 0, i)),
                   pl.BlockSpec((4, P_HEADS, tn), lambda i: (0, 0, i))],
        out_shape=[jax.ShapeDtypeStruct((P_HEADS, N_KEYS, n), jnp.float32),
                   jax.ShapeDtypeStruct((P_HEADS, N_KEYS, n), jnp.float32),
                   jax.ShapeDtypeStruct((4, P_HEADS, n), jnp.float32)],
        scratch_shapes=[pltpu.VMEM((2 * P_HEADS, N_KEYS, tn), jnp.float32),
                        pltpu.VMEM((P_HEADS, _round_up(sum(P_TOPK // (a + 1) for a in range(P_TOPK)), SUBLANE),
                                    tn),
                                   jnp.float32),
                        pltpu.VMEM((2 * P_HEADS, P_TOPK, tn), jnp.float32),
                        pltpu.VMEM((P_HEADS, P_TOPK, tn), jnp.float32)],
        compiler_params=pltpu.CompilerParams(dimension_semantics=("arbitrary",),
                                             vmem_limit_bytes=_VMEM_LIMIT),
        name="peer_scores",
    )(x, wq_bf, sk_bf)


def _peer_expert_kernel(x_ref, u_ref, v_ref, s1_ref, s2_ref, st_ref, o_ref, e2_scr, h_scr, p_scr,
                        *, chunk):
    j = pl.program_id(1)
    tn = x_ref.shape[0]
    te = u_ref.shape[1]

    @pl.when(j == 0)
    def _():
        o_ref[...] = jnp.zeros_like(o_ref)
        for h in range(P_HEADS):
            e2_scr[h] = jnp.exp(s2_ref[h] - st_ref[2, h:h + 1, :])

    def hidden(c):
        tsl = slice(c * chunk, (c + 1) * chunk)
        h_scr[tsl, :] = jnp.dot(x_ref[tsl, :], u_ref[...], preferred_element_type=jnp.float32)

    a_rows = [[s1_ref[h, pl.ds(j * (te // N_KEYS) + ii, 1), :] for h in range(P_HEADS)]
              for ii in range(te // N_KEYS)]
    e1_rows = [[jnp.exp(a_rows[ii][h] - st_ref[1, h:h + 1, :]) * st_ref[3, h:h + 1, :]
                for h in range(P_HEADS)] for ii in range(te // N_KEYS)]

    def weigh(c):
        for sub in range(chunk // LANE):
            weigh_lanes(slice(c * chunk + sub * LANE, c * chunk + (sub + 1) * LANE))

    def weigh_lanes(tsl):
        for ii in range(te // N_KEYS):
            w_t = jnp.zeros((N_KEYS, LANE), jnp.float32)
            for h in range(P_HEADS):
                val = a_rows[ii][h][:, tsl] + s2_ref[h, :, tsl]
                w_t = w_t + jnp.where(val >= st_ref[0, h:h + 1, tsl],
                                      e1_rows[ii][h][:, tsl] * e2_scr[h, :, tsl], 0.0)
            hh = h_scr[tsl, ii * N_KEYS:(ii + 1) * N_KEYS]
            g = 0.5 * hh * (1.0 + lax.erf(hh * 0.7071067811865476))
            p_scr[tsl, ii * N_KEYS:(ii + 1) * N_KEYS] = (w_t.T * g).astype(jnp.bfloat16)

    def project(c):
        tsl = slice(c * chunk, (c + 1) * chunk)
        o_ref[tsl, :] += jnp.dot(p_scr[tsl, :], v_ref[...], preferred_element_type=jnp.float32)

    n_chunks = tn // chunk
    hidden(0)
    for c in range(n_chunks):
        if c + 1 < n_chunks:
            hidden(c + 1)
        weigh(c)
        project(c)


def _peer_experts(x_bf, u_bf, v_bf, s1t, s2t, stats, *, tn=1024, te=512, chunk=256):
    n = x_bf.shape[0]
    tn = min(tn, n)
    chunk = min(chunk, tn)
    once = pl.Buffered(1)
    return pl.pallas_call(
        functools.partial(_peer_expert_kernel, chunk=chunk),
        grid=(n // tn, N_EXPERTS // te),
        in_specs=[pl.BlockSpec((tn, D_MODEL), lambda i, j: (i, 0), pipeline_mode=once),
                  pl.BlockSpec((D_MODEL, te), lambda i, j: (0, j)),
                  pl.BlockSpec((te, D_MODEL), lambda i, j: (j, 0)),
                  pl.BlockSpec((P_HEADS, N_KEYS, tn), lambda i, j: (0, 0, i), pipeline_mode=once),
                  pl.BlockSpec((P_HEADS, N_KEYS, tn), lambda i, j: (0, 0, i), pipeline_mode=once),
                  pl.BlockSpec((4, P_HEADS, tn), lambda i, j: (0, 0, i), pipeline_mode=once)],
        out_specs=pl.BlockSpec((tn, D_MODEL), lambda i, j: (i, 0)),
        out_shape=jax.ShapeDtypeStruct((n, D_MODEL), jnp.float32),
        scratch_shapes=[pltpu.VMEM((P_HEADS, N_KEYS, tn), jnp.float32),
                        pltpu.VMEM((tn, te), jnp.float32),
                        pltpu.VMEM((tn, te), jnp.bfloat16)],
        compiler_params=pltpu.CompilerParams(dimension_semantics=("arbitrary", "arbitrary"),
                                             vmem_limit_bytes=_VMEM_LIMIT),
        name="peer_experts",
    )(x_bf, u_bf, v_bf, s1t, s2t, stats)


def _peer_ffn(x, wq_bf, sk_bf, u_bf, v_bf):
    B, T, D = x.shape
    n = B * T
    xt = x.reshape(n, D)
    n_pad = _round_up(n, LANE)
    if n_pad != n:
        xt = jnp.pad(xt, ((0, n_pad - n), (0, 0)))
    s1t, s2t, stats = _peer_scores(xt, wq_bf, sk_bf)
    out = _peer_experts(xt.astype(jnp.bfloat16), u_bf, v_bf, s1t, s2t, stats)
    return out[:n].reshape(B, T, D)


NSA_TQ = 256
NSA_GSPLIT = 4


def _flash_tile(q4, k_t, v_t, bias, m, l, acc):
    tq, tk = bias.shape
    s = lax.dot_general(q4, k_t, (((1,), (1,)), ((), ())), preferred_element_type=jnp.float32)
    s = (s.reshape(-1, tq, tk) + bias[None]).reshape(-1, tk)
    m_new = jnp.maximum(m, jnp.max(s, axis=-1, keepdims=True))
    p = jnp.exp(s - m_new)
    alpha = jnp.exp(m - m_new)
    l_new = alpha * l + jnp.sum(p, axis=-1, keepdims=True)
    acc_new = alpha * acc + jnp.dot(p.astype(jnp.bfloat16), v_t, preferred_element_type=jnp.float32)
    return m_new, l_new, acc_new


def _nsa_prompt_kernel(q_ref, gl_ref, kc_ref, vc_ref, ks_ref, vs_ref, kw_ref, vw_ref, cw_ref, ex_ref,
                       o_ref, kcb_scr, vcb_scr, imp_scr, selx_scr, *, seq):
    qt = pl.program_id(2)
    tq = NSA_TQ
    n_cmp = seq // CMP_BLOCK
    n_sel = seq // SEL_BLOCK
    G = NSA_GROUP
    q0 = qt * tq

    @pl.when(qt == 0)
    def _():
        kc = kc_ref[...].reshape(n_cmp, CMP_BLOCK, HEAD_DIM)
        vc = vc_ref[...].reshape(n_cmp, CMP_BLOCK, HEAD_DIM)
        kcb_scr[...] = jnp.zeros_like(kcb_scr)
        vcb_scr[...] = jnp.zeros_like(vcb_scr)
        kcb_scr[0:n_cmp, :] = jnp.sum(kc * cw_ref[0][None], axis=1)
        vcb_scr[0:n_cmp, :] = jnp.sum(vc * cw_ref[1][None], axis=1).astype(jnp.bfloat16)

    qpos_col = q0 + lax.broadcasted_iota(jnp.int32, (tq, 1), 0)
    qf = [q_ref[g] * (HEAD_DIM ** -0.5) for g in range(G)]
    qs = [t.astype(jnp.bfloat16) for t in qf]
    gates = jax.nn.sigmoid(gl_ref[...])

    cmp_end = (lax.broadcasted_iota(jnp.int32, (1, LANE), 1) + 1) * CMP_BLOCK - 1
    cmask = (cmp_end <= qpos_col) & (lax.broadcasted_iota(jnp.int32, (1, LANE), 1) < n_cmp)
    o_cmp = []
    imp = jnp.zeros((tq, LANE), jnp.float32)
    for g in range(G):
        s = lax.dot_general(qf[g], kcb_scr[...], (((1,), (1,)), ((), ())),
                            precision=lax.Precision.HIGHEST, preferred_element_type=jnp.float32)
        s = jnp.where(cmask, s, NEG_INF)
        e = jnp.exp(s - jnp.max(s, axis=-1, keepdims=True))
        p = jnp.where(cmask, e / jnp.sum(e, axis=-1, keepdims=True), 0.0)
        o_cmp.append(jnp.dot(p.astype(jnp.bfloat16), vcb_scr[...], preferred_element_type=jnp.float32))
        imp = imp + p
    imp_t = imp.T
    ratio = SEL_BLOCK // CMP_BLOCK
    parts = []
    for c in range(tq // LANE):
        imp_scr[c] = imp_t[:, c * LANE:(c + 1) * LANE]
        part = imp_scr[c, pl.ds(0, n_sel, stride=ratio), :]
        for r in range(1, ratio):
            part = part + imp_scr[c, pl.ds(r, n_sel, stride=ratio), :]
        parts.append(part)
    imp_s = jnp.concatenate(parts, axis=1)
    qpos_row = q0 + lax.broadcasted_iota(jnp.int32, (1, tq), 1)
    cur = qpos_row // SEL_BLOCK
    blk = lax.broadcasted_iota(jnp.int32, (n_sel, tq), 0)
    forced = (blk == 0) | (blk == cur) | (blk == cur - 1)
    imp_s = jnp.where(blk > cur, -1.0, jnp.where(forced, FORCE_SCORE, imp_s))
    rank = jnp.zeros((n_sel, tq), jnp.int32)
    for mm in range(n_sel):
        row = imp_s[mm:mm + 1, :]
        ahead = (row > imp_s) | ((row == imp_s) & (mm < blk))
        rank = rank + jnp.where(ahead, 1, 0)
    sel = jnp.where((rank < min(SEL_TOPN, n_sel)) & (imp_s >= 0.0), 1.0, 0.0)
    sel_pad = jnp.concatenate([sel, jnp.zeros((LANE - n_sel, tq), jnp.float32)], axis=0)
    sel_q = sel_pad.T.astype(jnp.bfloat16)
    selx_scr[...] = jnp.dot(sel_q, ex_ref[...], preferred_element_type=jnp.float32)

    kpos_l = lax.broadcasted_iota(jnp.int32, (1, tq), 1)
    GS = NSA_GSPLIT
    init = (jnp.full((GS * tq, 1), NEG_INF, jnp.float32), jnp.zeros((GS * tq, 1), jnp.float32),
            jnp.zeros((GS * tq, HEAD_DIM), jnp.float32))

    for g0 in range(0, G, GS):
        q_r = jnp.concatenate(qs[g0:g0 + GS], axis=0)

        def sel_body(kt, carry):
            k0 = pl.multiple_of(kt * tq, tq)
            chosen = jnp.where(selx_scr[:, pl.ds(k0, tq)] > 0.5, 0.0, NEG_INF)
            bias = jnp.where((k0 + kpos_l) <= qpos_col, chosen, NEG_INF)
            return _flash_tile(q_r, ks_ref[pl.ds(k0, tq), :], vs_ref[pl.ds(k0, tq), :], bias, *carry)

        _, l_s, acc_s = lax.fori_loop(0, qt + 1, sel_body, init)
        o_sel = acc_s / l_s

        def win_body(kt, carry):
            k0 = pl.multiple_of(kt * tq, tq)
            kpos = k0 + kpos_l
            band = jnp.where(kpos >= qpos_col - (WINDOW - 1), 0.0, NEG_INF)
            bias = jnp.where(kpos <= qpos_col, band, NEG_INF)
            return _flash_tile(q_r, kw_ref[pl.ds(k0, tq), :], vw_ref[pl.ds(k0, tq), :], bias, *carry)

        _, l_w, acc_w = lax.fori_loop(jnp.maximum(qt - WINDOW // tq, 0), qt + 1, win_body, init)
        o_win = acc_w / l_w

        for gi in range(GS):
            g = g0 + gi
            rows = slice(gi * tq, (gi + 1) * tq)
            o_ref[g] = (gates[:, 3 * g:3 * g + 1] * o_cmp[g] + gates[:, 3 * g + 1:3 * g + 2] * o_sel[rows]
                        + gates[:, 3 * g + 2:3 * g + 3] * o_win[rows])


def _nsa_cols(proj, a, b):
    return proj[..., RWKV_COLS + a:RWKV_COLS + b]


def _nsa_prompt(proj, cmp_w):
    B, T, _ = proj.shape
    KVH, G, HD = NSA_KV_HEADS, NSA_GROUP, HEAD_DIM
    kvc = NSA_KV_COLS
    tq = NSA_TQ
    q = _nsa_cols(proj, 0, D_NSA).reshape(B, T, KVH, G, HD).transpose(0, 2, 3, 1, 4)
    kv = [_nsa_cols(proj, D_NSA + i * kvc, D_NSA + (i + 1) * kvc).reshape(B, T, KVH, HD).transpose(0, 2, 1, 3)
          for i in range(6)]
    kc, vc = kv[0], kv[1]
    ks, vs, kw, vw = [t.astype(jnp.bfloat16) for t in kv[2:]]
    gl = _nsa_cols(proj, D_NSA + 6 * kvc, NSA_COLS).reshape(B, T, KVH, 3 * G).transpose(0, 2, 1, 3)
    cw = jnp.broadcast_to(cmp_w[:, :, None], (2, CMP_BLOCK, HD))
    expand = (jnp.arange(LANE)[:, None] == (jnp.arange(T) // SEL_BLOCK)[None, :]).astype(jnp.bfloat16)
    kv_spec = pl.BlockSpec((None, None, T, HD), lambda b, k, t: (b, k, 0, 0))
    out = pl.pallas_call(
        functools.partial(_nsa_prompt_kernel, seq=T),
        grid=(B, KVH, T // tq),
        in_specs=[pl.BlockSpec((None, None, G, tq, HD), lambda b, k, t: (b, k, 0, t, 0)),
                  pl.BlockSpec((None, None, tq, 3 * G), lambda b, k, t: (b, k, t, 0)),
                  kv_spec, kv_spec, kv_spec, kv_spec, kv_spec, kv_spec,
                  pl.BlockSpec((2, CMP_BLOCK, HD), lambda b, k, t: (0, 0, 0)),
                  pl.BlockSpec((LANE, T), lambda b, k, t: (0, 0))],
        out_specs=pl.BlockSpec((None, None, G, tq, HD), lambda b, k, t: (b, k, 0, t, 0)),
        out_shape=jax.ShapeDtypeStruct((B, KVH, G, T, HD), jnp.float32),
        scratch_shapes=[pltpu.VMEM((LANE, HD), jnp.float32),
                        pltpu.VMEM((LANE, HD), jnp.bfloat16),
                        pltpu.VMEM((tq // LANE, LANE, LANE), jnp.float32),
                        pltpu.VMEM((tq, T), jnp.float32)],
        compiler_params=pltpu.CompilerParams(dimension_semantics=("arbitrary", "arbitrary", "arbitrary"),
                                             vmem_limit_bytes=_VMEM_LIMIT),
        name="nsa_prompt",
    )(q, gl, kc, vc, ks, vs, kw, vw, cw, expand)
    return out.transpose(0, 3, 1, 2, 4).reshape(B, T, D_NSA)


def _nsa_prompt_mix(proj, n_keep, cmp_w):
    B, T, _ = proj.shape
    kv5 = lambda i: _nsa_cols(proj, D_NSA + 2 * i * NSA_KV_COLS, D_NSA + 2 * (i + 1) * NSA_KV_COLS).reshape(
        B, T, 2, NSA_KV_HEADS, HEAD_DIM)
    return _nsa_prompt(proj, cmp_w), kv5(0), kv5(1), kv5(2)[:, -n_keep:]


NSA_PG = 4
NSA_POOL_PG = LANE * CMP_BLOCK // PAGE_SIZE
PAGE_COLS = 2 * NSA_KV_COLS
NSA_ROWS = NSA_KV_HEADS * NSA_GROUP


def _page_specs(layer, n):
    def spec(j):
        return pl.BlockSpec((None, None, 2, NSA_KV_HEADS, HEAD_DIM, PAGE_SIZE),
                            lambda b, s, pt: (layer, pt[b, s * n + j], 0, 0, 0, 0))
    return [spec(j) for j in range(n)]


def _rows_last(t):
    nd = t.ndim
    return jnp.transpose(t, tuple(range(nd - 4)) + (nd - 3, nd - 2, nd - 1, nd - 4))


def _cmp_pool_kernel(pt_ref, *refs):
    pages, cw_ref, o_ref = refs[:NSA_POOL_PG], refs[NSA_POOL_PG], refs[NSA_POOL_PG + 1]
    per_page = PAGE_SIZE // CMP_BLOCK
    row_blk = lax.broadcasted_iota(jnp.int32, (PAGE_SIZE, LANE), 0) // CMP_BLOCK
    col = lax.broadcasted_iota(jnp.int32, (PAGE_SIZE, LANE), 1)
    acc = [jnp.zeros((NSA_KV_COLS, LANE), jnp.float32) for _ in range(2)]
    for j, pg in enumerate(pages):
        seg = jnp.where(col == j * per_page + row_blk, 1.0, 0.0).astype(jnp.bfloat16)
        for kv in range(2):
            xw = pg[kv].reshape(NSA_KV_COLS, PAGE_SIZE) * cw_ref[kv:kv + 1, :]
            hi = xw.astype(jnp.bfloat16)
            lo = (xw - hi.astype(jnp.float32)).astype(jnp.bfloat16)
            acc[kv] = (acc[kv] + jnp.dot(hi, seg, preferred_element_type=jnp.float32)
                       + jnp.dot(lo, seg, preferred_element_type=jnp.float32))
    o_ref[0:NSA_KV_COLS, :] = acc[0]
    o_ref[NSA_KV_COLS:, :] = acc[1]


def _cmp_pool(cache, page_table, layer, cw):
    B, n_pages = page_table.shape
    return pl.pallas_call(
        _cmp_pool_kernel,
        grid_spec=pltpu.PrefetchScalarGridSpec(
            num_scalar_prefetch=1, grid=(B, n_pages // NSA_POOL_PG),
            in_specs=_page_specs(layer, NSA_POOL_PG) + [pl.BlockSpec((2, PAGE_SIZE), lambda b, s, pt: (0, 0))],
            out_specs=pl.BlockSpec((None, PAGE_COLS, LANE), lambda b, s, pt: (b, 0, s))),
        out_shape=jax.ShapeDtypeStruct((B, PAGE_COLS, n_pages * PAGE_SIZE // CMP_BLOCK), jnp.float32),
        compiler_params=pltpu.CompilerParams(dimension_semantics=("arbitrary", "arbitrary"),
                                             vmem_limit_bytes=_VMEM_LIMIT),
        name="nsa_cmp_pool",
    )(page_table, *([cache] * NSA_POOL_PG), cw)


def _cmp_select_kernel(q_ref, kv_ref, ocmp_ref, sel_ref, pt_scr, imp_scr, *, past, steps):
    T = steps
    R = NSA_ROWS * T
    n_cmp = past // CMP_BLOCK
    n_sel = past // SEL_BLOCK + 1
    n_lane = sel_ref.shape[-1]
    ratio = SEL_BLOCK // CMP_BLOCK
    rows_t = lax.broadcasted_iota(jnp.int32, (R, 1), 0) % T
    qpos = past + rows_t
    cmp_end = (lax.broadcasted_iota(jnp.int32, (1, n_cmp), 1) + 1) * CMP_BLOCK - 1
    cmask = cmp_end <= qpos
    q = q_ref[...]
    p_all, o_all = [], []
    for kh in range(NSA_KV_HEADS):
        rows = slice(kh * NSA_GROUP * T, (kh + 1) * NSA_GROUP * T)
        kcb_t = kv_ref[kh * HEAD_DIM:(kh + 1) * HEAD_DIM, :]
        vcb_t = kv_ref[NSA_KV_COLS + kh * HEAD_DIM:NSA_KV_COLS + (kh + 1) * HEAD_DIM, :].astype(jnp.bfloat16)
        s = jnp.dot(q[rows], kcb_t, precision=lax.Precision.HIGHEST, preferred_element_type=jnp.float32)
        s = jnp.where(cmask[rows], s, NEG_INF)
        e = jnp.exp(s - jnp.max(s, axis=-1, keepdims=True))
        p = jnp.where(cmask[rows], e / jnp.sum(e, axis=-1, keepdims=True), 0.0)
        o_all.append(lax.dot_general(p.astype(jnp.bfloat16), vcb_t, (((1,), (1,)), ((), ())),
                                     preferred_element_type=jnp.float32))
        imp = p[0:T]
        for g in range(1, NSA_GROUP):
            imp = imp + p[g * T:(g + 1) * T]
        p_all.extend([imp] * NSA_GROUP)
    ocmp_ref[...] = jnp.concatenate(o_all, axis=0)
    imp_rows = jnp.concatenate(p_all + [jnp.zeros((LANE - R, n_cmp), jnp.float32)], axis=0)
    for c in range(n_cmp // LANE):
        pt_scr[c] = imp_rows[:, c * LANE:(c + 1) * LANE].T
    per = LANE // ratio
    for c in range(n_cmp // LANE):
        part = pt_scr[c, pl.ds(0, per, stride=ratio), :]
        for r in range(1, ratio):
            part = part + pt_scr[c, pl.ds(r, per, stride=ratio), :]
        imp_scr[c * per:(c + 1) * per, :] = part
    n_rows = imp_scr.shape[0]
    imp_scr[n_sel - 1:n_rows, :] = jnp.zeros((n_rows - n_sel + 1, LANE), jnp.float32)
    lane_t = lax.broadcasted_iota(jnp.int32, (1, LANE), 1) % T
    cur = (past + lane_t) // SEL_BLOCK
    blk = lax.broadcasted_iota(jnp.int32, (n_rows, LANE), 0)
    forced = (blk == 0) | (blk == cur) | (blk == cur - 1)
    imp_s = jnp.where(blk > cur, -1.0, jnp.where(forced, FORCE_SCORE, imp_scr[...]))
    imp_s = jnp.where(blk < n_sel, imp_s, -2.0)
    imp_scr[...] = imp_s

    def rank_body(m, rank):
        row = imp_scr[pl.ds(m, 1), :]
        ahead = (row > imp_s) | ((row == imp_s) & (m < blk))
        return rank + jnp.where(ahead, 1, 0)

    rank = lax.fori_loop(0, n_sel, rank_body, jnp.zeros((n_rows, LANE), jnp.int32))
    sel = jnp.where((rank < min(SEL_TOPN, n_sel)) & (imp_s >= 0.0), 1.0, 0.0)
    sel = jnp.concatenate([sel, jnp.zeros((n_lane - n_rows, LANE), jnp.float32)], axis=0)
    for c in range(n_lane // LANE):
        sel_ref[:, c * LANE:(c + 1) * LANE] = sel[c * LANE:(c + 1) * LANE, :].T


def _cmp_select(q_rows, kvcb, past, steps):
    B, R, _ = q_rows.shape
    n_cmp = kvcb.shape[2]
    n_sel = past // SEL_BLOCK + 1
    n_lane = _round_up(n_sel, LANE)
    n_rows = _round_up(n_sel, SUBLANE)
    return pl.pallas_call(
        functools.partial(_cmp_select_kernel, past=past, steps=steps),
        grid=(B,),
        in_specs=[pl.BlockSpec((None, R, HEAD_DIM), lambda b: (b, 0, 0)),
                  pl.BlockSpec((None, PAGE_COLS, n_cmp), lambda b: (b, 0, 0))],
        out_specs=[pl.BlockSpec((None, R, HEAD_DIM), lambda b: (b, 0, 0)),
                   pl.BlockSpec((None, LANE, n_lane), lambda b: (b, 0, 0))],
        out_shape=[jax.ShapeDtypeStruct((B, R, HEAD_DIM), jnp.float32),
                   jax.ShapeDtypeStruct((B, LANE, n_lane), jnp.float32)],
        scratch_shapes=[pltpu.VMEM((n_cmp // LANE, LANE, LANE), jnp.float32),
                        pltpu.VMEM((n_rows, LANE), jnp.float32)],
        compiler_params=pltpu.CompilerParams(dimension_semantics=("arbitrary",),
                                             vmem_limit_bytes=_VMEM_LIMIT),
        name="nsa_cmp_select",
    )(q_rows, kvcb)


def _soft_update(s, mask, v, m, l, acc, keys_last=False):
    s = jnp.where(mask, s, NEG_INF)
    m_new = jnp.maximum(m, jnp.max(s, axis=-1, keepdims=True))
    p = jnp.where(mask, jnp.exp(s - m_new), 0.0)
    alpha = jnp.exp(m - m_new)
    pv = lax.dot_general(p.astype(jnp.bfloat16), v, (((1,), (1 if keys_last else 0,)), ((), ())),
                         preferred_element_type=jnp.float32)
    return m_new, alpha * l + jnp.sum(p, axis=-1, keepdims=True), alpha * acc + pv


def _sel_win_kernel(pt_ref, *refs, past, steps):
    pages = refs[:NSA_PG]
    (q_ref, sel_ref, ocmp_ref, nsel_ref, wpre_ref, nwin_ref, gate_ref,
     o_ref, m_scr, l_scr, acc_scr) = refs[NSA_PG:]
    s_id = pl.program_id(1)
    T = steps
    GT = NSA_GROUP * T
    R = NSA_ROWS * T
    n_lane = sel_ref.shape[-1]
    rows_t = lax.broadcasted_iota(jnp.int32, (GT, 1), 0) % T

    @pl.when(s_id == 0)
    def _():
        m_scr[...] = jnp.full_like(m_scr, NEG_INF)
        l_scr[...] = jnp.zeros_like(l_scr)
        acc_scr[...] = jnp.zeros_like(acc_scr)

    q = q_ref[...].astype(jnp.bfloat16)
    sel_bf = sel_ref[0:R, :].astype(jnp.bfloat16)
    n_keys = NSA_PG * PAGE_SIZE
    blk_id = lax.broadcasted_iota(jnp.int32, (n_lane, n_keys), 0)
    key_blk = s_id * (n_keys // SEL_BLOCK) + lax.broadcasted_iota(jnp.int32, (n_lane, n_keys), 1) // SEL_BLOCK

    def head_cols(x, kh, off):
        return x[:, off + kh * HEAD_DIM:off + (kh + 1) * HEAD_DIM]

    def keys_t(kv, kh):
        return jnp.concatenate([pg[kv, kh] for pg in pages], axis=1).astype(jnp.bfloat16)

    expand = jnp.where(blk_id == key_blk, 1.0, 0.0).astype(jnp.bfloat16)
    chosen = jnp.dot(sel_bf, expand, preferred_element_type=jnp.float32) > 0.5
    head_rows = [slice(kh * GT, (kh + 1) * GT) for kh in range(NSA_KV_HEADS)]
    scores = [jnp.dot(q[rows], keys_t(0, kh), preferred_element_type=jnp.float32)
              for kh, rows in enumerate(head_rows)]
    upd = [_soft_update(scores[kh], chosen[rows], keys_t(1, kh), m_scr[rows], l_scr[rows], acc_scr[rows],
                        keys_last=True) for kh, rows in enumerate(head_rows)]
    for kh, rows in enumerate(head_rows):
        m_scr[rows], l_scr[rows], acc_scr[rows] = upd[kh]

    @pl.when(s_id == pl.num_programs(1) - 1)
    def _():
        new_blk = past // SEL_BLOCK
        tpad = nsel_ref.shape[0]
        jn = lax.broadcasted_iota(jnp.int32, (1, tpad), 1)
        nsel = nsel_ref[...].astype(jnp.bfloat16)
        nwin = nwin_ref[...].astype(jnp.bfloat16)
        jp = lax.broadcasted_iota(jnp.int32, (1, WINDOW), 1)
        gates = jax.nn.sigmoid(gate_ref[...])
        for kh in range(NSA_KV_HEADS):
            rows = slice(kh * GT, (kh + 1) * GT)
            pick = sel_ref[kh * GT:(kh + 1) * GT, new_blk:new_blk + 1] > 0.5
            s = lax.dot_general(q[rows], head_cols(nsel, kh, 0), (((1,), (1,)), ((), ())),
                                preferred_element_type=jnp.float32)
            m, l, acc = _soft_update(s, pick & (jn <= rows_t) & (jn < T), head_cols(nsel, kh, NSA_KV_COLS),
                                     m_scr[rows], l_scr[rows], acc_scr[rows])
            o_sel = acc / l
            init = (jnp.full((GT, 1), NEG_INF, jnp.float32), jnp.zeros((GT, 1), jnp.float32),
                    jnp.zeros((GT, HEAD_DIM), jnp.float32))
            s = jnp.dot(q[rows], wpre_ref[0, kh].astype(jnp.bfloat16), preferred_element_type=jnp.float32)
            st = _soft_update(s, jp > rows_t, wpre_ref[1, kh].astype(jnp.bfloat16), *init, keys_last=True)
            s = lax.dot_general(q[rows], head_cols(nwin, kh, 0), (((1,), (1,)), ((), ())),
                                preferred_element_type=jnp.float32)
            _, l_w, acc_w = _soft_update(s, (jn <= rows_t) & (jn < T), head_cols(nwin, kh, NSA_KV_COLS), *st)
            o_win = acc_w / l_w
            gt = gates[rows]
            o_ref[rows, :] = gt[:, 0:1] * ocmp_ref[rows, :] + gt[:, 1:2] * o_sel + gt[:, 2:3] * o_win


def _sel_win(cache, page_table, layer, q_rows, sel_rows, o_cmp, new_sel, win_prefix, new_win, gate_rows,
             past, steps):
    B, n_pages = page_table.shape
    R = q_rows.shape[1]
    n_lane = sel_rows.shape[-1]
    tpad = new_sel.shape[1]
    per_b = lambda shape: pl.BlockSpec((None,) + shape, lambda b, s, pt: (b,) + (0,) * len(shape))
    return pl.pallas_call(
        functools.partial(_sel_win_kernel, past=past, steps=steps),
        grid_spec=pltpu.PrefetchScalarGridSpec(
            num_scalar_prefetch=1, grid=(B, n_pages // NSA_PG),
            in_specs=_page_specs(layer, NSA_PG) + [
                per_b((R, HEAD_DIM)), per_b((LANE, n_lane)), per_b((R, HEAD_DIM)),
                per_b((tpad, PAGE_COLS)), per_b((2, NSA_KV_HEADS, HEAD_DIM, WINDOW)), per_b((tpad, PAGE_COLS)),
                per_b((R, 3))],
            out_specs=per_b((R, HEAD_DIM)),
            scratch_shapes=[pltpu.VMEM((R, 1), jnp.float32), pltpu.VMEM((R, 1), jnp.float32),
                            pltpu.VMEM((R, HEAD_DIM), jnp.float32)]),
        out_shape=jax.ShapeDtypeStruct((B, R, HEAD_DIM), jnp.float32),
        compiler_params=pltpu.CompilerParams(dimension_semantics=("arbitrary", "arbitrary"),
                                             vmem_limit_bytes=_VMEM_LIMIT),
        name="nsa_sel_win",
    )(page_table, *([cache] * NSA_PG), q_rows, sel_rows, o_cmp, new_sel, win_prefix, new_win, gate_rows)


def _nsa_decode_mix(proj, cache_cmp, cache_sel, page_table, layer, win_prefix, cmp_w):
    B, T, _ = proj.shape
    xn = proj[..., RWKV_COLS:IN_COLS]
    n_pages = page_table.shape[1]
    past = n_pages * PAGE_SIZE
    assert T <= SEL_BLOCK and n_pages % NSA_POOL_PG == 0 and win_prefix.shape[1] == WINDOW
    KVH, G, HD = NSA_KV_HEADS, NSA_GROUP, HEAD_DIM
    kv5 = lambda i: xn[..., D_NSA + 2 * i * NSA_KV_COLS:D_NSA + 2 * (i + 1) * NSA_KV_COLS]
    new_cmp, new_sel, new_win = kv5(0), kv5(1), kv5(2)
    q_rows = (xn[..., :D_NSA] * (HD ** -0.5)).reshape(B, T, KVH, G, HD).transpose(0, 2, 3, 1, 4)
    q_rows = q_rows.reshape(B, KVH * G * T, HD)
    gate_rows = xn[..., D_NSA + 6 * NSA_KV_COLS:].reshape(B, T, KVH, G, 3).transpose(0, 2, 3, 1, 4)
    gate_rows = gate_rows.reshape(B, KVH * G * T, 3)
    cw = jnp.tile(cmp_w, (1, PAGE_SIZE // CMP_BLOCK))
    kvcb = _cmp_pool(_rows_last(cache_cmp), page_table, layer, cw)
    o_cmp, sel_rows = _cmp_select(q_rows, kvcb, past, T)
    tpad = _round_up(T, SUBLANE)
    padt = lambda t: jnp.pad(t, ((0, 0), (0, tpad - T), (0, 0)))
    out = _sel_win(_rows_last(cache_sel), page_table, layer, q_rows, sel_rows, o_cmp, padt(new_sel),
                   _rows_last(win_prefix), padt(new_win), gate_rows, past, T)
    y = out.reshape(B, KVH, G, T, HD).transpose(0, 3, 1, 2, 4).reshape(B, T, D_NSA)
    five = lambda t: t.reshape(B, -1, 2, KVH, HD)
    win_out = jnp.concatenate([win_prefix[:, T:], five(new_win)], axis=1)
    return y, five(new_cmp), five(new_sel), win_out


def _deepnorm(x, h, g, b):
    z = DEEPNORM_ALPHA * x + h
    mu = jnp.mean(z, axis=-1, keepdims=True)
    dev = z - mu
    var = jnp.mean(dev * dev, axis=-1, keepdims=True)
    return dev * lax.rsqrt(var + LN_EPS) * g + b


def _out_proj_kernel(yr_ref, yn_ref, x_ref, w_ref, g_ref, b_ref, o_ref):
    half = yr_ref.shape[1]
    h = (jnp.dot(yr_ref[...].astype(jnp.bfloat16), w_ref[0:half, :], preferred_element_type=jnp.float32)
         + jnp.dot(yn_ref[...].astype(jnp.bfloat16), w_ref[half:, :], preferred_element_type=jnp.float32))
    o_ref[...] = _deepnorm(x_ref[...], h, g_ref[...], b_ref[...])


def _residual_norm_kernel(x_ref, f_ref, g_ref, b_ref, o_ref):
    o_ref[...] = _deepnorm(x_ref[...], f_ref[...], g_ref[...], b_ref[...])


def _residual_norm(x, f, g, b, *, tm=256):
    n, d = x.shape
    tm = min(tm, n)
    rows = pl.BlockSpec((tm, d), lambda i: (i, 0))
    vec = pl.BlockSpec((1, d), lambda i: (0, 0))
    return pl.pallas_call(
        _residual_norm_kernel,
        grid=(n // tm,),
        in_specs=[rows, rows, vec, vec],
        out_specs=rows,
        out_shape=jax.ShapeDtypeStruct((n, d), jnp.float32),
        compiler_params=pltpu.CompilerParams(dimension_semantics=("arbitrary",),
                                             vmem_limit_bytes=_VMEM_LIMIT),
        name="residual_norm",
    )(x, f, g.reshape(1, d), b.reshape(1, d))


def _out_proj_norm(y_r, y_n, x, w_bf, g, b, *, tm=256):
    n, d = x.shape
    tm = min(tm, n)
    half = y_r.shape[1]
    rows = lambda w: pl.BlockSpec((tm, w), lambda i: (i, 0))
    full = lambda shape: pl.BlockSpec(shape, lambda i: (0, 0))
    return pl.pallas_call(
        _out_proj_kernel,
        grid=(n // tm,),
        in_specs=[rows(half), rows(y_n.shape[1]), rows(d), full(w_bf.shape), full((1, d)), full((1, d))],
        out_specs=rows(d),
        out_shape=jax.ShapeDtypeStruct((n, d), jnp.float32),
        compiler_params=pltpu.CompilerParams(dimension_semantics=("arbitrary",),
                                             vmem_limit_bytes=_VMEM_LIMIT),
        name="out_proj_norm",
    )(y_r, y_n, x, w_bf, g.reshape(1, d), b.reshape(1, d))


def _hybrid_layer(x, past, shift_prev, rwkv_s0, n_keep, p):
    B, T, D = x.shape
    proj = _matmul(x.reshape(B * T, D), p['w_in_bf']).reshape(B, T, IN_COLS_PAD)
    y_r, s_T, new_shift = _rwkv_time_mix(proj, shift_prev, rwkv_s0, p)
    if past is None:
        assert T % NSA_TQ == 0 and T >= n_keep
        y_n, new_cmp, new_sel, new_win = _nsa_prompt_mix(proj, n_keep, p['nsa_cmp_w'])
    else:
        assert n_keep == WINDOW
        y_n, new_cmp, new_sel, new_win = _nsa_decode_mix(proj, *past, p['nsa_cmp_w'])
    x = _out_proj_norm(y_r.reshape(B * T, D_RWKV), y_n.reshape(B * T, D_NSA), x.reshape(B * T, D),
                       p['w_out_bf'], p['ln1_g'], p['ln1_b']).reshape(B, T, D)
    f = _peer_ffn(x, p['peer_wq_bf'], p['peer_sk_bf'], p['peer_u_bf'], p['peer_v_bf'])
    x = _residual_norm(x.reshape(B * T, D), f.reshape(B * T, D), p['ln2_g'], p['ln2_b']).reshape(B, T, D)
    return x, (new_cmp, new_sel, new_win, s_T, new_shift)


def kernel(x_prompt, x_sample, cache_cmp_kv, cache_sel_kv, page_table, state_win_kv, state_rwkv,
           state_shift, w_in, rwkv_mu, rwkv_w0, rwkv_w2, rwkv_a0, rwkv_a2, rwkv_g2, rwkv_k_k,
           rwkv_k_a, rwkv_r_k, rwkv_gn_g, rwkv_gn_b, nsa_cmp_w, w_out, ln1_g, ln1_b, peer_wq,
           peer_subkeys, peer_u, peer_v, ln2_g, ln2_b):
    bp = x_prompt.shape[0]
    dt = x_prompt.dtype
    assert page_table.shape[1] * PAGE_SIZE == PAST_LEN
    n_keep = state_win_kv.shape[2]
    zero_shift = jnp.zeros((bp, 1, RWKV_COLS), dt)
    zero_state = jnp.zeros((bp, H_RWKV, HEAD_DIM, HEAD_DIM), dt)
    yp, ys = x_prompt, x_sample
    st_p, st_s = [], []
    for l in range(DEPTH):
        p = {'w_in_bf': jnp.pad(w_in[l].astype(jnp.bfloat16), ((0, 0), (0, IN_COLS_PAD - IN_COLS))),
             'rwkv_mu': rwkv_mu[l], 'rwkv_w0': rwkv_w0[l], 'rwkv_w2': rwkv_w2[l],
             'rwkv_a0': rwkv_a0[l], 'rwkv_a2': rwkv_a2[l], 'rwkv_g2': rwkv_g2[l],
             'rwkv_k_k': rwkv_k_k[l], 'rwkv_k_a': rwkv_k_a[l], 'rwkv_r_k': rwkv_r_k[l],
             'rwkv_gn_g': rwkv_gn_g[l], 'rwkv_gn_b': rwkv_gn_b[l], 'nsa_cmp_w': nsa_cmp_w[l],
             'w_out_bf': w_out[l].astype(jnp.bfloat16), 'ln1_g': ln1_g[l], 'ln1_b': ln1_b[l],
             'peer_wq_bf': peer_wq[l].astype(jnp.bfloat16),
             'peer_sk_bf': peer_subkeys[l].reshape(2 * P_HEADS, N_KEYS, P_DKEY // 2).astype(jnp.bfloat16),
             'peer_u_bf': peer_u[l].astype(jnp.bfloat16).T, 'peer_v_bf': peer_v[l].astype(jnp.bfloat16),
             'ln2_g': ln2_g[l], 'ln2_b': ln2_b[l]}
        yp, sp = _hybrid_layer(yp, None, zero_shift, zero_state, n_keep, p)
        past = (cache_cmp_kv, cache_sel_kv, page_table, l, state_win_kv[l])
        ys, ss = _hybrid_layer(ys, past, state_shift[l], state_rwkv[l], n_keep, p)
        st_p.append(sp)
        st_s.append(ss)
    stk = lambda sts, i: jnp.stack([s[i] for s in sts], axis=0)
    return (yp, ys, stk(st_p, 0), stk(st_p, 1), stk(st_p, 2), stk(st_p, 3), stk(st_p, 4),
            stk(st_s, 0), stk(st_s, 1), stk(st_s, 2), stk(st_s, 3), stk(st_s, 4))
```

```python
import functools
import math

import jax
import jax.numpy as jnp
from jax import lax
from jax.experimental import pallas as pl
from jax.experimental.pallas import tpu as pltpu

D_MODEL = 2048
DEPTH = 2
PAST_LEN = 16384
PAGE_SIZE = 128
HEAD_DIM = 64
D_RWKV = D_MODEL // 2
D_NSA = D_MODEL - D_RWKV
H_RWKV = D_RWKV // HEAD_DIM
H_NSA = D_NSA // HEAD_DIM
NSA_KV_HEADS = 4
NSA_GROUP = H_NSA // NSA_KV_HEADS
NSA_KV_COLS = NSA_KV_HEADS * HEAD_DIM
CMP_BLOCK = 32
SEL_BLOCK = 64
SEL_TOPN = 16
WINDOW = 512
W_LORA = 64
A_LORA = 64
G_LORA = 160
RWKV_COLS = 3 * D_RWKV + W_LORA + A_LORA + G_LORA
NSA_COLS = D_NSA + 6 * NSA_KV_COLS + 3 * H_NSA
IN_COLS = RWKV_COLS + NSA_COLS
P_HEADS = 8
N_KEYS = 128
P_DKEY = 256
P_TOPK = 16
LN_EPS = 1e-5
GN_EPS = 64e-5
DEEPNORM_ALPHA = (2 * DEPTH) ** 0.25
FORCE_SCORE = 1e4
NEG_INF = -1e30

N_EXPERTS = N_KEYS * N_KEYS

LANE = 128
SUBLANE = 8
_VMEM_LIMIT = 56 * 1024 * 1024


def _round_up(x, m):
    return -(-x // m) * m


PROJ_TILE = 1024
IN_COLS_PAD = _round_up(IN_COLS, PROJ_TILE)


def _mm_kernel(x_ref, w_ref, o_ref):
    o_ref[...] = jnp.dot(x_ref[...].astype(jnp.bfloat16), w_ref[...], preferred_element_type=jnp.float32)


def _matmul(x, w_bf):
    m, k = x.shape
    n = w_bf.shape[1]
    tm = min(PROJ_TILE, m)
    tn = PROJ_TILE
    return pl.pallas_call(
        _mm_kernel,
        grid=(m // tm, n // tn),
        in_specs=[pl.BlockSpec((tm, k), lambda i, j: (i, 0)),
                  pl.BlockSpec((k, tn), lambda i, j: (0, j))],
        out_specs=pl.BlockSpec((tm, tn), lambda i, j: (i, j)),
        out_shape=jax.ShapeDtypeStruct((m, n), jnp.float32),
        compiler_params=pltpu.CompilerParams(dimension_semantics=("arbitrary", "arbitrary"),
                                             vmem_limit_bytes=_VMEM_LIMIT),
        name="in_proj",
    )(x, w_bf)


RW_CHUNK = 64
RW_PAIRS = D_RWKV // LANE


def _bdot(a, b):
    return jnp.dot(a.astype(jnp.bfloat16), b.astype(jnp.bfloat16), preferred_element_type=jnp.float32)


def _bdot_nt(a, b):
    return lax.dot_general(a.astype(jnp.bfloat16), b.astype(jnp.bfloat16), (((1,), (1,)), ((), ())),
                           preferred_element_type=jnp.float32)


def _bdot_tn(a, b):
    return lax.dot_general(a.astype(jnp.bfloat16), b.astype(jnp.bfloat16), (((0,), (0,)), ((), ())),
                           preferred_element_type=jnp.float32)


def _head_sum(x, same_head_bf):
    hi = x.astype(jnp.bfloat16)
    lo = (x - hi.astype(jnp.float32)).astype(jnp.bfloat16)
    return (jnp.dot(hi, same_head_bf, preferred_element_type=jnp.float32)
            + jnp.dot(lo, same_head_bf, preferred_element_type=jnp.float32))


def _rwkv_mix_kernel(x_ref, shift_ref, mu_ref, pv_ref, w2_ref, a2_ref, g2_ref, s0_ref, y_ref, st_ref,
                     prev_scr, *, steps):
    c = pl.program_id(1)
    C = RW_CHUNK
    f32 = jnp.float32

    @pl.when(c == 0)
    def _():
        st_ref[...] = s0_ref[...]
        prev_scr[...] = shift_ref[...]

    xr = x_ref[:, 0:RWKV_COLS]
    row_c = lax.broadcasted_iota(jnp.int32, (C, 1), 0)
    prev = jnp.where(row_c == 0, prev_scr[...], pltpu.roll(xr, 1, 0))
    prev_scr[...] = xr[C - 1:C, :]
    xs = xr + mu_ref[...] * (prev - xr)
    o_wl = 3 * D_RWKV
    r_all, k_all, v_all = xs[:, 0:D_RWKV], xs[:, D_RWKV:2 * D_RWKV], xs[:, 2 * D_RWKV:o_wl]
    wl = xs[:, o_wl:o_wl + W_LORA]
    al = xs[:, o_wl + W_LORA:o_wl + W_LORA + A_LORA]
    gl = xs[:, o_wl + W_LORA + A_LORA:RWKV_COLS]
    w0, a0_, k_k, k_a = pv_ref[0:1, :], pv_ref[1:2, :], pv_ref[2:3, :], pv_ref[3:4, :]
    r_k, gn_g, gn_b = pv_ref[4:5, :], pv_ref[5:6, :], pv_ref[6:7, :]
    w_all = -jax.nn.softplus(-(w0 + _bdot(jnp.tanh(wl), w2_ref[...]))) - 0.5
    lw_all = -jnp.exp(w_all)
    a_all = jax.nn.sigmoid(a0_ + _bdot(al, a2_ref[...]))
    gate_all = _bdot(jax.nn.sigmoid(gl), g2_ref[...])
    kkf_all = k_all * k_k
    k2_all = k_all * (1.0 + (a_all - 1.0) * k_a)
    if steps % C:
        live = (c * C + row_c) < steps
        lw_all = jnp.where(live, lw_all, 0.0)
        kkf_all = jnp.where(live, kkf_all, 0.0)
        k2_all = jnp.where(live, k2_all, 0.0)
        v_all = jnp.where(live, v_all, 0.0)

    row = lax.broadcasted_iota(jnp.int32, (C, C), 0)
    col = lax.broadcasted_iota(jnp.int32, (C, C), 1)
    tri_incl = col <= row
    tri_strict = col < row
    tri_f = jnp.where(tri_incl, 1.0, 0.0).astype(f32)
    eye = jnp.where(row == col, 1.0, 0.0).astype(f32)
    lane = lax.broadcasted_iota(jnp.int32, (1, LANE), 1)
    head0 = lane < HEAD_DIM
    r128 = lax.broadcasted_iota(jnp.int32, (LANE, LANE), 0)
    c128 = lax.broadcasted_iota(jnp.int32, (LANE, LANE), 1)
    same_head = (r128 < HEAD_DIM) == (c128 < HEAD_DIM)
    same_head_bf = jnp.where(same_head, 1.0, 0.0).astype(jnp.bfloat16)
    eye128 = r128 == c128

    pairs = range(RW_PAIRS)
    heads = (head0, ~head0)
    sls = [slice(pr * LANE, (pr + 1) * LANE) for pr in pairs]
    lw = [lw_all[:, sl] for sl in sls]
    cum = [jnp.dot(tri_f, lw[pr], precision=lax.Precision.HIGHEST, preferred_element_type=f32)
           for pr in pairs]
    tot = [cum[pr][C - 1:C, :] for pr in pairs]
    v = [v_all[:, sl] for sl in sls]
    k = [k2_all[:, sl] for sl in sls]
    r = [r_all[:, sl] for sl in sls]
    kkf = [kkf_all[:, sl] for sl in sls]
    ss = [_head_sum(kkf[pr] * kkf[pr], same_head_bf) for pr in pairs]
    kk = [kkf[pr] * lax.rsqrt(jnp.maximum(ss[pr], 1e-24)) for pr in pairs]
    b = [kk[pr] * a_all[:, sls[pr]] for pr in pairs]
    x, ym = [], []
    for pr in pairs:
        e_neg = jnp.exp(-cum[pr])
        x.append(jnp.concatenate([kk[pr] * jnp.exp(cum[pr] - lw[pr]),
                                  r[pr] * jnp.exp(cum[pr])], axis=0))
        ym.append(jnp.concatenate([k[pr] * e_neg, b[pr] * e_neg], axis=0))
    a0 = [st_ref[pr] for pr in pairs]
    xa = [_bdot(x[pr], a0[pr]) for pr in pairs]
    g = [[_bdot_nt(jnp.where(hm, x[pr], 0.0), ym[pr]) for hm in heads] for pr in pairs]
    lkv = [[_bdot(jnp.where(tri_strict, g[pr][h][:C, :C], 0.0), v[pr]) for h in range(2)] for pr in pairs]
    npow = [[jnp.where(tri_strict, -g[pr][h][:C, C:], 0.0) for h in range(2)] for pr in pairs]
    tmat = [[eye + npow[pr][h] for h in range(2)] for pr in pairs]
    for _ in range(int(math.log2(C)) - 1):
        npow = [[_bdot(npow[pr][h], npow[pr][h]) for h in range(2)] for pr in pairs]
        tmat = [[tmat[pr][h] + _bdot(tmat[pr][h], npow[pr][h]) for h in range(2)] for pr in pairs]
    rhs = [xa[pr][:C] + jnp.where(head0, lkv[pr][0], lkv[pr][1]) for pr in pairs]
    w = [jnp.where(head0, _bdot(tmat[pr][0], rhs[pr]), _bdot(tmat[pr][1], rhs[pr])) for pr in pairs]
    vw = [jnp.concatenate([v[pr], w[pr]], axis=0) for pr in pairs]
    y = []
    for pr in pairs:
        mr = [jnp.concatenate([jnp.where(tri_incl, g[pr][h][C:, :C], 0.0),
                               jnp.where(tri_incl, -g[pr][h][C:, C:], 0.0)], axis=1) for h in range(2)]
        y.append(xa[pr][C:] + jnp.where(head0, _bdot(mr[0], vw[pr]), _bdot(mr[1], vw[pr])))
    inv_hd = 1.0 / HEAD_DIM
    mean = [_head_sum(y[pr], same_head_bf) * inv_hd for pr in pairs]
    dev = [y[pr] - mean[pr] for pr in pairs]
    var = [_head_sum(dev[pr] * dev[pr], same_head_bf) * inv_hd for pr in pairs]
    rk = [_head_sum(r[pr] * k[pr] * r_k[:, sls[pr]], same_head_bf) for pr in pairs]
    for pr in pairs:
        yn = dev[pr] * lax.rsqrt(var[pr] + GN_EPS) * gn_g[:, sls[pr]] + gn_b[:, sls[pr]]
        y_ref[:, sls[pr]] = (yn + rk[pr] * v[pr]) * gate_all[:, sls[pr]]
    for pr in pairs:
        e_rem = jnp.exp(tot[pr] - cum[pr])
        kb = jnp.concatenate([k[pr] * e_rem, -(b[pr] * e_rem)], axis=0)
        upd = _bdot_tn(kb, vw[pr])
        p_col = jnp.sum(jnp.where(eye128, jnp.exp(tot[pr]), 0.0), axis=1, keepdims=True)
        st_ref[pr] = a0[pr] * p_col + jnp.where(same_head, upd, 0.0)


def _rwkv_time_mix(proj, shift_prev, s0, p):
    B, T, W = proj.shape
    C = RW_CHUNK
    t_pad = _round_up(T, C)
    x = proj if t_pad == T else jnp.pad(proj, ((0, 0), (0, t_pad - T), (0, 0)))
    a = jnp.swapaxes(s0, -1, -2).reshape(B, RW_PAIRS, 2, HEAD_DIM, HEAD_DIM)
    z = jnp.zeros_like(a[:, :, 0])
    a0 = jnp.concatenate([jnp.concatenate([a[:, :, 0], z], axis=-1),
                          jnp.concatenate([z, a[:, :, 1]], axis=-1)], axis=-2)
    zero = jnp.zeros((D_RWKV,), jnp.float32)
    pv = jnp.stack([p['rwkv_w0'], p['rwkv_a0'], p['rwkv_k_k'], p['rwkv_k_a'], p['rwkv_r_k'].reshape(-1),
                    p['rwkv_gn_g'], p['rwkv_gn_b'], zero])
    bf = lambda t: t.astype(jnp.bfloat16)
    full = lambda shape: pl.BlockSpec(shape, lambda bi, ci: (0,) * len(shape))
    st_spec = pl.BlockSpec((None, RW_PAIRS, LANE, LANE), lambda bi, ci: (bi, 0, 0, 0))
    y, st = pl.pallas_call(
        functools.partial(_rwkv_mix_kernel, steps=T),
        grid=(B, t_pad // C),
        in_specs=[pl.BlockSpec((None, C, W), lambda bi, ci: (bi, ci, 0)),
                  pl.BlockSpec((None, 1, RWKV_COLS), lambda bi, ci: (bi, 0, 0)),
                  full((1, RWKV_COLS)), full((8, D_RWKV)), full((W_LORA, D_RWKV)),
                  full((A_LORA, D_RWKV)), full((G_LORA, D_RWKV)), st_spec],
        out_specs=[pl.BlockSpec((None, C, D_RWKV), lambda bi, ci: (bi, ci, 0)), st_spec],
        out_shape=[jax.ShapeDtypeStruct((B, t_pad, D_RWKV), jnp.float32),
                   jax.ShapeDtypeStruct((B, RW_PAIRS, LANE, LANE), jnp.float32)],
        scratch_shapes=[pltpu.VMEM((1, RWKV_COLS), jnp.float32)],
        compiler_params=pltpu.CompilerParams(dimension_semantics=("arbitrary", "arbitrary"),
                                             vmem_limit_bytes=_VMEM_LIMIT),
        name="rwkv_mix",
    )(x, shift_prev, p['rwkv_mu'].reshape(1, RWKV_COLS), pv, bf(p['rwkv_w2']), bf(p['rwkv_a2']),
      bf(p['rwkv_g2']), a0)
    s_t = jnp.stack([st[:, :, :HEAD_DIM, :HEAD_DIM], st[:, :, HEAD_DIM:, HEAD_DIM:]], axis=2)
    s_t = jnp.swapaxes(s_t.reshape(B, H_RWKV, HEAD_DIM, HEAD_DIM), -1, -2)
    return y[:, :T], s_t, proj[:, T - 1:T, :RWKV_COLS]


def _extract_top(buf_ref, out_ref):
    n_slabs, nrows, width = buf_ref.shape
    iota = lax.broadcasted_iota(jnp.int32, (nrows, width), 0)

    def body(r, carry):
        for g in range(n_slabs):
            s = buf_ref[g]
            mx = jnp.max(s, axis=0, keepdims=True)
            first = jnp.min(jnp.where(s == mx, iota, nrows), axis=0, keepdims=True)
            buf_ref[g] = jnp.where(iota == first, -jnp.inf, s)
            out_ref[g, pl.ds(r, 1), :] = mx
        return carry

    lax.fori_loop(0, P_TOPK, body, 0)


def _peer_score_kernel(x_ref, wq_ref, sk_ref, s1_ref, s2_ref, st_ref, buf, cand, hv, tv):
    tn = x_ref.shape[0]
    q = jnp.dot(x_ref[...].astype(jnp.bfloat16), wq_ref[...],
                preferred_element_type=jnp.float32).astype(jnp.bfloat16)
    for h in range(P_HEADS):
        for c, s_ref in enumerate((s1_ref, s2_ref)):
            col = (2 * h + c) * N_KEYS
            s_t = lax.dot_general(sk_ref[2 * h + c], q[:, col:col + N_KEYS],
                                  (((1,), (1,)), ((), ())), preferred_element_type=jnp.float32)
            s_ref[h] = s_t
            buf[2 * h + c] = s_t
    _extract_top(buf, hv)
    n_cand = sum(P_TOPK // (a + 1) for a in range(P_TOPK))
    for h in range(P_HEADS):
        pieces = [hv[2 * h, a:a + 1, :] + hv[2 * h + 1, 0:P_TOPK // (a + 1), :] for a in range(P_TOPK)]
        pieces.append(jnp.full((cand.shape[1] - n_cand, tn), -jnp.inf, jnp.float32))
        cand[h] = jnp.concatenate(pieces, axis=0)
    _extract_top(cand, tv)
    for h in range(P_HEADS):
        z = jnp.sum(jnp.exp(tv[h] - tv[h, 0:1, :]), axis=0, keepdims=True)
        st_ref[0, h:h + 1, :] = tv[h, P_TOPK - 1:P_TOPK, :]
        st_ref[1, h:h + 1, :] = hv[2 * h, 0:1, :]
        st_ref[2, h:h + 1, :] = hv[2 * h + 1, 0:1, :]
        st_ref[3, h:h + 1, :] = 1.0 / z


def _peer_scores(x, wq_bf, sk_bf, *, tn=256):
    n = x.shape[0]
    tn = min(tn, n)
    return pl.pallas_call(
        _peer_score_kernel,
        grid=(n // tn,),
        in_specs=[pl.BlockSpec((tn, D_MODEL), lambda i: (i, 0)),
                  pl.BlockSpec((D_MODEL, P_HEADS * P_DKEY), lambda i: (0, 0)),
                  pl.BlockSpec((2 * P_HEADS, N_KEYS, P_DKEY // 2), lambda i: (0, 0, 0))],
        out_specs=[pl.BlockSpec((P_HEADS, N_KEYS, tn), lambda i: (0, 0, i)),
                   pl.BlockSpec((P_HEADS, N_KEYS, tn), lambda i: (0, 0, i)),
                   pl.BlockSpec((4, P_HEADS, tn), lambda i: (0, 0, i))],
        out_shape=[jax.ShapeDtypeStruct((P_HEADS, N_KEYS, n), jnp.float32),
                   jax.ShapeDtypeStruct((P_HEADS, N_KEYS, n), jnp.float32),
                   jax.ShapeDtypeStruct((4, P_HEADS, n), jnp.float32)],
        scratch_shapes=[pltpu.VMEM((2 * P_HEADS, N_KEYS, tn), jnp.float32),
                        pltpu.VMEM((P_HEADS, _round_up(sum(P_TOPK // (a + 1) for a in range(P_TOPK)), SUBLANE),
                                    tn),
                                   jnp.float32),
                        pltpu.VMEM((2 * P_HEADS, P_TOPK, tn), jnp.float32),
                        pltpu.VMEM((P_HEADS, P_TOPK, tn), jnp.float32)],
        compiler_params=pltpu.CompilerParams(dimension_semantics=("arbitrary",),
                                             vmem_limit_bytes=_VMEM_LIMIT),
        name="peer_scores",
    )(x, wq_bf, sk_bf)


def _peer_expert_kernel(x_ref, u_ref, v_ref, s1_ref, s2_ref, st_ref, o_ref, e2_scr, h_scr, p_scr,
                        *, chunk):
    j = pl.program_id(1)
    tn = x_ref.shape[0]
    te = u_ref.shape[1]

    @pl.when(j == 0)
    def _():
        o_ref[...] = jnp.zeros_like(o_ref)
        for h in range(P_HEADS):
            e2_scr[h] = jnp.exp(s2_ref[h] - st_ref[2, h:h + 1, :])

    def hidden(c):
        tsl = slice(c * chunk, (c + 1) * chunk)
        h_scr[tsl, :] = jnp.dot(x_ref[tsl, :], u_ref[...], preferred_element_type=jnp.float32)

    a_rows = [[s1_ref[h, pl.ds(j * (te // N_KEYS) + ii, 1), :] for h in range(P_HEADS)]
              for ii in range(te // N_KEYS)]
    e1_rows = [[jnp.exp(a_rows[ii][h] - st_ref[1, h:h + 1, :]) * st_ref[3, h:h + 1, :]
                for h in range(P_HEADS)] for ii in range(te // N_KEYS)]

    def weigh(c):
        for sub in range(chunk // LANE):
            weigh_lanes(slice(c * chunk + sub * LANE, c * chunk + (sub + 1) * LANE))

    def weigh_lanes(tsl):
        for ii in range(te // N_KEYS):
            w_t = jnp.zeros((N_KEYS, LANE), jnp.float32)
            for h in range(P_HEADS):
                val = a_rows[ii][h][:, tsl] + s2_ref[h, :, tsl]
                w_t = w_t + jnp.where(val >= st_ref[0, h:h + 1, tsl],
                                      e1_rows[ii][h][:, tsl] * e2_scr[h, :, tsl], 0.0)
            hh = h_scr[tsl, ii * N_KEYS:(ii + 1) * N_KEYS]
            g = 0.5 * hh * (1.0 + lax.erf(hh * 0.7071067811865476))
            p_scr[tsl, ii * N_KEYS:(ii + 1) * N_KEYS] = (w_t.T * g).astype(jnp.bfloat16)

    def project(c):
        tsl = slice(c * chunk, (c + 1) * chunk)
        o_ref[tsl, :] += jnp.dot(p_scr[tsl, :], v_ref[...], preferred_element_type=jnp.float32)

    n_chunks = tn // chunk
    hidden(0)
    for c in range(n_chunks):
        if c + 1 < n_chunks:
            hidden(c + 1)
        weigh(c)
        project(c)


def _peer_experts(x_bf, u_bf, v_bf, s1t, s2t, stats, *, tn=1024, te=512, chunk=256):
    n = x_bf.shape[0]
    tn = min(tn, n)
    chunk = min(chunk, tn)
    once = pl.Buffered(1)
    return pl.pallas_call(
        functools.partial(_peer_expert_kernel, chunk=chunk),
        grid=(n // tn, N_EXPERTS // te),
        in_specs=[pl.BlockSpec((tn, D_MODEL), lambda i, j: (i, 0), pipeline_mode=once),
                  pl.BlockSpec((D_MODEL, te), lambda i, j: (0, j)),
                  pl.BlockSpec((te, D_MODEL), lambda i, j: (j, 0)),
                  pl.BlockSpec((P_HEADS, N_KEYS, tn), lambda i, j: (0, 0, i), pipeline_mode=once),
                  pl.BlockSpec((P_HEADS, N_KEYS, tn), lambda i, j: (0, 0, i), pipeline_mode=once),
                  pl.BlockSpec((4, P_HEADS, tn), lambda i, j: (0, 0, i), pipeline_mode=once)],
        out_specs=pl.BlockSpec((tn, D_MODEL), lambda i, j: (i, 0)),
        out_shape=jax.ShapeDtypeStruct((n, D_MODEL), jnp.float32),
        scratch_shapes=[pltpu.VMEM((P_HEADS, N_KEYS, tn), jnp.float32),
                        pltpu.VMEM((tn, te), jnp.float32),
                        pltpu.VMEM((tn, te), jnp.bfloat16)],
        compiler_params=pltpu.CompilerParams(dimension_semantics=("arbitrary", "arbitrary"),
                                             vmem_limit_bytes=_VMEM_LIMIT),
        name="peer_experts",
    )(x_bf, u_bf, v_bf, s1t, s2t, stats)


def _peer_ffn(x, wq_bf, sk_bf, u_bf, v_bf):
    B, T, D = x.shape
    n = B * T
    xt = x.reshape(n, D)
    n_pad = _round_up(n, LANE)
    if n_pad != n:
        xt = jnp.pad(xt, ((0, n_pad - n), (0, 0)))
    s1t, s2t, stats = _peer_scores(xt, wq_bf, sk_bf)
    out = _peer_experts(xt.astype(jnp.bfloat16), u_bf, v_bf, s1t, s2t, stats)
    return out[:n].reshape(B, T, D)


NSA_TQ = 256
NSA_GSPLIT = 4


def _flash_tile(q4, k_t, v_t, bias, m, l, acc):
    tq, tk = bias.shape
    s = lax.dot_general(q4, k_t, (((1,), (1,)), ((), ())), preferred_element_type=jnp.float32)
    s = (s.reshape(-1, tq, tk) + bias[None]).reshape(-1, tk)
    m_new = jnp.maximum(m, jnp.max(s, axis=-1, keepdims=True))
    p = jnp.exp(s - m_new)
    alpha = jnp.exp(m - m_new)
    l_new = alpha * l + jnp.sum(p, axis=-1, keepdims=True)
    acc_new = alpha * acc + jnp.dot(p.astype(jnp.bfloat16), v_t, preferred_element_type=jnp.float32)
    return m_new, l_new, acc_new


def _nsa_prompt_kernel(q_ref, gl_ref, kc_ref, vc_ref, ks_ref, vs_ref, kw_ref, vw_ref, cw_ref, ex_ref,
                       o_ref, kcb_scr, vcb_scr, imp_scr, selx_scr, *, seq):
    qt = pl.program_id(2)
    tq = NSA_TQ
    n_cmp = seq // CMP_BLOCK
    n_sel = seq // SEL_BLOCK
    G = NSA_GROUP
    q0 = qt * tq

    @pl.when(qt == 0)
    def _():
        kc = kc_ref[...].reshape(n_cmp, CMP_BLOCK, HEAD_DIM)
        vc = vc_ref[...].reshape(n_cmp, CMP_BLOCK, HEAD_DIM)
        kcb_scr[...] = jnp.zeros_like(kcb_scr)
        vcb_scr[...] = jnp.zeros_like(vcb_scr)
        kcb_scr[0:n_cmp, :] = jnp.sum(kc * cw_ref[0][None], axis=1)
        vcb_scr[0:n_cmp, :] = jnp.sum(vc * cw_ref[1][None], axis=1).astype(jnp.bfloat16)

    qpos_col = q0 + lax.broadcasted_iota(jnp.int32, (tq, 1), 0)
    qf = [q_ref[g] * (HEAD_DIM ** -0.5) for g in range(G)]
    qs = [t.astype(jnp.bfloat16) for t in qf]
    gates = jax.nn.sigmoid(gl_ref[...])

    cmp_end = (lax.broadcasted_iota(jnp.int32, (1, LANE), 1) + 1) * CMP_BLOCK - 1
    cmask = (cmp_end <= qpos_col) & (lax.broadcasted_iota(jnp.int32, (1, LANE), 1) < n_cmp)
    o_cmp = []
    imp = jnp.zeros((tq, LANE), jnp.float32)
    for g in range(G):
        s = lax.dot_general(qf[g], kcb_scr[...], (((1,), (1,)), ((), ())),
                            precision=lax.Precision.HIGHEST, preferred_element_type=jnp.float32)
        s = jnp.where(cmask, s, NEG_INF)
        e = jnp.exp(s - jnp.max(s, axis=-1, keepdims=True))
        p = jnp.where(cmask, e / jnp.sum(e, axis=-1, keepdims=True), 0.0)
        o_cmp.append(jnp.dot(p.astype(jnp.bfloat16), vcb_scr[...], preferred_element_type=jnp.float32))
        imp = imp + p
    imp_t = imp.T
    ratio = SEL_BLOCK // CMP_BLOCK
    parts = []
    for c in range(tq // LANE):
        imp_scr[c] = imp_t[:, c * LANE:(c + 1) * LANE]
        part = imp_scr[c, pl.ds(0, n_sel, stride=ratio), :]
        for r in range(1, ratio):
            part = part + imp_scr[c, pl.ds(r, n_sel, stride=ratio), :]
        parts.append(part)
    imp_s = jnp.concatenate(parts, axis=1)
    qpos_row = q0 + lax.broadcasted_iota(jnp.int32, (1, tq), 1)
    cur = qpos_row // SEL_BLOCK
    blk = lax.broadcasted_iota(jnp.int32, (n_sel, tq), 0)
    forced = (blk == 0) | (blk == cur) | (blk == cur - 1)
    imp_s = jnp.where(blk > cur, -1.0, jnp.where(forced, FORCE_SCORE, imp_s))
    rank = jnp.zeros((n_sel, tq), jnp.int32)
    for mm in range(n_sel):
        row = imp_s[mm:mm + 1, :]
        ahead = (row > imp_s) | ((row == imp_s) & (mm < blk))
        rank = rank + jnp.where(ahead, 1, 0)
    sel = jnp.where((rank < min(SEL_TOPN, n_sel)) & (imp_s >= 0.0), 1.0, 0.0)
    sel_pad = jnp.concatenate([sel, jnp.zeros((LANE - n_sel, tq), jnp.float32)], axis=0)
    sel_q = sel_pad.T.astype(jnp.bfloat16)
    selx_scr[...] = jnp.dot(sel_q, ex_ref[...], preferred_element_type=jnp.float32)

    kpos_l = lax.broadcasted_iota(jnp.int32, (1, tq), 1)
    GS = NSA_GSPLIT
    init = (jnp.full((GS * tq, 1), NEG_INF, jnp.float32), jnp.zeros((GS * tq, 1), jnp.float32),
            jnp.zeros((GS * tq, HEAD_DIM), jnp.float32))

    for g0 in range(0, G, GS):
        q_r = jnp.concatenate(qs[g0:g0 + GS], axis=0)

        def sel_body(kt, carry):
            k0 = pl.multiple_of(kt * tq, tq)
            chosen = jnp.where(selx_scr[:, pl.ds(k0, tq)] > 0.5, 0.0, NEG_INF)
            bias = jnp.where((k0 + kpos_l) <= qpos_col, chosen, NEG_INF)
            return _flash_tile(q_r, ks_ref[pl.ds(k0, tq), :], vs_ref[pl.ds(k0, tq), :], bias, *carry)

        _, l_s, acc_s = lax.fori_loop(0, qt + 1, sel_body, init)
        o_sel = acc_s / l_s

        def win_body(kt, carry):
            k0 = pl.multiple_of(kt * tq, tq)
            kpos = k0 + kpos_l
            band = jnp.where(kpos >= qpos_col - (WINDOW - 1), 0.0, NEG_INF)
            bias = jnp.where(kpos <= qpos_col, band, NEG_INF)
            return _flash_tile(q_r, kw_ref[pl.ds(k0, tq), :], vw_ref[pl.ds(k0, tq), :], bias, *carry)

        _, l_w, acc_w = lax.fori_loop(jnp.maximum(qt - WINDOW // tq, 0), qt + 1, win_body, init)
        o_win = acc_w / l_w

        for gi in range(GS):
            g = g0 + gi
            rows = slice(gi * tq, (gi + 1) * tq)
            o_ref[g] = (gates[:, 3 * g:3 * g + 1] * o_cmp[g] + gates[:, 3 * g + 1:3 * g + 2] * o_sel[rows]
                        + gates[:, 3 * g + 2:3 * g + 3] * o_win[rows])


def _nsa_cols(proj, a, b):
    return proj[..., RWKV_COLS + a:RWKV_COLS + b]


def _nsa_prompt(proj, cmp_w):
    B, T, _ = proj.shape
    KVH, G, HD = NSA_KV_HEADS, NSA_GROUP, HEAD_DIM
    kvc = NSA_KV_COLS
    tq = NSA_TQ
    q = _nsa_cols(proj, 0, D_NSA).reshape(B, T, KVH, G, HD).transpose(0, 2, 3, 1, 4)
    kv = [_nsa_cols(proj, D_NSA + i * kvc, D_NSA + (i + 1) * kvc).reshape(B, T, KVH, HD).transpose(0, 2, 1, 3)
          for i in range(6)]
    kc, vc = kv[0], kv[1]
    ks, vs, kw, vw = [t.astype(jnp.bfloat16) for t in kv[2:]]
    gl = _nsa_cols(proj, D_NSA + 6 * kvc, NSA_COLS).reshape(B, T, KVH, 3 * G).transpose(0, 2, 1, 3)
    cw = jnp.broadcast_to(cmp_w[:, :, None], (2, CMP_BLOCK, HD))
    expand = (jnp.arange(LANE)[:, None] == (jnp.arange(T) // SEL_BLOCK)[None, :]).astype(jnp.bfloat16)
    kv_spec = pl.BlockSpec((None, None, T, HD), lambda b, k, t: (b, k, 0, 0))
    out = pl.pallas_call(
        functools.partial(_nsa_prompt_kernel, seq=T),
        grid=(B, KVH, T // tq),
        in_specs=[pl.BlockSpec((None, None, G, tq, HD), lambda b, k, t: (b, k, 0, t, 0)),
                  pl.BlockSpec((None, None, tq, 3 * G), lambda b, k, t: (b, k, t, 0)),
                  kv_spec, kv_spec, kv_spec, kv_spec, kv_spec, kv_spec,
                  pl.BlockSpec((2, CMP_BLOCK, HD), lambda b, k, t: (0, 0, 0)),
                  pl.BlockSpec((LANE, T), lambda b, k, t: (0, 0))],
        out_specs=pl.BlockSpec((None, None, G, tq, HD), lambda b, k, t: (b, k, 0, t, 0)),
        out_shape=jax.ShapeDtypeStruct((B, KVH, G, T, HD), jnp.float32),
        scratch_shapes=[pltpu.VMEM((LANE, HD), jnp.float32),
                        pltpu.VMEM((LANE, HD), jnp.bfloat16),
                        pltpu.VMEM((tq // LANE, LANE, LANE), jnp.float32),
                        pltpu.VMEM((tq, T), jnp.float32)],
        compiler_params=pltpu.CompilerParams(dimension_semantics=("arbitrary", "arbitrary", "arbitrary"),
                                             vmem_limit_bytes=_VMEM_LIMIT),
        name="nsa_prompt",
    )(q, gl, kc, vc, ks, vs, kw, vw, cw, expand)
    return out.transpose(0, 3, 1, 2, 4).reshape(B, T, D_NSA)


def _nsa_prompt_mix(proj, n_keep, cmp_w):
    B, T, _ = proj.shape
    kv5 = lambda i: _nsa_cols(proj, D_NSA + 2 * i * NSA_KV_COLS, D_NSA + 2 * (i + 1) * NSA_KV_COLS).reshape(
        B, T, 2, NSA_KV_HEADS, HEAD_DIM)
    return _nsa_prompt(proj, cmp_w), kv5(0), kv5(1), kv5(2)[:, -n_keep:]


NSA_PG = 8
NSA_POOL_PG = LANE * CMP_BLOCK // PAGE_SIZE
PAGE_COLS = 2 * NSA_KV_COLS
NSA_ROWS = NSA_KV_HEADS * NSA_GROUP


def _page_specs(layer, n):
    def spec(j):
        return pl.BlockSpec((None, None, 2, NSA_KV_HEADS, HEAD_DIM, PAGE_SIZE),
                            lambda b, s, pt: (layer, pt[b, s * n + j], 0, 0, 0, 0))
    return [spec(j) for j in range(n)]


def _rows_last(t):
    nd = t.ndim
    return jnp.transpose(t, tuple(range(nd - 4)) + (nd - 3, nd - 2, nd - 1, nd - 4))


def _cmp_pool_kernel(pt_ref, *refs):
    pages, cw_ref, o_ref = refs[:NSA_POOL_PG], refs[NSA_POOL_PG], refs[NSA_POOL_PG + 1]
    per_page = PAGE_SIZE // CMP_BLOCK
    row_blk = lax.broadcasted_iota(jnp.int32, (PAGE_SIZE, LANE), 0) // CMP_BLOCK
    col = lax.broadcasted_iota(jnp.int32, (PAGE_SIZE, LANE), 1)
    acc = [jnp.zeros((NSA_KV_COLS, LANE), jnp.float32) for _ in range(2)]
    for j, pg in enumerate(pages):
        seg = jnp.where(col == j * per_page + row_blk, 1.0, 0.0).astype(jnp.bfloat16)
        for kv in range(2):
            xw = pg[kv].reshape(NSA_KV_COLS, PAGE_SIZE) * cw_ref[kv:kv + 1, :]
            hi = xw.astype(jnp.bfloat16)
            lo = (xw - hi.astype(jnp.float32)).astype(jnp.bfloat16)
            acc[kv] = (acc[kv] + jnp.dot(hi, seg, preferred_element_type=jnp.float32)
                       + jnp.dot(lo, seg, preferred_element_type=jnp.float32))
    o_ref[0:NSA_KV_COLS, :] = acc[0]
    o_ref[NSA_KV_COLS:, :] = acc[1]


def _cmp_pool(cache, page_table, layer, cw):
    B, n_pages = page_table.shape
    return pl.pallas_call(
        _cmp_pool_kernel,
        grid_spec=pltpu.PrefetchScalarGridSpec(
            num_scalar_prefetch=1, grid=(B, n_pages // NSA_POOL_PG),
            in_specs=_page_specs(layer, NSA_POOL_PG) + [pl.BlockSpec((2, PAGE_SIZE), lambda b, s, pt: (0, 0))],
            out_specs=pl.BlockSpec((None, PAGE_COLS, LANE), lambda b, s, pt: (b, 0, s))),
        out_shape=jax.ShapeDtypeStruct((B, PAGE_COLS, n_pages * PAGE_SIZE // CMP_BLOCK), jnp.float32),
        compiler_params=pltpu.CompilerParams(dimension_semantics=("arbitrary", "arbitrary"),
                                             vmem_limit_bytes=_VMEM_LIMIT),
        name="nsa_cmp_pool",
    )(page_table, *([cache] * NSA_POOL_PG), cw)


def _cmp_select_kernel(q_ref, kv_ref, ocmp_ref, sel_ref, pt_scr, imp_scr, *, past, steps):
    T = steps
    R = NSA_ROWS * T
    n_cmp = past // CMP_BLOCK
    n_sel = past // SEL_BLOCK + 1
    n_lane = sel_ref.shape[-1]
    ratio = SEL_BLOCK // CMP_BLOCK
    rows_t = lax.broadcasted_iota(jnp.int32, (R, 1), 0) % T
    qpos = past + rows_t
    cmp_end = (lax.broadcasted_iota(jnp.int32, (1, n_cmp), 1) + 1) * CMP_BLOCK - 1
    cmask = cmp_end <= qpos
    q = q_ref[...]
    p_all, o_all = [], []
    for kh in range(NSA_KV_HEADS):
        rows = slice(kh * NSA_GROUP * T, (kh + 1) * NSA_GROUP * T)
        kcb_t = kv_ref[kh * HEAD_DIM:(kh + 1) * HEAD_DIM, :]
        vcb_t = kv_ref[NSA_KV_COLS + kh * HEAD_DIM:NSA_KV_COLS + (kh + 1) * HEAD_DIM, :].astype(jnp.bfloat16)
        s = jnp.dot(q[rows], kcb_t, precision=lax.Precision.HIGHEST, preferred_element_type=jnp.float32)
        s = jnp.where(cmask[rows], s, NEG_INF)
        e = jnp.exp(s - jnp.max(s, axis=-1, keepdims=True))
        p = jnp.where(cmask[rows], e / jnp.sum(e, axis=-1, keepdims=True), 0.0)
        o_all.append(lax.dot_general(p.astype(jnp.bfloat16), vcb_t, (((1,), (1,)), ((), ())),
                                     preferred_element_type=jnp.float32))
        imp = p[0:T]
        for g in range(1, NSA_GROUP):
            imp = imp + p[g * T:(g + 1) * T]
        p_all.extend([imp] * NSA_GROUP)
    ocmp_ref[...] = jnp.concatenate(o_all, axis=0)
    imp_rows = jnp.concatenate(p_all + [jnp.zeros((LANE - R, n_cmp), jnp.float32)], axis=0)
    for c in range(n_cmp // LANE):
        pt_scr[c] = imp_rows[:, c * LANE:(c + 1) * LANE].T
    per = LANE // ratio
    for c in range(n_cmp // LANE):
        part = pt_scr[c, pl.ds(0, per, stride=ratio), :]
        for r in range(1, ratio):
            part = part + pt_scr[c, pl.ds(r, per, stride=ratio), :]
        imp_scr[c * per:(c + 1) * per, :] = part
    n_rows = imp_scr.shape[0]
    imp_scr[n_sel - 1:n_rows, :] = jnp.zeros((n_rows - n_sel + 1, LANE), jnp.float32)
    lane_t = lax.broadcasted_iota(jnp.int32, (1, LANE), 1) % T
    cur = (past + lane_t) // SEL_BLOCK
    blk = lax.broadcasted_iota(jnp.int32, (n_rows, LANE), 0)
    forced = (blk == 0) | (blk == cur) | (blk == cur - 1)
    imp_s = jnp.where(blk > cur, -1.0, jnp.where(forced, FORCE_SCORE, imp_scr[...]))
    imp_s = jnp.where(blk < n_sel, imp_s, -2.0)
    imp_scr[...] = imp_s

    def rank_body(m, rank):
        row = imp_scr[pl.ds(m, 1), :]
        ahead = (row > imp_s) | ((row == imp_s) & (m < blk))
        return rank + jnp.where(ahead, 1, 0)

    rank = lax.fori_loop(0, n_sel, rank_body, jnp.zeros((n_rows, LANE), jnp.int32))
    sel = jnp.where((rank < min(SEL_TOPN, n_sel)) & (imp_s >= 0.0), 1.0, 0.0)
    sel = jnp.concatenate([sel, jnp.zeros((n_lane - n_rows, LANE), jnp.float32)], axis=0)
    for c in range(n_lane // LANE):
        sel_ref[:, c * LANE:(c + 1) * LANE] = sel[c * LANE:(c + 1) * LANE, :].T


def _cmp_select(q_rows, kvcb, past, steps):
    B, R, _ = q_rows.shape
    n_cmp = kvcb.shape[2]
    n_sel = past // SEL_BLOCK + 1
    n_lane = _round_up(n_sel, LANE)
    n_rows = _round_up(n_sel, SUBLANE)
    return pl.pallas_call(
        functools.partial(_cmp_select_kernel, past=past, steps=steps),
        grid=(B,),
        in_specs=[pl.BlockSpec((None, R, HEAD_DIM), lambda b: (b, 0, 0)),
                  pl.BlockSpec((None, PAGE_COLS, n_cmp), lambda b: (b, 0, 0))],
        out_specs=[pl.BlockSpec((None, R, HEAD_DIM), lambda b: (b, 0, 0)),
                   pl.BlockSpec((None, LANE, n_lane), lambda b: (b, 0, 0))],
        out_shape=[jax.ShapeDtypeStruct((B, R, HEAD_DIM), jnp.float32),
                   jax.ShapeDtypeStruct((B, LANE, n_lane), jnp.float32)],
        scratch_shapes=[pltpu.VMEM((n_cmp // LANE, LANE, LANE), jnp.float32),
                        pltpu.VMEM((n_rows, LANE), jnp.float32)],
        compiler_params=pltpu.CompilerParams(dimension_semantics=("arbitrary",),
                                             vmem_limit_bytes=_VMEM_LIMIT),
        name="nsa_cmp_select",
    )(q_rows, kvcb)


def _soft_update(s, mask, v, m, l, acc, keys_last=False):
    s = jnp.where(mask, s, NEG_INF)
    m_new = jnp.maximum(m, jnp.max(s, axis=-1, keepdims=True))
    p = jnp.where(mask, jnp.exp(s - m_new), 0.0)
    alpha = jnp.exp(m - m_new)
    pv = lax.dot_general(p.astype(jnp.bfloat16), v, (((1,), (1 if keys_last else 0,)), ((), ())),
                         preferred_element_type=jnp.float32)
    return m_new, alpha * l + jnp.sum(p, axis=-1, keepdims=True), alpha * acc + pv


def _sel_win_kernel(pt_ref, *refs, past, steps):
    pages = refs[:NSA_PG]
    (q_ref, sel_ref, ocmp_ref, nsel_ref, wpre_ref, nwin_ref, gate_ref,
     o_ref, m_scr, l_scr, acc_scr) = refs[NSA_PG:]
    s_id = pl.program_id(1)
    T = steps
    GT = NSA_GROUP * T
    R = NSA_ROWS * T
    n_lane = sel_ref.shape[-1]
    rows_t = lax.broadcasted_iota(jnp.int32, (GT, 1), 0) % T

    @pl.when(s_id == 0)
    def _():
        m_scr[...] = jnp.full_like(m_scr, NEG_INF)
        l_scr[...] = jnp.zeros_like(l_scr)
        acc_scr[...] = jnp.zeros_like(acc_scr)

    q = q_ref[...].astype(jnp.bfloat16)
    sel_bf = sel_ref[0:R, :].astype(jnp.bfloat16)
    n_keys = NSA_PG * PAGE_SIZE
    blk_id = lax.broadcasted_iota(jnp.int32, (n_lane, n_keys), 0)
    key_blk = s_id * (n_keys // SEL_BLOCK) + lax.broadcasted_iota(jnp.int32, (n_lane, n_keys), 1) // SEL_BLOCK

    def head_cols(x, kh, off):
        return x[:, off + kh * HEAD_DIM:off + (kh + 1) * HEAD_DIM]

    def keys_t(kv, kh):
        return jnp.concatenate([pg[kv, kh] for pg in pages], axis=1).astype(jnp.bfloat16)

    expand = jnp.where(blk_id == key_blk, 1.0, 0.0).astype(jnp.bfloat16)
    chosen = jnp.dot(sel_bf, expand, preferred_element_type=jnp.float32) > 0.5
    head_rows = [slice(kh * GT, (kh + 1) * GT) for kh in range(NSA_KV_HEADS)]
    scores = [jnp.dot(q[rows], keys_t(0, kh), preferred_element_type=jnp.float32)
              for kh, rows in enumerate(head_rows)]
    upd = [_soft_update(scores[kh], chosen[rows], keys_t(1, kh), m_scr[rows], l_scr[rows], acc_scr[rows],
                        keys_last=True) for kh, rows in enumerate(head_rows)]
    for kh, rows in enumerate(head_rows):
        m_scr[rows], l_scr[rows], acc_scr[rows] = upd[kh]

    @pl.when(s_id == pl.num_programs(1) - 1)
    def _():
        new_blk = past // SEL_BLOCK
        tpad = nsel_ref.shape[0]
        jn = lax.broadcasted_iota(jnp.int32, (1, tpad), 1)
        nsel = nsel_ref[...].astype(jnp.bfloat16)
        nwin = nwin_ref[...].astype(jnp.bfloat16)
        jp = lax.broadcasted_iota(jnp.int32, (1, WINDOW), 1)
        gates = jax.nn.sigmoid(gate_ref[...])
        for kh in range(NSA_KV_HEADS):
            rows = slice(kh * GT, (kh + 1) * GT)
            pick = sel_ref[kh * GT:(kh + 1) * GT, new_blk:new_blk + 1] > 0.5
            s = lax.dot_general(q[rows], head_cols(nsel, kh, 0), (((1,), (1,)), ((), ())),
                                preferred_element_type=jnp.float32)
            m, l, acc = _soft_update(s, pick & (jn <= rows_t) & (jn < T), head_cols(nsel, kh, NSA_KV_COLS),
                                     m_scr[rows], l_scr[rows], acc_scr[rows])
            o_sel = acc / l
            init = (jnp.full((GT, 1), NEG_INF, jnp.float32), jnp.zeros((GT, 1), jnp.float32),
                    jnp.zeros((GT, HEAD_DIM), jnp.float32))
            s = jnp.dot(q[rows], wpre_ref[0, kh].astype(jnp.bfloat16), preferred_element_type=jnp.float32)
            st = _soft_update(s, jp > rows_t, wpre_ref[1, kh].astype(jnp.bfloat16), *init, keys_last=True)
            s = lax.dot_general(q[rows], head_cols(nwin, kh, 0), (((1,), (1,)), ((), ())),
                                preferred_element_type=jnp.float32)
            _, l_w, acc_w = _soft_update(s, (jn <= rows_t) & (jn < T), head_cols(nwin, kh, NSA_KV_COLS), *st)
            o_win = acc_w / l_w
            gt = gates[rows]
            o_ref[rows, :] = gt[:, 0:1] * ocmp_ref[rows, :] + gt[:, 1:2] * o_sel + gt[:, 2:3] * o_win


def _sel_win(cache, page_table, layer, q_rows, sel_rows, o_cmp, new_sel, win_prefix, new_win, gate_rows,
             past, steps):
    B, n_pages = page_table.shape
    R = q_rows.shape[1]
    n_lane = sel_rows.shape[-1]
    tpad = new_sel.shape[1]
    per_b = lambda shape: pl.BlockSpec((None,) + shape, lambda b, s, pt: (b,) + (0,) * len(shape))
    return pl.pallas_call(
        functools.partial(_sel_win_kernel, past=past, steps=steps),
        grid_spec=pltpu.PrefetchScalarGridSpec(
            num_scalar_prefetch=1, grid=(B, n_pages // NSA_PG),
            in_specs=_page_specs(layer, NSA_PG) + [
                per_b((R, HEAD_DIM)), per_b((LANE, n_lane)), per_b((R, HEAD_DIM)),
                per_b((tpad, PAGE_COLS)), per_b((2, NSA_KV_HEADS, HEAD_DIM, WINDOW)), per_b((tpad, PAGE_COLS)),
                per_b((R, 3))],
            out_specs=per_b((R, HEAD_DIM)),
            scratch_shapes=[pltpu.VMEM((R, 1), jnp.float32), pltpu.VMEM((R, 1), jnp.float32),
                            pltpu.VMEM((R, HEAD_DIM), jnp.float32)]),
        out_shape=jax.ShapeDtypeStruct((B, R, HEAD_DIM), jnp.float32),
        compiler_params=pltpu.CompilerParams(dimension_semantics=("arbitrary", "arbitrary"),
                                             vmem_limit_bytes=_VMEM_LIMIT),
        name="nsa_sel_win",
    )(page_table, *([cache] * NSA_PG), q_rows, sel_rows, o_cmp, new_sel, win_prefix, new_win, gate_rows)


def _nsa_decode_mix(proj, cache_cmp, cache_sel, page_table, layer, win_prefix, cmp_w):
    B, T, _ = proj.shape
    xn = proj[..., RWKV_COLS:IN_COLS]
    n_pages = page_table.shape[1]
    past = n_pages * PAGE_SIZE
    assert T <= SEL_BLOCK and n_pages % NSA_POOL_PG == 0 and win_prefix.shape[1] == WINDOW
    KVH, G, HD = NSA_KV_HEADS, NSA_GROUP, HEAD_DIM
    kv5 = lambda i: xn[..., D_NSA + 2 * i * NSA_KV_COLS:D_NSA + 2 * (i + 1) * NSA_KV_COLS]
    new_cmp, new_sel, new_win = kv5(0), kv5(1), kv5(2)
    q_rows = (xn[..., :D_NSA] * (HD ** -0.5)).reshape(B, T, KVH, G, HD).transpose(0, 2, 3, 1, 4)
    q_rows = q_rows.reshape(B, KVH * G * T, HD)
    gate_rows = xn[..., D_NSA + 6 * NSA_KV_COLS:].reshape(B, T, KVH, G, 3).transpose(0, 2, 3, 1, 4)
    gate_rows = gate_rows.reshape(B, KVH * G * T, 3)
    cw = jnp.tile(cmp_w, (1, PAGE_SIZE // CMP_BLOCK))
    kvcb = _cmp_pool(_rows_last(cache_cmp), page_table, layer, cw)
    o_cmp, sel_rows = _cmp_select(q_rows, kvcb, past, T)
    tpad = _round_up(T, SUBLANE)
    padt = lambda t: jnp.pad(t, ((0, 0), (0, tpad - T), (0, 0)))
    out = _sel_win(_rows_last(cache_sel), page_table, layer, q_rows, sel_rows, o_cmp, padt(new_sel),
                   _rows_last(win_prefix), padt(new_win), gate_rows, past, T)
    y = out.reshape(B, KVH, G, T, HD).transpose(0, 3, 1, 2, 4).reshape(B, T, D_NSA)
    five = lambda t: t.reshape(B, -1, 2, KVH, HD)
    win_out = jnp.concatenate([win_prefix[:, T:], five(new_win)], axis=1)
    return y, five(new_cmp), five(new_sel), win_out


def _deepnorm(x, h, g, b):
    z = DEEPNORM_ALPHA * x + h
    mu = jnp.mean(z, axis=-1, keepdims=True)
    dev = z - mu
    var = jnp.mean(dev * dev, axis=-1, keepdims=True)
    return dev * lax.rsqrt(var + LN_EPS) * g + b


def _out_proj_kernel(yr_ref, yn_ref, x_ref, w_ref, g_ref, b_ref, o_ref):
    half = yr_ref.shape[1]
    h = (jnp.dot(yr_ref[...].astype(jnp.bfloat16), w_ref[0:half, :], preferred_element_type=jnp.float32)
         + jnp.dot(yn_ref[...].astype(jnp.bfloat16), w_ref[half:, :], preferred_element_type=jnp.float32))
    o_ref[...] = _deepnorm(x_ref[...], h, g_ref[...], b_ref[...])


def _residual_norm_kernel(x_ref, f_ref, g_ref, b_ref, o_ref):
    o_ref[...] = _deepnorm(x_ref[...], f_ref[...], g_ref[...], b_ref[...])


def _residual_norm(x, f, g, b, *, tm=256):
    n, d = x.shape
    tm = min(tm, n)
    rows = pl.BlockSpec((tm, d), lambda i: (i, 0))
    vec = pl.BlockSpec((1, d), lambda i: (0, 0))
    return pl.pallas_call(
        _residual_norm_kernel,
        grid=(n // tm,),
        in_specs=[rows, rows, vec, vec],
        out_specs=rows,
        out_shape=jax.ShapeDtypeStruct((n, d), jnp.float32),
        compiler_params=pltpu.CompilerParams(dimension_semantics=("arbitrary",),
                                             vmem_limit_bytes=_VMEM_LIMIT),
        name="residual_norm",
    )(x, f, g.reshape(1, d), b.reshape(1, d))


def _out_proj_norm(y_r, y_n, x, w_bf, g, b, *, tm=256):
    n, d = x.shape
    tm = min(tm, n)
    half = y_r.shape[1]
    rows = lambda w: pl.BlockSpec((tm, w), lambda i: (i, 0))
    full = lambda shape: pl.BlockSpec(shape, lambda i: (0, 0))
    return pl.pallas_call(
        _out_proj_kernel,
        grid=(n // tm,),
        in_specs=[rows(half), rows(y_n.shape[1]), rows(d), full(w_bf.shape), full((1, d)), full((1, d))],
        out_specs=rows(d),
        out_shape=jax.ShapeDtypeStruct((n, d), jnp.float32),
        compiler_params=pltpu.CompilerParams(dimension_semantics=("arbitrary",),
                                             vmem_limit_bytes=_VMEM_LIMIT),
        name="out_proj_norm",
    )(y_r, y_n, x, w_bf, g.reshape(1, d), b.reshape(1, d))


def _hybrid_layer(x, past, shift_prev, rwkv_s0, n_keep, p):
    B, T, D = x.shape
    proj = _matmul(x.reshape(B * T, D), p['w_in_bf']).reshape(B, T, IN_COLS_PAD)
    y_r, s_T, new_shift = _rwkv_time_mix(proj, shift_prev, rwkv_s0, p)
    if past is None:
        assert T % NSA_TQ == 0 and T >= n_keep
        y_n, new_cmp, new_sel, new_win = _nsa_prompt_mix(proj, n_keep, p['nsa_cmp_w'])
    else:
        assert n_keep == WINDOW
        y_n, new_cmp, new_sel, new_win = _nsa_decode_mix(proj, *past, p['nsa_cmp_w'])
    x = _out_proj_norm(y_r.reshape(B * T, D_RWKV), y_n.reshape(B * T, D_NSA), x.reshape(B * T, D),
                       p['w_out_bf'], p['ln1_g'], p['ln1_b']).reshape(B, T, D)
    f = _peer_ffn(x, p['peer_wq_bf'], p['peer_sk_bf'], p['peer_u_bf'], p['peer_v_bf'])
    x = _residual_norm(x.reshape(B * T, D), f.reshape(B * T, D), p['ln2_g'], p['ln2_b']).reshape(B, T, D)
    return x, (new_cmp, new_sel, new_win, s_T, new_shift)


def kernel(x_prompt, x_sample, cache_cmp_kv, cache_sel_kv, page_table, state_win_kv, state_rwkv,
           state_shift, w_in, rwkv_mu, rwkv_w0, rwkv_w2, rwkv_a0, rwkv_a2, rwkv_g2, rwkv_k_k,
           rwkv_k_a, rwkv_r_k, rwkv_gn_g, rwkv_gn_b, nsa_cmp_w, w_out, ln1_g, ln1_b, peer_wq,
           peer_subkeys, peer_u, peer_v, ln2_g, ln2_b):
    bp = x_prompt.shape[0]
    dt = x_prompt.dtype
    assert page_table.shape[1] * PAGE_SIZE == PAST_LEN
    n_keep = state_win_kv.shape[2]
    zero_shift = jnp.zeros((bp, 1, RWKV_COLS), dt)
    zero_state = jnp.zeros((bp, H_RWKV, HEAD_DIM, HEAD_DIM), dt)
    yp, ys = x_prompt, x_sample
    st_p, st_s = [], []
    for l in range(DEPTH):
        p = {'w_in_bf': jnp.pad(w_in[l].astype(jnp.bfloat16), ((0, 0), (0, IN_COLS_PAD - IN_COLS))),
             'rwkv_mu': rwkv_mu[l], 'rwkv_w0': rwkv_w0[l], 'rwkv_w2': rwkv_w2[l],
             'rwkv_a0': rwkv_a0[l], 'rwkv_a2': rwkv_a2[l], 'rwkv_g2': rwkv_g2[l],
             'rwkv_k_k': rwkv_k_k[l], 'rwkv_k_a': rwkv_k_a[l], 'rwkv_r_k': rwkv_r_k[l],
             'rwkv_gn_g': rwkv_gn_g[l], 'rwkv_gn_b': rwkv_gn_b[l], 'nsa_cmp_w': nsa_cmp_w[l],
             'w_out_bf': w_out[l].astype(jnp.bfloat16), 'ln1_g': ln1_g[l], 'ln1_b': ln1_b[l],
             'peer_wq_bf': peer_wq[l].astype(jnp.bfloat16),
             'peer_sk_bf': peer_subkeys[l].reshape(2 * P_HEADS, N_KEYS, P_DKEY // 2).astype(jnp.bfloat16),
             'peer_u_bf': peer_u[l].astype(jnp.bfloat16).T, 'peer_v_bf': peer_v[l].astype(jnp.bfloat16),
             'ln2_g': ln2_g[l], 'ln2_b': ln2_b[l]}
        yp, sp = _hybrid_layer(yp, None, zero_shift, zero_state, n_keep, p)
        past = (cache_cmp_kv, cache_sel_kv, page_table, l, state_win_kv[l])
        ys, ss = _hybrid_layer(ys, past, state_shift[l], state_rwkv[l], n_keep, p)
        st_p.append(sp)
        st_s.append(ss)
    stk = lambda sts, i: jnp.stack([s[i] for s in sts], axis=0)
    return (yp, ys, stk(st_p, 0), stk(st_p, 1), stk(st_p, 2), stk(st_p, 3), stk(st_p, 4),
            stk(st_s, 0), stk(st_s, 1), stk(st_s, 2), stk(st_s, 3), stk(st_s, 4))
```

```python
import functools
import math

import jax
import jax.numpy as jnp
from jax import lax
from jax.experimental import pallas as pl
from jax.experimental.pallas import tpu as pltpu

D_MODEL = 2048
DEPTH = 2
PAST_LEN = 16384
PAGE_SIZE = 128
HEAD_DIM = 64
D_RWKV = D_MODEL // 2
D_NSA = D_MODEL - D_RWKV
H_RWKV = D_RWKV // HEAD_DIM
H_NSA = D_NSA // HEAD_DIM
NSA_KV_HEADS = 4
NSA_GROUP = H_NSA // NSA_KV_HEADS
NSA_KV_COLS = NSA_KV_HEADS * HEAD_DIM
CMP_BLOCK = 32
SEL_BLOCK = 64
SEL_TOPN = 16
WINDOW = 512
W_LORA = 64
A_LORA = 64
G_LORA = 160
RWKV_COLS = 3 * D_RWKV + W_LORA + A_LORA + G_LORA
NSA_COLS = D_NSA + 6 * NSA_KV_COLS + 3 * H_NSA
IN_COLS = RWKV_COLS + NSA_COLS
P_HEADS = 8
N_KEYS = 128
P_DKEY = 256
P_TOPK = 16
LN_EPS = 1e-5
GN_EPS = 64e-5
DEEPNORM_ALPHA = (2 * DEPTH) ** 0.25
FORCE_SCORE = 1e4
NEG_INF = -1e30

N_EXPERTS = N_KEYS * N_KEYS

LANE = 128
SUBLANE = 8
_VMEM_LIMIT = 56 * 1024 * 1024


def _round_up(x, m):
    return -(-x // m) * m


PROJ_TILE = 1024
IN_COLS_PAD = _round_up(IN_COLS, PROJ_TILE)


def _mm_kernel(x_ref, w_ref, o_ref):
    o_ref[...] = jnp.dot(x_ref[...].astype(jnp.bfloat16), w_ref[...], preferred_element_type=jnp.float32)


def _matmul(x, w_bf):
    m, k = x.shape
    n = w_bf.shape[1]
    tm = min(PROJ_TILE, m)
    tn = PROJ_TILE
    return pl.pallas_call(
        _mm_kernel,
        grid=(m // tm, n // tn),
        in_specs=[pl.BlockSpec((tm, k), lambda i, j: (i, 0)),
                  pl.BlockSpec((k, tn), lambda i, j: (0, j))],
        out_specs=pl.BlockSpec((tm, tn), lambda i, j: (i, j)),
        out_shape=jax.ShapeDtypeStruct((m, n), jnp.float32),
        compiler_params=pltpu.CompilerParams(dimension_semantics=("arbitrary", "arbitrary"),
                                             vmem_limit_bytes=_VMEM_LIMIT),
        name="in_proj",
    )(x, w_bf)


RW_CHUNK = 64
RW_PAIRS = D_RWKV // LANE


def _bdot(a, b):
    return jnp.dot(a.astype(jnp.bfloat16), b.astype(jnp.bfloat16), preferred_element_type=jnp.float32)


def _bdot_nt(a, b):
    return lax.dot_general(a.astype(jnp.bfloat16), b.astype(jnp.bfloat16), (((1,), (1,)), ((), ())),
                           preferred_element_type=jnp.float32)


def _bdot_tn(a, b):
    return lax.dot_general(a.astype(jnp.bfloat16), b.astype(jnp.bfloat16), (((0,), (0,)), ((), ())),
                           preferred_element_type=jnp.float32)


def _head_sum(x, same_head_bf):
    hi = x.astype(jnp.bfloat16)
    lo = (x - hi.astype(jnp.float32)).astype(jnp.bfloat16)
    return (jnp.dot(hi, same_head_bf, preferred_element_type=jnp.float32)
            + jnp.dot(lo, same_head_bf, preferred_element_type=jnp.float32))


def _rwkv_mix_kernel(x_ref, shift_ref, mu_ref, pv_ref, w2_ref, a2_ref, g2_ref, s0_ref, y_ref, st_ref,
                     prev_scr, *, steps):
    c = pl.program_id(1)
    C = RW_CHUNK
    f32 = jnp.float32

    @pl.when(c == 0)
    def _():
        st_ref[...] = s0_ref[...]
        prev_scr[...] = shift_ref[...]

    xr = x_ref[:, 0:RWKV_COLS]
    row_c = lax.broadcasted_iota(jnp.int32, (C, 1), 0)
    prev = jnp.where(row_c == 0, prev_scr[...], pltpu.roll(xr, 1, 0))
    prev_scr[...] = xr[C - 1:C, :]
    xs = xr + mu_ref[...] * (prev - xr)
    o_wl = 3 * D_RWKV
    r_all, k_all, v_all = xs[:, 0:D_RWKV], xs[:, D_RWKV:2 * D_RWKV], xs[:, 2 * D_RWKV:o_wl]
    wl = xs[:, o_wl:o_wl + W_LORA]
    al = xs[:, o_wl + W_LORA:o_wl + W_LORA + A_LORA]
    gl = xs[:, o_wl + W_LORA + A_LORA:RWKV_COLS]
    w0, a0_, k_k, k_a = pv_ref[0:1, :], pv_ref[1:2, :], pv_ref[2:3, :], pv_ref[3:4, :]
    r_k, gn_g, gn_b = pv_ref[4:5, :], pv_ref[5:6, :], pv_ref[6:7, :]
    w_all = -jax.nn.softplus(-(w0 + _bdot(jnp.tanh(wl), w2_ref[...]))) - 0.5
    lw_all = -jnp.exp(w_all)
    a_all = jax.nn.sigmoid(a0_ + _bdot(al, a2_ref[...]))
    gate_all = _bdot(jax.nn.sigmoid(gl), g2_ref[...])
    kkf_all = k_all * k_k
    k2_all = k_all * (1.0 + (a_all - 1.0) * k_a)
    if steps % C:
        live = (c * C + row_c) < steps
        lw_all = jnp.where(live, lw_all, 0.0)
        kkf_all = jnp.where(live, kkf_all, 0.0)
        k2_all = jnp.where(live, k2_all, 0.0)
        v_all = jnp.where(live, v_all, 0.0)

    row = lax.broadcasted_iota(jnp.int32, (C, C), 0)
    col = lax.broadcasted_iota(jnp.int32, (C, C), 1)
    tri_incl = col <= row
    tri_strict = col < row
    tri_f = jnp.where(tri_incl, 1.0, 0.0).astype(f32)
    eye = jnp.where(row == col, 1.0, 0.0).astype(f32)
    lane = lax.broadcasted_iota(jnp.int32, (1, LANE), 1)
    head0 = lane < HEAD_DIM
    r128 = lax.broadcasted_iota(jnp.int32, (LANE, LANE), 0)
    c128 = lax.broadcasted_iota(jnp.int32, (LANE, LANE), 1)
    same_head = (r128 < HEAD_DIM) == (c128 < HEAD_DIM)
    same_head_bf = jnp.where(same_head, 1.0, 0.0).astype(jnp.bfloat16)
    eye128 = r128 == c128

    pairs = range(RW_PAIRS)
    heads = (head0, ~head0)
    sls = [slice(pr * LANE, (pr + 1) * LANE) for pr in pairs]
    lw = [lw_all[:, sl] for sl in sls]
    cum = [jnp.dot(tri_f, lw[pr], precision=lax.Precision.HIGHEST, preferred_element_type=f32)
           for pr in pairs]
    tot = [cum[pr][C - 1:C, :] for pr in pairs]
    v = [v_all[:, sl] for sl in sls]
    k = [k2_all[:, sl] for sl in sls]
    r = [r_all[:, sl] for sl in sls]
    kkf = [kkf_all[:, sl] for sl in sls]
    ss = [_head_sum(kkf[pr] * kkf[pr], same_head_bf) for pr in pairs]
    kk = [kkf[pr] * lax.rsqrt(jnp.maximum(ss[pr], 1e-24)) for pr in pairs]
    b = [kk[pr] * a_all[:, sls[pr]] for pr in pairs]
    x, ym = [], []
    for pr in pairs:
        e_neg = jnp.exp(-cum[pr])
        x.append(jnp.concatenate([kk[pr] * jnp.exp(cum[pr] - lw[pr]),
                                  r[pr] * jnp.exp(cum[pr])], axis=0))
        ym.append(jnp.concatenate([k[pr] * e_neg, b[pr] * e_neg], axis=0))
    a0 = [st_ref[pr] for pr in pairs]
    xa = [_bdot(x[pr], a0[pr]) for pr in pairs]
    g = [[_bdot_nt(jnp.where(hm, x[pr], 0.0), ym[pr]) for hm in heads] for pr in pairs]
    lkv = [[_bdot(jnp.where(tri_strict, g[pr][h][:C, :C], 0.0), v[pr]) for h in range(2)] for pr in pairs]
    npow = [[jnp.where(tri_strict, -g[pr][h][:C, C:], 0.0) for h in range(2)] for pr in pairs]
    tmat = [[eye + npow[pr][h] for h in range(2)] for pr in pairs]
    for _ in range(int(math.log2(C)) - 1):
        npow = [[_bdot(npow[pr][h], npow[pr][h]) for h in range(2)] for pr in pairs]
        tmat = [[tmat[pr][h] + _bdot(tmat[pr][h], npow[pr][h]) for h in range(2)] for pr in pairs]
    rhs = [xa[pr][:C] + jnp.where(head0, lkv[pr][0], lkv[pr][1]) for pr in pairs]
    w = [jnp.where(head0, _bdot(tmat[pr][0], rhs[pr]), _bdot(tmat[pr][1], rhs[pr])) for pr in pairs]
    vw = [jnp.concatenate([v[pr], w[pr]], axis=0) for pr in pairs]
    y = []
    for pr in pairs:
        mr = [jnp.concatenate([jnp.where(tri_incl, g[pr][h][C:, :C], 0.0),
                               jnp.where(tri_incl, -g[pr][h][C:, C:], 0.0)], axis=1) for h in range(2)]
        y.append(xa[pr][C:] + jnp.where(head0, _bdot(mr[0], vw[pr]), _bdot(mr[1], vw[pr])))
    inv_hd = 1.0 / HEAD_DIM
    mean = [_head_sum(y[pr], same_head_bf) * inv_hd for pr in pairs]
    dev = [y[pr] - mean[pr] for pr in pairs]
    var = [_head_sum(dev[pr] * dev[pr], same_head_bf) * inv_hd for pr in pairs]
    rk = [_head_sum(r[pr] * k[pr] * r_k[:, sls[pr]], same_head_bf) for pr in pairs]
    for pr in pairs:
        yn = dev[pr] * lax.rsqrt(var[pr] + GN_EPS) * gn_g[:, sls[pr]] + gn_b[:, sls[pr]]
        y_ref[:, sls[pr]] = (yn + rk[pr] * v[pr]) * gate_all[:, sls[pr]]
    for pr in pairs:
        e_rem = jnp.exp(tot[pr] - cum[pr])
        kb = jnp.concatenate([k[pr] * e_rem, -(b[pr] * e_rem)], axis=0)
        upd = _bdot_tn(kb, vw[pr])
        p_col = jnp.sum(jnp.where(eye128, jnp.exp(tot[pr]), 0.0), axis=1, keepdims=True)
        st_ref[pr] = a0[pr] * p_col + jnp.where(same_head, upd, 0.0)


def _rwkv_time_mix(proj, shift_prev, s0, p):
    B, T, W = proj.shape
    C = RW_CHUNK
    t_pad = _round_up(T, C)
    x = proj if t_pad == T else jnp.pad(proj, ((0, 0), (0, t_pad - T), (0, 0)))
    a = jnp.swapaxes(s0, -1, -2).reshape(B, RW_PAIRS, 2, HEAD_DIM, HEAD_DIM)
    z = jnp.zeros_like(a[:, :, 0])
    a0 = jnp.concatenate([jnp.concatenate([a[:, :, 0], z], axis=-1),
                          jnp.concatenate([z, a[:, :, 1]], axis=-1)], axis=-2)
    zero = jnp.zeros((D_RWKV,), jnp.float32)
    pv = jnp.stack([p['rwkv_w0'], p['rwkv_a0'], p['rwkv_k_k'], p['rwkv_k_a'], p['rwkv_r_k'].reshape(-1),
                    p['rwkv_gn_g'], p['rwkv_gn_b'], zero])
    bf = lambda t: t.astype(jnp.bfloat16)
    full = lambda shape: pl.BlockSpec(shape, lambda bi, ci: (0,) * len(shape))
    st_spec = pl.BlockSpec((None, RW_PAIRS, LANE, LANE), lambda bi, ci: (bi, 0, 0, 0))
    y, st = pl.pallas_call(
        functools.partial(_rwkv_mix_kernel, steps=T),
        grid=(B, t_pad // C),
        in_specs=[pl.BlockSpec((None, C, W), lambda bi, ci: (bi, ci, 0)),
                  pl.BlockSpec((None, 1, RWKV_COLS), lambda bi, ci: (bi, 0, 0)),
                  full((1, RWKV_COLS)), full((8, D_RWKV)), full((W_LORA, D_RWKV)),
                  full((A_LORA, D_RWKV)), full((G_LORA, D_RWKV)), st_spec],
        out_specs=[pl.BlockSpec((None, C, D_RWKV), lambda bi, ci: (bi, ci, 0)), st_spec],
        out_shape=[jax.ShapeDtypeStruct((B, t_pad, D_RWKV), jnp.float32),
                   jax.ShapeDtypeStruct((B, RW_PAIRS, LANE, LANE), jnp.float32)],
        scratch_shapes=[pltpu.VMEM((1, RWKV_COLS), jnp.float32)],
        compiler_params=pltpu.CompilerParams(dimension_semantics=("arbitrary", "arbitrary"),
                                             vmem_limit_bytes=_VMEM_LIMIT),
        name="rwkv_mix",
    )(x, shift_prev, p['rwkv_mu'].reshape(1, RWKV_COLS), pv, bf(p['rwkv_w2']), bf(p['rwkv_a2']),
      bf(p['rwkv_g2']), a0)
    s_t = jnp.stack([st[:, :, :HEAD_DIM, :HEAD_DIM], st[:, :, HEAD_DIM:, HEAD_DIM:]], axis=2)
    s_t = jnp.swapaxes(s_t.reshape(B, H_RWKV, HEAD_DIM, HEAD_DIM), -1, -2)
    return y[:, :T], s_t, proj[:, T - 1:T, :RWKV_COLS]


def _extract_top(buf_ref, out_ref):
    n_slabs, nrows, width = buf_ref.shape
    iota = lax.broadcasted_iota(jnp.int32, (nrows, width), 0)

    def body(r, carry):
        for g in range(n_slabs):
            s = buf_ref[g]
            mx = jnp.max(s, axis=0, keepdims=True)
            first = jnp.min(jnp.where(s == mx, iota, nrows), axis=0, keepdims=True)
            buf_ref[g] = jnp.where(iota == first, -jnp.inf, s)
            out_ref[g, pl.ds(r, 1), :] = mx
        return carry

    lax.fori_loop(0, P_TOPK, body, 0)


def _peer_score_kernel(x_ref, wq_ref, sk_ref, s1_ref, s2_ref, st_ref, buf, cand, hv, tv):
    tn = x_ref.shape[0]
    q = jnp.dot(x_ref[...].astype(jnp.bfloat16), wq_ref[...],
                preferred_element_type=jnp.float32).astype(jnp.bfloat16)
    for h in range(P_HEADS):
        for c, s_ref in enumerate((s1_ref, s2_ref)):
            col = (2 * h + c) * N_KEYS
            s_t = lax.dot_general(sk_ref[2 * h + c], q[:, col:col + N_KEYS],
                                  (((1,), (1,)), ((), ())), preferred_element_type=jnp.float32)
            s_ref[h] = s_t
            buf[2 * h + c] = s_t
    _extract_top(buf, hv)
    n_cand = sum(P_TOPK // (a + 1) for a in range(P_TOPK))
    for h in range(P_HEADS):
        pieces = [hv[2 * h, a:a + 1, :] + hv[2 * h + 1, 0:P_TOPK // (a + 1), :] for a in range(P_TOPK)]
        pieces.append(jnp.full((cand.shape[1] - n_cand, tn), -jnp.inf, jnp.float32))
        cand[h] = jnp.concatenate(pieces, axis=0)
    _extract_top(cand, tv)
    for h in range(P_HEADS):
        z = jnp.sum(jnp.exp(tv[h] - tv[h, 0:1, :]), axis=0, keepdims=True)
        st_ref[0, h:h + 1, :] = tv[h, P_TOPK - 1:P_TOPK, :]
        st_ref[1, h:h + 1, :] = hv[2 * h, 0:1, :]
        st_ref[2, h:h + 1, :] = hv[2 * h + 1, 0:1, :]
        st_ref[3, h:h + 1, :] = 1.0 / z


def _peer_scores(x, wq_bf, sk_bf, *, tn=256):
    n = x.shape[0]
    tn = min(tn, n)
    return pl.pallas_call(
        _peer_score_kernel,
        grid=(n // tn,),
        in_specs=[pl.BlockSpec((tn, D_MODEL), lambda i: (i, 0)),
                  pl.BlockSpec((D_MODEL, P_HEADS * P_DKEY), lambda i: (0, 0)),
                  pl.BlockSpec((2 * P_HEADS, N_KEYS, P_DKEY // 2), lambda i: (0, 0, 0))],
        out_specs=[pl.BlockSpec((P_HEADS, N_KEYS, tn), lambda i: (0, 0, i)),
                   pl.BlockSpec((P_HEADS, N_KEYS, tn), lambda i: (0, 0, i)),
                   pl.BlockSpec((4, P_HEADS, tn), lambda i: (0, 0, i))],
        out_shape=[jax.ShapeDtypeStruct((P_HEADS, N_KEYS, n), jnp.float32),
                   jax.ShapeDtypeStruct((P_HEADS, N_KEYS, n), jnp.float32),
                   jax.ShapeDtypeStruct((4, P_HEADS, n), jnp.float32)],
        scratch_shapes=[pltpu.VMEM((2 * P_HEADS, N_KEYS, tn), jnp.float32),
                        pltpu.VMEM((P_HEADS, _round_up(sum(P_TOPK // (a + 1) for a in range(P_TOPK)), SUBLANE),
                                    tn),
                                   jnp.float32),
                        pltpu.VMEM((2 * P_HEADS, P_TOPK, tn), jnp.float32),
                        pltpu.VMEM((P_HEADS, P_TOPK, tn), jnp.float32)],
        compiler_params=pltpu.CompilerParams(dimension_semantics=("arbitrary",),
                                             vmem_limit_bytes=_VMEM_LIMIT),
        name="peer_scores",
    )(x, wq_bf, sk_bf)


def _peer_expert_kernel(x_ref, u_ref, v_ref, s1_ref, s2_ref, st_ref, o_ref, e2_scr, h_scr, p_scr,
                        *, chunk):
    j = pl.program_id(1)
    tn = x_ref.shape[0]
    te = u_ref.shape[1]

    @pl.when(j == 0)
    def _():
        o_ref[...] = jnp.zeros_like(o_ref)
        for h in range(P_HEADS):
            e2_scr[h] = jnp.exp(s2_ref[h] - st_ref[2, h:h + 1, :])

    def hidden(c):
        tsl = slice(c * chunk, (c + 1) * chunk)
        h_scr[tsl, :] = jnp.dot(x_ref[tsl, :], u_ref[...], preferred_element_type=jnp.float32)

    a_rows = [[s1_ref[h, pl.ds(j * (te // N_KEYS) + ii, 1), :] for h in range(P_HEADS)]
              for ii in range(te // N_KEYS)]
    e1_rows = [[jnp.exp(a_rows[ii][h] - st_ref[1, h:h + 1, :]) * st_ref[3, h:h + 1, :]
                for h in range(P_HEADS)] for ii in range(te // N_KEYS)]

    def weigh(c):
        for sub in range(chunk // LANE):
            weigh_lanes(slice(c * chunk + sub * LANE, c * chunk + (sub + 1) * LANE))

    def weigh_lanes(tsl):
        for ii in range(te // N_KEYS):
            w_t = jnp.zeros((N_KEYS, LANE), jnp.float32)
            for h in range(P_HEADS):
                val = a_rows[ii][h][:, tsl] + s2_ref[h, :, tsl]
                w_t = w_t + jnp.where(val >= st_ref[0, h:h + 1, tsl],
                                      e1_rows[ii][h][:, tsl] * e2_scr[h, :, tsl], 0.0)
            hh = h_scr[tsl, ii * N_KEYS:(ii + 1) * N_KEYS]
            g = 0.5 * hh * (1.0 + lax.erf(hh * 0.7071067811865476))
            p_scr[tsl, ii * N_KEYS:(ii + 1) * N_KEYS] = (w_t.T * g).astype(jnp.bfloat16)

    def project(c):
        tsl = slice(c * chunk, (c + 1) * chunk)
        o_ref[tsl, :] += jnp.dot(p_scr[tsl, :], v_ref[...], preferred_element_type=jnp.float32)

    n_chunks = tn // chunk
    hidden(0)
    for c in range(n_chunks):
        if c + 1 < n_chunks:
            hidden(c + 1)
        weigh(c)
        project(c)


def _peer_experts(x_bf, u_bf, v_bf, s1t, s2t, stats, *, tn=1024, te=512, chunk=256):
    n = x_bf.shape[0]
    tn = min(tn, n)
    chunk = min(chunk, tn)
    once = pl.Buffered(1)
    return pl.pallas_call(
        functools.partial(_peer_expert_kernel, chunk=chunk),
        grid=(n // tn, N_EXPERTS // te),
        in_specs=[pl.BlockSpec((tn, D_MODEL), lambda i, j: (i, 0), pipeline_mode=once),
                  pl.BlockSpec((D_MODEL, te), lambda i, j: (0, j)),
                  pl.BlockSpec((te, D_MODEL), lambda i, j: (j, 0)),
                  pl.BlockSpec((P_HEADS, N_KEYS, tn), lambda i, j: (0, 0, i), pipeline_mode=once),
                  pl.BlockSpec((P_HEADS, N_KEYS, tn), lambda i, j: (0, 0, i), pipeline_mode=once),
                  pl.BlockSpec((4, P_HEADS, tn), lambda i, j: (0, 0, i), pipeline_mode=once)],
        out_specs=pl.BlockSpec((tn, D_MODEL), lambda i, j: (i, 0)),
        out_shape=jax.ShapeDtypeStruct((n, D_MODEL), jnp.float32),
        scratch_shapes=[pltpu.VMEM((P_HEADS, N_KEYS, tn), jnp.float32),
                        pltpu.VMEM((tn, te), jnp.float32),
                        pltpu.VMEM((tn, te), jnp.bfloat16)],
        compiler_params=pltpu.CompilerParams(dimension_semantics=("arbitrary", "arbitrary"),
                                             vmem_limit_bytes=_VMEM_LIMIT),
        name="peer_experts",
    )(x_bf, u_bf, v_bf, s1t, s2t, stats)


def _peer_ffn(x, wq_bf, sk_bf, u_bf, v_bf):
    B, T, D = x.shape
    n = B * T
    xt = x.reshape(n, D)
    n_pad = _round_up(n, LANE)
    if n_pad != n:
        xt = jnp.pad(xt, ((0, n_pad - n), (0, 0)))
    s1t, s2t, stats = _peer_scores(xt, wq_bf, sk_bf)
    out = _peer_experts(xt.astype(jnp.bfloat16), u_bf, v_bf, s1t, s2t, stats)
    return out[:n].reshape(B, T, D)


NSA_TQ = 256
NSA_GSPLIT = 4


def _flash_tile(q4, k_t, v_t, bias, m, l, acc):
    tq, tk = bias.shape
    s = lax.dot_general(q4, k_t, (((1,), (1,)), ((), ())), preferred_element_type=jnp.float32)
    s = (s.reshape(-1, tq, tk) + bias[None]).reshape(-1, tk)
    m_new = jnp.maximum(m, jnp.max(s, axis=-1, keepdims=True))
    p = jnp.exp(s - m_new)
    alpha = jnp.exp(m - m_new)
    l_new = alpha * l + jnp.sum(p, axis=-1, keepdims=True)
    acc_new = alpha * acc + jnp.dot(p.astype(jnp.bfloat16), v_t, preferred_element_type=jnp.float32)
    return m_new, l_new, acc_new


def _nsa_prompt_kernel(q_ref, gl_ref, kc_ref, vc_ref, ks_ref, vs_ref, kw_ref, vw_ref, cw_ref, ex_ref,
                       o_ref, kcb_scr, vcb_scr, imp_scr, selx_scr, *, seq):
    qt = pl.program_id(2)
    tq = NSA_TQ
    n_cmp = seq // CMP_BLOCK
    n_sel = seq // SEL_BLOCK
    G = NSA_GROUP
    q0 = qt * tq

    @pl.when(qt == 0)
    def _():
        kc = kc_ref[...].reshape(n_cmp, CMP_BLOCK, HEAD_DIM)
        vc = vc_ref[...].reshape(n_cmp, CMP_BLOCK, HEAD_DIM)
        kcb_scr[...] = jnp.zeros_like(kcb_scr)
        vcb_scr[...] = jnp.zeros_like(vcb_scr)
        kcb_scr[0:n_cmp, :] = jnp.sum(kc * cw_ref[0][None], axis=1)
        vcb_scr[0:n_cmp, :] = jnp.sum(vc * cw_ref[1][None], axis=1).astype(jnp.bfloat16)

    qpos_col = q0 + lax.broadcasted_iota(jnp.int32, (tq, 1), 0)
    qf = [q_ref[g] * (HEAD_DIM ** -0.5) for g in range(G)]
    qs = [t.astype(jnp.bfloat16) for t in qf]
    gates = jax.nn.sigmoid(gl_ref[...])

    cmp_end = (lax.broadcasted_iota(jnp.int32, (1, LANE), 1) + 1) * CMP_BLOCK - 1
    cmask = (cmp_end <= qpos_col) & (lax.broadcasted_iota(jnp.int32, (1, LANE), 1) < n_cmp)
    o_cmp = []
    imp = jnp.zeros((tq, LANE), jnp.float32)
    for g in range(G):
        s = lax.dot_general(qf[g], kcb_scr[...], (((1,), (1,)), ((), ())),
                            precision=lax.Precision.HIGHEST, preferred_element_type=jnp.float32)
        s = jnp.where(cmask, s, NEG_INF)
        e = jnp.exp(s - jnp.max(s, axis=-1, keepdims=True))
        p = jnp.where(cmask, e / jnp.sum(e, axis=-1, keepdims=True), 0.0)
        o_cmp.append(jnp.dot(p.astype(jnp.bfloat16), vcb_scr[...], preferred_element_type=jnp.float32))
        imp = imp + p
    imp_t = imp.T
    ratio = SEL_BLOCK // CMP_BLOCK
    parts = []
    for c in range(tq // LANE):
        imp_scr[c] = imp_t[:, c * LANE:(c + 1) * LANE]
        part = imp_scr[c, pl.ds(0, n_sel, stride=ratio), :]
        for r in range(1, ratio):
            part = part + imp_scr[c, pl.ds(r, n_sel, stride=ratio), :]
        parts.append(part)
    imp_s = jnp.concatenate(parts, axis=1)
    qpos_row = q0 + lax.broadcasted_iota(jnp.int32, (1, tq), 1)
    cur = qpos_row // SEL_BLOCK
    blk = lax.broadcasted_iota(jnp.int32, (n_sel, tq), 0)
    forced = (blk == 0) | (blk == cur) | (blk == cur - 1)
    imp_s = jnp.where(blk > cur, -1.0, jnp.where(forced, FORCE_SCORE, imp_s))
    rank = jnp.zeros((n_sel, tq), jnp.int32)
    for mm in range(n_sel):
        row = imp_s[mm:mm + 1, :]
        ahead = (row > imp_s) | ((row == imp_s) & (mm < blk))
        rank = rank + jnp.where(ahead, 1, 0)
    sel = jnp.where((rank < min(SEL_TOPN, n_sel)) & (imp_s >= 0.0), 1.0, 0.0)
    sel_pad = jnp.concatenate([sel, jnp.zeros((LANE - n_sel, tq), jnp.float32)], axis=0)
    sel_q = sel_pad.T.astype(jnp.bfloat16)
    selx_scr[...] = jnp.dot(sel_q, ex_ref[...], preferred_element_type=jnp.float32)

    kpos_l = lax.broadcasted_iota(jnp.int32, (1, tq), 1)
    GS = NSA_GSPLIT
    init = (jnp.full((GS * tq, 1), NEG_INF, jnp.float32), jnp.zeros((GS * tq, 1), jnp.float32),
            jnp.zeros((GS * tq, HEAD_DIM), jnp.float32))

    for g0 in range(0, G, GS):
        q_r = jnp.concatenate(qs[g0:g0 + GS], axis=0)

        def sel_body(kt, carry):
            k0 = pl.multiple_of(kt * tq, tq)
            chosen = jnp.where(selx_scr[:, pl.ds(k0, tq)] > 0.5, 0.0, NEG_INF)
            bias = jnp.where((k0 + kpos_l) <= qpos_col, chosen, NEG_INF)
            return _flash_tile(q_r, ks_ref[pl.ds(k0, tq), :], vs_ref[pl.ds(k0, tq), :], bias, *carry)

        _, l_s, acc_s = lax.fori_loop(0, qt + 1, sel_body, init)
        o_sel = acc_s / l_s

        def win_body(kt, carry):
            k0 = pl.multiple_of(kt * tq, tq)
            kpos = k0 + kpos_l
            band = jnp.where(kpos >= qpos_col - (WINDOW - 1), 0.0, NEG_INF)
            bias = jnp.where(kpos <= qpos_col, band, NEG_INF)
            return _flash_tile(q_r, kw_ref[pl.ds(k0, tq), :], vw_ref[pl.ds(k0, tq), :], bias, *carry)

        _, l_w, acc_w = lax.fori_loop(jnp.maximum(qt - WINDOW // tq, 0), qt + 1, win_body, init)
        o_win = acc_w / l_w

        for gi in range(GS):
            g = g0 + gi
            rows = slice(gi * tq, (gi + 1) * tq)
            o_ref[g] = (gates[:, 3 * g:3 * g + 1] * o_cmp[g] + gates[:, 3 * g + 1:3 * g + 2] * o_sel[rows]
                        + gates[:, 3 * g + 2:3 * g + 3] * o_win[rows])


def _nsa_cols(proj, a, b):
    return proj[..., RWKV_COLS + a:RWKV_COLS + b]


def _nsa_prompt(proj, cmp_w):
    B, T, _ = proj.shape
    KVH, G, HD = NSA_KV_HEADS, NSA_GROUP, HEAD_DIM
    kvc = NSA_KV_COLS
    tq = NSA_TQ
    q = _nsa_cols(proj, 0, D_NSA).reshape(B, T, KVH, G, HD).transpose(0, 2, 3, 1, 4)
    kv = [_nsa_cols(proj, D_NSA + i * kvc, D_NSA + (i + 1) * kvc).reshape(B, T, KVH, HD).transpose(0, 2, 1, 3)
          for i in range(6)]
    kc, vc = kv[0], kv[1]
    ks, vs, kw, vw = [t.astype(jnp.bfloat16) for t in kv[2:]]
    gl = _nsa_cols(proj, D_NSA + 6 * kvc, NSA_COLS).reshape(B, T, KVH, 3 * G).transpose(0, 2, 1, 3)
    cw = jnp.broadcast_to(cmp_w[:, :, None], (2, CMP_BLOCK, HD))
    expand = (jnp.arange(LANE)[:, None] == (jnp.arange(T) // SEL_BLOCK)[None, :]).astype(jnp.bfloat16)
    kv_spec = pl.BlockSpec((None, None, T, HD), lambda b, k, t: (b, k, 0, 0))
    out = pl.pallas_call(
        functools.partial(_nsa_prompt_kernel, seq=T),
        grid=(B, KVH, T // tq),
        in_specs=[pl.BlockSpec((None, None, G, tq, HD), lambda b, k, t: (b, k, 0, t, 0)),
                  pl.BlockSpec((None, None, tq, 3 * G), lambda b, k, t: (b, k, t, 0)),
                  kv_spec, kv_spec, kv_spec, kv_spec, kv_spec, kv_spec,
                  pl.BlockSpec((2, CMP_BLOCK, HD), lambda b, k, t: (0, 0, 0)),
                  pl.BlockSpec((LANE, T), lambda b, k, t: (0, 0))],
        out_specs=pl.BlockSpec((None, None, G, tq, HD), lambda b, k, t: (b, k, 0, t, 0)),
        out_shape=jax.ShapeDtypeStruct((B, KVH, G, T, HD), jnp.float32),
        scratch_shapes=[pltpu.VMEM((LANE, HD), jnp.float32),
                        pltpu.VMEM((LANE, HD), jnp.bfloat16),
                        pltpu.VMEM((tq // LANE, LANE, LANE), jnp.float32),
                        pltpu.VMEM((tq, T), jnp.float32)],
        compiler_params=pltpu.CompilerParams(dimension_semantics=("arbitrary", "arbitrary", "arbitrary"),
                                             vmem_limit_bytes=_VMEM_LIMIT),
        name="nsa_prompt",
    )(q, gl, kc, vc, ks, vs, kw, vw, cw, expand)
    return out.transpose(0, 3, 1, 2, 4).reshape(B, T, D_NSA)


def _nsa_prompt_mix(proj, n_keep, cmp_w):
    B, T, _ = proj.shape
    kv5 = lambda i: _nsa_cols(proj, D_NSA + 2 * i * NSA_KV_COLS, D_NSA + 2 * (i + 1) * NSA_KV_COLS).reshape(
        B, T, 2, NSA_KV_HEADS, HEAD_DIM)
    return _nsa_prompt(proj, cmp_w), kv5(0), kv5(1), kv5(2)[:, -n_keep:]


NSA_PG = 8
NSA_POOL_PG = LANE * CMP_BLOCK // PAGE_SIZE
PAGE_COLS = 2 * NSA_KV_COLS
NSA_ROWS = NSA_KV_HEADS * NSA_GROUP


def _page_specs(layer, n):
    def spec(j):
        return pl.BlockSpec((None, None, 2, NSA_KV_HEADS, HEAD_DIM, PAGE_SIZE),
                            lambda b, s, pt: (layer, pt[b, s * n + j], 0, 0, 0, 0))
    return [spec(j) for j in range(n)]


def _rows_last(t):
    nd = t.ndim
    return jnp.transpose(t, tuple(range(nd - 4)) + (nd - 3, nd - 2, nd - 1, nd - 4))


def _cmp_pool_kernel(pt_ref, *refs):
    pages, cw_ref, o_ref = refs[:NSA_POOL_PG], refs[NSA_POOL_PG], refs[NSA_POOL_PG + 1]
    per_page = PAGE_SIZE // CMP_BLOCK
    row_blk = lax.broadcasted_iota(jnp.int32, (PAGE_SIZE, LANE), 0) // CMP_BLOCK
    col = lax.broadcasted_iota(jnp.int32, (PAGE_SIZE, LANE), 1)
    acc = [jnp.zeros((NSA_KV_COLS, LANE), jnp.float32) for _ in range(2)]
    for j, pg in enumerate(pages):
        seg = jnp.where(col == j * per_page + row_blk, 1.0, 0.0).astype(jnp.bfloat16)
        for kv in range(2):
            xw = pg[kv].reshape(NSA_KV_COLS, PAGE_SIZE) * cw_ref[kv:kv + 1, :]
            hi = xw.astype(jnp.bfloat16)
            lo = (xw - hi.astype(jnp.float32)).astype(jnp.bfloat16)
            acc[kv] = (acc[kv] + jnp.dot(hi, seg, preferred_element_type=jnp.float32)
                       + jnp.dot(lo, seg, preferred_element_type=jnp.float32))
    o_ref[0:NSA_KV_COLS, :] = acc[0]
    o_ref[NSA_KV_COLS:, :] = acc[1]


def _cmp_pool(cache, page_table, layer, cw):
    B, n_pages = page_table.shape
    return pl.pallas_call(
        _cmp_pool_kernel,
        grid_spec=pltpu.PrefetchScalarGridSpec(
            num_scalar_prefetch=1, grid=(B, n_pages // NSA_POOL_PG),
            in_specs=_page_specs(layer, NSA_POOL_PG) + [pl.BlockSpec((2, PAGE_SIZE), lambda b, s, pt: (0, 0))],
            out_specs=pl.BlockSpec((None, PAGE_COLS, LANE), lambda b, s, pt: (b, 0, s))),
        out_shape=jax.ShapeDtypeStruct((B, PAGE_COLS, n_pages * PAGE_SIZE // CMP_BLOCK), jnp.float32),
        compiler_params=pltpu.CompilerParams(dimension_semantics=("arbitrary", "arbitrary"),
                                             vmem_limit_bytes=_VMEM_LIMIT),
        name="nsa_cmp_pool",
    )(page_table, *([cache] * NSA_POOL_PG), cw)


def _cmp_select_kernel(q_ref, kv_ref, ocmp_ref, sel_ref, pt_scr, imp_scr, *, past, steps):
    T = steps
    R = NSA_ROWS * T
    n_cmp = past // CMP_BLOCK
    n_sel = past // SEL_BLOCK + 1
    n_lane = sel_ref.shape[-1]
    ratio = SEL_BLOCK // CMP_BLOCK
    rows_t = lax.broadcasted_iota(jnp.int32, (R, 1), 0) % T
    qpos = past + rows_t
    cmp_end = (lax.broadcasted_iota(jnp.int32, (1, n_cmp), 1) + 1) * CMP_BLOCK - 1
    cmask = cmp_end <= qpos
    q = q_ref[...]
    p_all, o_all = [], []
    for kh in range(NSA_KV_HEADS):
        rows = slice(kh * NSA_GROUP * T, (kh + 1) * NSA_GROUP * T)
        kcb_t = kv_ref[kh * HEAD_DIM:(kh + 1) * HEAD_DIM, :]
        vcb_t = kv_ref[NSA_KV_COLS + kh * HEAD_DIM:NSA_KV_COLS + (kh + 1) * HEAD_DIM, :].astype(jnp.bfloat16)
        s = jnp.dot(q[rows], kcb_t, precision=lax.Precision.HIGHEST, preferred_element_type=jnp.float32)
        s = jnp.where(cmask[rows], s, NEG_INF)
        e = jnp.exp(s - jnp.max(s, axis=-1, keepdims=True))
        p = jnp.where(cmask[rows], e / jnp.sum(e, axis=-1, keepdims=True), 0.0)
        o_all.append(lax.dot_general(p.astype(jnp.bfloat16), vcb_t, (((1,), (1,)), ((), ())),
                                     preferred_element_type=jnp.float32))
        imp = p[0:T]
        for g in range(1, NSA_GROUP):
            imp = imp + p[g * T:(g + 1) * T]
        p_all.extend([imp] * NSA_GROUP)
    ocmp_ref[...] = jnp.concatenate(o_all, axis=0)
    imp_rows = jnp.concatenate(p_all + [jnp.zeros((LANE - R, n_cmp), jnp.float32)], axis=0)
    for c in range(n_cmp // LANE):
        pt_scr[c] = imp_rows[:, c * LANE:(c + 1) * LANE].T
    per = LANE // ratio
    for c in range(n_cmp // LANE):
        part = pt_scr[c, pl.ds(0, per, stride=ratio), :]
        for r in range(1, ratio):
            part = part + pt_scr[c, pl.ds(r, per, stride=ratio), :]
        imp_scr[c * per:(c + 1) * per, :] = part
    n_rows = imp_scr.shape[0]
    imp_scr[n_sel - 1:n_rows, :] = jnp.zeros((n_rows - n_sel + 1, LANE), jnp.float32)
    lane_t = lax.broadcasted_iota(jnp.int32, (1, LANE), 1) % T
    cur = (past + lane_t) // SEL_BLOCK
    blk = lax.broadcasted_iota(jnp.int32, (n_rows, LANE), 0)
    forced = (blk == 0) | (blk == cur) | (blk == cur - 1)
    imp_s = jnp.where(blk > cur, -1.0, jnp.where(forced, FORCE_SCORE, imp_scr[...]))
    imp_s = jnp.where(blk < n_sel, imp_s, -2.0)
    imp_scr[...] = imp_s

    def rank_body(m, rank):
        row = imp_scr[pl.ds(m, 1), :]
        ahead = (row > imp_s) | ((row == imp_s) & (m < blk))
        return rank + jnp.where(ahead, 1, 0)

    rank = lax.fori_loop(0, n_sel, rank_body, jnp.zeros((n_rows, LANE), jnp.int32))
    sel = jnp.where((rank < min(SEL_TOPN, n_sel)) & (imp_s >= 0.0), 1.0, 0.0)
    sel = jnp.concatenate([sel, jnp.zeros((n_lane - n_rows, LANE), jnp.float32)], axis=0)
    for c in range(n_lane // LANE):
        sel_ref[:, c * LANE:(c + 1) * LANE] = sel[c * LANE:(c + 1) * LANE, :].T


def _cmp_select(q_rows, kvcb, past, steps):
    B, R, _ = q_rows.shape
    n_cmp = kvcb.shape[2]
    n_sel = past // SEL_BLOCK + 1
    n_lane = _round_up(n_sel, LANE)
    n_rows = _round_up(n_sel, SUBLANE)
    return pl.pallas_call(
        functools.partial(_cmp_select_kernel, past=past, steps=steps),
        grid=(B,),
        in_specs=[pl.BlockSpec((None, R, HEAD_DIM), lambda b: (b, 0, 0)),
                  pl.BlockSpec((None, PAGE_COLS, n_cmp), lambda b: (b, 0, 0))],
        out_specs=[pl.BlockSpec((None, R, HEAD_DIM), lambda b: (b, 0, 0)),
                   pl.BlockSpec((None, LANE, n_lane), lambda b: (b, 0, 0))],
        out_shape=[jax.ShapeDtypeStruct((B, R, HEAD_DIM), jnp.float32),
                   jax.ShapeDtypeStruct((B, LANE, n_lane), jnp.float32)],
        scratch_shapes=[pltpu.VMEM((n_cmp // LANE, LANE, LANE), jnp.float32),
                        pltpu.VMEM((n_rows, LANE), jnp.float32)],
        compiler_params=pltpu.CompilerParams(dimension_semantics=("arbitrary",),
                                             vmem_limit_bytes=_VMEM_LIMIT),
        name="nsa_cmp_select",
    )(q_rows, kvcb)


def _soft_update(s, mask, v, m, l, acc, keys_last=False):
    s = jnp.where(mask, s, NEG_INF)
    m_new = jnp.maximum(m, jnp.max(s, axis=-1, keepdims=True))
    p = jnp.where(mask, jnp.exp(s - m_new), 0.0)
    alpha = jnp.exp(m - m_new)
    pv = lax.dot_general(p.astype(jnp.bfloat16), v, (((1,), (1 if keys_last else 0,)), ((), ())),
                         preferred_element_type=jnp.float32)
    return m_new, alpha * l + jnp.sum(p, axis=-1, keepdims=True), alpha * acc + pv


def _sel_win_kernel(pt_ref, *refs, past, steps):
    pages = refs[:NSA_PG]
    (q_ref, sel_ref, ocmp_ref, nsel_ref, wpre_ref, nwin_ref, gate_ref,
     o_ref, m_scr, l_scr, acc_scr) = refs[NSA_PG:]
    s_id = pl.program_id(1)
    T = steps
    GT = NSA_GROUP * T
    R = NSA_ROWS * T
    n_lane = sel_ref.shape[-1]
    rows_t = lax.broadcasted_iota(jnp.int32, (GT, 1), 0) % T

    @pl.when(s_id == 0)
    def _():
        m_scr[...] = jnp.full_like(m_scr, NEG_INF)
        l_scr[...] = jnp.zeros_like(l_scr)
        acc_scr[...] = jnp.zeros_like(acc_scr)

    q = q_ref[...].astype(jnp.bfloat16)
    sel_bf = sel_ref[0:R, :].astype(jnp.bfloat16)
    n_keys = NSA_PG * PAGE_SIZE
    blk_id = lax.broadcasted_iota(jnp.int32, (n_lane, n_keys), 0)
    key_blk = s_id * (n_keys // SEL_BLOCK) + lax.broadcasted_iota(jnp.int32, (n_lane, n_keys), 1) // SEL_BLOCK

    def head_cols(x, kh, off):
        return x[:, off + kh * HEAD_DIM:off + (kh + 1) * HEAD_DIM]

    def keys_t(kv, kh):
        return jnp.concatenate([pg[kv, kh] for pg in pages], axis=1).astype(jnp.bfloat16)

    expand = jnp.where(blk_id == key_blk, 1.0, 0.0).astype(jnp.bfloat16)
    chosen = jnp.dot(sel_bf, expand, preferred_element_type=jnp.float32) > 0.5
    head_rows = [slice(kh * GT, (kh + 1) * GT) for kh in range(NSA_KV_HEADS)]
    scores = [jnp.dot(q[rows], keys_t(0, kh), preferred_element_type=jnp.float32)
              for kh, rows in enumerate(head_rows)]
    upd = [_soft_update(scores[kh], chosen[rows], keys_t(1, kh), m_scr[rows], l_scr[rows], acc_scr[rows],
                        keys_last=True) for kh, rows in enumerate(head_rows)]
    for kh, rows in enumerate(head_rows):
        m_scr[rows], l_scr[rows], acc_scr[rows] = upd[kh]

    @pl.when(s_id == pl.num_programs(1) - 1)
    def _():
        new_blk = past // SEL_BLOCK
        tpad = nsel_ref.shape[0]
        jn = lax.broadcasted_iota(jnp.int32, (1, tpad), 1)
        nsel = nsel_ref[...].astype(jnp.bfloat16)
        nwin = nwin_ref[...].astype(jnp.bfloat16)
        jp = lax.broadcasted_iota(jnp.int32, (1, WINDOW), 1)
        gates = jax.nn.sigmoid(gate_ref[...])
        for kh in range(NSA_KV_HEADS):
            rows = slice(kh * GT, (kh + 1) * GT)
            pick = sel_ref[kh * GT:(kh + 1) * GT, new_blk:new_blk + 1] > 0.5
            s = lax.dot_general(q[rows], head_cols(nsel, kh, 0), (((1,), (1,)), ((), ())),
                                preferred_element_type=jnp.float32)
            m, l, acc = _soft_update(s, pick & (jn <= rows_t) & (jn < T), head_cols(nsel, kh, NSA_KV_COLS),
                                     m_scr[rows], l_scr[rows], acc_scr[rows])
            o_sel = acc / l
            init = (jnp.full((GT, 1), NEG_INF, jnp.float32), jnp.zeros((GT, 1), jnp.float32),
                    jnp.zeros((GT, HEAD_DIM), jnp.float32))
            s = jnp.dot(q[rows], wpre_ref[0, kh].astype(jnp.bfloat16), preferred_element_type=jnp.float32)
            st = _soft_update(s, jp > rows_t, wpre_ref[1, kh].astype(jnp.bfloat16), *init, keys_last=True)
            s = lax.dot_general(q[rows], head_cols(nwin, kh, 0), (((1,), (1,)), ((), ())),
                                preferred_element_type=jnp.float32)
            _, l_w, acc_w = _soft_update(s, (jn <= rows_t) & (jn < T), head_cols(nwin, kh, NSA_KV_COLS), *st)
            o_win = acc_w / l_w
            gt = gates[rows]
            o_ref[rows, :] = gt[:, 0:1] * ocmp_ref[rows, :] + gt[:, 1:2] * o_sel + gt[:, 2:3] * o_win


def _sel_win(cache, page_table, layer, q_rows, sel_rows, o_cmp, new_sel, win_prefix, new_win, gate_rows,
             past, steps):
    B, n_pages = page_table.shape
    R = q_rows.shape[1]
    n_lane = sel_rows.shape[-1]
    tpad = new_sel.shape[1]
    per_b = lambda shape: pl.BlockSpec((None,) + shape, lambda b, s, pt: (b,) + (0,) * len(shape))
    return pl.pallas_call(
        functools.partial(_sel_win_kernel, past=past, steps=steps),
        grid_spec=pltpu.PrefetchScalarGridSpec(
            num_scalar_prefetch=1, grid=(B, n_pages // NSA_PG),
            in_specs=_page_specs(layer, NSA_PG) + [
                per_b((R, HEAD_DIM)), per_b((LANE, n_lane)), per_b((R, HEAD_DIM)),
                per_b((tpad, PAGE_COLS)), per_b((2, NSA_KV_HEADS, HEAD_DIM, WINDOW)), per_b((tpad, PAGE_COLS)),
                per_b((R, 3))],
            out_specs=per_b((R, HEAD_DIM)),
            scratch_shapes=[pltpu.VMEM((R, 1), jnp.float32), pltpu.VMEM((R, 1), jnp.float32),
                            pltpu.VMEM((R, HEAD_DIM), jnp.float32)]),
        out_shape=jax.ShapeDtypeStruct((B, R, HEAD_DIM), jnp.float32),
        compiler_params=pltpu.CompilerParams(dimension_semantics=("arbitrary", "arbitrary"),
                                             vmem_limit_bytes=_VMEM_LIMIT),
        name="nsa_sel_win",
    )(page_table, *([cache] * NSA_PG), q_rows, sel_rows, o_cmp, new_sel, win_prefix, new_win, gate_rows)


def _nsa_decode_mix(proj, cache_cmp, cache_sel, page_table, layer, win_prefix, cmp_w):
    B, T, _ = proj.shape
    xn = proj[..., RWKV_COLS:IN_COLS]
    n_pages = page_table.shape[1]
    past = n_pages * PAGE_SIZE
    assert T <= SEL_BLOCK and n_pages % NSA_POOL_PG == 0 and win_prefix.shape[1] == WINDOW
    KVH, G, HD = NSA_KV_HEADS, NSA_GROUP, HEAD_DIM
    kv5 = lambda i: xn[..., D_NSA + 2 * i * NSA_KV_COLS:D_NSA + 2 * (i + 1) * NSA_KV_COLS]
    new_cmp, new_sel, new_win = kv5(0), kv5(1), kv5(2)
    q_rows = (xn[..., :D_NSA] * (HD ** -0.5)).reshape(B, T, KVH, G, HD).transpose(0, 2, 3, 1, 4)
    q_rows = q_rows.reshape(B, KVH * G * T, HD)
    gate_rows = xn[..., D_NSA + 6 * NSA_KV_COLS:].reshape(B, T, KVH, G, 3).transpose(0, 2, 3, 1, 4)
    gate_rows = gate_rows.reshape(B, KVH * G * T, 3)
    cw = jnp.tile(cmp_w, (1, PAGE_SIZE // CMP_BLOCK))
    kvcb = _cmp_pool(_rows_last(cache_cmp), page_table, layer, cw)
    o_cmp, sel_rows = _cmp_select(q_rows, kvcb, past, T)
    tpad = _round_up(T, SUBLANE)
    padt = lambda t: jnp.pad(t, ((0, 0), (0, tpad - T), (0, 0)))
    out = _sel_win(_rows_last(cache_sel), page_table, layer, q_rows, sel_rows, o_cmp, padt(new_sel),
                   _rows_last(win_prefix), padt(new_win), gate_rows, past, T)
    y = out.reshape(B, KVH, G, T, HD).transpose(0, 3, 1, 2, 4).reshape(B, T, D_NSA)
    five = lambda t: t.reshape(B, -1, 2, KVH, HD)
    win_out = jnp.concatenate([win_prefix[:, T:], five(new_win)], axis=1)
    return y, five(new_cmp), five(new_sel), win_out


def _deepnorm(x, h, g, b):
    z = DEEPNORM_ALPHA * x + h
    mu = jnp.mean(z, axis=-1, keepdims=True)
    dev = z - mu
    var = jnp.mean(dev * dev, axis=-1, keepdims=True)
    return dev * lax.rsqrt(var + LN_EPS) * g + b


def _out_proj_kernel(yr_ref, yn_ref, x_ref, w_ref, g_ref, b_ref, o_ref):
    half = yr_ref.shape[1]
    h = (jnp.dot(yr_ref[...].astype(jnp.bfloat16), w_ref[0:half, :], preferred_element_type=jnp.float32)
         + jnp.dot(yn_ref[...].astype(jnp.bfloat16), w_ref[half:, :], preferred_element_type=jnp.float32))
    o_ref[...] = _deepnorm(x_ref[...], h, g_ref[...], b_ref[...])


def _residual_norm_kernel(x_ref, f_ref, g_ref, b_ref, o_ref):
    o_ref[...] = _deepnorm(x_ref[...], f_ref[...], g_ref[...], b_ref[...])


def _residual_norm(x, f, g, b, *, tm=256):
    n, d = x.shape
    tm = min(tm, n)
    rows = pl.BlockSpec((tm, d), lambda i: (i, 0))
    vec = pl.BlockSpec((1, d), lambda i: (0, 0))
    return pl.pallas_call(
        _residual_norm_kernel,
        grid=(n // tm,),
        in_specs=[rows, rows, vec, vec],
        out_specs=rows,
        out_shape=jax.ShapeDtypeStruct((n, d), jnp.float32),
        compiler_params=pltpu.CompilerParams(dimension_semantics=("arbitrary",),
                                             vmem_limit_bytes=_VMEM_LIMIT),
        name="residual_norm",
    )(x, f, g.reshape(1, d), b.reshape(1, d))


def _out_proj_norm(y_r, y_n, x, w_bf, g, b, *, tm=256):
    n, d = x.shape
    tm = min(tm, n)
    half = y_r.shape[1]
    rows = lambda w: pl.BlockSpec((tm, w), lambda i: (i, 0))
    full = lambda shape: pl.BlockSpec(shape, lambda i: (0, 0))
    return pl.pallas_call(
        _out_proj_kernel,
        grid=(n // tm,),
        in_specs=[rows(half), rows(y_n.shape[1]), rows(d), full(w_bf.shape), full((1, d)), full((1, d))],
        out_specs=rows(d),
        out_shape=jax.ShapeDtypeStruct((n, d), jnp.float32),
        compiler_params=pltpu.CompilerParams(dimension_semantics=("arbitrary",),
                                             vmem_limit_bytes=_VMEM_LIMIT),
        name="out_proj_norm",
    )(y_r, y_n, x, w_bf, g.reshape(1, d), b.reshape(1, d))


def _hybrid_layer(x, past, shift_prev, rwkv_s0, n_keep, p):
    B, T, D = x.shape
    proj = _matmul(x.reshape(B * T, D), p['w_in_bf']).reshape(B, T, IN_COLS_PAD)
    y_r, s_T, new_shift = _rwkv_time_mix(proj, shift_prev, rwkv_s0, p)
    if past is None:
        assert T % NSA_TQ == 0 and T >= n_keep
        y_n, new_cmp, new_sel, new_win = _nsa_prompt_mix(proj, n_keep, p['nsa_cmp_w'])
    else:
        assert n_keep == WINDOW
        y_n, new_cmp, new_sel, new_win = _nsa_decode_mix(proj, *past, p['nsa_cmp_w'])
    x = _out_proj_norm(y_r.reshape(B * T, D_RWKV), y_n.reshape(B * T, D_NSA), x.reshape(B * T, D),
                       p['w_out_bf'], p['ln1_g'], p['ln1_b']).reshape(B, T, D)
    f = _peer_ffn(x, p['peer_wq_bf'], p['peer_sk_bf'], p['peer_u_bf'], p['peer_v_bf'])
    x = _residual_norm(x.reshape(B * T, D), f.reshape(B * T, D), p['ln2_g'], p['ln2_b']).reshape(B, T, D)
    return x, (new_cmp, new_sel, new_win, s_T, new_shift)


def kernel(x_prompt, x_sample, cache_cmp_kv, cache_sel_kv, page_table, state_win_kv, state_rwkv,
           state_shift, w_in, rwkv_mu, rwkv_w0, rwkv_w2, rwkv_a0, rwkv_a2, rwkv_g2, rwkv_k_k,
           rwkv_k_a, rwkv_r_k, rwkv_gn_g, rwkv_gn_b, nsa_cmp_w, w_out, ln1_g, ln1_b, peer_wq,
           peer_subkeys, peer_u, peer_v, ln2_g, ln2_b):
    bp = x_prompt.shape[0]
    dt = x_prompt.dtype
    assert page_table.shape[1] * PAGE_SIZE == PAST_LEN
    n_keep = state_win_kv.shape[2]
    zero_shift = jnp.zeros((bp, 1, RWKV_COLS), dt)
    zero_state = jnp.zeros((bp, H_RWKV, HEAD_DIM, HEAD_DIM), dt)
    yp, ys = x_prompt, x_sample
    st_p, st_s = [], []
    for l in range(DEPTH):
        p = {'w_in_bf': jnp.pad(w_in[l].astype(jnp.bfloat16), ((0, 0), (0, IN_COLS_PAD - IN_COLS))),
             'rwkv_mu': rwkv_mu[l], 'rwkv_w0': rwkv_w0[l], 'rwkv_w2': rwkv_w2[l],
             'rwkv_a0': rwkv_a0[l], 'rwkv_a2': rwkv_a2[l], 'rwkv_g2': rwkv_g2[l],
             'rwkv_k_k': rwkv_k_k[l], 'rwkv_k_a': rwkv_k_a[l], 'rwkv_r_k': rwkv_r_k[l],
             'rwkv_gn_g': rwkv_gn_g[l], 'rwkv_gn_b': rwkv_gn_b[l], 'nsa_cmp_w': nsa_cmp_w[l],
             'w_out_bf': w_out[l].astype(jnp.bfloat16), 'ln1_g': ln1_g[l], 'ln1_b': ln1_b[l],
             'peer_wq_bf': peer_wq[l].astype(jnp.bfloat16),
             'peer_sk_bf': peer_subkeys[l].reshape(2 * P_HEADS, N_KEYS, P_DKEY // 2).astype(jnp.bfloat16),
             'peer_u_bf': peer_u[l].T.astype(jnp.bfloat16), 'peer_v_bf': peer_v[l].astype(jnp.bfloat16),
             'ln2_g': ln2_g[l], 'ln2_b': ln2_b[l]}
        yp, sp = _hybrid_layer(yp, None, zero_shift, zero_state, n_keep, p)
        past = (cache_cmp_kv, cache_sel_kv, page_table, l, state_win_kv[l])
        ys, ss = _hybrid_layer(ys, past, state_shift[l], state_rwkv[l], n_keep, p)
        st_p.append(sp)
        st_s.append(ss)
    stk = lambda sts, i: jnp.stack([s[i] for s in sts], axis=0)
    return (yp, ys, stk(st_p, 0), stk(st_p, 1), stk(st_p, 2), stk(st_p, 3), stk(st_p, 4),
            stk(st_s, 0), stk(st_s, 1), stk(st_s, 2), stk(st_s, 3), stk(st_s, 4))
```

```python
import functools
import math

import jax
import jax.numpy as jnp
from jax import lax
from jax.experimental import pallas as pl
from jax.experimental.pallas import tpu as pltpu

D_MODEL = 2048
DEPTH = 2
PAST_LEN = 16384
PAGE_SIZE = 128
HEAD_DIM = 64
D_RWKV = D_MODEL // 2
D_NSA = D_MODEL - D_RWKV
H_RWKV = D_RWKV // HEAD_DIM
H_NSA = D_NSA // HEAD_DIM
NSA_KV_HEADS = 4
NSA_GROUP = H_NSA // NSA_KV_HEADS
NSA_KV_COLS = NSA_KV_HEADS * HEAD_DIM
CMP_BLOCK = 32
SEL_BLOCK = 64
SEL_TOPN = 16
WINDOW = 512
W_LORA = 64
A_LORA = 64
G_LORA = 160
RWKV_COLS = 3 * D_RWKV + W_LORA + A_LORA + G_LORA
NSA_COLS = D_NSA + 6 * NSA_KV_COLS + 3 * H_NSA
IN_COLS = RWKV_COLS + NSA_COLS
P_HEADS = 8
N_KEYS = 128
P_DKEY = 256
P_TOPK = 16
LN_EPS = 1e-5
GN_EPS = 64e-5
DEEPNORM_ALPHA = (2 * DEPTH) ** 0.25
FORCE_SCORE = 1e4
NEG_INF = -1e30

N_EXPERTS = N_KEYS * N_KEYS

LANE = 128
SUBLANE = 8
_VMEM_LIMIT = 56 * 1024 * 1024


def _round_up(x, m):
    return -(-x // m) * m


PROJ_TILE = 1024
IN_COLS_PAD = _round_up(IN_COLS, PROJ_TILE)


def _mm_kernel(x_ref, w_ref, o_ref):
    o_ref[...] = jnp.dot(x_ref[...].astype(jnp.bfloat16), w_ref[...], preferred_element_type=jnp.float32)


def _matmul(x, w_bf):
    m, k = x.shape
    n = w_bf.shape[1]
    tm = min(PROJ_TILE, m)
    tn = PROJ_TILE
    return pl.pallas_call(
        _mm_kernel,
        grid=(m // tm, n // tn),
        in_specs=[pl.BlockSpec((tm, k), lambda i, j: (i, 0)),
                  pl.BlockSpec((k, tn), lambda i, j: (0, j))],
        out_specs=pl.BlockSpec((tm, tn), lambda i, j: (i, j)),
        out_shape=jax.ShapeDtypeStruct((m, n), jnp.float32),
        compiler_params=pltpu.CompilerParams(dimension_semantics=("arbitrary", "arbitrary"),
                                             vmem_limit_bytes=_VMEM_LIMIT),
        name="in_proj",
    )(x, w_bf)


RW_CHUNK = 64
RW_PAIRS = D_RWKV // LANE
RW_SEQS = 2


def _bdot(a, b):
    return jnp.dot(a.astype(jnp.bfloat16), b.astype(jnp.bfloat16), preferred_element_type=jnp.float32)


def _bdot_nt(a, b):
    return lax.dot_general(a.astype(jnp.bfloat16), b.astype(jnp.bfloat16), (((1,), (1,)), ((), ())),
                           preferred_element_type=jnp.float32)


def _bdot_tn(a, b):
    return lax.dot_general(a.astype(jnp.bfloat16), b.astype(jnp.bfloat16), (((0,), (0,)), ((), ())),
                           preferred_element_type=jnp.float32)


def _head_sum(x, same_head_bf):
    hi = x.astype(jnp.bfloat16)
    lo = (x - hi.astype(jnp.float32)).astype(jnp.bfloat16)
    return (jnp.dot(hi, same_head_bf, preferred_element_type=jnp.float32)
            + jnp.dot(lo, same_head_bf, preferred_element_type=jnp.float32))


def _rwkv_mix_kernel(x_ref, shift_ref, mu_ref, pv_ref, w2_ref, a2_ref, g2_ref, s0_ref, y_ref, st_ref,
                     prev_scr, *, steps):
    c = pl.program_id(1)
    C = RW_CHUNK
    f32 = jnp.float32

    @pl.when(c == 0)
    def _():
        st_ref[...] = s0_ref[...]
        prev_scr[...] = shift_ref[...]

    row_c = lax.broadcasted_iota(jnp.int32, (C, 1), 0)
    o_wl = 3 * D_RWKV
    w0, a0_, k_k, k_a = pv_ref[0:1, :], pv_ref[1:2, :], pv_ref[2:3, :], pv_ref[3:4, :]
    r_k, gn_g, gn_b = pv_ref[4:5, :], pv_ref[5:6, :], pv_ref[6:7, :]

    def prep(nb):
        xr = x_ref[nb, :, 0:RWKV_COLS]
        prev = jnp.where(row_c == 0, prev_scr[nb], pltpu.roll(xr, 1, 0))
        prev_scr[nb] = xr[C - 1:C, :]
        xs = xr + mu_ref[...] * (prev - xr)
        r_s, k_s, v_s = xs[:, 0:D_RWKV], xs[:, D_RWKV:2 * D_RWKV], xs[:, 2 * D_RWKV:o_wl]
        wl = xs[:, o_wl:o_wl + W_LORA]
        al = xs[:, o_wl + W_LORA:o_wl + W_LORA + A_LORA]
        gl = xs[:, o_wl + W_LORA + A_LORA:RWKV_COLS]
        w_s = -jax.nn.softplus(-(w0 + _bdot(jnp.tanh(wl), w2_ref[...]))) - 0.5
        lw_s = -jnp.exp(w_s)
        a_s = jax.nn.sigmoid(a0_ + _bdot(al, a2_ref[...]))
        gate_s = _bdot(jax.nn.sigmoid(gl), g2_ref[...])
        kkf_s = k_s * k_k
        k2_s = k_s * (1.0 + (a_s - 1.0) * k_a)
        if steps % C:
            live = (c * C + row_c) < steps
            lw_s = jnp.where(live, lw_s, 0.0)
            kkf_s = jnp.where(live, kkf_s, 0.0)
            k2_s = jnp.where(live, k2_s, 0.0)
            v_s = jnp.where(live, v_s, 0.0)
        return lw_s, v_s, k2_s, r_s, kkf_s, a_s, gate_s

    seqs = [prep(nb) for nb in range(x_ref.shape[0])]

    row = lax.broadcasted_iota(jnp.int32, (C, C), 0)
    col = lax.broadcasted_iota(jnp.int32, (C, C), 1)
    tri_incl = col <= row
    tri_strict = col < row
    tri_f = jnp.where(tri_incl, 1.0, 0.0).astype(f32)
    eye = jnp.where(row == col, 1.0, 0.0).astype(f32)
    lane = lax.broadcasted_iota(jnp.int32, (1, LANE), 1)
    head0 = lane < HEAD_DIM
    r128 = lax.broadcasted_iota(jnp.int32, (LANE, LANE), 0)
    c128 = lax.broadcasted_iota(jnp.int32, (LANE, LANE), 1)
    same_head = (r128 < HEAD_DIM) == (c128 < HEAD_DIM)
    same_head_bf = jnp.where(same_head, 1.0, 0.0).astype(jnp.bfloat16)
    eye128 = r128 == c128

    pairs = range(len(seqs) * RW_PAIRS)
    seq_of = [it // RW_PAIRS for it in pairs]
    pair_of = [it % RW_PAIRS for it in pairs]
    heads = (head0, ~head0)
    sls = [slice(pair_of[it] * LANE, (pair_of[it] + 1) * LANE) for it in pairs]
    field = lambda i: [seqs[seq_of[it]][i][:, sls[it]] for it in pairs]
    lw, v, k, r, kkf, a_lr, gate = [field(i) for i in range(7)]
    cum = [jnp.dot(tri_f, lw[pr], precision=lax.Precision.HIGHEST, preferred_element_type=f32)
           for pr in pairs]
    tot = [cum[pr][C - 1:C, :] for pr in pairs]
    ss = [_head_sum(kkf[pr] * kkf[pr], same_head_bf) for pr in pairs]
    kk = [kkf[pr] * lax.rsqrt(jnp.maximum(ss[pr], 1e-24)) for pr in pairs]
    b = [kk[pr] * a_lr[pr] for pr in pairs]
    x, ym = [], []
    for pr in pairs:
        e_neg = jnp.exp(-cum[pr])
        x.append(jnp.concatenate([kk[pr] * jnp.exp(cum[pr] - lw[pr]),
                                  r[pr] * jnp.exp(cum[pr])], axis=0))
        ym.append(jnp.concatenate([k[pr] * e_neg, b[pr] * e_neg], axis=0))
    a0 = [st_ref[seq_of[pr], pair_of[pr]] for pr in pairs]
    xa = [_bdot(x[pr], a0[pr]) for pr in pairs]
    g = [[_bdot_nt(jnp.where(hm, x[pr], 0.0), ym[pr]) for hm in heads] for pr in pairs]
    lkv = [[_bdot(jnp.where(tri_strict, g[pr][h][:C, :C], 0.0), v[pr]) for h in range(2)] for pr in pairs]
    npow = [[jnp.where(tri_strict, -g[pr][h][:C, C:], 0.0) for h in range(2)] for pr in pairs]
    tmat = [[eye + npow[pr][h] for h in range(2)] for pr in pairs]
    for _ in range(int(math.log2(C)) - 1):
        npow = [[_bdot(npow[pr][h], npow[pr][h]) for h in range(2)] for pr in pairs]
        tmat = [[tmat[pr][h] + _bdot(tmat[pr][h], npow[pr][h]) for h in range(2)] for pr in pairs]
    rhs = [xa[pr][:C] + jnp.where(head0, lkv[pr][0], lkv[pr][1]) for pr in pairs]
    w = [jnp.where(head0, _bdot(tmat[pr][0], rhs[pr]), _bdot(tmat[pr][1], rhs[pr])) for pr in pairs]
    vw = [jnp.concatenate([v[pr], w[pr]], axis=0) for pr in pairs]
    y = []
    for pr in pairs:
        mr = [jnp.concatenate([jnp.where(tri_incl, g[pr][h][C:, :C], 0.0),
                               jnp.where(tri_incl, -g[pr][h][C:, C:], 0.0)], axis=1) for h in range(2)]
        y.append(xa[pr][C:] + jnp.where(head0, _bdot(mr[0], vw[pr]), _bdot(mr[1], vw[pr])))
    inv_hd = 1.0 / HEAD_DIM
    mean = [_head_sum(y[pr], same_head_bf) * inv_hd for pr in pairs]
    dev = [y[pr] - mean[pr] for pr in pairs]
    var = [_head_sum(dev[pr] * dev[pr], same_head_bf) * inv_hd for pr in pairs]
    rk = [_head_sum(r[pr] * k[pr] * r_k[:, sls[pr]], same_head_bf) for pr in pairs]
    for pr in pairs:
        yn = dev[pr] * lax.rsqrt(var[pr] + GN_EPS) * gn_g[:, sls[pr]] + gn_b[:, sls[pr]]
        y_ref[seq_of[pr], :, sls[pr]] = (yn + rk[pr] * v[pr]) * gate[pr]
    for pr in pairs:
        e_rem = jnp.exp(tot[pr] - cum[pr])
        kb = jnp.concatenate([k[pr] * e_rem, -(b[pr] * e_rem)], axis=0)
        upd = _bdot_tn(kb, vw[pr])
        p_col = jnp.sum(jnp.where(eye128, jnp.exp(tot[pr]), 0.0), axis=1, keepdims=True)
        st_ref[seq_of[pr], pair_of[pr]] = a0[pr] * p_col + jnp.where(same_head, upd, 0.0)


def _rwkv_time_mix(proj, shift_prev, s0, p):
    B, T, W = proj.shape
    C = RW_CHUNK
    t_pad = _round_up(T, C)
    x = proj if t_pad == T else jnp.pad(proj, ((0, 0), (0, t_pad - T), (0, 0)))
    a = jnp.swapaxes(s0, -1, -2).reshape(B, RW_PAIRS, 2, HEAD_DIM, HEAD_DIM)
    z = jnp.zeros_like(a[:, :, 0])
    a0 = jnp.concatenate([jnp.concatenate([a[:, :, 0], z], axis=-1),
                          jnp.concatenate([z, a[:, :, 1]], axis=-1)], axis=-2)
    zero = jnp.zeros((D_RWKV,), jnp.float32)
    pv = jnp.stack([p['rwkv_w0'], p['rwkv_a0'], p['rwkv_k_k'], p['rwkv_k_a'], p['rwkv_r_k'].reshape(-1),
                    p['rwkv_gn_g'], p['rwkv_gn_b'], zero])
    bf = lambda t: t.astype(jnp.bfloat16)
    full = lambda shape: pl.BlockSpec(shape, lambda bi, ci: (0,) * len(shape))
    nb = RW_SEQS
    assert B % nb == 0
    st_spec = pl.BlockSpec((nb, RW_PAIRS, LANE, LANE), lambda bi, ci: (bi, 0, 0, 0))
    y, st = pl.pallas_call(
        functools.partial(_rwkv_mix_kernel, steps=T),
        grid=(B // nb, t_pad // C),
        in_specs=[pl.BlockSpec((nb, C, W), lambda bi, ci: (bi, ci, 0)),
                  pl.BlockSpec((nb, 1, RWKV_COLS), lambda bi, ci: (bi, 0, 0)),
                  full((1, RWKV_COLS)), full((8, D_RWKV)), full((W_LORA, D_RWKV)),
                  full((A_LORA, D_RWKV)), full((G_LORA, D_RWKV)), st_spec],
        out_specs=[pl.BlockSpec((nb, C, D_RWKV), lambda bi, ci: (bi, ci, 0)), st_spec],
        out_shape=[jax.ShapeDtypeStruct((B, t_pad, D_RWKV), jnp.float32),
                   jax.ShapeDtypeStruct((B, RW_PAIRS, LANE, LANE), jnp.float32)],
        scratch_shapes=[pltpu.VMEM((nb, 1, RWKV_COLS), jnp.float32)],
        compiler_params=pltpu.CompilerParams(dimension_semantics=("arbitrary", "arbitrary"),
                                             vmem_limit_bytes=_VMEM_LIMIT),
        name="rwkv_mix",
    )(x, shift_prev, p['rwkv_mu'].reshape(1, RWKV_COLS), pv, bf(p['rwkv_w2']), bf(p['rwkv_a2']),
      bf(p['rwkv_g2']), a0)
    s_t = jnp.stack([st[:, :, :HEAD_DIM, :HEAD_DIM], st[:, :, HEAD_DIM:, HEAD_DIM:]], axis=2)
    s_t = jnp.swapaxes(s_t.reshape(B, H_RWKV, HEAD_DIM, HEAD_DIM), -1, -2)
    return y[:, :T], s_t, proj[:, T - 1:T, :RWKV_COLS]


def _extract_top(buf_ref, out_ref):
    n_slabs, nrows, width = buf_ref.shape
    iota = lax.broadcasted_iota(jnp.int32, (nrows, width), 0)

    def body(r, carry):
        for g in range(n_slabs):
            s = buf_ref[g]
            mx = jnp.max(s, axis=0, keepdims=True)
            first = jnp.min(jnp.where(s == mx, iota, nrows), axis=0, keepdims=True)
            buf_ref[g] = jnp.where(iota == first, -jnp.inf, s)
            out_ref[g, pl.ds(r, 1), :] = mx
        return carry

    lax.fori_loop(0, P_TOPK, body, 0)


def _peer_score_kernel(x_ref, wq_ref, sk_ref, s1_ref, s2_ref, st_ref, buf, cand, hv, tv):
    tn = x_ref.shape[0]
    q = jnp.dot(x_ref[...].astype(jnp.bfloat16), wq_ref[...],
                preferred_element_type=jnp.float32).astype(jnp.bfloat16)
    for h in range(P_HEADS):
        for c, s_ref in enumerate((s1_ref, s2_ref)):
            col = (2 * h + c) * N_KEYS
            s_t = lax.dot_general(sk_ref[2 * h + c], q[:, col:col + N_KEYS],
                                  (((1,), (1,)), ((), ())), preferred_element_type=jnp.float32)
            s_ref[h] = s_t
            buf[2 * h + c] = s_t
    _extract_top(buf, hv)
    n_cand = sum(P_TOPK // (a + 1) for a in range(P_TOPK))
    for h in range(P_HEADS):
        pieces = [hv[2 * h, a:a + 1, :] + hv[2 * h + 1, 0:P_TOPK // (a + 1), :] for a in range(P_TOPK)]
        pieces.append(jnp.full((cand.shape[1] - n_cand, tn), -jnp.inf, jnp.float32))
        cand[h] = jnp.concatenate(pieces, axis=0)
    _extract_top(cand, tv)
    for h in range(P_HEADS):
        z = jnp.sum(jnp.exp(tv[h] - tv[h, 0:1, :]), axis=0, keepdims=True)
        st_ref[0, h:h + 1, :] = tv[h, P_TOPK - 1:P_TOPK, :]
        st_ref[1, h:h + 1, :] = hv[2 * h, 0:1, :]
        st_ref[2, h:h + 1, :] = hv[2 * h + 1, 0:1, :]
        st_ref[3, h:h + 1, :] = 1.0 / z


def _peer_scores(x, wq_bf, sk_bf, *, tn=256):
    n = x.shape[0]
    tn = min(tn, n)
    return pl.pallas_call(
        _peer_score_kernel,
        grid=(n // tn,),
        in_specs=[pl.BlockSpec((tn, D_MODEL), lambda i: (i, 0)),
                  pl.BlockSpec((D_MODEL, P_HEADS * P_DKEY), lambda i: (0, 0)),
                  pl.BlockSpec((2 * P_HEADS, N_KEYS, P_DKEY // 2), lambda i: (0, 0, 0))],
        out_specs=[pl.BlockSpec((P_HEADS, N_KEYS, tn), lambda i: (0, 0, i)),
                   pl.BlockSpec((P_HEADS, N_KEYS, tn), lambda i: (0, 0, i)),
                   pl.BlockSpec((4, P_HEADS, tn), lambda i: (0, 0, i))],
        out_shape=[jax.ShapeDtypeStruct((P_HEADS, N_KEYS, n), jnp.float32),
                   jax.ShapeDtypeStruct((P_HEADS, N_KEYS, n), jnp.float32),
                   jax.ShapeDtypeStruct((4, P_HEADS, n), jnp.float32)],
        scratch_shapes=[pltpu.VMEM((2 * P_HEADS, N_KEYS, tn), jnp.float32),
                        pltpu.VMEM((P_HEADS, _round_up(sum(P_TOPK // (a + 1) for a in range(P_TOPK)), SUBLANE),
                                    tn),
                                   jnp.float32),
                        pltpu.VMEM((2 * P_HEADS, P_TOPK, tn), jnp.float32),
                        pltpu.VMEM((P_HEADS, P_TOPK, tn), jnp.float32)],
        compiler_params=pltpu.CompilerParams(dimension_semantics=("arbitrary",),
                                             vmem_limit_bytes=_VMEM_LIMIT),
        name="peer_scores",
    )(x, wq_bf, sk_bf)


def _peer_expert_kernel(x_ref, u_ref, v_ref, s1_ref, s2_ref, st_ref, o_ref, e2_scr, h_scr, p_scr,
                        *, chunk):
    j = pl.program_id(1)
    tn = x_ref.shape[0]
    te = u_ref.shape[1]

    @pl.when(j == 0)
    def _():
        o_ref[...] = jnp.zeros_like(o_ref)
        for h in range(P_HEADS):
            e2_scr[h] = jnp.exp(s2_ref[h] - st_ref[2, h:h + 1, :])

    def hidden(c):
        tsl = slice(c * chunk, (c + 1) * chunk)
        h_scr[tsl, :] = jnp.dot(x_ref[tsl, :], u_ref[...], preferred_element_type=jnp.float32)

    a_rows = [[s1_ref[h, pl.ds(j * (te // N_KEYS) + ii, 1), :] for h in range(P_HEADS)]
              for ii in range(te // N_KEYS)]
    e1_rows = [[jnp.exp(a_rows[ii][h] - st_ref[1, h:h + 1, :]) * st_ref[3, h:h + 1, :]
                for h in range(P_HEADS)] for ii in range(te // N_KEYS)]

    def weigh(c):
        for sub in range(chunk // LANE):
            weigh_lanes(slice(c * chunk + sub * LANE, c * chunk + (sub + 1) * LANE))

    def weigh_lanes(tsl):
        for ii in range(te // N_KEYS):
            w_t = jnp.zeros((N_KEYS, LANE), jnp.float32)
            for h in range(P_HEADS):
                val = a_rows[ii][h][:, tsl] + s2_ref[h, :, tsl]
                w_t = w_t + jnp.where(val >= st_ref[0, h:h + 1, tsl],
                                      e1_rows[ii][h][:, tsl] * e2_scr[h, :, tsl], 0.0)
            hh = h_scr[tsl, ii * N_KEYS:(ii + 1) * N_KEYS]
            g = 0.5 * hh * (1.0 + lax.erf(hh * 0.7071067811865476))
            p_scr[tsl, ii * N_KEYS:(ii + 1) * N_KEYS] = (w_t.T * g).astype(jnp.bfloat16)

    def project(c):
        tsl = slice(c * chunk, (c + 1) * chunk)
        o_ref[tsl, :] += jnp.dot(p_scr[tsl, :], v_ref[...], preferred_element_type=jnp.float32)

    n_chunks = tn // chunk
    hidden(0)
    for c in range(n_chunks):
        if c + 1 < n_chunks:
            hidden(c + 1)
        weigh(c)
        project(c)


def _peer_experts(x_bf, u_bf, v_bf, s1t, s2t, stats, *, tn=1024, te=512, chunk=256):
    n = x_bf.shape[0]
    tn = min(tn, n)
    chunk = min(chunk, tn)
    once = pl.Buffered(1)
    return pl.pallas_call(
        functools.partial(_peer_expert_kernel, chunk=chunk),
        grid=(n // tn, N_EXPERTS // te),
        in_specs=[pl.BlockSpec((tn, D_MODEL), lambda i, j: (i, 0), pipeline_mode=once),
                  pl.BlockSpec((D_MODEL, te), lambda i, j: (0, j)),
                  pl.BlockSpec((te, D_MODEL), lambda i, j: (j, 0)),
                  pl.BlockSpec((P_HEADS, N_KEYS, tn), lambda i, j: (0, 0, i), pipeline_mode=once),
                  pl.BlockSpec((P_HEADS, N_KEYS, tn), lambda i, j: (0, 0, i), pipeline_mode=once),
                  pl.BlockSpec((4, P_HEADS, tn), lambda i, j: (0, 0, i), pipeline_mode=once)],
        out_specs=pl.BlockSpec((tn, D_MODEL), lambda i, j: (i, 0)),
        out_shape=jax.ShapeDtypeStruct((n, D_MODEL), jnp.float32),
        scratch_shapes=[pltpu.VMEM((P_HEADS, N_KEYS, tn), jnp.float32),
                        pltpu.VMEM((tn, te), jnp.float32),
                        pltpu.VMEM((tn, te), jnp.bfloat16)],
        compiler_params=pltpu.CompilerParams(dimension_semantics=("arbitrary", "arbitrary"),
                                             vmem_limit_bytes=_VMEM_LIMIT),
        name="peer_experts",
    )(x_bf, u_bf, v_bf, s1t, s2t, stats)


def _peer_ffn(x, wq_bf, sk_bf, u_bf, v_bf):
    B, T, D = x.shape
    n = B * T
    xt = x.reshape(n, D)
    n_pad = _round_up(n, LANE)
    if n_pad != n:
        xt = jnp.pad(xt, ((0, n_pad - n), (0, 0)))
    s1t, s2t, stats = _peer_scores(xt, wq_bf, sk_bf)
    out = _peer_experts(xt.astype(jnp.bfloat16), u_bf, v_bf, s1t, s2t, stats)
    return out[:n].reshape(B, T, D)


NSA_TQ = 256
NSA_GSPLIT = 4


def _flash_tile(q4, k_t, v_t, bias, m, l, acc):
    tq, tk = bias.shape
    s = lax.dot_general(q4, k_t, (((1,), (1,)), ((), ())), preferred_element_type=jnp.float32)
    s = (s.reshape(-1, tq, tk) + bias[None]).reshape(-1, tk)
    m_new = jnp.maximum(m, jnp.max(s, axis=-1, keepdims=True))
    p = jnp.exp(s - m_new)
    alpha = jnp.exp(m - m_new)
    l_new = alpha * l + jnp.sum(p, axis=-1, keepdims=True)
    acc_new = alpha * acc + jnp.dot(p.astype(jnp.bfloat16), v_t, preferred_element_type=jnp.float32)
    return m_new, l_new, acc_new


def _nsa_prompt_kernel(q_ref, gl_ref, kc_ref, vc_ref, ks_ref, vs_ref, kw_ref, vw_ref, cw_ref, ex_ref,
                       o_ref, kcb_scr, vcb_scr, imp_scr, selx_scr, *, seq):
    qt = pl.program_id(2)
    tq = NSA_TQ
    n_cmp = seq // CMP_BLOCK
    n_sel = seq // SEL_BLOCK
    G = NSA_GROUP
    q0 = qt * tq

    @pl.when(qt == 0)
    def _():
        kc = kc_ref[...].reshape(n_cmp, CMP_BLOCK, HEAD_DIM)
        vc = vc_ref[...].reshape(n_cmp, CMP_BLOCK, HEAD_DIM)
        kcb_scr[...] = jnp.zeros_like(kcb_scr)
        vcb_scr[...] = jnp.zeros_like(vcb_scr)
        kcb_scr[0:n_cmp, :] = jnp.sum(kc * cw_ref[0][None], axis=1)
        vcb_scr[0:n_cmp, :] = jnp.sum(vc * cw_ref[1][None], axis=1).astype(jnp.bfloat16)

    qpos_col = q0 + lax.broadcasted_iota(jnp.int32, (tq, 1), 0)
    qf = [q_ref[g] * (HEAD_DIM ** -0.5) for g in range(G)]
    qs = [t.astype(jnp.bfloat16) for t in qf]
    gates = jax.nn.sigmoid(gl_ref[...])

    cmp_end = (lax.broadcasted_iota(jnp.int32, (1, LANE), 1) + 1) * CMP_BLOCK - 1
    cmask = (cmp_end <= qpos_col) & (lax.broadcasted_iota(jnp.int32, (1, LANE), 1) < n_cmp)
    o_cmp = []
    imp = jnp.zeros((tq, LANE), jnp.float32)
    for g in range(G):
        s = lax.dot_general(qf[g], kcb_scr[...], (((1,), (1,)), ((), ())),
                            precision=lax.Precision.HIGHEST, preferred_element_type=jnp.float32)
        s = jnp.where(cmask, s, NEG_INF)
        e = jnp.exp(s - jnp.max(s, axis=-1, keepdims=True))
        p = jnp.where(cmask, e / jnp.sum(e, axis=-1, keepdims=True), 0.0)
        o_cmp.append(jnp.dot(p.astype(jnp.bfloat16), vcb_scr[...], preferred_element_type=jnp.float32))
        imp = imp + p
    imp_t = imp.T
    ratio = SEL_BLOCK // CMP_BLOCK
    parts = []
    for c in range(tq // LANE):
        imp_scr[c] = imp_t[:, c * LANE:(c + 1) * LANE]
        part = imp_scr[c, pl.ds(0, n_sel, stride=ratio), :]
        for r in range(1, ratio):
            part = part + imp_scr[c, pl.ds(r, n_sel, stride=ratio), :]
        parts.append(part)
    imp_s = jnp.concatenate(parts, axis=1)
    qpos_row = q0 + lax.broadcasted_iota(jnp.int32, (1, tq), 1)
    cur = qpos_row // SEL_BLOCK
    blk = lax.broadcasted_iota(jnp.int32, (n_sel, tq), 0)
    forced = (blk == 0) | (blk == cur) | (blk == cur - 1)
    imp_s = jnp.where(blk > cur, -1.0, jnp.where(forced, FORCE_SCORE, imp_s))
    rank = jnp.zeros((n_sel, tq), jnp.int32)
    for mm in range(n_sel):
        row = imp_s[mm:mm + 1, :]
        ahead = (row > imp_s) | ((row == imp_s) & (mm < blk))
        rank = rank + jnp.where(ahead, 1, 0)
    sel = jnp.where((rank < min(SEL_TOPN, n_sel)) & (imp_s >= 0.0), 1.0, 0.0)
    sel_pad = jnp.concatenate([sel, jnp.zeros((LANE - n_sel, tq), jnp.float32)], axis=0)
    sel_q = sel_pad.T.astype(jnp.bfloat16)
    selx_scr[...] = jnp.dot(sel_q, ex_ref[...], preferred_element_type=jnp.float32)

    kpos_l = lax.broadcasted_iota(jnp.int32, (1, tq), 1)
    GS = NSA_GSPLIT
    init = (jnp.full((GS * tq, 1), NEG_INF, jnp.float32), jnp.zeros((GS * tq, 1), jnp.float32),
            jnp.zeros((GS * tq, HEAD_DIM), jnp.float32))

    for g0 in range(0, G, GS):
        q_r = jnp.concatenate(qs[g0:g0 + GS], axis=0)

        def sel_body(kt, carry):
            k0 = pl.multiple_of(kt * tq, tq)
            chosen = jnp.where(selx_scr[:, pl.ds(k0, tq)] > 0.5, 0.0, NEG_INF)
            bias = jnp.where((k0 + kpos_l) <= qpos_col, chosen, NEG_INF)
            return _flash_tile(q_r, ks_ref[pl.ds(k0, tq), :], vs_ref[pl.ds(k0, tq), :], bias, *carry)

        _, l_s, acc_s = lax.fori_loop(0, qt + 1, sel_body, init)
        o_sel = acc_s / l_s

        def win_body(kt, carry):
            k0 = pl.multiple_of(kt * tq, tq)
            kpos = k0 + kpos_l
            band = jnp.where(kpos >= qpos_col - (WINDOW - 1), 0.0, NEG_INF)
            bias = jnp.where(kpos <= qpos_col, band, NEG_INF)
            return _flash_tile(q_r, kw_ref[pl.ds(k0, tq), :], vw_ref[pl.ds(k0, tq), :], bias, *carry)

        _, l_w, acc_w = lax.fori_loop(jnp.maximum(qt - WINDOW // tq, 0), qt + 1, win_body, init)
        o_win = acc_w / l_w

        for gi in range(GS):
            g = g0 + gi
            rows = slice(gi * tq, (gi + 1) * tq)
            o_ref[g] = (gates[:, 3 * g:3 * g + 1] * o_cmp[g] + gates[:, 3 * g + 1:3 * g + 2] * o_sel[rows]
                        + gates[:, 3 * g + 2:3 * g + 3] * o_win[rows])


def _nsa_cols(proj, a, b):
    return proj[..., RWKV_COLS + a:RWKV_COLS + b]


def _nsa_prompt(proj, cmp_w):
    B, T, _ = proj.shape
    KVH, G, HD = NSA_KV_HEADS, NSA_GROUP, HEAD_DIM
    kvc = NSA_KV_COLS
    tq = NSA_TQ
    q = _nsa_cols(proj, 0, D_NSA).reshape(B, T, KVH, G, HD).transpose(0, 2, 3, 1, 4)
    kv = [_nsa_cols(proj, D_NSA + i * kvc, D_NSA + (i + 1) * kvc).reshape(B, T, KVH, HD).transpose(0, 2, 1, 3)
          for i in range(6)]
    kc, vc = kv[0], kv[1]
    ks, vs, kw, vw = [t.astype(jnp.bfloat16) for t in kv[2:]]
    gl = _nsa_cols(proj, D_NSA + 6 * kvc, NSA_COLS).reshape(B, T, KVH, 3 * G).transpose(0, 2, 1, 3)
    cw = jnp.broadcast_to(cmp_w[:, :, None], (2, CMP_BLOCK, HD))
    expand = (jnp.arange(LANE)[:, None] == (jnp.arange(T) // SEL_BLOCK)[None, :]).astype(jnp.bfloat16)
    kv_spec = pl.BlockSpec((None, None, T, HD), lambda b, k, t: (b, k, 0, 0))
    out = pl.pallas_call(
        functools.partial(_nsa_prompt_kernel, seq=T),
        grid=(B, KVH, T // tq),
        in_specs=[pl.BlockSpec((None, None, G, tq, HD), lambda b, k, t: (b, k, 0, t, 0)),
                  pl.BlockSpec((None, None, tq, 3 * G), lambda b, k, t: (b, k, t, 0)),
                  kv_spec, kv_spec, kv_spec, kv_spec, kv_spec, kv_spec,
                  pl.BlockSpec((2, CMP_BLOCK, HD), lambda b, k, t: (0, 0, 0)),
                  pl.BlockSpec((LANE, T), lambda b, k, t: (0, 0))],
        out_specs=pl.BlockSpec((None, None, G, tq, HD), lambda b, k, t: (b, k, 0, t, 0)),
        out_shape=jax.ShapeDtypeStruct((B, KVH, G, T, HD), jnp.float32),
        scratch_shapes=[pltpu.VMEM((LANE, HD), jnp.float32),
                        pltpu.VMEM((LANE, HD), jnp.bfloat16),
                        pltpu.VMEM((tq // LANE, LANE, LANE), jnp.float32),
                        pltpu.VMEM((tq, T), jnp.float32)],
        compiler_params=pltpu.CompilerParams(dimension_semantics=("arbitrary", "arbitrary", "arbitrary"),
                                             vmem_limit_bytes=_VMEM_LIMIT),
        name="nsa_prompt",
    )(q, gl, kc, vc, ks, vs, kw, vw, cw, expand)
    return out.transpose(0, 3, 1, 2, 4).reshape(B, T, D_NSA)


def _nsa_prompt_mix(proj, n_keep, cmp_w):
    B, T, _ = proj.shape
    kv5 = lambda i: _nsa_cols(proj, D_NSA + 2 * i * NSA_KV_COLS, D_NSA + 2 * (i + 1) * NSA_KV_COLS).reshape(
        B, T, 2, NSA_KV_HEADS, HEAD_DIM)
    return _nsa_prompt(proj, cmp_w), kv5(0), kv5(1), kv5(2)[:, -n_keep:]


NSA_PG = 8
NSA_POOL_PG = LANE * CMP_BLOCK // PAGE_SIZE
PAGE_COLS = 2 * NSA_KV_COLS
NSA_ROWS = NSA_KV_HEADS * NSA_GROUP


def _page_specs(layer, n):
    def spec(j):
        return pl.BlockSpec((None, None, 2, NSA_KV_HEADS, HEAD_DIM, PAGE_SIZE),
                            lambda b, s, pt: (layer, pt[b, s * n + j], 0, 0, 0, 0))
    return [spec(j) for j in range(n)]


def _rows_last(t):
    nd = t.ndim
    return jnp.transpose(t, tuple(range(nd - 4)) + (nd - 3, nd - 2, nd - 1, nd - 4))


def _cmp_pool_kernel(pt_ref, *refs):
    pages, cw_ref, o_ref = refs[:NSA_POOL_PG], refs[NSA_POOL_PG], refs[NSA_POOL_PG + 1]
    per_page = PAGE_SIZE // CMP_BLOCK
    row_blk = lax.broadcasted_iota(jnp.int32, (PAGE_SIZE, LANE), 0) // CMP_BLOCK
    col = lax.broadcasted_iota(jnp.int32, (PAGE_SIZE, LANE), 1)
    acc = [jnp.zeros((NSA_KV_COLS, LANE), jnp.float32) for _ in range(2)]
    for j, pg in enumerate(pages):
        seg = jnp.where(col == j * per_page + row_blk, 1.0, 0.0).astype(jnp.bfloat16)
        for kv in range(2):
            xw = pg[kv].reshape(NSA_KV_COLS, PAGE_SIZE) * cw_ref[kv:kv + 1, :]
            hi = xw.astype(jnp.bfloat16)
            lo = (xw - hi.astype(jnp.float32)).astype(jnp.bfloat16)
            acc[kv] = (acc[kv] + jnp.dot(hi, seg, preferred_element_type=jnp.float32)
                       + jnp.dot(lo, seg, preferred_element_type=jnp.float32))
    o_ref[0:NSA_KV_COLS, :] = acc[0]
    o_ref[NSA_KV_COLS:, :] = acc[1]


def _cmp_pool(cache, page_table, layer, cw):
    B, n_pages = page_table.shape
    return pl.pallas_call(
        _cmp_pool_kernel,
        grid_spec=pltpu.PrefetchScalarGridSpec(
            num_scalar_prefetch=1, grid=(B, n_pages // NSA_POOL_PG),
            in_specs=_page_specs(layer, NSA_POOL_PG) + [pl.BlockSpec((2, PAGE_SIZE), lambda b, s, pt: (0, 0))],
            out_specs=pl.BlockSpec((None, PAGE_COLS, LANE), lambda b, s, pt: (b, 0, s))),
        out_shape=jax.ShapeDtypeStruct((B, PAGE_COLS, n_pages * PAGE_SIZE // CMP_BLOCK), jnp.float32),
        compiler_params=pltpu.CompilerParams(dimension_semantics=("arbitrary", "arbitrary"),
                                             vmem_limit_bytes=_VMEM_LIMIT),
        name="nsa_cmp_pool",
    )(page_table, *([cache] * NSA_POOL_PG), cw)


def _cmp_select_kernel(q_ref, kv_ref, ocmp_ref, sel_ref, pt_scr, imp_scr, *, past, steps):
    T = steps
    R = NSA_ROWS * T
    n_cmp = past // CMP_BLOCK
    n_sel = past // SEL_BLOCK + 1
    n_lane = sel_ref.shape[-1]
    ratio = SEL_BLOCK // CMP_BLOCK
    rows_t = lax.broadcasted_iota(jnp.int32, (R, 1), 0) % T
    qpos = past + rows_t
    cmp_end = (lax.broadcasted_iota(jnp.int32, (1, n_cmp), 1) + 1) * CMP_BLOCK - 1
    cmask = cmp_end <= qpos
    q = q_ref[...]
    p_all, o_all = [], []
    for kh in range(NSA_KV_HEADS):
        rows = slice(kh * NSA_GROUP * T, (kh + 1) * NSA_GROUP * T)
        kcb_t = kv_ref[kh * HEAD_DIM:(kh + 1) * HEAD_DIM, :]
        vcb_t = kv_ref[NSA_KV_COLS + kh * HEAD_DIM:NSA_KV_COLS + (kh + 1) * HEAD_DIM, :].astype(jnp.bfloat16)
        s = jnp.dot(q[rows], kcb_t, precision=lax.Precision.HIGHEST, preferred_element_type=jnp.float32)
        s = jnp.where(cmask[rows], s, NEG_INF)
        e = jnp.exp(s - jnp.max(s, axis=-1, keepdims=True))
        p = jnp.where(cmask[rows], e / jnp.sum(e, axis=-1, keepdims=True), 0.0)
        o_all.append(lax.dot_general(p.astype(jnp.bfloat16), vcb_t, (((1,), (1,)), ((), ())),
                                     preferred_element_type=jnp.float32))
        imp = p[0:T]
        for g in range(1, NSA_GROUP):
            imp = imp + p[g * T:(g + 1) * T]
        p_all.extend([imp] * NSA_GROUP)
    ocmp_ref[...] = jnp.concatenate(o_all, axis=0)
    imp_rows = jnp.concatenate(p_all + [jnp.zeros((LANE - R, n_cmp), jnp.float32)], axis=0)
    for c in range(n_cmp // LANE):
        pt_scr[c] = imp_rows[:, c * LANE:(c + 1) * LANE].T
    per = LANE // ratio
    for c in range(n_cmp // LANE):
        part = pt_scr[c, pl.ds(0, per, stride=ratio), :]
        for r in range(1, ratio):
            part = part + pt_scr[c, pl.ds(r, per, stride=ratio), :]
        imp_scr[c * per:(c + 1) * per, :] = part
    n_rows = imp_scr.shape[0]
    imp_scr[n_sel - 1:n_rows, :] = jnp.zeros((n_rows - n_sel + 1, LANE), jnp.float32)
    lane_t = lax.broadcasted_iota(jnp.int32, (1, LANE), 1) % T
    cur = (past + lane_t) // SEL_BLOCK
    blk = lax.broadcasted_iota(jnp.int32, (n_rows, LANE), 0)
    forced = (blk == 0) | (blk == cur) | (blk == cur - 1)
    imp_s = jnp.where(blk > cur, -1.0, jnp.where(forced, FORCE_SCORE, imp_scr[...]))
    imp_s = jnp.where(blk < n_sel, imp_s, -2.0)
    imp_scr[...] = imp_s

    def rank_body(m, rank):
        row = imp_scr[pl.ds(m, 1), :]
        ahead = (row > imp_s) | ((row == imp_s) & (m < blk))
        return rank + jnp.where(ahead, 1, 0)

    rank = lax.fori_loop(0, n_sel, rank_body, jnp.zeros((n_rows, LANE), jnp.int32))
    sel = jnp.where((rank < min(SEL_TOPN, n_sel)) & (imp_s >= 0.0), 1.0, 0.0)
    sel = jnp.concatenate([sel, jnp.zeros((n_lane - n_rows, LANE), jnp.float32)], axis=0)
    for c in range(n_lane // LANE):
        sel_ref[:, c * LANE:(c + 1) * LANE] = sel[c * LANE:(c + 1) * LANE, :].T


def _cmp_select(q_rows, kvcb, past, steps):
    B, R, _ = q_rows.shape
    n_cmp = kvcb.shape[2]
    n_sel = past // SEL_BLOCK + 1
    n_lane = _round_up(n_sel, LANE)
    n_rows = _round_up(n_sel, SUBLANE)
    return pl.pallas_call(
        functools.partial(_cmp_select_kernel, past=past, steps=steps),
        grid=(B,),
        in_specs=[pl.BlockSpec((None, R, HEAD_DIM), lambda b: (b, 0, 0)),
                  pl.BlockSpec((None, PAGE_COLS, n_cmp), lambda b: (b, 0, 0))],
        out_specs=[pl.BlockSpec((None, R, HEAD_DIM), lambda b: (b, 0, 0)),
                   pl.BlockSpec((None, LANE, n_lane), lambda b: (b, 0, 0))],
        out_shape=[jax.ShapeDtypeStruct((B, R, HEAD_DIM), jnp.float32),
                   jax.ShapeDtypeStruct((B, LANE, n_lane), jnp.float32)],
        scratch_shapes=[pltpu.VMEM((n_cmp // LANE, LANE, LANE), jnp.float32),
                        pltpu.VMEM((n_rows, LANE), jnp.float32)],
        compiler_params=pltpu.CompilerParams(dimension_semantics=("arbitrary",),
                                             vmem_limit_bytes=_VMEM_LIMIT),
        name="nsa_cmp_select",
    )(q_rows, kvcb)


def _soft_update(s, mask, v, m, l, acc, keys_last=False):
    s = jnp.where(mask, s, NEG_INF)
    m_new = jnp.maximum(m, jnp.max(s, axis=-1, keepdims=True))
    p = jnp.where(mask, jnp.exp(s - m_new), 0.0)
    alpha = jnp.exp(m - m_new)
    pv = lax.dot_general(p.astype(jnp.bfloat16), v, (((1,), (1 if keys_last else 0,)), ((), ())),
                         preferred_element_type=jnp.float32)
    return m_new, alpha * l + jnp.sum(p, axis=-1, keepdims=True), alpha * acc + pv


def _sel_win_kernel(pt_ref, *refs, past, steps):
    pages = refs[:NSA_PG]
    (q_ref, sel_ref, ocmp_ref, nsel_ref, wpre_ref, nwin_ref, gate_ref,
     o_ref, m_scr, l_scr, acc_scr) = refs[NSA_PG:]
    s_id = pl.program_id(1)
    T = steps
    GT = NSA_GROUP * T
    R = NSA_ROWS * T
    n_lane = sel_ref.shape[-1]
    rows_t = lax.broadcasted_iota(jnp.int32, (GT, 1), 0) % T

    @pl.when(s_id == 0)
    def _():
        m_scr[...] = jnp.full_like(m_scr, NEG_INF)
        l_scr[...] = jnp.zeros_like(l_scr)
        acc_scr[...] = jnp.zeros_like(acc_scr)

    q = q_ref[...].astype(jnp.bfloat16)
    sel_bf = sel_ref[0:R, :].astype(jnp.bfloat16)
    n_keys = NSA_PG * PAGE_SIZE
    blk_id = lax.broadcasted_iota(jnp.int32, (n_lane, n_keys), 0)
    key_blk = s_id * (n_keys // SEL_BLOCK) + lax.broadcasted_iota(jnp.int32, (n_lane, n_keys), 1) // SEL_BLOCK

    def head_cols(x, kh, off):
        return x[:, off + kh * HEAD_DIM:off + (kh + 1) * HEAD_DIM]

    def keys_t(kv, kh):
        return jnp.concatenate([pg[kv, kh] for pg in pages], axis=1).astype(jnp.bfloat16)

    expand = jnp.where(blk_id == key_blk, 1.0, 0.0).astype(jnp.bfloat16)
    chosen = jnp.dot(sel_bf, expand, preferred_element_type=jnp.float32) > 0.5
    head_rows = [slice(kh * GT, (kh + 1) * GT) for kh in range(NSA_KV_HEADS)]
    scores = [jnp.dot(q[rows], keys_t(0, kh), preferred_element_type=jnp.float32)
              for kh, rows in enumerate(head_rows)]
    upd = [_soft_update(scores[kh], chosen[rows], keys_t(1, kh), m_scr[rows], l_scr[rows], acc_scr[rows],
                        keys_last=True) for kh, rows in enumerate(head_rows)]
    for kh, rows in enumerate(head_rows):
        m_scr[rows], l_scr[rows], acc_scr[rows] = upd[kh]

    @pl.when(s_id == pl.num_programs(1) - 1)
    def _():
        new_blk = past // SEL_BLOCK
        tpad = nsel_ref.shape[0]
        jn = lax.broadcasted_iota(jnp.int32, (1, tpad), 1)
        nsel = nsel_ref[...].astype(jnp.bfloat16)
        nwin = nwin_ref[...].astype(jnp.bfloat16)
        jp = lax.broadcasted_iota(jnp.int32, (1, WINDOW), 1)
        gates = jax.nn.sigmoid(gate_ref[...])
        for kh in range(NSA_KV_HEADS):
            rows = slice(kh * GT, (kh + 1) * GT)
            pick = sel_ref[kh * GT:(kh + 1) * GT, new_blk:new_blk + 1] > 0.5
            s = lax.dot_general(q[rows], head_cols(nsel, kh, 0), (((1,), (1,)), ((), ())),
                                preferred_element_type=jnp.float32)
            m, l, acc = _soft_update(s, pick & (jn <= rows_t) & (jn < T), head_cols(nsel, kh, NSA_KV_COLS),
                                     m_scr[rows], l_scr[rows], acc_scr[rows])
            o_sel = acc / l
            init = (jnp.full((GT, 1), NEG_INF, jnp.float32), jnp.zeros((GT, 1), jnp.float32),
                    jnp.zeros((GT, HEAD_DIM), jnp.float32))
            s = jnp.dot(q[rows], wpre_ref[0, kh].astype(jnp.bfloat16), preferred_element_type=jnp.float32)
            st = _soft_update(s, jp > rows_t, wpre_ref[1, kh].astype(jnp.bfloat16), *init, keys_last=True)
            s = lax.dot_general(q[rows], head_cols(nwin, kh, 0), (((1,), (1,)), ((), ())),
                                preferred_element_type=jnp.float32)
            _, l_w, acc_w = _soft_update(s, (jn <= rows_t) & (jn < T), head_cols(nwin, kh, NSA_KV_COLS), *st)
            o_win = acc_w / l_w
            gt = gates[rows]
            o_ref[rows, :] = gt[:, 0:1] * ocmp_ref[rows, :] + gt[:, 1:2] * o_sel + gt[:, 2:3] * o_win


def _sel_win(cache, page_table, layer, q_rows, sel_rows, o_cmp, new_sel, win_prefix, new_win, gate_rows,
             past, steps):
    B, n_pages = page_table.shape
    R = q_rows.shape[1]
    n_lane = sel_rows.shape[-1]
    tpad = new_sel.shape[1]
    per_b = lambda shape: pl.BlockSpec((None,) + shape, lambda b, s, pt: (b,) + (0,) * len(shape))
    return pl.pallas_call(
        functools.partial(_sel_win_kernel, past=past, steps=steps),
        grid_spec=pltpu.PrefetchScalarGridSpec(
            num_scalar_prefetch=1, grid=(B, n_pages // NSA_PG),
            in_specs=_page_specs(layer, NSA_PG) + [
                per_b((R, HEAD_DIM)), per_b((LANE, n_lane)), per_b((R, HEAD_DIM)),
                per_b((tpad, PAGE_COLS)), per_b((2, NSA_KV_HEADS, HEAD_DIM, WINDOW)), per_b((tpad, PAGE_COLS)),
                per_b((R, 3))],
            out_specs=per_b((R, HEAD_DIM)),
            scratch_shapes=[pltpu.VMEM((R, 1), jnp.float32), pltpu.VMEM((R, 1), jnp.float32),
                            pltpu.VMEM((R, HEAD_DIM), jnp.float32)]),
        out_shape=jax.ShapeDtypeStruct((B, R, HEAD_DIM), jnp.float32),
        compiler_params=pltpu.CompilerParams(dimension_semantics=("arbitrary", "arbitrary"),
                                             vmem_limit_bytes=_VMEM_LIMIT),
        name="nsa_sel_win",
    )(page_table, *([cache] * NSA_PG), q_rows, sel_rows, o_cmp, new_sel, win_prefix, new_win, gate_rows)


def _nsa_decode_mix(proj, cache_cmp, cache_sel, page_table, layer, win_prefix, cmp_w):
    B, T, _ = proj.shape
    xn = proj[..., RWKV_COLS:IN_COLS]
    n_pages = page_table.shape[1]
    past = n_pages * PAGE_SIZE
    assert T <= SEL_BLOCK and n_pages % NSA_POOL_PG == 0 and win_prefix.shape[1] == WINDOW
    KVH, G, HD = NSA_KV_HEADS, NSA_GROUP, HEAD_DIM
    kv5 = lambda i: xn[..., D_NSA + 2 * i * NSA_KV_COLS:D_NSA + 2 * (i + 1) * NSA_KV_COLS]
    new_cmp, new_sel, new_win = kv5(0), kv5(1), kv5(2)
    q_rows = (xn[..., :D_NSA] * (HD ** -0.5)).reshape(B, T, KVH, G, HD).transpose(0, 2, 3, 1, 4)
    q_rows = q_rows.reshape(B, KVH * G * T, HD)
    gate_rows = xn[..., D_NSA + 6 * NSA_KV_COLS:].reshape(B, T, KVH, G, 3).transpose(0, 2, 3, 1, 4)
    gate_rows = gate_rows.reshape(B, KVH * G * T, 3)
    cw = jnp.tile(cmp_w, (1, PAGE_SIZE // CMP_BLOCK))
    kvcb = _cmp_pool(_rows_last(cache_cmp), page_table, layer, cw)
    o_cmp, sel_rows = _cmp_select(q_rows, kvcb, past, T)
    tpad = _round_up(T, SUBLANE)
    padt = lambda t: jnp.pad(t, ((0, 0), (0, tpad - T), (0, 0)))
    out = _sel_win(_rows_last(cache_sel), page_table, layer, q_rows, sel_rows, o_cmp, padt(new_sel),
                   _rows_last(win_prefix), padt(new_win), gate_rows, past, T)
    y = out.reshape(B, KVH, G, T, HD).transpose(0, 3, 1, 2, 4).reshape(B, T, D_NSA)
    five = lambda t: t.reshape(B, -1, 2, KVH, HD)
    win_out = jnp.concatenate([win_prefix[:, T:], five(new_win)], axis=1)
    return y, five(new_cmp), five(new_sel), win_out


def _deepnorm(x, h, g, b):
    z = DEEPNORM_ALPHA * x + h
    mu = jnp.mean(z, axis=-1, keepdims=True)
    dev = z - mu
    var = jnp.mean(dev * dev, axis=-1, keepdims=True)
    return dev * lax.rsqrt(var + LN_EPS) * g + b


def _out_proj_kernel(yr_ref, yn_ref, x_ref, w_ref, g_ref, b_ref, o_ref):
    half = yr_ref.shape[1]
    h = (jnp.dot(yr_ref[...].astype(jnp.bfloat16), w_ref[0:half, :], preferred_element_type=jnp.float32)
         + jnp.dot(yn_ref[...].astype(jnp.bfloat16), w_ref[half:, :], preferred_element_type=jnp.float32))
    o_ref[...] = _deepnorm(x_ref[...], h, g_ref[...], b_ref[...])


def _residual_norm_kernel(x_ref, f_ref, g_ref, b_ref, o_ref):
    o_ref[...] = _deepnorm(x_ref[...], f_ref[...], g_ref[...], b_ref[...])


def _residual_norm(x, f, g, b, *, tm=256):
    n, d = x.shape
    tm = min(tm, n)
    rows = pl.BlockSpec((tm, d), lambda i: (i, 0))
    vec = pl.BlockSpec((1, d), lambda i: (0, 0))
    return pl.pallas_call(
        _residual_norm_kernel,
        grid=(n // tm,),
        in_specs=[rows, rows, vec, vec],
        out_specs=rows,
        out_shape=jax.ShapeDtypeStruct((n, d), jnp.float32),
        compiler_params=pltpu.CompilerParams(dimension_semantics=("arbitrary",),
                                             vmem_limit_bytes=_VMEM_LIMIT),
        name="residual_norm",
    )(x, f, g.reshape(1, d), b.reshape(1, d))


def _out_proj_norm(y_r, y_n, x, w_bf, g, b, *, tm=256):
    n, d = x.shape
    tm = min(tm, n)
    half = y_r.shape[1]
    rows = lambda w: pl.BlockSpec((tm, w), lambda i: (i, 0))
    full = lambda shape: pl.BlockSpec(shape, lambda i: (0, 0))
    return pl.pallas_call(
        _out_proj_kernel,
        grid=(n // tm,),
        in_specs=[rows(half), rows(y_n.shape[1]), rows(d), full(w_bf.shape), full((1, d)), full((1, d))],
        out_specs=rows(d),
        out_shape=jax.ShapeDtypeStruct((n, d), jnp.float32),
        compiler_params=pltpu.CompilerParams(dimension_semantics=("arbitrary",),
                                             vmem_limit_bytes=_VMEM_LIMIT),
        name="out_proj_norm",
    )(y_r, y_n, x, w_bf, g.reshape(1, d), b.reshape(1, d))


def _hybrid_layer(x, past, shift_prev, rwkv_s0, n_keep, p):
    B, T, D = x.shape
    proj = _matmul(x.reshape(B * T, D), p['w_in_bf']).reshape(B, T, IN_COLS_PAD)
    y_r, s_T, new_shift = _rwkv_time_mix(proj, shift_prev, rwkv_s0, p)
    if past is None:
        assert T % NSA_TQ == 0 and T >= n_keep
        y_n, new_cmp, new_sel, new_win = _nsa_prompt_mix(proj, n_keep, p['nsa_cmp_w'])
    else:
        assert n_keep == WINDOW
        y_n, new_cmp, new_sel, new_win = _nsa_decode_mix(proj, *past, p['nsa_cmp_w'])
    x = _out_proj_norm(y_r.reshape(B * T, D_RWKV), y_n.reshape(B * T, D_NSA), x.reshape(B * T, D),
                       p['w_out_bf'], p['ln1_g'], p['ln1_b']).reshape(B, T, D)
    f = _peer_ffn(x, p['peer_wq_bf'], p['peer_sk_bf'], p['peer_u_bf'], p['peer_v_bf'])
    x = _residual_norm(x.reshape(B * T, D), f.reshape(B * T, D), p['ln2_g'], p['ln2_b']).reshape(B, T, D)
    return x, (new_cmp, new_sel, new_win, s_T, new_shift)


def kernel(x_prompt, x_sample, cache_cmp_kv, cache_sel_kv, page_table, state_win_kv, state_rwkv,
           state_shift, w_in, rwkv_mu, rwkv_w0, rwkv_w2, rwkv_a0, rwkv_a2, rwkv_g2, rwkv_k_k,
           rwkv_k_a, rwkv_r_k, rwkv_gn_g, rwkv_gn_b, nsa_cmp_w, w_out, ln1_g, ln1_b, peer_wq,
           peer_subkeys, peer_u, peer_v, ln2_g, ln2_b):
    bp = x_prompt.shape[0]
    dt = x_prompt.dtype
    assert page_table.shape[1] * PAGE_SIZE == PAST_LEN
    n_keep = state_win_kv.shape[2]
    zero_shift = jnp.zeros((bp, 1, RWKV_COLS), dt)
    zero_state = jnp.zeros((bp, H_RWKV, HEAD_DIM, HEAD_DIM), dt)
    yp, ys = x_prompt, x_sample
    st_p, st_s = [], []
    for l in range(DEPTH):
        p = {'w_in_bf': jnp.pad(w_in[l].astype(jnp.bfloat16), ((0, 0), (0, IN_COLS_PAD - IN_COLS))),
             'rwkv_mu': rwkv_mu[l], 'rwkv_w0': rwkv_w0[l], 'rwkv_w2': rwkv_w2[l],
             'rwkv_a0': rwkv_a0[l], 'rwkv_a2': rwkv_a2[l], 'rwkv_g2': rwkv_g2[l],
             'rwkv_k_k': rwkv_k_k[l], 'rwkv_k_a': rwkv_k_a[l], 'rwkv_r_k': rwkv_r_k[l],
             'rwkv_gn_g': rwkv_gn_g[l], 'rwkv_gn_b': rwkv_gn_b[l], 'nsa_cmp_w': nsa_cmp_w[l],
             'w_out_bf': w_out[l].astype(jnp.bfloat16), 'ln1_g': ln1_g[l], 'ln1_b': ln1_b[l],
             'peer_wq_bf': peer_wq[l].astype(jnp.bfloat16),
             'peer_sk_bf': peer_subkeys[l].reshape(2 * P_HEADS, N_KEYS, P_DKEY // 2).astype(jnp.bfloat16),
             'peer_u_bf': peer_u[l].astype(jnp.bfloat16).T, 'peer_v_bf': peer_v[l].astype(jnp.bfloat16),
             'ln2_g': ln2_g[l], 'ln2_b': ln2_b[l]}
        yp, sp = _hybrid_layer(yp, None, zero_shift, zero_state, n_keep, p)
        past = (cache_cmp_kv, cache_sel_kv, page_table, l, state_win_kv[l])
        ys, ss = _hybrid_layer(ys, past, state_shift[l], state_rwkv[l], n_keep, p)
        st_p.append(sp)
        st_s.append(ss)
    stk = lambda sts, i: jnp.stack([s[i] for s in sts], axis=0)
    return (yp, ys, stk(st_p, 0), stk(st_p, 1), stk(st_p, 2), stk(st_p, 3), stk(st_p, 4),
            stk(st_s, 0), stk(st_s, 1), stk(st_s, 2), stk(st_s, 3), stk(st_s, 4))
```

```python
import functools
import math

import jax
import jax.numpy as jnp
from jax import lax
from jax.experimental import pallas as pl
from jax.experimental.pallas import tpu as pltpu

D_MODEL = 2048
DEPTH = 2
PAST_LEN = 16384
PAGE_SIZE = 128
HEAD_DIM = 64
D_RWKV = D_MODEL // 2
D_NSA = D_MODEL - D_RWKV
H_RWKV = D_RWKV // HEAD_DIM
H_NSA = D_NSA // HEAD_DIM
NSA_KV_HEADS = 4
NSA_GROUP = H_NSA // NSA_KV_HEADS
NSA_KV_COLS = NSA_KV_HEADS * HEAD_DIM
CMP_BLOCK = 32
SEL_BLOCK = 64
SEL_TOPN = 16
WINDOW = 512
W_LORA = 64
A_LORA = 64
G_LORA = 160
RWKV_COLS = 3 * D_RWKV + W_LORA + A_LORA + G_LORA
NSA_COLS = D_NSA + 6 * NSA_KV_COLS + 3 * H_NSA
IN_COLS = RWKV_COLS + NSA_COLS
P_HEADS = 8
N_KEYS = 128
P_DKEY = 256
P_TOPK = 16
LN_EPS = 1e-5
GN_EPS = 64e-5
DEEPNORM_ALPHA = (2 * DEPTH) ** 0.25
FORCE_SCORE = 1e4
NEG_INF = -1e30

N_EXPERTS = N_KEYS * N_KEYS

LANE = 128
SUBLANE = 8
_VMEM_LIMIT = 56 * 1024 * 1024


def _round_up(x, m):
    return -(-x // m) * m


PROJ_TILE = 1024
IN_COLS_PAD = _round_up(IN_COLS, PROJ_TILE)


def _mm_kernel(x_ref, w_ref, o_ref):
    o_ref[...] = jnp.dot(x_ref[...].astype(jnp.bfloat16), w_ref[...], preferred_element_type=jnp.float32)


def _matmul(x, w_bf):
    m, k = x.shape
    n = w_bf.shape[1]
    tm = min(PROJ_TILE, m)
    tn = PROJ_TILE
    return pl.pallas_call(
        _mm_kernel,
        grid=(m // tm, n // tn),
        in_specs=[pl.BlockSpec((tm, k), lambda i, j: (i, 0)),
                  pl.BlockSpec((k, tn), lambda i, j: (0, j))],
        out_specs=pl.BlockSpec((tm, tn), lambda i, j: (i, j)),
        out_shape=jax.ShapeDtypeStruct((m, n), jnp.float32),
        compiler_params=pltpu.CompilerParams(dimension_semantics=("arbitrary", "arbitrary"),
                                             vmem_limit_bytes=_VMEM_LIMIT),
        name="in_proj",
    )(x, w_bf)


RW_CHUNK = 64
RW_PAIRS = D_RWKV // LANE
RW_SEQS = 2


def _bdot(a, b):
    return jnp.dot(a.astype(jnp.bfloat16), b.astype(jnp.bfloat16), preferred_element_type=jnp.float32)


def _bdot_nt(a, b):
    return lax.dot_general(a.astype(jnp.bfloat16), b.astype(jnp.bfloat16), (((1,), (1,)), ((), ())),
                           preferred_element_type=jnp.float32)


def _bdot_tn(a, b):
    return lax.dot_general(a.astype(jnp.bfloat16), b.astype(jnp.bfloat16), (((0,), (0,)), ((), ())),
                           preferred_element_type=jnp.float32)


def _head_sum(x, same_head_bf):
    hi = x.astype(jnp.bfloat16)
    lo = (x - hi.astype(jnp.float32)).astype(jnp.bfloat16)
    return (jnp.dot(hi, same_head_bf, preferred_element_type=jnp.float32)
            + jnp.dot(lo, same_head_bf, preferred_element_type=jnp.float32))


def _rwkv_mix_kernel(x_ref, shift_ref, mu_ref, pv_ref, w2_ref, a2_ref, g2_ref, s0_ref, y_ref, st_ref,
                     prev_scr, *, steps):
    c = pl.program_id(1)
    C = RW_CHUNK
    f32 = jnp.float32

    @pl.when(c == 0)
    def _():
        st_ref[...] = s0_ref[...]
        prev_scr[...] = shift_ref[...]

    row_c = lax.broadcasted_iota(jnp.int32, (C, 1), 0)
    o_wl = 3 * D_RWKV
    w0, a0_, k_k, k_a = pv_ref[0:1, :], pv_ref[1:2, :], pv_ref[2:3, :], pv_ref[3:4, :]
    r_k, gn_g, gn_b = pv_ref[4:5, :], pv_ref[5:6, :], pv_ref[6:7, :]

    def prep(nb):
        xr = x_ref[nb, :, 0:RWKV_COLS]
        prev = jnp.where(row_c == 0, prev_scr[nb], pltpu.roll(xr, 1, 0))
        prev_scr[nb] = xr[C - 1:C, :]
        xs = xr + mu_ref[...] * (prev - xr)
        r_s, k_s, v_s = xs[:, 0:D_RWKV], xs[:, D_RWKV:2 * D_RWKV], xs[:, 2 * D_RWKV:o_wl]
        wl = xs[:, o_wl:o_wl + W_LORA]
        al = xs[:, o_wl + W_LORA:o_wl + W_LORA + A_LORA]
        gl = xs[:, o_wl + W_LORA + A_LORA:RWKV_COLS]
        w_s = -jax.nn.softplus(-(w0 + _bdot(jnp.tanh(wl), w2_ref[...]))) - 0.5
        lw_s = -jnp.exp(w_s)
        a_s = jax.nn.sigmoid(a0_ + _bdot(al, a2_ref[...]))
        gate_s = _bdot(jax.nn.sigmoid(gl), g2_ref[...])
        kkf_s = k_s * k_k
        k2_s = k_s * (1.0 + (a_s - 1.0) * k_a)
        if steps % C:
            live = (c * C + row_c) < steps
            lw_s = jnp.where(live, lw_s, 0.0)
            kkf_s = jnp.where(live, kkf_s, 0.0)
            k2_s = jnp.where(live, k2_s, 0.0)
            v_s = jnp.where(live, v_s, 0.0)
        return lw_s, v_s, k2_s, r_s, kkf_s, a_s, gate_s

    seqs = [prep(nb) for nb in range(x_ref.shape[0])]

    row = lax.broadcasted_iota(jnp.int32, (C, C), 0)
    col = lax.broadcasted_iota(jnp.int32, (C, C), 1)
    tri_incl = col <= row
    tri_strict = col < row
    tri_f = jnp.where(tri_incl, 1.0, 0.0).astype(f32)
    eye = jnp.where(row == col, 1.0, 0.0).astype(f32)
    lane = lax.broadcasted_iota(jnp.int32, (1, LANE), 1)
    head0 = lane < HEAD_DIM
    r128 = lax.broadcasted_iota(jnp.int32, (LANE, LANE), 0)
    c128 = lax.broadcasted_iota(jnp.int32, (LANE, LANE), 1)
    same_head = (r128 < HEAD_DIM) == (c128 < HEAD_DIM)
    same_head_bf = jnp.where(same_head, 1.0, 0.0).astype(jnp.bfloat16)
    eye128 = r128 == c128

    pairs = range(len(seqs) * RW_PAIRS)
    seq_of = [it // RW_PAIRS for it in pairs]
    pair_of = [it % RW_PAIRS for it in pairs]
    heads = (head0, ~head0)
    sls = [slice(pair_of[it] * LANE, (pair_of[it] + 1) * LANE) for it in pairs]
    field = lambda i: [seqs[seq_of[it]][i][:, sls[it]] for it in pairs]
    lw, v, k, r, kkf, a_lr, gate = [field(i) for i in range(7)]
    cum = [jnp.dot(tri_f, lw[pr], precision=lax.Precision.HIGHEST, preferred_element_type=f32)
           for pr in pairs]
    tot = [cum[pr][C - 1:C, :] for pr in pairs]
    ss = [_head_sum(kkf[pr] * kkf[pr], same_head_bf) for pr in pairs]
    kk = [kkf[pr] * lax.rsqrt(jnp.maximum(ss[pr], 1e-24)) for pr in pairs]
    b = [kk[pr] * a_lr[pr] for pr in pairs]
    x, ym = [], []
    for pr in pairs:
        e_neg = jnp.exp(-cum[pr])
        x.append(jnp.concatenate([kk[pr] * jnp.exp(cum[pr] - lw[pr]),
                                  r[pr] * jnp.exp(cum[pr])], axis=0))
        ym.append(jnp.concatenate([k[pr] * e_neg, b[pr] * e_neg], axis=0))
    a0 = [st_ref[seq_of[pr], pair_of[pr]] for pr in pairs]
    xa = [_bdot(x[pr], a0[pr]) for pr in pairs]
    g = [[_bdot_nt(jnp.where(hm, x[pr], 0.0), ym[pr]) for hm in heads] for pr in pairs]
    lkv = [[_bdot(jnp.where(tri_strict, g[pr][h][:C, :C], 0.0), v[pr]) for h in range(2)] for pr in pairs]
    npow = [[jnp.where(tri_strict, -g[pr][h][:C, C:], 0.0) for h in range(2)] for pr in pairs]
    tmat = [[eye + npow[pr][h] for h in range(2)] for pr in pairs]
    for _ in range(int(math.log2(C)) - 1):
        npow = [[_bdot(npow[pr][h], npow[pr][h]) for h in range(2)] for pr in pairs]
        tmat = [[tmat[pr][h] + _bdot(tmat[pr][h], npow[pr][h]) for h in range(2)] for pr in pairs]
    rhs = [xa[pr][:C] + jnp.where(head0, lkv[pr][0], lkv[pr][1]) for pr in pairs]
    w = [jnp.where(head0, _bdot(tmat[pr][0], rhs[pr]), _bdot(tmat[pr][1], rhs[pr])) for pr in pairs]
    vw = [jnp.concatenate([v[pr], w[pr]], axis=0) for pr in pairs]
    y = []
    for pr in pairs:
        mr = [jnp.concatenate([jnp.where(tri_incl, g[pr][h][C:, :C], 0.0),
                               jnp.where(tri_incl, -g[pr][h][C:, C:], 0.0)], axis=1) for h in range(2)]
        y.append(xa[pr][C:] + jnp.where(head0, _bdot(mr[0], vw[pr]), _bdot(mr[1], vw[pr])))
    inv_hd = 1.0 / HEAD_DIM
    mean = [_head_sum(y[pr], same_head_bf) * inv_hd for pr in pairs]
    dev = [y[pr] - mean[pr] for pr in pairs]
    var = [_head_sum(dev[pr] * dev[pr], same_head_bf) * inv_hd for pr in pairs]
    rk = [_head_sum(r[pr] * k[pr] * r_k[:, sls[pr]], same_head_bf) for pr in pairs]
    for pr in pairs:
        yn = dev[pr] * lax.rsqrt(var[pr] + GN_EPS) * gn_g[:, sls[pr]] + gn_b[:, sls[pr]]
        y_ref[seq_of[pr], :, sls[pr]] = (yn + rk[pr] * v[pr]) * gate[pr]
    for pr in pairs:
        e_rem = jnp.exp(tot[pr] - cum[pr])
        kb = jnp.concatenate([k[pr] * e_rem, -(b[pr] * e_rem)], axis=0)
        upd = _bdot_tn(kb, vw[pr])
        p_col = jnp.sum(jnp.where(eye128, jnp.exp(tot[pr]), 0.0), axis=1, keepdims=True)
        st_ref[seq_of[pr], pair_of[pr]] = a0[pr] * p_col + jnp.where(same_head, upd, 0.0)


def _rwkv_time_mix(proj, shift_prev, s0, p):
    B, T, W = proj.shape
    C = RW_CHUNK
    t_pad = _round_up(T, C)
    x = proj if t_pad == T else jnp.pad(proj, ((0, 0), (0, t_pad - T), (0, 0)))
    a = jnp.swapaxes(s0, -1, -2).reshape(B, RW_PAIRS, 2, HEAD_DIM, HEAD_DIM)
    z = jnp.zeros_like(a[:, :, 0])
    a0 = jnp.concatenate([jnp.concatenate([a[:, :, 0], z], axis=-1),
                          jnp.concatenate([z, a[:, :, 1]], axis=-1)], axis=-2)
    zero = jnp.zeros((D_RWKV,), jnp.float32)
    pv = jnp.stack([p['rwkv_w0'], p['rwkv_a0'], p['rwkv_k_k'], p['rwkv_k_a'], p['rwkv_r_k'].reshape(-1),
                    p['rwkv_gn_g'], p['rwkv_gn_b'], zero])
    bf = lambda t: t.astype(jnp.bfloat16)
    full = lambda shape: pl.BlockSpec(shape, lambda bi, ci: (0,) * len(shape))
    nb = RW_SEQS
    assert B % nb == 0
    st_spec = pl.BlockSpec((nb, RW_PAIRS, LANE, LANE), lambda bi, ci: (bi, 0, 0, 0))
    y, st = pl.pallas_call(
        functools.partial(_rwkv_mix_kernel, steps=T),
        grid=(B // nb, t_pad // C),
        in_specs=[pl.BlockSpec((nb, C, W), lambda bi, ci: (bi, ci, 0)),
                  pl.BlockSpec((nb, 1, RWKV_COLS), lambda bi, ci: (bi, 0, 0)),
                  full((1, RWKV_COLS)), full((8, D_RWKV)), full((W_LORA, D_RWKV)),
                  full((A_LORA, D_RWKV)), full((G_LORA, D_RWKV)), st_spec],
        out_specs=[pl.BlockSpec((nb, C, D_RWKV), lambda bi, ci: (bi, ci, 0)), st_spec],
        out_shape=[jax.ShapeDtypeStruct((B, t_pad, D_RWKV), jnp.float32),
                   jax.ShapeDtypeStruct((B, RW_PAIRS, LANE, LANE), jnp.float32)],
        scratch_shapes=[pltpu.VMEM((nb, 1, RWKV_COLS), jnp.float32)],
        compiler_params=pltpu.CompilerParams(dimension_semantics=("arbitrary", "arbitrary"),
                                             vmem_limit_bytes=_VMEM_LIMIT),
        name="rwkv_mix",
    )(x, shift_prev, p['rwkv_mu'].reshape(1, RWKV_COLS), pv, bf(p['rwkv_w2']), bf(p['rwkv_a2']),
      bf(p['rwkv_g2']), a0)
    s_t = jnp.stack([st[:, :, :HEAD_DIM, :HEAD_DIM], st[:, :, HEAD_DIM:, HEAD_DIM:]], axis=2)
    s_t = jnp.swapaxes(s_t.reshape(B, H_RWKV, HEAD_DIM, HEAD_DIM), -1, -2)
    return y[:, :T], s_t, proj[:, T - 1:T, :RWKV_COLS]


def _extract_top(buf_ref, out_ref):
    n_slabs, nrows, width = buf_ref.shape
    iota = lax.broadcasted_iota(jnp.int32, (nrows, width), 0)

    def body(r, carry):
        for g in range(n_slabs):
            s = buf_ref[g]
            mx = jnp.max(s, axis=0, keepdims=True)
            first = jnp.min(jnp.where(s == mx, iota, nrows), axis=0, keepdims=True)
            buf_ref[g] = jnp.where(iota == first, -jnp.inf, s)
            out_ref[g, pl.ds(r, 1), :] = mx
        return carry

    lax.fori_loop(0, P_TOPK, body, 0)


def _peer_score_kernel(x_ref, wq_ref, sk_ref, s1_ref, s2_ref, st_ref, buf, cand, hv, tv):
    tn = x_ref.shape[0]
    q = jnp.dot(x_ref[...].astype(jnp.bfloat16), wq_ref[...],
                preferred_element_type=jnp.float32).astype(jnp.bfloat16)
    for h in range(P_HEADS):
        for c, s_ref in enumerate((s1_ref, s2_ref)):
            col = (2 * h + c) * N_KEYS
            s_t = lax.dot_general(sk_ref[2 * h + c], q[:, col:col + N_KEYS],
                                  (((1,), (1,)), ((), ())), preferred_element_type=jnp.float32)
            s_ref[h] = s_t
            buf[2 * h + c] = s_t
    _extract_top(buf, hv)
    n_cand = sum(P_TOPK // (a + 1) for a in range(P_TOPK))
    for h in range(P_HEADS):
        pieces = [hv[2 * h, a:a + 1, :] + hv[2 * h + 1, 0:P_TOPK // (a + 1), :] for a in range(P_TOPK)]
        pieces.append(jnp.full((cand.shape[1] - n_cand, tn), -jnp.inf, jnp.float32))
        cand[h] = jnp.concatenate(pieces, axis=0)
    _extract_top(cand, tv)
    for h in range(P_HEADS):
        z = jnp.sum(jnp.exp(tv[h] - tv[h, 0:1, :]), axis=0, keepdims=True)
        st_ref[0, h:h + 1, :] = tv[h, P_TOPK - 1:P_TOPK, :]
        st_ref[1, h:h + 1, :] = hv[2 * h, 0:1, :]
        st_ref[2, h:h + 1, :] = hv[2 * h + 1, 0:1, :]
        st_ref[3, h:h + 1, :] = 1.0 / z


def _peer_scores(x, wq_bf, sk_bf, *, tn=256):
    n = x.shape[0]
    tn = min(tn, n)
    return pl.pallas_call(
        _peer_score_kernel,
        grid=(n // tn,),
        in_specs=[pl.BlockSpec((tn, D_MODEL), lambda i: (i, 0)),
                  pl.BlockSpec((D_MODEL, P_HEADS * P_DKEY), lambda i: (0, 0)),
                  pl.BlockSpec((2 * P_HEADS, N_KEYS, P_DKEY // 2), lambda i: (0, 0, 0))],
        out_specs=[pl.BlockSpec((P_HEADS, N_KEYS, tn), lambda i: (0, 0, i)),
                   pl.BlockSpec((P_HEADS, N_KEYS, tn), lambda i: (0, 0, i)),
                   pl.BlockSpec((4, P_HEADS, tn), lambda i: (0, 0, i))],
        out_shape=[jax.ShapeDtypeStruct((P_HEADS, N_KEYS, n), jnp.float32),
                   jax.ShapeDtypeStruct((P_HEADS, N_KEYS, n), jnp.float32),
                   jax.ShapeDtypeStruct((4, P_HEADS, n), jnp.float32)],
        scratch_shapes=[pltpu.VMEM((2 * P_HEADS, N_KEYS, tn), jnp.float32),
                        pltpu.VMEM((P_HEADS, _round_up(sum(P_TOPK // (a + 1) for a in range(P_TOPK)), SUBLANE),
                                    tn),
                                   jnp.float32),
                        pltpu.VMEM((2 * P_HEADS, P_TOPK, tn), jnp.float32),
                        pltpu.VMEM((P_HEADS, P_TOPK, tn), jnp.float32)],
        compiler_params=pltpu.CompilerParams(dimension_semantics=("arbitrary",),
                                             vmem_limit_bytes=_VMEM_LIMIT),
        name="peer_scores",
    )(x, wq_bf, sk_bf)


def _peer_expert_kernel(x_ref, u_ref, v_ref, s1_ref, s2_ref, st_ref, o_ref, e2_scr, h_scr, p_scr,
                        *, chunk):
    j = pl.program_id(1)
    tn = x_ref.shape[0]
    te = u_ref.shape[1]

    @pl.when(j == 0)
    def _():
        o_ref[...] = jnp.zeros_like(o_ref)
        for h in range(P_HEADS):
            e2_scr[h] = jnp.exp(s2_ref[h] - st_ref[2, h:h + 1, :])

    def hidden(c):
        tsl = slice(c * chunk, (c + 1) * chunk)
        h_scr[tsl, :] = jnp.dot(x_ref[tsl, :], u_ref[...], preferred_element_type=jnp.float32)

    a_rows = [[s1_ref[h, pl.ds(j * (te // N_KEYS) + ii, 1), :] for h in range(P_HEADS)]
              for ii in range(te // N_KEYS)]
    e1_rows = [[jnp.exp(a_rows[ii][h] - st_ref[1, h:h + 1, :]) * st_ref[3, h:h + 1, :]
                for h in range(P_HEADS)] for ii in range(te // N_KEYS)]

    def weigh(c):
        for sub in range(chunk // LANE):
            weigh_lanes(slice(c * chunk + sub * LANE, c * chunk + (sub + 1) * LANE))

    def weigh_lanes(tsl):
        for ii in range(te // N_KEYS):
            w_t = jnp.zeros((N_KEYS, LANE), jnp.float32)
            for h in range(P_HEADS):
                val = a_rows[ii][h][:, tsl] + s2_ref[h, :, tsl]
                w_t = w_t + jnp.where(val >= st_ref[0, h:h + 1, tsl],
                                      e1_rows[ii][h][:, tsl] * e2_scr[h, :, tsl], 0.0)
            hh = h_scr[tsl, ii * N_KEYS:(ii + 1) * N_KEYS]
            g = 0.5 * hh * (1.0 + lax.erf(hh * 0.7071067811865476))
            p_scr[tsl, ii * N_KEYS:(ii + 1) * N_KEYS] = (w_t.T * g).astype(jnp.bfloat16)

    def project(c):
        tsl = slice(c * chunk, (c + 1) * chunk)
        o_ref[tsl, :] += jnp.dot(p_scr[tsl, :], v_ref[...], preferred_element_type=jnp.float32)

    n_chunks = tn // chunk
    hidden(0)
    for c in range(n_chunks):
        if c + 1 < n_chunks:
            hidden(c + 1)
        weigh(c)
        project(c)


def _peer_experts(x_bf, u_bf, v_bf, s1t, s2t, stats, *, tn=1024, te=512, chunk=256):
    n = x_bf.shape[0]
    tn = min(tn, n)
    chunk = min(chunk, tn)
    once = pl.Buffered(1)
    return pl.pallas_call(
        functools.partial(_peer_expert_kernel, chunk=chunk),
        grid=(n // tn, N_EXPERTS // te),
        in_specs=[pl.BlockSpec((tn, D_MODEL), lambda i, j: (i, 0), pipeline_mode=once),
                  pl.BlockSpec((D_MODEL, te), lambda i, j: (0, j)),
                  pl.BlockSpec((te, D_MODEL), lambda i, j: (j, 0)),
                  pl.BlockSpec((P_HEADS, N_KEYS, tn), lambda i, j: (0, 0, i), pipeline_mode=once),
                  pl.BlockSpec((P_HEADS, N_KEYS, tn), lambda i, j: (0, 0, i), pipeline_mode=once),
                  pl.BlockSpec((4, P_HEADS, tn), lambda i, j: (0, 0, i), pipeline_mode=once)],
        out_specs=pl.BlockSpec((tn, D_MODEL), lambda i, j: (i, 0)),
        out_shape=jax.ShapeDtypeStruct((n, D_MODEL), jnp.float32),
        scratch_shapes=[pltpu.VMEM((P_HEADS, N_KEYS, tn), jnp.float32),
                        pltpu.VMEM((tn, te), jnp.float32),
                        pltpu.VMEM((tn, te), jnp.bfloat16)],
        compiler_params=pltpu.CompilerParams(dimension_semantics=("arbitrary", "arbitrary"),
                                             vmem_limit_bytes=_VMEM_LIMIT),
        name="peer_experts",
    )(x_bf, u_bf, v_bf, s1t, s2t, stats)


def _peer_ffn(x, wq_bf, sk_bf, u_bf, v_bf):
    B, T, D = x.shape
    n = B * T
    xt = x.reshape(n, D)
    n_pad = _round_up(n, LANE)
    if n_pad != n:
        xt = jnp.pad(xt, ((0, n_pad - n), (0, 0)))
    s1t, s2t, stats = _peer_scores(xt, wq_bf, sk_bf)
    out = _peer_experts(xt.astype(jnp.bfloat16), u_bf, v_bf, s1t, s2t, stats)
    return out[:n].reshape(B, T, D)


NSA_TQ = 256
NSA_GSPLIT = 4


def _flash_tile(q4, k_t, v_t, bias, m, l, acc):
    tq, tk = bias.shape
    s = lax.dot_general(q4, k_t, (((1,), (1,)), ((), ())), preferred_element_type=jnp.float32)
    s = (s.reshape(-1, tq, tk) + bias[None]).reshape(-1, tk)
    m_new = jnp.maximum(m, jnp.max(s, axis=-1, keepdims=True))
    p = jnp.exp(s - m_new)
    alpha = jnp.exp(m - m_new)
    l_new = alpha * l + jnp.sum(p, axis=-1, keepdims=True)
    acc_new = alpha * acc + jnp.dot(p.astype(jnp.bfloat16), v_t, preferred_element_type=jnp.float32)
    return m_new, l_new, acc_new


def _nsa_prompt_kernel(q_ref, gl_ref, kc_ref, vc_ref, ks_ref, vs_ref, kw_ref, vw_ref, cw_ref, ex_ref,
                       o_ref, kcb_scr, vcb_scr, imp_scr, *, seq):
    qt = pl.program_id(2)
    tq = NSA_TQ
    n_cmp = seq // CMP_BLOCK
    n_sel = seq // SEL_BLOCK
    G = NSA_GROUP
    q0 = qt * tq

    @pl.when(qt == 0)
    def _():
        kc = kc_ref[...].reshape(n_cmp, CMP_BLOCK, HEAD_DIM)
        vc = vc_ref[...].reshape(n_cmp, CMP_BLOCK, HEAD_DIM)
        kcb_scr[...] = jnp.zeros_like(kcb_scr)
        vcb_scr[...] = jnp.zeros_like(vcb_scr)
        kcb_scr[0:n_cmp, :] = jnp.sum(kc * cw_ref[0][None], axis=1)
        vcb_scr[0:n_cmp, :] = jnp.sum(vc * cw_ref[1][None], axis=1).astype(jnp.bfloat16)

    qpos_col = q0 + lax.broadcasted_iota(jnp.int32, (tq, 1), 0)
    qf = [q_ref[g] * (HEAD_DIM ** -0.5) for g in range(G)]
    qs = [t.astype(jnp.bfloat16) for t in qf]
    gates = jax.nn.sigmoid(gl_ref[...])

    cmp_end = (lax.broadcasted_iota(jnp.int32, (1, LANE), 1) + 1) * CMP_BLOCK - 1
    cmask = (cmp_end <= qpos_col) & (lax.broadcasted_iota(jnp.int32, (1, LANE), 1) < n_cmp)
    s = lax.dot_general(jnp.concatenate(qf, axis=0), kcb_scr[...], (((1,), (1,)), ((), ())),
                        precision=lax.Precision.HIGHEST, preferred_element_type=jnp.float32)
    s = jnp.where(cmask[None], s.reshape(G, tq, LANE), NEG_INF).reshape(G * tq, LANE)
    e = jnp.exp(s - jnp.max(s, axis=-1, keepdims=True))
    p = e / jnp.sum(e, axis=-1, keepdims=True)
    p = jnp.where(cmask[None], p.reshape(G, tq, LANE), 0.0)
    o_cmp_all = jnp.dot(p.reshape(G * tq, LANE).astype(jnp.bfloat16), vcb_scr[...],
                        preferred_element_type=jnp.float32)
    o_cmp = [o_cmp_all[g * tq:(g + 1) * tq] for g in range(G)]
    imp = p[0]
    for g in range(1, G):
        imp = imp + p[g]
    imp_t = imp.T
    ratio = SEL_BLOCK // CMP_BLOCK
    parts = []
    for c in range(tq // LANE):
        imp_scr[c] = imp_t[:, c * LANE:(c + 1) * LANE]
        part = imp_scr[c, pl.ds(0, n_sel, stride=ratio), :]
        for r in range(1, ratio):
            part = part + imp_scr[c, pl.ds(r, n_sel, stride=ratio), :]
        parts.append(part)
    imp_s = jnp.concatenate(parts, axis=1)
    qpos_row = q0 + lax.broadcasted_iota(jnp.int32, (1, tq), 1)
    cur = qpos_row // SEL_BLOCK
    blk = lax.broadcasted_iota(jnp.int32, (n_sel, tq), 0)
    forced = (blk == 0) | (blk == cur) | (blk == cur - 1)
    imp_s = jnp.where(blk > cur, -1.0, jnp.where(forced, FORCE_SCORE, imp_s))
    rank = jnp.zeros((n_sel, tq), jnp.int32)
    for mm in range(n_sel):
        row = imp_s[mm:mm + 1, :]
        ahead = (row > imp_s) | ((row == imp_s) & (mm < blk))
        rank = rank + jnp.where(ahead, 1, 0)
    sel = jnp.where((rank < min(SEL_TOPN, n_sel)) & (imp_s >= 0.0), 1.0, 0.0)
    sel_pad = jnp.concatenate([sel, jnp.zeros((LANE - n_sel, tq), jnp.float32)], axis=0)
    sel_q = sel_pad.T.astype(jnp.bfloat16)

    kpos_l = lax.broadcasted_iota(jnp.int32, (1, tq), 1)
    GS = NSA_GSPLIT
    init = (jnp.full((GS * tq, 1), NEG_INF, jnp.float32), jnp.zeros((GS * tq, 1), jnp.float32),
            jnp.zeros((GS * tq, HEAD_DIM), jnp.float32))

    for g0 in range(0, G, GS):
        q_r = jnp.concatenate(qs[g0:g0 + GS], axis=0)

        def sel_body(kt, carry):
            k0 = pl.multiple_of(kt * tq, tq)
            picked = jnp.dot(sel_q, ex_ref[:, pl.ds(k0, tq)], preferred_element_type=jnp.float32)
            chosen = jnp.where(picked > 0.5, 0.0, NEG_INF)
            bias = jnp.where((k0 + kpos_l) <= qpos_col, chosen, NEG_INF)
            return _flash_tile(q_r, ks_ref[pl.ds(k0, tq), :], vs_ref[pl.ds(k0, tq), :], bias, *carry)

        _, l_s, acc_s = lax.fori_loop(0, qt + 1, sel_body, init)

        def win_body(kt, carry):
            k0 = pl.multiple_of(kt * tq, tq)
            kpos = k0 + kpos_l
            band = jnp.where(kpos >= qpos_col - (WINDOW - 1), 0.0, NEG_INF)
            bias = jnp.where(kpos <= qpos_col, band, NEG_INF)
            return _flash_tile(q_r, kw_ref[pl.ds(k0, tq), :], vw_ref[pl.ds(k0, tq), :], bias, *carry)

        _, l_w, acc_w = lax.fori_loop(jnp.maximum(qt - WINDOW // tq, 0), qt + 1, win_body, init)

        for gi in range(GS):
            g = g0 + gi
            rows = slice(gi * tq, (gi + 1) * tq)
            c_sel = gates[:, 3 * g + 1:3 * g + 2] / l_s[rows]
            c_win = gates[:, 3 * g + 2:3 * g + 3] / l_w[rows]
            o_ref[g] = gates[:, 3 * g:3 * g + 1] * o_cmp[g] + c_sel * acc_s[rows] + c_win * acc_w[rows]


def _nsa_cols(proj, a, b):
    return proj[..., RWKV_COLS + a:RWKV_COLS + b]


def _nsa_prompt(proj, cmp_w):
    B, T, _ = proj.shape
    KVH, G, HD = NSA_KV_HEADS, NSA_GROUP, HEAD_DIM
    kvc = NSA_KV_COLS
    tq = NSA_TQ
    q = _nsa_cols(proj, 0, D_NSA).reshape(B, T, KVH, G, HD).transpose(0, 2, 3, 1, 4)
    kv = [_nsa_cols(proj, D_NSA + i * kvc, D_NSA + (i + 1) * kvc).reshape(B, T, KVH, HD).transpose(0, 2, 1, 3)
          for i in range(6)]
    kc, vc = kv[0], kv[1]
    ks, vs, kw, vw = [t.astype(jnp.bfloat16) for t in kv[2:]]
    gl = _nsa_cols(proj, D_NSA + 6 * kvc, NSA_COLS).reshape(B, T, KVH, 3 * G).transpose(0, 2, 1, 3)
    cw = jnp.broadcast_to(cmp_w[:, :, None], (2, CMP_BLOCK, HD))
    expand = (jnp.arange(LANE)[:, None] == (jnp.arange(T) // SEL_BLOCK)[None, :]).astype(jnp.bfloat16)
    kv_spec = pl.BlockSpec((None, None, T, HD), lambda b, k, t: (b, k, 0, 0))
    out = pl.pallas_call(
        functools.partial(_nsa_prompt_kernel, seq=T),
        grid=(B, KVH, T // tq),
        in_specs=[pl.BlockSpec((None, None, G, tq, HD), lambda b, k, t: (b, k, 0, t, 0)),
                  pl.BlockSpec((None, None, tq, 3 * G), lambda b, k, t: (b, k, t, 0)),
                  kv_spec, kv_spec, kv_spec, kv_spec, kv_spec, kv_spec,
                  pl.BlockSpec((2, CMP_BLOCK, HD), lambda b, k, t: (0, 0, 0)),
                  pl.BlockSpec((LANE, T), lambda b, k, t: (0, 0))],
        out_specs=pl.BlockSpec((None, None, G, tq, HD), lambda b, k, t: (b, k, 0, t, 0)),
        out_shape=jax.ShapeDtypeStruct((B, KVH, G, T, HD), jnp.float32),
        scratch_shapes=[pltpu.VMEM((LANE, HD), jnp.float32),
                        pltpu.VMEM((LANE, HD), jnp.bfloat16),
                        pltpu.VMEM((tq // LANE, LANE, LANE), jnp.float32)],
        compiler_params=pltpu.CompilerParams(dimension_semantics=("arbitrary", "arbitrary", "arbitrary"),
                                             vmem_limit_bytes=_VMEM_LIMIT),
        name="nsa_prompt",
    )(q, gl, kc, vc, ks, vs, kw, vw, cw, expand)
    return out.transpose(0, 3, 1, 2, 4).reshape(B, T, D_NSA)


def _nsa_prompt_mix(proj, n_keep, cmp_w):
    B, T, _ = proj.shape
    kv5 = lambda i: _nsa_cols(proj, D_NSA + 2 * i * NSA_KV_COLS, D_NSA + 2 * (i + 1) * NSA_KV_COLS).reshape(
        B, T, 2, NSA_KV_HEADS, HEAD_DIM)
    return _nsa_prompt(proj, cmp_w), kv5(0), kv5(1), kv5(2)[:, -n_keep:]


NSA_PG = 8
NSA_POOL_PG = LANE * CMP_BLOCK // PAGE_SIZE
PAGE_COLS = 2 * NSA_KV_COLS
NSA_ROWS = NSA_KV_HEADS * NSA_GROUP


def _page_specs(layer, n):
    def spec(j):
        return pl.BlockSpec((None, None, 2, NSA_KV_HEADS, HEAD_DIM, PAGE_SIZE),
                            lambda b, s, pt: (layer, pt[b, s * n + j], 0, 0, 0, 0))
    return [spec(j) for j in range(n)]


def _rows_last(t):
    nd = t.ndim
    return jnp.transpose(t, tuple(range(nd - 4)) + (nd - 3, nd - 2, nd - 1, nd - 4))


def _cmp_pool_kernel(pt_ref, *refs):
    pages, cw_ref, o_ref = refs[:NSA_POOL_PG], refs[NSA_POOL_PG], refs[NSA_POOL_PG + 1]
    per_page = PAGE_SIZE // CMP_BLOCK
    row_blk = lax.broadcasted_iota(jnp.int32, (PAGE_SIZE, LANE), 0) // CMP_BLOCK
    col = lax.broadcasted_iota(jnp.int32, (PAGE_SIZE, LANE), 1)
    acc = [jnp.zeros((NSA_KV_COLS, LANE), jnp.float32) for _ in range(2)]
    for j, pg in enumerate(pages):
        seg = jnp.where(col == j * per_page + row_blk, 1.0, 0.0).astype(jnp.bfloat16)
        for kv in range(2):
            xw = pg[kv].reshape(NSA_KV_COLS, PAGE_SIZE) * cw_ref[kv:kv + 1, :]
            hi = xw.astype(jnp.bfloat16)
            lo = (xw - hi.astype(jnp.float32)).astype(jnp.bfloat16)
            acc[kv] = (acc[kv] + jnp.dot(hi, seg, preferred_element_type=jnp.float32)
                       + jnp.dot(lo, seg, preferred_element_type=jnp.float32))
    o_ref[0:NSA_KV_COLS, :] = acc[0]
    o_ref[NSA_KV_COLS:, :] = acc[1]


def _cmp_pool(cache, page_table, layer, cw):
    B, n_pages = page_table.shape
    return pl.pallas_call(
        _cmp_pool_kernel,
        grid_spec=pltpu.PrefetchScalarGridSpec(
            num_scalar_prefetch=1, grid=(B, n_pages // NSA_POOL_PG),
            in_specs=_page_specs(layer, NSA_POOL_PG) + [pl.BlockSpec((2, PAGE_SIZE), lambda b, s, pt: (0, 0))],
            out_specs=pl.BlockSpec((None, PAGE_COLS, LANE), lambda b, s, pt: (b, 0, s))),
        out_shape=jax.ShapeDtypeStruct((B, PAGE_COLS, n_pages * PAGE_SIZE // CMP_BLOCK), jnp.float32),
        compiler_params=pltpu.CompilerParams(dimension_semantics=("arbitrary", "arbitrary"),
                                             vmem_limit_bytes=_VMEM_LIMIT),
        name="nsa_cmp_pool",
    )(page_table, *([cache] * NSA_POOL_PG), cw)


def _cmp_select_kernel(q_ref, kv_ref, ocmp_ref, sel_ref, pt_scr, imp_scr, *, past, steps):
    T = steps
    R = NSA_ROWS * T
    n_cmp = past // CMP_BLOCK
    n_sel = past // SEL_BLOCK + 1
    n_lane = sel_ref.shape[-1]
    ratio = SEL_BLOCK // CMP_BLOCK
    rows_t = lax.broadcasted_iota(jnp.int32, (R, 1), 0) % T
    qpos = past + rows_t
    cmp_end = (lax.broadcasted_iota(jnp.int32, (1, n_cmp), 1) + 1) * CMP_BLOCK - 1
    cmask = cmp_end <= qpos
    q = q_ref[...]
    p_all, o_all = [], []
    for kh in range(NSA_KV_HEADS):
        rows = slice(kh * NSA_GROUP * T, (kh + 1) * NSA_GROUP * T)
        kcb_t = kv_ref[kh * HEAD_DIM:(kh + 1) * HEAD_DIM, :]
        vcb_t = kv_ref[NSA_KV_COLS + kh * HEAD_DIM:NSA_KV_COLS + (kh + 1) * HEAD_DIM, :].astype(jnp.bfloat16)
        s = jnp.dot(q[rows], kcb_t, precision=lax.Precision.HIGHEST, preferred_element_type=jnp.float32)
        s = jnp.where(cmask[rows], s, NEG_INF)
        e = jnp.exp(s - jnp.max(s, axis=-1, keepdims=True))
        p = jnp.where(cmask[rows], e / jnp.sum(e, axis=-1, keepdims=True), 0.0)
        o_all.append(lax.dot_general(p.astype(jnp.bfloat16), vcb_t, (((1,), (1,)), ((), ())),
                                     preferred_element_type=jnp.float32))
        imp = p[0:T]
        for g in range(1, NSA_GROUP):
            imp = imp + p[g * T:(g + 1) * T]
        p_all.extend([imp] * NSA_GROUP)
    ocmp_ref[...] = jnp.concatenate(o_all, axis=0)
    imp_rows = jnp.concatenate(p_all + [jnp.zeros((LANE - R, n_cmp), jnp.float32)], axis=0)
    for c in range(n_cmp // LANE):
        pt_scr[c] = imp_rows[:, c * LANE:(c + 1) * LANE].T
    per = LANE // ratio
    for c in range(n_cmp // LANE):
        part = pt_scr[c, pl.ds(0, per, stride=ratio), :]
        for r in range(1, ratio):
            part = part + pt_scr[c, pl.ds(r, per, stride=ratio), :]
        imp_scr[c * per:(c + 1) * per, :] = part
    n_rows = imp_scr.shape[0]
    imp_scr[n_sel - 1:n_rows, :] = jnp.zeros((n_rows - n_sel + 1, LANE), jnp.float32)
    lane_t = lax.broadcasted_iota(jnp.int32, (1, LANE), 1) % T
    cur = (past + lane_t) // SEL_BLOCK
    blk = lax.broadcasted_iota(jnp.int32, (n_rows, LANE), 0)
    forced = (blk == 0) | (blk == cur) | (blk == cur - 1)
    imp_s = jnp.where(blk > cur, -1.0, jnp.where(forced, FORCE_SCORE, imp_scr[...]))
    imp_s = jnp.where(blk < n_sel, imp_s, -2.0)
    imp_scr[...] = imp_s

    def rank_body(m, rank):
        row = imp_scr[pl.ds(m, 1), :]
        ahead = (row > imp_s) | ((row == imp_s) & (m < blk))
        return rank + jnp.where(ahead, 1, 0)

    rank = lax.fori_loop(0, n_sel, rank_body, jnp.zeros((n_rows, LANE), jnp.int32))
    sel = jnp.where((rank < min(SEL_TOPN, n_sel)) & (imp_s >= 0.0), 1.0, 0.0)
    sel = jnp.concatenate([sel, jnp.zeros((n_lane - n_rows, LANE), jnp.float32)], axis=0)
    for c in range(n_lane // LANE):
        sel_ref[:, c * LANE:(c + 1) * LANE] = sel[c * LANE:(c + 1) * LANE, :].T


def _cmp_select(q_rows, kvcb, past, steps):
    B, R, _ = q_rows.shape
    n_cmp = kvcb.shape[2]
    n_sel = past // SEL_BLOCK + 1
    n_lane = _round_up(n_sel, LANE)
    n_rows = _round_up(n_sel, SUBLANE)
    return pl.pallas_call(
        functools.partial(_cmp_select_kernel, past=past, steps=steps),
        grid=(B,),
        in_specs=[pl.BlockSpec((None, R, HEAD_DIM), lambda b: (b, 0, 0)),
                  pl.BlockSpec((None, PAGE_COLS, n_cmp), lambda b: (b, 0, 0))],
        out_specs=[pl.BlockSpec((None, R, HEAD_DIM), lambda b: (b, 0, 0)),
                   pl.BlockSpec((None, LANE, n_lane), lambda b: (b, 0, 0))],
        out_shape=[jax.ShapeDtypeStruct((B, R, HEAD_DIM), jnp.float32),
                   jax.ShapeDtypeStruct((B, LANE, n_lane), jnp.float32)],
        scratch_shapes=[pltpu.VMEM((n_cmp // LANE, LANE, LANE), jnp.float32),
                        pltpu.VMEM((n_rows, LANE), jnp.float32)],
        compiler_params=pltpu.CompilerParams(dimension_semantics=("arbitrary",),
                                             vmem_limit_bytes=_VMEM_LIMIT),
        name="nsa_cmp_select",
    )(q_rows, kvcb)


def _soft_update(s, mask, v, m, l, acc, keys_last=False):
    s = jnp.where(mask, s, NEG_INF)
    m_new = jnp.maximum(m, jnp.max(s, axis=-1, keepdims=True))
    p = jnp.where(mask, jnp.exp(s - m_new), 0.0)
    alpha = jnp.exp(m - m_new)
    pv = lax.dot_general(p.astype(jnp.bfloat16), v, (((1,), (1 if keys_last else 0,)), ((), ())),
                         preferred_element_type=jnp.float32)
    return m_new, alpha * l + jnp.sum(p, axis=-1, keepdims=True), alpha * acc + pv


def _sel_win_kernel(pt_ref, *refs, past, steps):
    pages = refs[:NSA_PG]
    (q_ref, sel_ref, ocmp_ref, nsel_ref, wpre_ref, nwin_ref, gate_ref,
     o_ref, m_scr, l_scr, acc_scr) = refs[NSA_PG:]
    s_id = pl.program_id(1)
    T = steps
    GT = NSA_GROUP * T
    R = NSA_ROWS * T
    n_lane = sel_ref.shape[-1]
    rows_t = lax.broadcasted_iota(jnp.int32, (GT, 1), 0) % T

    @pl.when(s_id == 0)
    def _():
        m_scr[...] = jnp.full_like(m_scr, NEG_INF)
        l_scr[...] = jnp.zeros_like(l_scr)
        acc_scr[...] = jnp.zeros_like(acc_scr)

    q = q_ref[...].astype(jnp.bfloat16)
    sel_bf = sel_ref[0:R, :].astype(jnp.bfloat16)
    n_keys = NSA_PG * PAGE_SIZE
    blk_id = lax.broadcasted_iota(jnp.int32, (n_lane, n_keys), 0)
    key_blk = s_id * (n_keys // SEL_BLOCK) + lax.broadcasted_iota(jnp.int32, (n_lane, n_keys), 1) // SEL_BLOCK

    def head_cols(x, kh, off):
        return x[:, off + kh * HEAD_DIM:off + (kh + 1) * HEAD_DIM]

    def keys_t(kv, kh):
        return jnp.concatenate([pg[kv, kh] for pg in pages], axis=1).astype(jnp.bfloat16)

    expand = jnp.where(blk_id == key_blk, 1.0, 0.0).astype(jnp.bfloat16)
    chosen = jnp.dot(sel_bf, expand, preferred_element_type=jnp.float32) > 0.5
    head_rows = [slice(kh * GT, (kh + 1) * GT) for kh in range(NSA_KV_HEADS)]
    scores = [jnp.dot(q[rows], keys_t(0, kh), preferred_element_type=jnp.float32)
              for kh, rows in enumerate(head_rows)]
    upd = [_soft_update(scores[kh], chosen[rows], keys_t(1, kh), m_scr[rows], l_scr[rows], acc_scr[rows],
                        keys_last=True) for kh, rows in enumerate(head_rows)]
    for kh, rows in enumerate(head_rows):
        m_scr[rows], l_scr[rows], acc_scr[rows] = upd[kh]

    @pl.when(s_id == pl.num_programs(1) - 1)
    def _():
        new_blk = past // SEL_BLOCK
        tpad = nsel_ref.shape[0]
        jn = lax.broadcasted_iota(jnp.int32, (1, tpad), 1)
        nsel = nsel_ref[...].astype(jnp.bfloat16)
        nwin = nwin_ref[...].astype(jnp.bfloat16)
        jp = lax.broadcasted_iota(jnp.int32, (1, WINDOW), 1)
        gates = jax.nn.sigmoid(gate_ref[...])
        for kh in range(NSA_KV_HEADS):
            rows = slice(kh * GT, (kh + 1) * GT)
            pick = sel_ref[kh * GT:(kh + 1) * GT, new_blk:new_blk + 1] > 0.5
            s = lax.dot_general(q[rows], head_cols(nsel, kh, 0), (((1,), (1,)), ((), ())),
                                preferred_element_type=jnp.float32)
            m, l, acc = _soft_update(s, pick & (jn <= rows_t) & (jn < T), head_cols(nsel, kh, NSA_KV_COLS),
                                     m_scr[rows], l_scr[rows], acc_scr[rows])
            o_sel = acc / l
            init = (jnp.full((GT, 1), NEG_INF, jnp.float32), jnp.zeros((GT, 1), jnp.float32),
                    jnp.zeros((GT, HEAD_DIM), jnp.float32))
            s = jnp.dot(q[rows], wpre_ref[0, kh].astype(jnp.bfloat16), preferred_element_type=jnp.float32)
            st = _soft_update(s, jp > rows_t, wpre_ref[1, kh].astype(jnp.bfloat16), *init, keys_last=True)
            s = lax.dot_general(q[rows], head_cols(nwin, kh, 0), (((1,), (1,)), ((), ())),
                                preferred_element_type=jnp.float32)
            _, l_w, acc_w = _soft_update(s, (jn <= rows_t) & (jn < T), head_cols(nwin, kh, NSA_KV_COLS), *st)
            o_win = acc_w / l_w
            gt = gates[rows]
            o_ref[rows, :] = gt[:, 0:1] * ocmp_ref[rows, :] + gt[:, 1:2] * o_sel + gt[:, 2:3] * o_win


def _sel_win(cache, page_table, layer, q_rows, sel_rows, o_cmp, new_sel, win_prefix, new_win, gate_rows,
             past, steps):
    B, n_pages = page_table.shape
    R = q_rows.shape[1]
    n_lane = sel_rows.shape[-1]
    tpad = new_sel.shape[1]
    per_b = lambda shape: pl.BlockSpec((None,) + shape, lambda b, s, pt: (b,) + (0,) * len(shape))
    return pl.pallas_call(
        functools.partial(_sel_win_kernel, past=past, steps=steps),
        grid_spec=pltpu.PrefetchScalarGridSpec(
            num_scalar_prefetch=1, grid=(B, n_pages // NSA_PG),
            in_specs=_page_specs(layer, NSA_PG) + [
                per_b((R, HEAD_DIM)), per_b((LANE, n_lane)), per_b((R, HEAD_DIM)),
                per_b((tpad, PAGE_COLS)), per_b((2, NSA_KV_HEADS, HEAD_DIM, WINDOW)), per_b((tpad, PAGE_COLS)),
                per_b((R, 3))],
            out_specs=per_b((R, HEAD_DIM)),
            scratch_shapes=[pltpu.VMEM((R, 1), jnp.float32), pltpu.VMEM((R, 1), jnp.float32),
                            pltpu.VMEM((R, HEAD_DIM), jnp.float32)]),
        out_shape=jax.ShapeDtypeStruct((B, R, HEAD_DIM), jnp.float32),
        compiler_params=pltpu.CompilerParams(dimension_semantics=("arbitrary", "arbitrary"),
                                             vmem_limit_bytes=_VMEM_LIMIT),
        name="nsa_sel_win",
    )(page_table, *([cache] * NSA_PG), q_rows, sel_rows, o_cmp, new_sel, win_prefix, new_win, gate_rows)


def _nsa_decode_mix(proj, cache_cmp, cache_sel, page_table, layer, win_prefix, cmp_w):
    B, T, _ = proj.shape
    xn = proj[..., RWKV_COLS:IN_COLS]
    n_pages = page_table.shape[1]
    past = n_pages * PAGE_SIZE
    assert T <= SEL_BLOCK and n_pages % NSA_POOL_PG == 0 and win_prefix.shape[1] == WINDOW
    KVH, G, HD = NSA_KV_HEADS, NSA_GROUP, HEAD_DIM
    kv5 = lambda i: xn[..., D_NSA + 2 * i * NSA_KV_COLS:D_NSA + 2 * (i + 1) * NSA_KV_COLS]
    new_cmp, new_sel, new_win = kv5(0), kv5(1), kv5(2)
    q_rows = (xn[..., :D_NSA] * (HD ** -0.5)).reshape(B, T, KVH, G, HD).transpose(0, 2, 3, 1, 4)
    q_rows = q_rows.reshape(B, KVH * G * T, HD)
    gate_rows = xn[..., D_NSA + 6 * NSA_KV_COLS:].reshape(B, T, KVH, G, 3).transpose(0, 2, 3, 1, 4)
    gate_rows = gate_rows.reshape(B, KVH * G * T, 3)
    cw = jnp.tile(cmp_w, (1, PAGE_SIZE // CMP_BLOCK))
    kvcb = _cmp_pool(_rows_last(cache_cmp), page_table, layer, cw)
    o_cmp, sel_rows = _cmp_select(q_rows, kvcb, past, T)
    tpad = _round_up(T, SUBLANE)
    padt = lambda t: jnp.pad(t, ((0, 0), (0, tpad - T), (0, 0)))
    out = _sel_win(_rows_last(cache_sel), page_table, layer, q_rows, sel_rows, o_cmp, padt(new_sel),
                   _rows_last(win_prefix), padt(new_win), gate_rows, past, T)
    y = out.reshape(B, KVH, G, T, HD).transpose(0, 3, 1, 2, 4).reshape(B, T, D_NSA)
    five = lambda t: t.reshape(B, -1, 2, KVH, HD)
    win_out = jnp.concatenate([win_prefix[:, T:], five(new_win)], axis=1)
    return y, five(new_cmp), five(new_sel), win_out


def _deepnorm(x, h, g, b):
    z = DEEPNORM_ALPHA * x + h
    mu = jnp.mean(z, axis=-1, keepdims=True)
    dev = z - mu
    var = jnp.mean(dev * dev, axis=-1, keepdims=True)
    return dev * lax.rsqrt(var + LN_EPS) * g + b


def _out_proj_kernel(yr_ref, yn_ref, x_ref, w_ref, g_ref, b_ref, o_ref):
    half = yr_ref.shape[1]
    h = (jnp.dot(yr_ref[...].astype(jnp.bfloat16), w_ref[0:half, :], preferred_element_type=jnp.float32)
         + jnp.dot(yn_ref[...].astype(jnp.bfloat16), w_ref[half:, :], preferred_element_type=jnp.float32))
    o_ref[...] = _deepnorm(x_ref[...], h, g_ref[...], b_ref[...])


def _residual_norm_kernel(x_ref, f_ref, g_ref, b_ref, o_ref):
    o_ref[...] = _deepnorm(x_ref[...], f_ref[...], g_ref[...], b_ref[...])


def _residual_norm(x, f, g, b, *, tm=256):
    n, d = x.shape
    tm = min(tm, n)
    rows = pl.BlockSpec((tm, d), lambda i: (i, 0))
    vec = pl.BlockSpec((1, d), lambda i: (0, 0))
    return pl.pallas_call(
        _residual_norm_kernel,
        grid=(n // tm,),
        in_specs=[rows, rows, vec, vec],
        out_specs=rows,
        out_shape=jax.ShapeDtypeStruct((n, d), jnp.float32),
        compiler_params=pltpu.CompilerParams(dimension_semantics=("arbitrary",),
                                             vmem_limit_bytes=_VMEM_LIMIT),
        name="residual_norm",
    )(x, f, g.reshape(1, d), b.reshape(1, d))


def _out_proj_norm(y_r, y_n, x, w_bf, g, b, *, tm=256):
    n, d = x.shape
    tm = min(tm, n)
    half = y_r.shape[1]
    rows = lambda w: pl.BlockSpec((tm, w), lambda i: (i, 0))
    full = lambda shape: pl.BlockSpec(shape, lambda i: (0, 0))
    return pl.pallas_call(
        _out_proj_kernel,
        grid=(n // tm,),
        in_specs=[rows(half), rows(y_n.shape[1]), rows(d), full(w_bf.shape), full((1, d)), full((1, d))],
        out_specs=rows(d),
        out_shape=jax.ShapeDtypeStruct((n, d), jnp.float32),
        compiler_params=pltpu.CompilerParams(dimension_semantics=("arbitrary",),
                                             vmem_limit_bytes=_VMEM_LIMIT),
        name="out_proj_norm",
    )(y_r, y_n, x, w_bf, g.reshape(1, d), b.reshape(1, d))


def _hybrid_layer(x, past, shift_prev, rwkv_s0, n_keep, p):
    B, T, D = x.shape
    proj = _matmul(x.reshape(B * T, D), p['w_in_bf']).reshape(B, T, IN_COLS_PAD)
    y_r, s_T, new_shift = _rwkv_time_mix(proj, shift_prev, rwkv_s0, p)
    if past is None:
        assert T % NSA_TQ == 0 and T >= n_keep
        y_n, new_cmp, new_sel, new_win = _nsa_prompt_mix(proj, n_keep, p['nsa_cmp_w'])
    else:
        assert n_keep == WINDOW
        y_n, new_cmp, new_sel, new_win = _nsa_decode_mix(proj, *past, p['nsa_cmp_w'])
    x = _out_proj_norm(y_r.reshape(B * T, D_RWKV), y_n.reshape(B * T, D_NSA), x.reshape(B * T, D),
                       p['w_out_bf'], p['ln1_g'], p['ln1_b']).reshape(B, T, D)
    f = _peer_ffn(x, p['peer_wq_bf'], p['peer_sk_bf'], p['peer_u_bf'], p['peer_v_bf'])
    x = _residual_norm(x.reshape(B * T, D), f.reshape(B * T, D), p['ln2_g'], p['ln2_b']).reshape(B, T, D)
    return x, (new_cmp, new_sel, new_win, s_T, new_shift)


def kernel(x_prompt, x_sample, cache_cmp_kv, cache_sel_kv, page_table, state_win_kv, state_rwkv,
           state_shift, w_in, rwkv_mu, rwkv_w0, rwkv_w2, rwkv_a0, rwkv_a2, rwkv_g2, rwkv_k_k,
           rwkv_k_a, rwkv_r_k, rwkv_gn_g, rwkv_gn_b, nsa_cmp_w, w_out, ln1_g, ln1_b, peer_wq,
           peer_subkeys, peer_u, peer_v, ln2_g, ln2_b):
    bp = x_prompt.shape[0]
    dt = x_prompt.dtype
    assert page_table.shape[1] * PAGE_SIZE == PAST_LEN
    n_keep = state_win_kv.shape[2]
    zero_shift = jnp.zeros((bp, 1, RWKV_COLS), dt)
    zero_state = jnp.zeros((bp, H_RWKV, HEAD_DIM, HEAD_DIM), dt)
    yp, ys = x_prompt, x_sample
    st_p, st_s = [], []
    for l in range(DEPTH):
        p = {'w_in_bf': jnp.pad(w_in[l].astype(jnp.bfloat16), ((0, 0), (0, IN_COLS_PAD - IN_COLS))),
             'rwkv_mu': rwkv_mu[l], 'rwkv_w0': rwkv_w0[l], 'rwkv_w2': rwkv_w2[l],
             'rwkv_a0': rwkv_a0[l], 'rwkv_a2': rwkv_a2[l], 'rwkv_g2': rwkv_g2[l],
             'rwkv_k_k': rwkv_k_k[l], 'rwkv_k_a': rwkv_k_a[l], 'rwkv_r_k': rwkv_r_k[l],
             'rwkv_gn_g': rwkv_gn_g[l], 'rwkv_gn_b': rwkv_gn_b[l], 'nsa_cmp_w': nsa_cmp_w[l],
             'w_out_bf': w_out[l].astype(jnp.bfloat16), 'ln1_g': ln1_g[l], 'ln1_b': ln1_b[l],
             'peer_wq_bf': peer_wq[l].astype(jnp.bfloat16),
             'peer_sk_bf': peer_subkeys[l].reshape(2 * P_HEADS, N_KEYS, P_DKEY // 2).astype(jnp.bfloat16),
             'peer_u_bf': peer_u[l].astype(jnp.bfloat16).T, 'peer_v_bf': peer_v[l].astype(jnp.bfloat16),
             'ln2_g': ln2_g[l], 'ln2_b': ln2_b[l]}
        yp, sp = _hybrid_layer(yp, None, zero_shift, zero_state, n_keep, p)
        past = (cache_cmp_kv, cache_sel_kv, page_table, l, state_win_kv[l])
        ys, ss = _hybrid_layer(ys, past, state_shift[l], state_rwkv[l], n_keep, p)
        st_p.append(sp)
        st_s.append(ss)
    stk = lambda sts, i: jnp.stack([s[i] for s in sts], axis=0)
    return (yp, ys, stk(st_p, 0), stk(st_p, 1), stk(st_p, 2), stk(st_p, 3), stk(st_p, 4),
            stk(st_s, 0), stk(st_s, 1), stk(st_s, 2), stk(st_s, 3), stk(st_s, 4))
```
